```python
import math, functools
import numpy as np
import jax
import jax.numpy as jnp
from jax import lax

D_MODEL = 1024
BATCH = 2
SEQ = 8192
DEPTH = 2
DEC_BATCH = 128
DEC_SEQ = 4
PAST_LEN = 2048
PAGE_SIZE = 128

MIX_WIDTH = D_MODEL // 2
N_BRANCH = 3
NSA_HEADS = 8
NSA_HD = MIX_WIDTH // NSA_HEADS
NSA_KV = 2
NSA_GRP = NSA_HEADS // NSA_KV
CMP_LEN = 32
CMP_STRIDE = 16
CMP_HIDDEN = 64
SEL_LEN = 64
SEL_TOPK = 16
WINDOW = 512
Q_BLOCK = 128
FORCE_BONUS = 1000.0
GDN_HEADS = 4
GDN_HD = MIX_WIDTH // GDN_HEADS
CONV_W = 4
GDN_CHUNK = 64
RET_HEADS = 4
RET_HD = MIX_WIDTH // RET_HEADS
RET_CHUNK = 128
ROPE_BASE = 10000.0
D_FF = 4 * D_MODEL
PLE_DIM = 256
EPS = 1e-6
NEG = -1e30
IN_SIZES = (NSA_HEADS * NSA_HD, 6 * NSA_KV * NSA_HD, 3 * NSA_HEADS,
            3 * MIX_WIDTH, GDN_HEADS, GDN_HEADS, MIX_WIDTH,
            3 * MIX_WIDTH, MIX_WIDTH, N_BRANCH * D_MODEL)
IN_COLS = sum(IN_SIZES)

kernel_name = 'nsa_gdn_retention_hybrid_step'


def rmsnorm(x, g):
    xf = x.astype(jnp.float32)
    y = xf * lax.rsqrt(jnp.mean(xf * xf, axis=-1, keepdims=True) + EPS)
    return (y * g.astype(jnp.float32)).astype(x.dtype)


def l2norm(x):
    return x * lax.rsqrt(jnp.sum(x * x, axis=-1, keepdims=True) + EPS)


def masked_softmax(s, mask):
    s = jnp.where(mask, s.astype(jnp.float32), NEG)
    return jax.nn.softmax(s, axis=-1) * jnp.any(mask, axis=-1, keepdims=True)


def to_chunks(x, c):
    b, t = x.shape[:2]
    x = x.reshape(b, t // c, c, *x.shape[2:])
    return jnp.swapaxes(jnp.moveaxis(x, 1, 0), 2, 3)


def from_chunks(x):
    x = jnp.moveaxis(jnp.swapaxes(x, 2, 3), 0, 1)
    return x.reshape(x.shape[0], -1, *x.shape[3:])


def rotary(x, pos):
    half = x.shape[-1] // 2
    inv = ROPE_BASE ** (-jnp.linspace(0.0, 1.0, half, dtype=jnp.float32))
    ang = pos.astype(jnp.float32)[:, None] * inv[None, :]
    cos = jnp.cos(ang)[None, :, None, :]
    sin = jnp.sin(ang)[None, :, None, :]
    xf = x.astype(jnp.float32)
    x1, x2 = xf[..., :half], xf[..., half:]
    return jnp.concatenate([x1 * cos - x2 * sin, x2 * cos + x1 * sin], axis=-1)


def compress(k, pe, w1, w2):
    b, t = k.shape[:2]
    nsub = t // CMP_STRIDE
    sub = k[:, :nsub * CMP_STRIDE].reshape(b, nsub, CMP_STRIDE, NSA_KV, NSA_HD)
    blocks = jnp.concatenate([sub[:, :-1], sub[:, 1:]], axis=2) + pe[:, None, :]
    flat = jnp.swapaxes(blocks, 2, 3).reshape(b, nsub - 1, NSA_KV, CMP_LEN * NSA_HD)
    return jax.nn.gelu(flat @ w1) @ w2


def nsa_core(q, q_pos, kc, vc, ks, vs, kw, vw, kw_pos, g):
    bsz, tq = q.shape[:2]
    scale = NSA_HD ** -0.5
    qg = q.reshape(bsz, tq, NSA_KV, NSA_GRP, NSA_HD)
    nc, ns = kc.shape[1], ks.shape[1]
    c_start = jnp.arange(nc) * CMP_STRIDE
    m_c = (c_start[None, :] + CMP_LEN - 1) <= q_pos[:, None]
    p_c = masked_softmax(jnp.einsum('btgjd,bngd->btgjn', qg, kc) * scale, m_c[None, :, None, None, :])
    o_c = jnp.einsum('btgjn,bngd->btgjd', p_c.astype(vc.dtype), vc)
    blk = jnp.arange(ns)
    overlap = ((c_start[:, None] < (blk[None, :] + 1) * SEL_LEN)
               & (blk[None, :] * SEL_LEN < c_start[:, None] + CMP_LEN)).astype(jnp.float32)
    imp = jnp.einsum('btgn,ns->btgs', jnp.sum(p_c, axis=3), overlap)
    valid = blk[None, :] * SEL_LEN <= q_pos[:, None]
    cur = q_pos[:, None] // SEL_LEN
    forced = (blk[None, :] == 0) | (blk[None, :] == cur) | (blk[None, :] == cur - 1)
    score = jnp.where(valid[None, :, None, :], imp + jnp.where(forced, FORCE_BONUS, 0.0)[None, :, None, :], NEG)
    _, idx = lax.top_k(score, min(SEL_TOPK, ns))
    sel_ok = jnp.take_along_axis(jnp.broadcast_to(valid[None, :, None, :], score.shape), idx, axis=-1)
    bi = jnp.arange(bsz)[:, None, None, None]
    gi = jnp.arange(NSA_KV)[None, None, :, None]
    kb = ks[bi, idx, :, gi, :]
    vb = vs[bi, idx, :, gi, :]
    kpos = idx[..., None] * SEL_LEN + jnp.arange(SEL_LEN)
    m_s = sel_ok[..., None] & (kpos <= q_pos[None, :, None, None, None])
    s_s = jnp.einsum('btgjd,btgkld->btgjkl', qg, kb) * scale
    p_s = masked_softmax(s_s.reshape(bsz, tq, NSA_KV, NSA_GRP, -1), m_s.reshape(bsz, tq, NSA_KV, 1, -1))
    o_s = jnp.einsum('btgjkl,btgkld->btgjd', p_s.reshape(s_s.shape).astype(vb.dtype), vb)
    m_w = ((kw_pos[None, :] <= q_pos[:, None]) & (kw_pos[None, :] > q_pos[:, None] - WINDOW)
           & (kw_pos[None, :] >= 0))
    p_w = masked_softmax(jnp.einsum('btgjd,bwgd->btgjw', qg, kw) * scale, m_w[None, :, None, None, :])
    o_w = jnp.einsum('btgjw,bwgd->btgjd', p_w.astype(vw.dtype), vw)
    gg = g.reshape(bsz, tq, NSA_KV, NSA_GRP, 3)
    o = gg[..., 0:1] * o_c + gg[..., 1:2] * o_s + gg[..., 2:3] * o_w
    return o.reshape(bsz, tq, NSA_HEADS * NSA_HD)


def nsa_prompt(q, kv, kvw, g, cmp_pe, cmp_w1, cmp_w2):
    bsz, t = q.shape[:2]
    kc = compress(kv[:, :, 0], cmp_pe[0], cmp_w1[0], cmp_w2[0])
    vc = compress(kv[:, :, 1], cmp_pe[1], cmp_w1[1], cmp_w2[1])
    ks = kv[:, :, 2].reshape(bsz, t // SEL_LEN, SEL_LEN, NSA_KV, NSA_HD)
    vs = kv[:, :, 3].reshape(bsz, t // SEL_LEN, SEL_LEN, NSA_KV, NSA_HD)
    win_pad = jnp.pad(kvw, ((0, 0), (WINDOW, 0), (0, 0), (0, 0), (0, 0)))
    span = WINDOW + Q_BLOCK

    def one_block(blk):
        s0 = blk * Q_BLOCK
        qb = lax.dynamic_slice_in_dim(q, s0, Q_BLOCK, axis=1)
        gb = lax.dynamic_slice_in_dim(g, s0, Q_BLOCK, axis=1)
        wb = lax.dynamic_slice_in_dim(win_pad, s0, span, axis=1)
        q_pos = s0 + jnp.arange(Q_BLOCK)
        kw_pos = s0 - WINDOW + jnp.arange(span)
        return nsa_core(qb, q_pos, kc, vc, ks, vs, wb[:, :, 0], wb[:, :, 1], kw_pos, gb)

    out = lax.map(one_block, jnp.arange(t // Q_BLOCK))
    out = jnp.moveaxis(out, 0, 1).reshape(bsz, t, NSA_HEADS * NSA_HD)
    return out, kv, kvw[:, t - min(WINDOW, t):]


def nsa_sample(q, kv, kvw, g, cache_kv, page_table, cache_win, cmp_pe, cmp_w1, cmp_w2):
    bsz, t = q.shape[:2]
    past = cache_kv[page_table].reshape(bsz, -1, 4, NSA_KV, NSA_HD).astype(kv.dtype)
    full = jnp.concatenate([past, kv], axis=1)
    tk = full.shape[1]
    kc = compress(full[:, :, 0], cmp_pe[0], cmp_w1[0], cmp_w2[0])
    vc = compress(full[:, :, 1], cmp_pe[1], cmp_w1[1], cmp_w2[1])
    ns = -(-tk // SEL_LEN)
    sel = jnp.pad(full[:, :, 2:4], ((0, 0), (0, ns * SEL_LEN - tk), (0, 0), (0, 0), (0, 0)))
    sel = sel.reshape(bsz, ns, SEL_LEN, 2, NSA_KV, NSA_HD)
    win = jnp.concatenate([cache_win.astype(kvw.dtype), kvw], axis=1)
    wk = cache_win.shape[1]
    kw_pos = PAST_LEN - wk + jnp.arange(wk + t)
    q_pos = PAST_LEN + jnp.arange(t)
    out = nsa_core(q, q_pos, kc, vc, sel[:, :, :, 0], sel[:, :, :, 1], win[:, :, 0], win[:, :, 1], kw_pos, g)
    return out, kv, win[:, win.shape[1] - min(WINDOW, tk):]


def causal_conv(x, prev, w):
    t = x.shape[1]
    xp = jnp.concatenate([prev.astype(x.dtype), x], axis=1)
    y = sum(xp[:, j:j + t] * w[j] for j in range(CONV_W))
    return y, xp[:, t:]


def gated_delta(q, k, v, beta, g, s0):
    c = GDN_CHUNK if q.shape[1] % GDN_CHUNK == 0 else q.shape[1]
    incl = jnp.tril(jnp.ones((c, c), bool))
    strict = jnp.tril(jnp.ones((c, c), bool), -1)
    eye = jnp.eye(c, dtype=jnp.float32)

    def step(s, inp):
        qc, kc, vc, bc, gc = inp
        gcum = jnp.cumsum(gc, axis=-1)
        diff = gcum[..., :, None] - gcum[..., None, :]
        decay = jnp.where(incl, jnp.exp(jnp.where(incl, diff, 0.0)), 0.0)
        a = jnp.where(strict, jnp.einsum('bhik,bhjk->bhij', kc, kc) * decay * bc[..., :, None], 0.0)
        rhs = jnp.concatenate([vc * bc[..., None], kc * (bc * jnp.exp(gcum))[..., None]], axis=-1)
        sol = lax.linalg.triangular_solve(eye + a, rhs, left_side=True, lower=True, unit_diagonal=True)
        u, w = sol[..., :GDN_HD], sol[..., GDN_HD:]
        v_new = u - jnp.einsum('bhik,bhkv->bhiv', w, s)
        qk = jnp.einsum('bhik,bhjk->bhij', qc, kc) * decay
        o = (jnp.einsum('bhik,bhkv->bhiv', qc * jnp.exp(gcum)[..., None], s)
             + jnp.einsum('bhij,bhjv->bhiv', qk, v_new))
        glast = gcum[..., -1:]
        s = s * jnp.exp(glast)[..., None] + jnp.einsum('bhjk,bhjv->bhkv', kc * jnp.exp(glast - gcum)[..., None], v_new)
        return s, o

    s, o = lax.scan(step, s0, tuple(to_chunks(a_, c) for a_ in (q, k, v, beta, g)))
    return from_chunks(o), s


def gdn_branch(qkv, a_in, b_in, z, conv_prev, s0, conv_w, a_log, dt_bias, norm_g):
    bsz, t, _ = qkv.shape
    y, conv_state = causal_conv(qkv, conv_prev, conv_w)
    y = jax.nn.silu(y).astype(jnp.float32)
    q, k, v = jnp.split(y, 3, axis=-1)
    shp = (bsz, t, GDN_HEADS, GDN_HD)
    q = l2norm(q.reshape(shp)) * (GDN_HD ** -0.5)
    k = l2norm(k.reshape(shp))
    v = v.reshape(shp)
    beta = jax.nn.sigmoid(b_in.astype(jnp.float32))
    g = -jnp.exp(a_log.astype(jnp.float32)) * jax.nn.softplus(a_in.astype(jnp.float32) + dt_bias.astype(jnp.float32))
    o, s = gated_delta(q, k, v, beta, g, s0.astype(jnp.float32))
    o = rmsnorm(o, norm_g) * jax.nn.silu(z.astype(jnp.float32).reshape(shp))
    return o.reshape(bsz, t, MIX_WIDTH).astype(qkv.dtype), conv_state, s


def retention(qkv, gate, pos, s0):
    bsz, t, _ = qkv.shape
    shp = (bsz, t, RET_HEADS, RET_HD)
    q, k, v = jnp.split(qkv, 3, axis=-1)
    q = rotary(q.reshape(shp), pos)
    k = rotary(k.reshape(shp), pos) * (RET_HD ** -0.5)
    v = v.reshape(shp).astype(jnp.float32)
    c = RET_CHUNK if t % RET_CHUNK == 0 else t
    lg = jnp.log1p(-jnp.exp2(-5.0 - jnp.arange(RET_HEADS, dtype=jnp.float32)))
    n = jnp.arange(c, dtype=jnp.float32)
    diff = n[:, None] - n[None, :]
    dmat = jnp.where(diff >= 0, jnp.exp(jnp.maximum(diff, 0.0)[None] * lg[:, None, None]), 0.0)
    q_dec = jnp.exp((n + 1.0)[None, :] * lg[:, None])[..., None]
    k_dec = jnp.exp((c - 1.0 - n)[None, :] * lg[:, None])[..., None]
    c_dec = jnp.exp(c * lg)[:, None, None]

    def step(s, inp):
        qc, kc, vc = inp
        o = (jnp.einsum('bhij,bhjv->bhiv', jnp.einsum('bhik,bhjk->bhij', qc, kc) * dmat, vc)
             + jnp.einsum('bhik,bhkv->bhiv', qc, s) * q_dec)
        s = s * c_dec + jnp.einsum('bhjk,bhjv->bhkv', kc * k_dec, vc)
        return s, o

    s, o = lax.scan(step, s0.astype(jnp.float32), (to_chunks(q, c), to_chunks(k, c), to_chunks(v, c)))
    o = from_chunks(o)
    o = o * lax.rsqrt(jnp.mean(o * o, axis=-1, keepdims=True) + EPS)
    o = o * jax.nn.silu(gate.astype(jnp.float32).reshape(shp))
    return o.reshape(bsz, t, MIX_WIDTH).astype(qkv.dtype), s


def trunk_layer(x, p_l, pos, nsa_fn, conv_prev, gdn_s0, ret_s0, W):
    bsz, t, _ = x.shape
    h = rmsnorm(x, W['g_mix'])
    z = h @ W['w_in']
    cuts = np.cumsum(IN_SIZES)[:-1].tolist()
    nq, nkv, ngate, gqkv, g_a, g_b, g_z, rqkv, rgate, mgate = jnp.split(z, cuts, axis=-1)
    q = nq.reshape(bsz, t, NSA_HEADS, NSA_HD)
    kv = nkv.reshape(bsz, t, 6, NSA_KV, NSA_HD)
    ng = jax.nn.sigmoid(ngate.reshape(bsz, t, NSA_HEADS, 3))
    o_nsa, nsa_rows, nsa_win = nsa_fn(q, kv[:, :, :4], kv[:, :, 4:], ng)
    o_gdn, conv_state, gdn_s = gdn_branch(gqkv, g_a, g_b, g_z, conv_prev, gdn_s0,
                                          W['conv_w'], W['a_log'], W['dt_bias'], W['norm_g'])
    o_ret, ret_s = retention(rqkv, rgate, pos, ret_s0)
    branches = jnp.einsum('btnc,ncd->btnd', jnp.stack([o_nsa, o_gdn, o_ret], axis=2), W['w_branch'])
    gates = jax.nn.sigmoid(mgate.reshape(bsz, t, N_BRANCH, D_MODEL))
    x = x + jnp.sum(gates * branches, axis=2) @ W['w_out']
    h = rmsnorm(x, W['g_mlp'])
    x = x + jnp.square(jax.nn.relu(h @ W['w_up'])) @ W['w_down']
    x = x + (p_l @ W['w_ple']) * jax.nn.sigmoid(x @ W['w_ple_gate'])
    return x, nsa_rows, nsa_win, conv_state, gdn_s, ret_s


def setup_inputs(seed: int = 0) -> dict:
    key = jax.random.key(seed)
    ks = jax.random.split(key, 32)
    f32 = jnp.float32

    def nrm(k, shape, scale):
        return jax.random.normal(k, shape, f32) * scale

    n_pages = PAST_LEN // PAGE_SIZE
    n_used = DEC_BATCH * n_pages
    n_phys = n_used + max(n_used // 4, 1)
    perm = jax.random.permutation(ks[0], n_phys).astype(jnp.int32)
    page_table = perm[:n_used].reshape(DEC_BATCH, n_pages)
    wk = min(WINDOW, PAST_LEN)
    dt = jnp.exp(jax.random.uniform(ks[1], (DEPTH, GDN_HEADS), f32, math.log(1e-3), math.log(1e-1)))
    return {
        'x_prompt': nrm(ks[2], (BATCH, SEQ, D_MODEL), 1.0),
        'x_sample': nrm(ks[3], (DEC_BATCH, DEC_SEQ, D_MODEL), 1.0),
        'cache_nsa_kv': nrm(ks[4], (DEPTH, n_phys, PAGE_SIZE, 4, NSA_KV, NSA_HD), 1.0),
        'cache_nsa_win': nrm(ks[5], (DEPTH, DEC_BATCH, wk, 2, NSA_KV, NSA_HD), 1.0),
        'state_gdn_conv': nrm(ks[6], (DEPTH, DEC_BATCH, CONV_W - 1, 3 * MIX_WIDTH), 1.0),
        'state_gdn': nrm(ks[7], (DEPTH, DEC_BATCH, GDN_HEADS, GDN_HD, GDN_HD), 0.1),
        'state_ret': nrm(ks[8], (DEPTH, DEC_BATCH, RET_HEADS, RET_HD, RET_HD), 0.3),
        'page_table': page_table,
        'p_prompt': nrm(ks[9], (DEPTH, BATCH, SEQ, PLE_DIM), 1.0),
        'p_sample': nrm(ks[10], (DEPTH, DEC_BATCH, DEC_SEQ, PLE_DIM), 1.0),
        'g_mix': 1.0 + nrm(ks[11], (DEPTH, D_MODEL), 0.1),
        'w_in': nrm(ks[12], (DEPTH, D_MODEL, IN_COLS), D_MODEL ** -0.5),
        'nsa_cmp_pe': nrm(ks[13], (DEPTH, 2, CMP_LEN, NSA_HD), 0.1),
        'nsa_cmp_w1': nrm(ks[14], (DEPTH, 2, CMP_LEN * NSA_HD, CMP_HIDDEN), (CMP_LEN * NSA_HD) ** -0.5),
        'nsa_cmp_w2': nrm(ks[15], (DEPTH, 2, CMP_HIDDEN, NSA_HD), CMP_HIDDEN ** -0.5),
        'gdn_conv_w': nrm(ks[16], (DEPTH, CONV_W, 3 * MIX_WIDTH), CONV_W ** -0.5),
        'gdn_a_log': jnp.log(jax.random.uniform(ks[17], (DEPTH, GDN_HEADS), f32, 1.0, 16.0)),
        'gdn_dt_bias': dt + jnp.log(-jnp.expm1(-dt)),
        'gdn_norm_g': 1.0 + nrm(ks[18], (DEPTH, GDN_HD), 0.1),
        'w_branch': nrm(ks[19], (DEPTH, N_BRANCH, MIX_WIDTH, D_MODEL), MIX_WIDTH ** -0.5),
        'w_out': nrm(ks[20], (DEPTH, D_MODEL, D_MODEL), D_MODEL ** -0.5),
        'g_mlp': 1.0 + nrm(ks[21], (DEPTH, D_MODEL), 0.1),
        'w_up': nrm(ks[22], (DEPTH, D_MODEL, D_FF), D_MODEL ** -0.5),
        'w_down': nrm(ks[23], (DEPTH, D_FF, D_MODEL), D_FF ** -0.5),
        'w_ple': nrm(ks[24], (DEPTH, PLE_DIM, D_MODEL), PLE_DIM ** -0.5),
        'w_ple_gate': nrm(ks[25], (DEPTH, D_MODEL, D_MODEL), D_MODEL ** -0.5),
        'g_final': 1.0 + nrm(ks[26], (D_MODEL,), 0.1),
    }


def reference(x_prompt, x_sample, cache_nsa_kv, cache_nsa_win, state_gdn_conv, state_gdn, state_ret,
              page_table, p_prompt, p_sample, g_mix, w_in, nsa_cmp_pe, nsa_cmp_w1, nsa_cmp_w2,
              gdn_conv_w, gdn_a_log, gdn_dt_bias, gdn_norm_g, w_branch, w_out, g_mlp, w_up, w_down,
              w_ple, w_ple_gate, g_final):
    xp, xs = x_prompt, x_sample
    bp, tp = xp.shape[:2]
    ts = xs.shape[1]
    pos_p = jnp.arange(tp)
    pos_s = PAST_LEN + jnp.arange(ts)
    conv0 = jnp.zeros((bp, CONV_W - 1, 3 * MIX_WIDTH), xp.dtype)
    gdn0 = jnp.zeros((bp, GDN_HEADS, GDN_HD, GDN_HD), jnp.float32)
    ret0 = jnp.zeros((bp, RET_HEADS, RET_HD, RET_HD), jnp.float32)
    kv_p, kv_s, win_p, win_s, cv_p, cv_s, sg_p, sg_s, sr_p, sr_s = ([] for _ in range(10))
    for i in range(DEPTH):
        W = dict(g_mix=g_mix[i], w_in=w_in[i], conv_w=gdn_conv_w[i], a_log=gdn_a_log[i],
                 dt_bias=gdn_dt_bias[i], norm_g=gdn_norm_g[i], w_branch=w_branch[i], w_out=w_out[i],
                 g_mlp=g_mlp[i], w_up=w_up[i], w_down=w_down[i], w_ple=w_ple[i], w_ple_gate=w_ple_gate[i])
        cmp_w = dict(cmp_pe=nsa_cmp_pe[i], cmp_w1=nsa_cmp_w1[i], cmp_w2=nsa_cmp_w2[i])
        xp, r, w, c, sg, sr = trunk_layer(xp, p_prompt[i], pos_p, functools.partial(nsa_prompt, **cmp_w),
                                          conv0, gdn0, ret0, W)
        kv_p.append(r); win_p.append(w); cv_p.append(c)
        sg_p.append(sg.astype(xp.dtype)); sr_p.append(sr.astype(xp.dtype))
        sample_nsa = functools.partial(nsa_sample, cache_kv=cache_nsa_kv[i], page_table=page_table,
                                       cache_win=cache_nsa_win[i], **cmp_w)
        xs, r, w, c, sg, sr = trunk_layer(xs, p_sample[i], pos_s, sample_nsa, state_gdn_conv[i],
                                          state_gdn[i], state_ret[i], W)
        kv_s.append(r); win_s.append(w); cv_s.append(c)
        sg_s.append(sg.astype(xs.dtype)); sr_s.append(sr.astype(xs.dtype))
    y_prompt = rmsnorm(xp, g_final)
    y_sample = rmsnorm(xs, g_final)
    return (y_prompt, y_sample, jnp.stack(kv_p), jnp.stack(kv_s), jnp.stack(win_p), jnp.stack(win_s),
            jnp.stack(cv_p), jnp.stack(cv_s), jnp.stack(sg_p), jnp.stack(sg_s), jnp.stack(sr_p), jnp.stack(sr_s))
```

```python
import functools
import math

import numpy as np
import jax
import jax.numpy as jnp
from jax import lax
from jax.experimental import pallas as pl
from jax.experimental.pallas import tpu as pltpu

F32 = jnp.float32
BF16 = jnp.bfloat16

D_MODEL = 1024
MIX_WIDTH = D_MODEL // 2
N_BRANCH = 3
NSA_HEADS = 8
NSA_HD = 64
NSA_KV = 2
NSA_GRP = NSA_HEADS // NSA_KV
CMP_LEN = 32
CMP_STRIDE = 16
CMP_HIDDEN = 64
SEL_LEN = 64
SEL_TOPK = 16
WINDOW = 512
FORCE_BONUS = 1000.0
GDN_HEADS = 4
GDN_HD = 128
CONV_W = 4
GDN_CHUNK = 64
RET_HEADS = 4
RET_HD = 128
RET_CHUNK = 128
ROPE_BASE = 10000.0
D_FF = 4 * D_MODEL
PLE_DIM = 256
EPS = 1e-6
NEG = -1e30
PAGE_SIZE = 128

LANES = 128
SUBLANES = 8
SEQ_PAD = SUBLANES
VMEM_LIMIT = 56 * 1024 * 1024

ZQ = 0
ZKV = 1024
ZKW = 1536
ZNG = 1792
ZGAB = 1920
ZGZ = 2048
ZRG = 2560
ZGQ = 3072
ZRQ = 4608
ZMG = 6144
ZW = 9216


def _sigmoid(x):
    return 1.0 / (1.0 + jnp.exp(-x))


def _silu(x):
    return x * _sigmoid(x)


def _softplus(x):
    return jnp.maximum(x, 0.0) + jnp.log1p(jnp.exp(-jnp.abs(x)))


def _gelu_tanh(x):
    return 0.5 * x * (1.0 + jnp.tanh(math.sqrt(2.0 / math.pi) * (x + 0.044715 * (x * x * x))))


def _rms(x, g):
    return x * lax.rsqrt(jnp.mean(x * x, axis=-1, keepdims=True) + EPS) * g


def _dot(a, b):
    return jnp.dot(a, b, preferred_element_type=F32)


def _dot_nt(a, b):
    return lax.dot_general(a, b, (((1,), (1,)), ((), ())), preferred_element_type=F32)


def _bdot(a, b):
    return _dot(a.astype(BF16), b.astype(BF16))


def _bdot_nt(a, b):
    return _dot_nt(a.astype(BF16), b.astype(BF16))


def _split2(a):
    hi = a.astype(BF16)
    lo = (a - hi.astype(F32)).astype(BF16)
    return hi, lo


def _dot3(a, b):
    ah, al = _split2(a)
    bh, bl = _split2(b)
    return _dot(ah, bh) + (_dot(ah, bl) + _dot(al, bh))


def _dot_mask_left(m01, x):
    m = m01.astype(BF16)
    x1 = x.astype(BF16)
    r1 = x - x1.astype(F32)
    x2 = r1.astype(BF16)
    x3 = (r1 - x2.astype(F32)).astype(BF16)
    return _dot(m, x1) + (_dot(m, x2) + _dot(m, x3))


def _pick_tile(n, prefs):
    for p in prefs:
        if n % p == 0:
            return p
    raise ValueError(f"no tile in {prefs} divides {n}")


def _cparams(sem):
    return pltpu.CompilerParams(dimension_semantics=sem, vmem_limit_bytes=VMEM_LIMIT)


def _in_kernel(x_ref, g_ref, w_ref, z_ref, h_ref):
    @pl.when(pl.program_id(1) == 0)
    def _():
        h_ref[...] = _rms(x_ref[...], g_ref[...]).astype(BF16)

    z_ref[...] = _dot(h_ref[...], w_ref[...])


def in_proj(x, g, w):
    n = x.shape[0]
    tm = _pick_tile(n, (1024, 512, 256, 128))
    tn = 1024
    return pl.pallas_call(
        _in_kernel,
        grid=(n // tm, ZW // tn),
        in_specs=[pl.BlockSpec((tm, D_MODEL), lambda i, j: (i, 0)),
                  pl.BlockSpec((1, D_MODEL), lambda i, j: (0, 0)),
                  pl.BlockSpec((D_MODEL, tn), lambda i, j: (0, j))],
        out_specs=pl.BlockSpec((tm, tn), lambda i, j: (i, j)),
        out_shape=jax.ShapeDtypeStruct((n, ZW), F32),
        scratch_shapes=[pltpu.VMEM((tm, D_MODEL), BF16)],
        compiler_params=_cparams(("parallel", "arbitrary")),
        name="in_proj",
    )(x, g, w)


def _merge_kernel(x_ref, on_ref, og_ref, or_ref, mg_ref, wb_ref, wo_ref, out_ref):
    m = None
    for b, o_ref in enumerate((on_ref, og_ref, or_ref)):
        br = _dot(o_ref[...].astype(BF16), wb_ref[b])
        t = _sigmoid(mg_ref[:, b * D_MODEL:(b + 1) * D_MODEL]) * br
        m = t if m is None else m + t
    out_ref[...] = x_ref[...] + _dot(m.astype(BF16), wo_ref[...])


def merge(x, o_nsa, o_gdn, o_ret, z, wb, wo):
    n = x.shape[0]
    tm = _pick_tile(n, (512, 256, 128))
    row = lambda i: (i, 0)
    return pl.pallas_call(
        _merge_kernel,
        grid=(n // tm,),
        in_specs=[pl.BlockSpec((tm, D_MODEL), row),
                  pl.BlockSpec((tm, MIX_WIDTH), row),
                  pl.BlockSpec((tm, MIX_WIDTH), row),
                  pl.BlockSpec((tm, MIX_WIDTH), row),
                  pl.BlockSpec((tm, N_BRANCH * D_MODEL), lambda i: (i, ZMG // (N_BRANCH * D_MODEL))),
                  pl.BlockSpec((N_BRANCH, MIX_WIDTH, D_MODEL), lambda i: (0, 0, 0)),
                  pl.BlockSpec((D_MODEL, D_MODEL), lambda i: (0, 0))],
        out_specs=pl.BlockSpec((tm, D_MODEL), row),
        out_shape=jax.ShapeDtypeStruct((n, D_MODEL), F32),
        compiler_params=_cparams(("parallel",)),
        name="merge",
    )(x, o_nsa, o_gdn, o_ret, z, wb, wo)


def _mlp_kernel(x_ref, g_ref, wu_ref, wd_ref, p_ref, wp_ref, wpg_ref, gf_ref, out_ref, h_ref, acc_ref,
                *, final_norm):
    f = pl.program_id(1)

    @pl.when(f == 0)
    def _():
        h_ref[...] = _rms(x_ref[...], g_ref[...]).astype(BF16)
        acc_ref[...] = jnp.zeros_like(acc_ref)

    up = jnp.maximum(_dot(h_ref[...], wu_ref[...]), 0.0)
    acc_ref[...] += _dot((up * up).astype(BF16), wd_ref[...])

    @pl.when(f == pl.num_programs(1) - 1)
    def _():
        x2 = x_ref[...] + acc_ref[...]
        ple = _dot(p_ref[...].astype(BF16), wp_ref[...])
        x3 = x2 + ple * _sigmoid(_dot(x2.astype(BF16), wpg_ref[...]))
        if final_norm:
            x3 = _rms(x3, gf_ref[...])
        out_ref[...] = x3


def mlp_ple(x, g, wu, wd, p, wp, wpg, gf, final_norm):
    n = x.shape[0]
    tm = _pick_tile(n, (512, 256, 128))
    tf = 1024
    return pl.pallas_call(
        functools.partial(_mlp_kernel, final_norm=final_norm),
        grid=(n // tm, D_FF // tf),
        in_specs=[pl.BlockSpec((tm, D_MODEL), lambda i, f: (i, 0)),
                  pl.BlockSpec((1, D_MODEL), lambda i, f: (0, 0)),
                  pl.BlockSpec((D_MODEL, tf), lambda i, f: (0, f)),
                  pl.BlockSpec((tf, D_MODEL), lambda i, f: (f, 0)),
                  pl.BlockSpec((tm, PLE_DIM), lambda i, f: (i, 0)),
                  pl.BlockSpec((PLE_DIM, D_MODEL), lambda i, f: (0, 0)),
                  pl.BlockSpec((D_MODEL, D_MODEL), lambda i, f: (0, 0)),
                  pl.BlockSpec((1, D_MODEL), lambda i, f: (0, 0))],
        out_specs=pl.BlockSpec((tm, D_MODEL), lambda i, f: (i, 0)),
        out_shape=jax.ShapeDtypeStruct((n, D_MODEL), F32),
        scratch_shapes=[pltpu.VMEM((tm, D_MODEL), BF16), pltpu.VMEM((tm, D_MODEL), F32)],
        compiler_params=_cparams(("parallel", "arbitrary")),
        name="mlp_ple",
    )(x, g, wu, wd, p, wp, wpg, gf)


def _compress_math(load_sub, pea_ref, peb_ref, w1a_ref, w1b_ref, w2_ref, nsub):
    acc_a = jnp.zeros((nsub, 4 * CMP_HIDDEN), F32)
    acc_b = jnp.zeros((nsub, 4 * CMP_HIDDEN), F32)
    for j in range(CMP_STRIDE):
        xj = load_sub(j)
        acc_a = acc_a + _dot((xj + pea_ref[j:j + 1, :]).astype(BF16), w1a_ref[j])
        acc_b = acc_b + _dot((xj + peb_ref[j:j + 1, :]).astype(BF16), w1b_ref[j])
    hid = _gelu_tanh(acc_a + pltpu.roll(acc_b, nsub - 1, axis=0))
    return _dot(hid.astype(BF16), w2_ref[...])


def _cmp_kernel(*refs, nsub):
    x_refs = refs[:CMP_STRIDE]
    pea_ref, peb_ref, w1a_ref, w1b_ref, w2_ref, out_ref = refs[CMP_STRIDE:]
    load = lambda j: x_refs[j][...]
    out_ref[0] = _compress_math(load, pea_ref, peb_ref, w1a_ref, w1b_ref, w2_ref, nsub)


def compress_prompt(z, cw, bsz, t):
    nsub = t // CMP_STRIDE
    zsub = z.reshape(z.shape[0] // CMP_STRIDE, CMP_STRIDE * ZW)
    const2 = lambda b: (0, 0)
    const3 = lambda b: (0, 0, 0)
    x_specs = [pl.BlockSpec((nsub, 256), lambda b, j=j: (b, j * (ZW // 256) + ZKV // 256))
               for j in range(CMP_STRIDE)]
    return pl.pallas_call(
        functools.partial(_cmp_kernel, nsub=nsub),
        grid=(bsz,),
        in_specs=x_specs + [
                  pl.BlockSpec((CMP_STRIDE, 256), const2),
                  pl.BlockSpec((CMP_STRIDE, 256), const2),
                  pl.BlockSpec((CMP_STRIDE, 256, 256), const3),
                  pl.BlockSpec((CMP_STRIDE, 256, 256), const3),
                  pl.BlockSpec((256, 256), const2)],
        out_specs=pl.BlockSpec((1, nsub, 256), lambda b: (b, 0, 0)),
        out_shape=jax.ShapeDtypeStruct((bsz, nsub, 256), F32),
        compiler_params=_cparams(("parallel",)),
        name="nsa_compress",
    )(*([zsub] * CMP_STRIDE), cw["pea"], cw["peb"], cw["w1a"], cw["w1b"], cw["w2"])


def _topk_mask(score, nblk, k):
    lane = lax.broadcasted_iota(jnp.int32, score.shape, 1)
    cnt = jnp.zeros(score.shape, F32)
    for j in range(nblk):
        cj = score[:, j:j + 1]
        ge = jnp.where(cj >= score, 1.0, 0.0)
        gt = jnp.where(cj > score, 1.0, 0.0)
        cnt = cnt + jnp.where(lane > j, ge, gt)
    return cnt < float(k)


def _tile_rows(a, reps):
    return jnp.concatenate([a] * reps, axis=0)


def _nsa_queries(zq_ref, g):
    scale = NSA_HD ** -0.5
    q = jnp.concatenate([zq_ref[:, (NSA_GRP * g + h) * LANES:(NSA_GRP * g + h + 1) * LANES]
                         for h in range(NSA_GRP)], axis=0)
    return (q * scale).astype(BF16)


def _nsa_compressed_and_select(qg, kc, vc, m_c, ov, pos1, ns, r):
    s = _dot_nt(qg, kc) + _tile_rows(jnp.where(m_c, 0.0, NEG), NSA_GRP)
    mx = jnp.max(s, axis=-1, keepdims=True)
    p = jnp.exp(s - mx)
    p = p / jnp.sum(p, axis=-1, keepdims=True) * jnp.where(mx > 0.5 * NEG, 1.0, 0.0)
    o_c = _dot(p.astype(BF16), vc)
    psum = (p[0:r] + p[r:2 * r]) + (p[2 * r:3 * r] + p[3 * r:4 * r])
    imp = _dot_mask_left_t(psum, ov)
    blk = lax.broadcasted_iota(jnp.int32, (r, LANES), 1)
    valid = blk * SEL_LEN <= pos1
    cur = pos1 // SEL_LEN
    forced = (blk == 0) | (blk == cur) | (blk == cur - 1)
    score = jnp.where(valid, imp + jnp.where(forced, FORCE_BONUS, 0.0), NEG)
    sel = _topk_mask(score, min(ns, LANES), min(SEL_TOPK, ns)) & valid
    return o_c, sel


def _dot_mask_left_t(x, m01):
    m = m01.astype(BF16)
    x1 = x.astype(BF16)
    r1 = x - x1.astype(F32)
    x2 = r1.astype(BF16)
    x3 = (r1 - x2.astype(F32)).astype(BF16)
    return _dot(x1, m) + (_dot(x2, m) + _dot(x3, m))


def _gate_cols(gates, g, c, r):
    return jnp.concatenate([gates[:, 3 * (NSA_GRP * g + h) + c:3 * (NSA_GRP * g + h) + c + 1]
                            for h in range(NSA_GRP)], axis=0)


def _nsa_assemble(o0, o1, r):
    lane = lax.broadcasted_iota(jnp.int32, (r, LANES), 1)
    return jnp.concatenate([jnp.where(lane < NSA_HD, o0[j * r:(j + 1) * r], o1[j * r:(j + 1) * r])
                            for j in range(NSA_GRP)], axis=1)


SEL_TILE = 512
Q_TILE = 128


def _nsa_prompt_kernel(zq_ref, zg_ref, kcvc_ref, kv_ref, e_ref, ov_ref, out_ref, *, t):
    r = Q_TILE
    qb = pl.program_id(1)
    s0 = qb * r
    pos1 = s0 + lax.broadcasted_iota(jnp.int32, (r, 1), 0)
    nc = t // CMP_STRIDE
    ns = t // SEL_LEN
    kc = kcvc_ref[0][:, 0:LANES].astype(BF16)
    vc = kcvc_ref[0][:, LANES:2 * LANES].astype(BF16)
    ncol = lax.broadcasted_iota(jnp.int32, (1, nc), 1)
    m_c = ((ncol * CMP_STRIDE + (CMP_LEN - 1)) <= pos1) & (ncol < nc - 1)
    gates = _sigmoid(zg_ref[...])
    ov = ov_ref[...]

    qs, ocs, sels = [], [], []
    for g in range(NSA_KV):
        qg = _nsa_queries(zq_ref, g)
        o_c, sel = _nsa_compressed_and_select(qg, kc, vc, m_c, ov, pos1, ns, r)
        qs.append(qg)
        ocs.append(o_c)
        sels.append(jnp.where(sel, 1.0, 0.0).astype(BF16))

    def body(kt, carry):
        k0 = pl.multiple_of(kt * SEL_TILE, SEL_TILE)
        kpos = k0 + lax.broadcasted_iota(jnp.int32, (1, SEL_TILE), 1)
        kk = kv_ref[pl.ds(k0, SEL_TILE), 0:LANES]
        vv = kv_ref[pl.ds(k0, SEL_TILE), LANES:2 * LANES]
        ekt = e_ref[kt]
        new = []
        for g in range(NSA_KV):
            m, l, acc = carry[g]
            allow = (_dot(sels[g], ekt) > 0.5) & (kpos <= pos1)
            s = _dot_nt(qs[g], kk) + _tile_rows(jnp.where(allow, 0.0, NEG), NSA_GRP)
            m_new = jnp.maximum(m, jnp.max(s, axis=-1, keepdims=True))
            alpha = jnp.exp(m - m_new)
            p = jnp.exp(s - m_new)
            l = alpha * l + jnp.sum(p, axis=-1, keepdims=True)
            acc = alpha * acc + _dot(p.astype(BF16), vv)
            new.append((m_new, l, acc))
        return tuple(new)

    init = tuple((jnp.full((NSA_GRP * r, 1), NEG, F32), jnp.zeros((NSA_GRP * r, 1), F32),
                  jnp.zeros((NSA_GRP * r, LANES), F32)) for _ in range(NSA_KV))
    n_tiles = (s0 + r + SEL_TILE - 1) // SEL_TILE
    fin = lax.fori_loop(0, n_tiles, body, init)

    span = WINDOW + r
    start = pl.multiple_of(jnp.maximum(s0 - WINDOW, 0), r)
    kw = kv_ref[pl.ds(start, span), 2 * LANES:3 * LANES]
    vw = kv_ref[pl.ds(start, span), 3 * LANES:4 * LANES]
    kposw = start + lax.broadcasted_iota(jnp.int32, (1, span), 1)
    bias_w = _tile_rows(jnp.where((kposw <= pos1) & (kposw > pos1 - WINDOW), 0.0, NEG), NSA_GRP)

    outs = []
    for g in range(NSA_KV):
        m, l, acc = fin[g]
        o_s = acc / l
        s = _dot_nt(qs[g], kw) + bias_w
        mx = jnp.max(s, axis=-1, keepdims=True)
        p = jnp.exp(s - mx)
        o_w = _dot(p.astype(BF16), vw) / jnp.sum(p, axis=-1, keepdims=True)
        outs.append(_gate_cols(gates, g, 0, r) * ocs[g] + _gate_cols(gates, g, 1, r) * o_s
                    + _gate_cols(gates, g, 2, r) * o_w)
    out_ref[...] = _nsa_assemble(outs[0], outs[1], r)


def nsa_prompt(z, kcvc, kvb, e_mat, ov, bsz, t):
    nqb = t // Q_TILE
    return pl.pallas_call(
        functools.partial(_nsa_prompt_kernel, t=t),
        grid=(bsz, nqb),
        in_specs=[pl.BlockSpec((Q_TILE, NSA_HEADS * LANES), lambda b, q: (b * nqb + q, 0)),
                  pl.BlockSpec((Q_TILE, LANES), lambda b, q: (b * nqb + q, ZNG // LANES)),
                  pl.BlockSpec((1, t // CMP_STRIDE, 256), lambda b, q: (b, 0, 0)),
                  pl.BlockSpec((t, 4 * LANES), lambda b, q: (b, 0)),
                  pl.BlockSpec((t // SEL_TILE, LANES, SEL_TILE), lambda b, q: (0, 0, 0)),
                  pl.BlockSpec((t // CMP_STRIDE, LANES), lambda b, q: (0, 0))],
        out_specs=pl.BlockSpec((Q_TILE, MIX_WIDTH), lambda b, q: (b * nqb + q, 0)),
        out_shape=jax.ShapeDtypeStruct((bsz * t, MIX_WIDTH), F32),
        compiler_params=_cparams(("parallel", "arbitrary")),
        name="nsa_prompt",
    )(z, z, kcvc, kvb, e_mat, ov)


def _nsa_sample_kernel(pt_ref, zq_ref, zkv_ref, cwin_ref, cache_ref, pea_ref, peb_ref, w1a_ref, w1b_ref,
                       w2_ref, e_ref, ov_ref, out_ref, buf_ref, sem_ref, *, past, n_pages, dec):
    r = SEQ_PAD
    b = pl.program_id(0)
    nb = pl.num_programs(0)
    slot = lax.rem(b, 2)

    sub_per_page = PAGE_SIZE // CMP_STRIDE

    def page_copy(seq, sl, p):
        return pltpu.make_async_copy(cache_ref.at[pt_ref[seq * n_pages + p]],
                                     buf_ref.at[sl, pl.ds(p * sub_per_page, sub_per_page), :],
                                     sem_ref.at[sl])

    def start_all(seq, sl):
        for p in range(n_pages):
            page_copy(seq, sl, p).start()

    @pl.when(b == 0)
    def _():
        start_all(b, slot)

    @pl.when(b + 1 < nb)
    def _():
        start_all(b + 1, 1 - slot)

    for p in range(n_pages):
        page_copy(b, slot, p).wait()

    nsub = past // CMP_STRIDE
    nc = nsub - 1
    ns = -(-(past + dec) // SEL_LEN)
    tok = lambda j, c0: buf_ref[slot, :, j * 4 * LANES + c0 * LANES:j * 4 * LANES + (c0 + 1) * LANES]
    load = lambda j: buf_ref[slot, :, j * 4 * LANES:j * 4 * LANES + 2 * LANES]
    kcvc = _compress_math(load, pea_ref, peb_ref, w1a_ref, w1b_ref, w2_ref, nsub)
    kc = kcvc[:, 0:LANES].astype(BF16)
    vc = kcvc[:, LANES:2 * LANES].astype(BF16)

    row = lax.broadcasted_iota(jnp.int32, (r, 1), 0)
    pos1 = past + row
    ncol = lax.broadcasted_iota(jnp.int32, (1, nsub), 1)
    m_c = ((ncol * CMP_STRIDE + (CMP_LEN - 1)) <= pos1) & (ncol < nc)
    gates = _sigmoid(zkv_ref[:, ZNG - ZKV:ZNG - ZKV + LANES])
    ov = ov_ref[...]

    k_sel = jnp.concatenate([tok(j, 2).astype(BF16) for j in range(CMP_STRIDE)], axis=0)
    v_sel = jnp.concatenate([tok(j, 3).astype(BF16) for j in range(CMP_STRIDE)], axis=0)
    kn_sel = zkv_ref[:, 2 * LANES:3 * LANES].astype(BF16)
    vn_sel = zkv_ref[:, 3 * LANES:4 * LANES].astype(BF16)
    k_win = cwin_ref[0][:, 0:LANES].astype(BF16)
    v_win = cwin_ref[0][:, LANES:2 * LANES].astype(BF16)
    kn_win = zkv_ref[:, ZKW - ZKV:ZKW - ZKV + LANES].astype(BF16)
    vn_win = zkv_ref[:, ZKW - ZKV + LANES:ZKW - ZKV + 2 * LANES].astype(BF16)

    kidx = lax.broadcasted_iota(jnp.int32, (1, past), 1)
    kpos_p = (kidx % nsub) * CMP_STRIDE + kidx // nsub
    tnew = lax.broadcasted_iota(jnp.int32, (1, r), 1)
    kpos_n = past + tnew
    wk = cwin_ref.shape[1]
    kpos_w = past - wk + lax.broadcasted_iota(jnp.int32, (1, wk), 1)
    in_win = lambda kp: (kp <= pos1) & (kp > pos1 - WINDOW)
    bias_wp = _tile_rows(jnp.where(in_win(kpos_w), 0.0, NEG), NSA_GRP)
    bias_wn = _tile_rows(jnp.where(in_win(kpos_n) & (tnew < dec), 0.0, NEG), NSA_GRP)
    new_blk = past // SEL_LEN

    def two_part_attention(qg, kp, vp, bias_p, kn, vn, bias_n):
        s_p = _dot_nt(qg, kp) + bias_p
        s_n = _dot_nt(qg, kn) + bias_n
        mx = jnp.maximum(jnp.max(s_p, axis=-1, keepdims=True), jnp.max(s_n, axis=-1, keepdims=True))
        p_p = jnp.exp(s_p - mx)
        p_n = jnp.exp(s_n - mx)
        den = jnp.sum(p_p, axis=-1, keepdims=True) + jnp.sum(p_n, axis=-1, keepdims=True)
        return (_dot(p_p.astype(BF16), vp) + _dot(p_n.astype(BF16), vn)) / den

    outs = []
    for g in range(NSA_KV):
        qg = _nsa_queries(zq_ref, g)
        o_c, sel = _nsa_compressed_and_select(qg, kc, vc, m_c, ov, pos1, ns, r)
        self = jnp.where(sel, 1.0, 0.0)
        allow_p = (_dot(self.astype(BF16), e_ref[...]) > 0.5) & (kpos_p <= pos1)
        allow_n = (self[:, new_blk:new_blk + 1] > 0.5) & (kpos_n <= pos1) & (tnew < dec)
        o_s = two_part_attention(qg, k_sel, v_sel, _tile_rows(jnp.where(allow_p, 0.0, NEG), NSA_GRP),
                                 kn_sel, vn_sel, _tile_rows(jnp.where(allow_n, 0.0, NEG), NSA_GRP))
        o_w = two_part_attention(qg, k_win, v_win, bias_wp, kn_win, vn_win, bias_wn)
        outs.append(_gate_cols(gates, g, 0, r) * o_c + _gate_cols(gates, g, 1, r) * o_s
                    + _gate_cols(gates, g, 2, r) * o_w)
    out_ref[...] = _nsa_assemble(outs[0], outs[1], r)


def nsa_sample(z, page_table, cache, cache_win, cw, e_mat, ov, row0, dbsz, past, dec):
    n_pages = past // PAGE_SIZE
    rb0 = row0 // SEQ_PAD
    wk = cache_win.shape[1]
    c2 = lambda b, pt: (0, 0)
    c3 = lambda b, pt: (0, 0, 0)
    grid_spec = pltpu.PrefetchScalarGridSpec(
        num_scalar_prefetch=1,
        grid=(dbsz,),
        in_specs=[pl.BlockSpec((SEQ_PAD, NSA_HEADS * LANES), lambda b, pt: (rb0 + b, 0)),
                  pl.BlockSpec((SEQ_PAD, 1024), lambda b, pt: (rb0 + b, ZKV // 1024)),
                  pl.BlockSpec((1, wk, 256), lambda b, pt: (b, 0, 0)),
                  pl.BlockSpec(memory_space=pl.ANY),
                  pl.BlockSpec((CMP_STRIDE, 256), c2),
                  pl.BlockSpec((CMP_STRIDE, 256), c2),
                  pl.BlockSpec((CMP_STRIDE, 256, 256), c3),
                  pl.BlockSpec((CMP_STRIDE, 256, 256), c3),
                  pl.BlockSpec((256, 256), c2),
                  pl.BlockSpec((LANES, past), c2),
                  pl.BlockSpec((past // CMP_STRIDE, LANES), c2)],
        out_specs=pl.BlockSpec((SEQ_PAD, MIX_WIDTH), lambda b, pt: (b, 0)),
        scratch_shapes=[pltpu.VMEM((2, past // CMP_STRIDE, CMP_STRIDE * 4 * LANES), F32),
                        pltpu.SemaphoreType.DMA((2,))],
    )
    return pl.pallas_call(
        functools.partial(_nsa_sample_kernel, past=past, n_pages=n_pages, dec=dec),
        grid_spec=grid_spec,
        out_shape=jax.ShapeDtypeStruct((dbsz * SEQ_PAD, MIX_WIDTH), F32),
        compiler_params=_cparams(("arbitrary",)),
        name="nsa_sample",
    )(page_table.reshape(-1), z, z, cache_win, cache, cw["pea"], cw["peb"], cw["w1a"], cw["w1b"], cw["w2"],
      e_mat, ov)


def _seg_masks(r, c):
    ri = lax.broadcasted_iota(jnp.int32, (r, r), 0)
    ci = lax.broadcasted_iota(jnp.int32, (r, r), 1)
    same = (ri // c) == (ci // c)
    return same, same & (ri >= ci), same & (ri > ci)


def _gdn_prep(y, ab, par_ref, valid, c, u_ref, w_ref, qe_ref, qk_ref, kdt_ref, eg_ref):
    r = y.shape[0]
    same, incl, strict = _seg_masks(r, c)
    y = _silu(y)
    beta_all = _sigmoid(ab)
    g_all = -jnp.exp(par_ref[0:1, :]) * _softplus(ab + par_ref[1:2, :])
    if valid is not None:
        y = jnp.where(valid, y, 0.0)
        beta_all = jnp.where(valid, beta_all, 0.0)
        g_all = jnp.where(valid, g_all, 0.0)
    gcum_all = _dot_mask_left(jnp.where(incl, 1.0, 0.0), g_all)
    glast_all = _dot_mask_left(jnp.where(same, 1.0, 0.0), g_all)
    gcum_t = jnp.transpose(gcum_all)
    eye = jnp.where(lax.broadcasted_iota(jnp.int32, (r, r), 0) == lax.broadcasted_iota(jnp.int32, (r, r), 1),
                    1.0, 0.0)
    n_double = int(math.log2(c))
    for h in range(GDN_HEADS):
        q = y[:, h * GDN_HD:(h + 1) * GDN_HD]
        k = y[:, MIX_WIDTH + h * GDN_HD:MIX_WIDTH + (h + 1) * GDN_HD]
        v = y[:, 2 * MIX_WIDTH + h * GDN_HD:2 * MIX_WIDTH + (h + 1) * GDN_HD]
        q = q * lax.rsqrt(jnp.sum(q * q, axis=-1, keepdims=True) + EPS) * (GDN_HD ** -0.5)
        k = k * lax.rsqrt(jnp.sum(k * k, axis=-1, keepdims=True) + EPS)
        g1 = jnp.broadcast_to(gcum_all[:, h:h + 1], (r, GDN_HD))
        g2 = jnp.broadcast_to(gcum_t[h:h + 1, :], (r, r))
        gl = jnp.broadcast_to(glast_all[:, h:h + 1], (r, GDN_HD))
        beta = jnp.broadcast_to(beta_all[:, GDN_HEADS + h:GDN_HEADS + h + 1], (r, GDN_HD))
        g1r = g1 if r == GDN_HD else jnp.broadcast_to(gcum_all[:, h:h + 1], (r, r))
        decay = jnp.where(incl, jnp.exp(jnp.where(incl, g1r - g2, 0.0)), 0.0)
        betar = beta if r == GDN_HD else jnp.broadcast_to(beta_all[:, GDN_HEADS + h:GDN_HEADS + h + 1], (r, r))
        a = jnp.where(strict, _bdot_nt(k, k) * decay * betar, 0.0)
        tm = eye - a
        pw = a
        for _ in range(n_double - 1):
            pw = _dot3(pw, pw)
            tm = tm + _dot3(tm, pw)
        eg1 = jnp.exp(g1)
        rhs = jnp.concatenate([v * beta, k * (beta * eg1)], axis=1)
        sol = _dot3(tm, rhs)
        u_ref[h] = sol[:, 0:GDN_HD]
        w_ref[h] = sol[:, GDN_HD:2 * GDN_HD]
        qe_ref[h] = q * eg1
        qk_ref[h] = jnp.where(incl, _bdot_nt(q, k) * decay, 0.0)
        kdt_ref[h] = jnp.transpose(k * jnp.exp(gl - g1))
        eg_ref[h] = jnp.exp(gl)


def _gdn_segment(h, row0, seg, c, s, u_ref, w_ref, qe_ref, qk_ref, kdt_ref, eg_ref, vn_ref, col_seg):
    rows = pl.ds(row0, c)
    vn = u_ref[h, rows, :] - _bdot(w_ref[h, rows, :], s)
    vn_ref[h, rows, :] = vn
    vn_all = vn_ref[h].astype(BF16)
    o = _bdot(qe_ref[h, rows, :], s) + _dot(qk_ref[h, rows, :].astype(BF16), vn_all)
    kdt = jnp.where(col_seg == seg, kdt_ref[h], 0.0)
    s_new = s * eg_ref[h, pl.ds(row0, 1), :] + _dot(kdt.astype(BF16), vn_all)
    return o, s_new


def _gdn_finish(o, zg, ng_ref):
    return _rms(o, ng_ref[...]) * _silu(zg)


GDN_TILE = 128


def _gdn_prompt_kernel(x_ref, ab_ref, zg_ref, cw_ref, par_ref, ng_ref, out_ref, st_ref,
                       halo_ref, u_ref, w_ref, qe_ref, qk_ref, kdt_ref, eg_ref, vn_ref):
    i = pl.program_id(1)
    r = GDN_TILE
    c = GDN_CHUNK

    @pl.when(i == 0)
    def _():
        halo_ref[...] = jnp.zeros_like(halo_ref)
        st_ref[...] = jnp.zeros_like(st_ref)

    x = x_ref[...]
    xx = jnp.concatenate([halo_ref[...], x], axis=0)
    halo_ref[...] = x[r - SUBLANES:r, :]
    y = None
    for j in range(CONV_W):
        sh = CONV_W - 1 - j
        xs = xx if sh == 0 else pltpu.roll(xx, sh, axis=0)
        t = xs[SUBLANES:, :] * cw_ref[j:j + 1, :]
        y = t if y is None else y + t
    _gdn_prep(y, ab_ref[...], par_ref, None, c, u_ref, w_ref, qe_ref, qk_ref, kdt_ref, eg_ref)
    vn_ref[...] = jnp.zeros_like(vn_ref)
    col_seg = lax.broadcasted_iota(jnp.int32, (GDN_HD, r), 1) // c
    for h in range(GDN_HEADS):
        s = st_ref[0, h]
        os_ = []
        for ck in range(r // c):
            o, s = _gdn_segment(h, ck * c, ck, c, s, u_ref, w_ref, qe_ref, qk_ref, kdt_ref, eg_ref, vn_ref,
                                col_seg)
            os_.append(o)
        st_ref[0, h] = s
        o = jnp.concatenate(os_, axis=0)
        out_ref[:, h * GDN_HD:(h + 1) * GDN_HD] = _gdn_finish(o, zg_ref[:, h * GDN_HD:(h + 1) * GDN_HD], ng_ref)


def _gdn_scratch(r):
    hs = (GDN_HEADS, r, GDN_HD)
    return [pltpu.VMEM(hs, F32), pltpu.VMEM(hs, F32), pltpu.VMEM(hs, F32), pltpu.VMEM((GDN_HEADS, r, r), F32),
            pltpu.VMEM((GDN_HEADS, GDN_HD, r), F32), pltpu.VMEM(hs, F32), pltpu.VMEM(hs, F32)]


def gdn_prompt(z, conv_w, par, norm_g, bsz, t):
    nt = t // GDN_TILE
    r = GDN_TILE
    c2 = lambda b, i: (0, 0)
    return pl.pallas_call(
        _gdn_prompt_kernel,
        grid=(bsz, nt),
        in_specs=[pl.BlockSpec((r, 3 * MIX_WIDTH), lambda b, i: (b * nt + i, ZGQ // (3 * MIX_WIDTH))),
                  pl.BlockSpec((r, LANES), lambda b, i: (b * nt + i, ZGAB // LANES)),
                  pl.BlockSpec((r, MIX_WIDTH), lambda b, i: (b * nt + i, ZGZ // MIX_WIDTH)),
                  pl.BlockSpec((CONV_W, 3 * MIX_WIDTH), c2),
                  pl.BlockSpec((SUBLANES, LANES), c2),
                  pl.BlockSpec((1, GDN_HD), c2)],
        out_specs=[pl.BlockSpec((r, MIX_WIDTH), lambda b, i: (b * nt + i, 0)),
                   pl.BlockSpec((1, GDN_HEADS, GDN_HD, GDN_HD), lambda b, i: (b, 0, 0, 0))],
        out_shape=[jax.ShapeDtypeStruct((bsz * t, MIX_WIDTH), F32),
                   jax.ShapeDtypeStruct((bsz, GDN_HEADS, GDN_HD, GDN_HD), F32)],
        scratch_shapes=[pltpu.VMEM((SUBLANES, 3 * MIX_WIDTH), F32)] + _gdn_scratch(r),
        compiler_params=_cparams(("parallel", "arbitrary")),
        name="gdn_prompt",
    )(z, z, z, conv_w, par, norm_g)


SEQ_PER_STEP = 16


def _gdn_sample_kernel(x_ref, ab_ref, zg_ref, prev_ref, s0_ref, cw_ref, par_ref, ng_ref, out_ref, st_ref,
                       u_ref, w_ref, qe_ref, qk_ref, kdt_ref, eg_ref, vn_ref, o_ref, *, dec):
    r = SEQ_PER_STEP * SEQ_PAD
    c = SEQ_PAD
    row = lax.broadcasted_iota(jnp.int32, (r, 1), 0)
    tin = row % c
    valid = tin < dec
    xx = jnp.where(tin < CONV_W - 1, prev_ref[...], pltpu.roll(x_ref[...], CONV_W - 1, axis=0))
    y = None
    for j in range(CONV_W):
        xs = xx if j == 0 else pltpu.roll(xx, r - j, axis=0)
        t = xs * cw_ref[j:j + 1, :]
        y = t if y is None else y + t
    _gdn_prep(y, ab_ref[...], par_ref, valid, c, u_ref, w_ref, qe_ref, qk_ref, kdt_ref, eg_ref)
    vn_ref[...] = jnp.zeros_like(vn_ref)
    col_seg = lax.broadcasted_iota(jnp.int32, (GDN_HD, r), 1) // c

    def body(sq, carry):
        row0 = pl.multiple_of(sq * c, c)
        for h in range(GDN_HEADS):
            o, s = _gdn_segment(h, row0, sq, c, s0_ref[sq, h], u_ref, w_ref, qe_ref, qk_ref, kdt_ref, eg_ref,
                                vn_ref, col_seg)
            st_ref[sq, h] = s
            o_ref[h, pl.ds(row0, c), :] = o
        return carry

    lax.fori_loop(0, SEQ_PER_STEP, body, 0)
    for h in range(GDN_HEADS):
        out_ref[:, h * GDN_HD:(h + 1) * GDN_HD] = _gdn_finish(o_ref[h], zg_ref[:, h * GDN_HD:(h + 1) * GDN_HD],
                                                              ng_ref)


def gdn_sample(z, prev_pad, s0, conv_w, par, norm_g, row0, dbsz, dec):
    r = SEQ_PER_STEP * SEQ_PAD
    rb0 = row0 // r
    c2 = lambda i: (0, 0)
    return pl.pallas_call(
        functools.partial(_gdn_sample_kernel, dec=dec),
        grid=(dbsz // SEQ_PER_STEP,),
        in_specs=[pl.BlockSpec((r, 3 * MIX_WIDTH), lambda i: (rb0 + i, ZGQ // (3 * MIX_WIDTH))),
                  pl.BlockSpec((r, LANES), lambda i: (rb0 + i, ZGAB // LANES)),
                  pl.BlockSpec((r, MIX_WIDTH), lambda i: (rb0 + i, ZGZ // MIX_WIDTH)),
                  pl.BlockSpec((r, 3 * MIX_WIDTH), lambda i: (i, 0)),
                  pl.BlockSpec((SEQ_PER_STEP, GDN_HEADS, GDN_HD, GDN_HD), lambda i: (i, 0, 0, 0)),
                  pl.BlockSpec((CONV_W, 3 * MIX_WIDTH), c2),
                  pl.BlockSpec((SUBLANES, LANES), c2),
                  pl.BlockSpec((1, GDN_HD), c2)],
        out_specs=[pl.BlockSpec((r, MIX_WIDTH), lambda i: (i, 0)),
                   pl.BlockSpec((SEQ_PER_STEP, GDN_HEADS, GDN_HD, GDN_HD), lambda i: (i, 0, 0, 0))],
        out_shape=[jax.ShapeDtypeStruct((dbsz * SEQ_PAD, MIX_WIDTH), F32),
                   jax.ShapeDtypeStruct((dbsz, GDN_HEADS, GDN_HD, GDN_HD), F32)],
        scratch_shapes=_gdn_scratch(r) + [pltpu.VMEM((GDN_HEADS, r, GDN_HD), F32)],
        compiler_params=_cparams(("parallel",)),
        name="gdn_sample",
    )(z, z, z, prev_pad, s0, conv_w, par, norm_g)


def _ret_log_gamma(h):
    return math.log1p(-(2.0 ** (-5.0 - h)))


def _ret_prep(x, cs, sn, valid, h):
    qh = x[:, h * RET_HD:(h + 1) * RET_HD]
    kh = x[:, MIX_WIDTH + h * RET_HD:MIX_WIDTH + (h + 1) * RET_HD]
    vh = x[:, 2 * MIX_WIDTH + h * RET_HD:2 * MIX_WIDTH + (h + 1) * RET_HD]
    qr = qh * cs + pltpu.roll(qh, RET_HD // 2, axis=1) * sn
    kr = (kh * cs + pltpu.roll(kh, RET_HD // 2, axis=1) * sn) * (RET_HD ** -0.5)
    if valid is not None:
        kr = jnp.where(valid, kr, 0.0)
        vh = jnp.where(valid, vh, 0.0)
    return qr, kr, vh


def _ret_finish(o, gate):
    o = o * lax.rsqrt(jnp.mean(o * o, axis=-1, keepdims=True) + EPS)
    return o * _silu(gate)


def _ret_prompt_kernel(x_ref, gt_ref, cs_ref, sn_ref, out_ref, st_ref):
    i = pl.program_id(1)
    c = RET_CHUNK

    @pl.when(i == 0)
    def _():
        st_ref[...] = jnp.zeros_like(st_ref)

    x = x_ref[...]
    cs = cs_ref[...]
    sn = sn_ref[...]
    ri = lax.broadcasted_iota(jnp.int32, (c, c), 0)
    ci = lax.broadcasted_iota(jnp.int32, (c, c), 1)
    diff = (ri - ci).astype(F32)
    n = lax.broadcasted_iota(jnp.int32, (c, 1), 0).astype(F32)
    for h in range(RET_HEADS):
        lg = _ret_log_gamma(h)
        qr, kr, vh = _ret_prep(x, cs, sn, None, h)
        dmat = jnp.where(diff >= 0.0, jnp.exp(jnp.maximum(diff, 0.0) * lg), 0.0)
        s = st_ref[0, h]
        o = _bdot(_bdot_nt(qr, kr) * dmat, vh) + _bdot(qr, s) * jnp.exp((n + 1.0) * lg)
        kd = kr * jnp.exp((c - 1.0 - n) * lg)
        st_ref[0, h] = s * math.exp(c * lg) + _dot(jnp.transpose(kd).astype(BF16), vh.astype(BF16))
        out_ref[:, h * RET_HD:(h + 1) * RET_HD] = _ret_finish(o, gt_ref[:, h * RET_HD:(h + 1) * RET_HD])


def ret_prompt(z, cs, sn, bsz, t):
    c = RET_CHUNK
    nt = t // c
    return pl.pallas_call(
        _ret_prompt_kernel,
        grid=(bsz, nt),
        in_specs=[pl.BlockSpec((c, 3 * MIX_WIDTH), lambda b, i: (b * nt + i, ZRQ // (3 * MIX_WIDTH))),
                  pl.BlockSpec((c, MIX_WIDTH), lambda b, i: (b * nt + i, ZRG // MIX_WIDTH)),
                  pl.BlockSpec((c, RET_HD), lambda b, i: (i, 0)),
                  pl.BlockSpec((c, RET_HD), lambda b, i: (i, 0))],
        out_specs=[pl.BlockSpec((c, MIX_WIDTH), lambda b, i: (b * nt + i, 0)),
                   pl.BlockSpec((1, RET_HEADS, RET_HD, RET_HD), lambda b, i: (b, 0, 0, 0))],
        out_shape=[jax.ShapeDtypeStruct((bsz * t, MIX_WIDTH), F32),
                   jax.ShapeDtypeStruct((bsz, RET_HEADS, RET_HD, RET_HD), F32)],
        compiler_params=_cparams(("parallel", "arbitrary")),
        name="ret_prompt",
    )(z, z, cs, sn)


def _ret_sample_kernel(x_ref, gt_ref, cs_ref, sn_ref, s0_ref, out_ref, st_ref, q_ref, kdt_ref, v_ref, o_ref,
                       *, dec):
    r = SEQ_PER_STEP * SEQ_PAD
    c = SEQ_PAD
    row = lax.broadcasted_iota(jnp.int32, (r, 1), 0)
    tin = row % c
    valid = tin < dec
    n = tin.astype(F32)
    x = x_ref[...]
    cs = cs_ref[...]
    sn = sn_ref[...]
    same, incl, _ = _seg_masks(r, c)
    ri = lax.broadcasted_iota(jnp.int32, (r, r), 0)
    ci = lax.broadcasted_iota(jnp.int32, (r, r), 1)
    diff = (ri - ci).astype(F32)
    for h in range(RET_HEADS):
        lg = _ret_log_gamma(h)
        qr, kr, vh = _ret_prep(x, cs, sn, valid, h)
        dmat = jnp.where(incl, jnp.exp(jnp.maximum(diff, 0.0) * lg), 0.0)
        o_ref[h] = _bdot(_bdot_nt(qr, kr) * dmat, vh)
        q_ref[h] = qr
        kdt_ref[h] = jnp.transpose(kr * jnp.exp((dec - 1.0 - n) * lg))
        v_ref[h] = vh
    col_seg = lax.broadcasted_iota(jnp.int32, (RET_HD, r), 1) // c
    qdec = [jnp.exp((lax.broadcasted_iota(jnp.int32, (c, 1), 0).astype(F32) + 1.0) * _ret_log_gamma(h))
            for h in range(RET_HEADS)]

    def body(sq, carry):
        row0 = pl.multiple_of(sq * c, c)
        rows = pl.ds(row0, c)
        for h in range(RET_HEADS):
            s = s0_ref[sq, h]
            o_ref[h, rows, :] = o_ref[h, rows, :] + _bdot(q_ref[h, rows, :], s) * qdec[h]
            kdt = jnp.where(col_seg == sq, kdt_ref[h], 0.0)
            st_ref[sq, h] = s * math.exp(dec * _ret_log_gamma(h)) + _dot(kdt.astype(BF16), v_ref[h].astype(BF16))
        return carry

    lax.fori_loop(0, SEQ_PER_STEP, body, 0)
    for h in range(RET_HEADS):
        out_ref[:, h * RET_HD:(h + 1) * RET_HD] = _ret_finish(o_ref[h], gt_ref[:, h * RET_HD:(h + 1) * RET_HD])


def ret_sample(z, cs, sn, s0, row0, dbsz, dec):
    r = SEQ_PER_STEP * SEQ_PAD
    rb0 = row0 // r
    hs = (RET_HEADS, r, RET_HD)
    return pl.pallas_call(
        functools.partial(_ret_sample_kernel, dec=dec),
        grid=(dbsz // SEQ_PER_STEP,),
        in_specs=[pl.BlockSpec((r, 3 * MIX_WIDTH), lambda i: (rb0 + i, ZRQ // (3 * MIX_WIDTH))),
                  pl.BlockSpec((r, MIX_WIDTH), lambda i: (rb0 + i, ZRG // MIX_WIDTH)),
                  pl.BlockSpec((r, RET_HD), lambda i: (0, 0)),
                  pl.BlockSpec((r, RET_HD), lambda i: (0, 0)),
                  pl.BlockSpec((SEQ_PER_STEP, RET_HEADS, RET_HD, RET_HD), lambda i: (i, 0, 0, 0))],
        out_specs=[pl.BlockSpec((r, MIX_WIDTH), lambda i: (i, 0)),
                   pl.BlockSpec((SEQ_PER_STEP, RET_HEADS, RET_HD, RET_HD), lambda i: (i, 0, 0, 0))],
        out_shape=[jax.ShapeDtypeStruct((dbsz * SEQ_PAD, MIX_WIDTH), F32),
                   jax.ShapeDtypeStruct((dbsz, RET_HEADS, RET_HD, RET_HD), F32)],
        scratch_shapes=[pltpu.VMEM(hs, F32), pltpu.VMEM((RET_HEADS, RET_HD, r), F32), pltpu.VMEM(hs, F32),
                        pltpu.VMEM(hs, F32)],
        compiler_params=_cparams(("parallel",)),
        name="ret_sample",
    )(z, z, cs, sn, s0)


def _prep_w_in(w):
    sizes = (NSA_HEADS * NSA_HD, 6 * NSA_KV * NSA_HD, 3 * NSA_HEADS, 3 * MIX_WIDTH, GDN_HEADS, GDN_HEADS,
             MIX_WIDTH, 3 * MIX_WIDTH, MIX_WIDTH, N_BRANCH * D_MODEL)
    cuts = np.cumsum(sizes)[:-1].tolist()
    wq, wkv, wng, wgq, wga, wgb, wgz, wrq, wrg, wmg = jnp.split(w, cuts, axis=1)
    d = w.shape[0]
    zeros = lambda n: jnp.zeros((d, n), w.dtype)
    qcols = []
    for h in range(NSA_HEADS):
        wh = wq[:, h * NSA_HD:(h + 1) * NSA_HD]
        qcols += [wh, zeros(NSA_HD)] if h // NSA_GRP == 0 else [zeros(NSA_HD), wh]
    parts = qcols + [wkv, wng, zeros(LANES - 3 * NSA_HEADS), wga, wgb, zeros(LANES - 2 * GDN_HEADS),
                     wgz, wrg, wgq, wrq, wmg]
    out = jnp.concatenate(parts, axis=1)
    assert out.shape[1] == ZW
    return out.astype(BF16)


def _prep_w_branch(wb):
    w0 = wb[0].reshape(NSA_HEADS, NSA_HD, D_MODEL)
    order = [h for j in range(NSA_GRP) for h in (j, j + NSA_GRP)]
    w0 = w0[np.array(order)].reshape(MIX_WIDTH, D_MODEL)
    return jnp.stack([w0, wb[1], wb[2]]).astype(BF16)


def _prep_compress(pe, w1, w2):
    def blockdiag(ms):
        n = len(ms)
        rows = []
        for i, m in enumerate(ms):
            rows.append(jnp.concatenate([m if i == j else jnp.zeros_like(m) for j in range(n)], axis=1))
        return jnp.concatenate(rows, axis=0)

    w1r = w1.reshape(2, CMP_LEN, NSA_HD, CMP_HIDDEN)
    w1a = jnp.stack([blockdiag([w1r[0, j], w1r[0, j], w1r[1, j], w1r[1, j]]) for j in range(CMP_STRIDE)])
    w1b = jnp.stack([blockdiag([w1r[0, CMP_STRIDE + j], w1r[0, CMP_STRIDE + j], w1r[1, CMP_STRIDE + j],
                                w1r[1, CMP_STRIDE + j]]) for j in range(CMP_STRIDE)])
    pea = jnp.concatenate([pe[0, :CMP_STRIDE], pe[0, :CMP_STRIDE], pe[1, :CMP_STRIDE], pe[1, :CMP_STRIDE]], axis=1)
    peb = jnp.concatenate([pe[0, CMP_STRIDE:], pe[0, CMP_STRIDE:], pe[1, CMP_STRIDE:], pe[1, CMP_STRIDE:]], axis=1)
    w2b = blockdiag([w2[0], w2[0], w2[1], w2[1]])
    return dict(pea=pea, peb=peb, w1a=w1a.astype(BF16), w1b=w1b.astype(BF16), w2=w2b.astype(BF16))


def _overlap_matrix(nc_rows, nc, ns):
    n = np.arange(nc_rows)[:, None]
    s = np.arange(LANES)[None, :]
    c_start = n * CMP_STRIDE
    ov = (c_start < (s + 1) * SEL_LEN) & (s * SEL_LEN < c_start + CMP_LEN) & (n < nc) & (s < ns)
    return jnp.asarray(ov.astype(np.float32))


def _expand_matrix(nkeys):
    s = np.arange(LANES)[:, None]
    k = np.arange(nkeys)[None, :]
    return (k // SEL_LEN == s).astype(np.float32)


def _rope_tables(pos):
    half = RET_HD // 2
    inv = ROPE_BASE ** (-jnp.linspace(0.0, 1.0, half, dtype=F32))
    ang = pos.astype(F32)[:, None] * inv[None, :]
    cos, sin = jnp.cos(ang), jnp.sin(ang)
    return jnp.concatenate([cos, cos], axis=1), jnp.concatenate([-sin, sin], axis=1)


def _pad_rows(a, n):
    return jnp.pad(a, ((0, 0), (0, n - a.shape[1])) + ((0, 0),) * (a.ndim - 2))


def kernel(x_prompt, x_sample, cache_nsa_kv, cache_nsa_win, state_gdn_conv, state_gdn, state_ret, page_table,
           p_prompt, p_sample, g_mix, w_in, nsa_cmp_pe, nsa_cmp_w1, nsa_cmp_w2, gdn_conv_w, gdn_a_log,
           gdn_dt_bias, gdn_norm_g, w_branch, w_out, g_mlp, w_up, w_down, w_ple, w_ple_gate, g_final):
    bsz, t, d = x_prompt.shape
    dbsz, dec, _ = x_sample.shape
    depth = w_in.shape[0]
    past = page_table.shape[1] * PAGE_SIZE
    n_p = bsz * t
    n_s = dbsz * SEQ_PAD
    assert dec <= SEQ_PAD and dec >= CONV_W - 1 and (past % SEL_LEN) + dec <= SEL_LEN
    assert t % SEL_TILE == 0 and t >= WINDOW + Q_TILE and past % PAGE_SIZE == 0 and dbsz % SEQ_PER_STEP == 0

    x = jnp.concatenate([x_prompt.reshape(n_p, d), _pad_rows(x_sample, SEQ_PAD).reshape(n_s, d)], axis=0)
    p_all = jnp.concatenate([p_prompt.reshape(depth, n_p, PLE_DIM),
                             jnp.pad(p_sample, ((0, 0), (0, 0), (0, SEQ_PAD - dec), (0, 0))).reshape(depth, n_s, PLE_DIM)],
                            axis=1)

    nc_p = t // CMP_STRIDE - 1
    ov_p = _overlap_matrix(t // CMP_STRIDE, nc_p, t // SEL_LEN)
    e_p = jnp.asarray(_expand_matrix(t).reshape(LANES, t // SEL_TILE, SEL_TILE).transpose(1, 0, 2)).astype(BF16)
    nc_s = past // CMP_STRIDE - 1
    ov_s = _overlap_matrix(past // CMP_STRIDE, nc_s, -(-(past + dec) // SEL_LEN))
    nsub_s = past // CMP_STRIDE
    key_order = (np.arange(past) % nsub_s) * CMP_STRIDE + np.arange(past) // nsub_s
    e_s = jnp.asarray(_expand_matrix(past)[:, key_order]).astype(BF16)
    cs_p, sn_p = _rope_tables(jnp.arange(t))
    cs_s, sn_s = _rope_tables(past + jnp.arange(SEQ_PAD))
    cs_s = jnp.tile(cs_s, (SEQ_PER_STEP, 1))
    sn_s = jnp.tile(sn_s, (SEQ_PER_STEP, 1))

    outs = [[] for _ in range(10)]
    wk = cache_nsa_win.shape[2]
    for i in range(depth):
        w_in_p = _prep_w_in(w_in[i])
        wb = _prep_w_branch(w_branch[i])
        cw = _prep_compress(nsa_cmp_pe[i], nsa_cmp_w1[i], nsa_cmp_w2[i])
        par = jnp.zeros((SUBLANES, LANES), F32).at[0, :GDN_HEADS].set(gdn_a_log[i]).at[1, :GDN_HEADS].set(gdn_dt_bias[i])
        ng = gdn_norm_g[i].reshape(1, GDN_HD)

        z = in_proj(x, g_mix[i].reshape(1, d), w_in_p)
        zp = z[:n_p]
        zs = z[n_p:].reshape(dbsz, SEQ_PAD, ZW)

        kcvc = compress_prompt(z, cw, bsz, t)
        kvb = zp[:, ZKV + 2 * LANES:ZKW + 2 * LANES].astype(BF16)
        o_nsa_p = nsa_prompt(z, kcvc, kvb, e_p, ov_p, bsz, t)
        cache = cache_nsa_kv[i].reshape(cache_nsa_kv.shape[1], PAGE_SIZE // CMP_STRIDE, CMP_STRIDE * 4 * LANES)
        cwin = cache_nsa_win[i].reshape(dbsz, wk, 2 * LANES)
        o_nsa_s = nsa_sample(z, page_table, cache, cwin, cw, e_s, ov_s, n_p, dbsz, past, dec)

        o_gdn_p, gs_p = gdn_prompt(z, gdn_conv_w[i], par, ng, bsz, t)
        prev_pad = _pad_rows(state_gdn_conv[i], SEQ_PAD).reshape(n_s, 3 * MIX_WIDTH)
        o_gdn_s, gs_s = gdn_sample(z, prev_pad, state_gdn[i], gdn_conv_w[i], par, ng, n_p, dbsz, dec)

        o_ret_p, rs_p = ret_prompt(z, cs_p, sn_p, bsz, t)
        o_ret_s, rs_s = ret_sample(z, cs_s, sn_s, state_ret[i], n_p, dbsz, dec)

        o_nsa = jnp.concatenate([o_nsa_p, o_nsa_s], axis=0)
        o_gdn = jnp.concatenate([o_gdn_p, o_gdn_s], axis=0)
        o_ret = jnp.concatenate([o_ret_p, o_ret_s], axis=0)
        x = merge(x, o_nsa, o_gdn, o_ret, z, wb, w_out[i].astype(BF16))
        x = mlp_ple(x, g_mlp[i].reshape(1, d), w_up[i].astype(BF16), w_down[i].astype(BF16), p_all[i],
                    w_ple[i].astype(BF16), w_ple_gate[i].astype(BF16), g_final.reshape(1, d),
                    final_norm=(i == depth - 1))

        outs[0].append(zp[:, ZKV:ZKW].reshape(bsz, t, 4, NSA_KV, NSA_HD))
        outs[1].append(zs[:, :dec, ZKV:ZKW].reshape(dbsz, dec, 4, NSA_KV, NSA_HD))
        wlen = min(WINDOW, t)
        outs[2].append(zp.reshape(bsz, t, ZW)[:, t - wlen:, ZKW:ZKW + 2 * LANES].reshape(bsz, wlen, 2, NSA_KV, NSA_HD))
        win = jnp.concatenate([cwin, zs[:, :dec, ZKW:ZKW + 2 * LANES]], axis=1)
        wlen_s = min(WINDOW, past + dec)
        outs[3].append(win[:, win.shape[1] - wlen_s:].reshape(dbsz, wlen_s, 2, NSA_KV, NSA_HD))
        outs[4].append(zp.reshape(bsz, t, ZW)[:, t - (CONV_W - 1):, ZGQ:ZGQ + 3 * MIX_WIDTH])
        outs[5].append(zs[:, dec - (CONV_W - 1):dec, ZGQ:ZGQ + 3 * MIX_WIDTH])
        outs[6].append(gs_p)
        outs[7].append(gs_s)
        outs[8].append(rs_p)
        outs[9].append(rs_s)

    y_prompt = x[:n_p].reshape(bsz, t, d)
    y_sample = x[n_p:].reshape(dbsz, SEQ_PAD, d)[:, :dec]
    return (y_prompt, y_sample) + tuple(jnp.stack(o) for o in outs)
```

```python
import functools
import math

import numpy as np
import jax
import jax.numpy as jnp
from jax import lax
from jax.experimental import pallas as pl
from jax.experimental.pallas import tpu as pltpu

F32 = jnp.float32
BF16 = jnp.bfloat16

D_MODEL = 1024
MIX_WIDTH = D_MODEL // 2
N_BRANCH = 3
NSA_HEADS = 8
NSA_HD = 64
NSA_KV = 2
NSA_GRP = NSA_HEADS // NSA_KV
CMP_LEN = 32
CMP_STRIDE = 16
CMP_HIDDEN = 64
SEL_LEN = 64
SEL_TOPK = 16
WINDOW = 512
FORCE_BONUS = 1000.0
GDN_HEADS = 4
GDN_HD = 128
CONV_W = 4
GDN_CHUNK = 64
RET_HEADS = 4
RET_HD = 128
RET_CHUNK = 128
ROPE_BASE = 10000.0
D_FF = 4 * D_MODEL
PLE_DIM = 256
EPS = 1e-6
NEG = -1e30
PAGE_SIZE = 128

LANES = 128
SUBLANES = 8
SEQ_PAD = SUBLANES
VMEM_LIMIT = 56 * 1024 * 1024

ZQ = 0
ZKV = 1024
ZKW = 1536
ZNG = 1792
ZGAB = 1920
ZGZ = 2048
ZRG = 2560
ZGQ = 3072
ZRQ = 4608
ZMG = 6144
ZW = 9216


def _sigmoid(x):
    return 1.0 / (1.0 + jnp.exp(-x))


def _silu(x):
    return x * _sigmoid(x)


def _softplus(x):
    return jnp.maximum(x, 0.0) + jnp.log1p(jnp.exp(-jnp.abs(x)))


def _gelu_tanh(x):
    return 0.5 * x * (1.0 + jnp.tanh(math.sqrt(2.0 / math.pi) * (x + 0.044715 * (x * x * x))))


def _rms(x, g):
    return x * lax.rsqrt(jnp.mean(x * x, axis=-1, keepdims=True) + EPS) * g


def _dot(a, b):
    return jnp.dot(a, b, preferred_element_type=F32)


def _dot_nt(a, b):
    return lax.dot_general(a, b, (((1,), (1,)), ((), ())), preferred_element_type=F32)


def _bdot(a, b):
    return _dot(a.astype(BF16), b.astype(BF16))


def _bdot_nt(a, b):
    return _dot_nt(a.astype(BF16), b.astype(BF16))


def _split2(a):
    hi = a.astype(BF16)
    lo = (a - hi.astype(F32)).astype(BF16)
    return hi, lo


def _dot3(a, b):
    ah, al = _split2(a)
    bh, bl = _split2(b)
    return _dot(ah, bh) + (_dot(ah, bl) + _dot(al, bh))


def _dot_mask_left(m01, x):
    m = m01.astype(BF16)
    x1 = x.astype(BF16)
    r1 = x - x1.astype(F32)
    x2 = r1.astype(BF16)
    x3 = (r1 - x2.astype(F32)).astype(BF16)
    return _dot(m, x1) + (_dot(m, x2) + _dot(m, x3))


def _pick_tile(n, prefs):
    for p in prefs:
        if n % p == 0:
            return p
    raise ValueError(f"no tile in {prefs} divides {n}")


def _cparams(sem):
    return pltpu.CompilerParams(dimension_semantics=sem, vmem_limit_bytes=VMEM_LIMIT)


def _in_kernel(x_ref, g_ref, w_ref, z_ref, h_ref):
    @pl.when(pl.program_id(1) == 0)
    def _():
        h_ref[...] = _rms(x_ref[...], g_ref[...]).astype(BF16)

    z_ref[...] = _dot(h_ref[...], w_ref[...])


def in_proj(x, g, w):
    n = x.shape[0]
    tm = _pick_tile(n, (1024, 512, 256, 128))
    tn = 1024
    return pl.pallas_call(
        _in_kernel,
        grid=(n // tm, ZW // tn),
        in_specs=[pl.BlockSpec((tm, D_MODEL), lambda i, j: (i, 0)),
                  pl.BlockSpec((1, D_MODEL), lambda i, j: (0, 0)),
                  pl.BlockSpec((D_MODEL, tn), lambda i, j: (0, j))],
        out_specs=pl.BlockSpec((tm, tn), lambda i, j: (i, j)),
        out_shape=jax.ShapeDtypeStruct((n, ZW), F32),
        scratch_shapes=[pltpu.VMEM((tm, D_MODEL), BF16)],
        compiler_params=_cparams(("parallel", "arbitrary")),
        name="in_proj",
    )(x, g, w)


def _merge_kernel(x_ref, onp_ref, ogp_ref, orp_ref, ons_ref, ogs_ref, ors_ref, mg_ref, wb_ref, wo_ref, out_ref,
                  *, n_prompt_tiles):
    is_prompt = pl.program_id(0) < n_prompt_tiles
    m = None
    for b, (op_ref, os_ref) in enumerate(((onp_ref, ons_ref), (ogp_ref, ogs_ref), (orp_ref, ors_ref))):
        o = jnp.where(is_prompt, op_ref[...], os_ref[...])
        br = _dot(o.astype(BF16), wb_ref[b])
        t = _sigmoid(mg_ref[:, b * D_MODEL:(b + 1) * D_MODEL]) * br
        m = t if m is None else m + t
    out_ref[...] = x_ref[...] + _dot(m.astype(BF16), wo_ref[...])


def merge(x, o_prompt, o_sample, z, wb, wo):
    n = x.shape[0]
    n_p = o_prompt[0].shape[0]
    n_s = o_sample[0].shape[0]
    tm = _pick_tile(math.gcd(n_p, n_s), (512, 256, 128))
    npt = n_p // tm
    row = lambda i: (i, 0)
    prow = lambda i: (jnp.minimum(i, npt - 1), 0)
    srow = lambda i: (jnp.maximum(i - npt, 0), 0)
    return pl.pallas_call(
        functools.partial(_merge_kernel, n_prompt_tiles=npt),
        grid=(n // tm,),
        in_specs=[pl.BlockSpec((tm, D_MODEL), row)]
                 + [pl.BlockSpec((tm, MIX_WIDTH), prow)] * N_BRANCH
                 + [pl.BlockSpec((tm, MIX_WIDTH), srow)] * N_BRANCH
                 + [
                  pl.BlockSpec((tm, N_BRANCH * D_MODEL), lambda i: (i, ZMG // (N_BRANCH * D_MODEL))),
                  pl.BlockSpec((N_BRANCH, MIX_WIDTH, D_MODEL), lambda i: (0, 0, 0)),
                  pl.BlockSpec((D_MODEL, D_MODEL), lambda i: (0, 0))],
        out_specs=pl.BlockSpec((tm, D_MODEL), row),
        out_shape=jax.ShapeDtypeStruct((n, D_MODEL), F32),
        compiler_params=_cparams(("parallel",)),
        name="merge",
    )(x, *o_prompt, *o_sample, z, wb, wo)


def _mlp_kernel(x_ref, g_ref, wu_ref, wd_ref, p_ref, wp_ref, wpg_ref, gf_ref, out_ref, h_ref, acc_ref,
                *, final_norm):
    f = pl.program_id(1)

    @pl.when(f == 0)
    def _():
        h_ref[...] = _rms(x_ref[...], g_ref[...]).astype(BF16)
        acc_ref[...] = jnp.zeros_like(acc_ref)

    up = jnp.maximum(_dot(h_ref[...], wu_ref[...]), 0.0)
    acc_ref[...] += _dot((up * up).astype(BF16), wd_ref[...])

    @pl.when(f == pl.num_programs(1) - 1)
    def _():
        x2 = x_ref[...] + acc_ref[...]
        ple = _dot(p_ref[...].astype(BF16), wp_ref[...])
        x3 = x2 + ple * _sigmoid(_dot(x2.astype(BF16), wpg_ref[...]))
        if final_norm:
            x3 = _rms(x3, gf_ref[...])
        out_ref[...] = x3


def mlp_ple(x, g, wu, wd, p, wp, wpg, gf, final_norm):
    n = x.shape[0]
    tm = _pick_tile(n, (512, 256, 128))
    tf = 1024
    return pl.pallas_call(
        functools.partial(_mlp_kernel, final_norm=final_norm),
        grid=(n // tm, D_FF // tf),
        in_specs=[pl.BlockSpec((tm, D_MODEL), lambda i, f: (i, 0)),
                  pl.BlockSpec((1, D_MODEL), lambda i, f: (0, 0)),
                  pl.BlockSpec((D_MODEL, tf), lambda i, f: (0, f)),
                  pl.BlockSpec((tf, D_MODEL), lambda i, f: (f, 0)),
                  pl.BlockSpec((tm, PLE_DIM), lambda i, f: (i, 0)),
                  pl.BlockSpec((PLE_DIM, D_MODEL), lambda i, f: (0, 0)),
                  pl.BlockSpec((D_MODEL, D_MODEL), lambda i, f: (0, 0)),
                  pl.BlockSpec((1, D_MODEL), lambda i, f: (0, 0))],
        out_specs=pl.BlockSpec((tm, D_MODEL), lambda i, f: (i, 0)),
        out_shape=jax.ShapeDtypeStruct((n, D_MODEL), F32),
        scratch_shapes=[pltpu.VMEM((tm, D_MODEL), BF16), pltpu.VMEM((tm, D_MODEL), F32)],
        compiler_params=_cparams(("parallel", "arbitrary")),
        name="mlp_ple",
    )(x, g, wu, wd, p, wp, wpg, gf)


def _compress_math(load_sub, pea_ref, peb_ref, w1a_ref, w1b_ref, w2_ref, nsub):
    acc_a = jnp.zeros((nsub, 4 * CMP_HIDDEN), F32)
    acc_b = jnp.zeros((nsub, 4 * CMP_HIDDEN), F32)
    for j in range(CMP_STRIDE):
        xj = load_sub(j)
        acc_a = acc_a + _dot((xj + pea_ref[j:j + 1, :]).astype(BF16), w1a_ref[j])
        acc_b = acc_b + _dot((xj + peb_ref[j:j + 1, :]).astype(BF16), w1b_ref[j])
    hid = _gelu_tanh(acc_a + pltpu.roll(acc_b, nsub - 1, axis=0))
    return _dot(hid.astype(BF16), w2_ref[...])


def _cmp_kernel(x_ref, pea_ref, peb_ref, w1a_ref, w1b_ref, w2_ref, out_ref, *, nsub):
    load = lambda j: x_ref[:, j * 256:(j + 1) * 256]
    out_ref[0] = _compress_math(load, pea_ref, peb_ref, w1a_ref, w1b_ref, w2_ref, nsub)


def compress_prompt(kcmp_sub, cw, bsz, t):
    nsub = t // CMP_STRIDE
    const2 = lambda b: (0, 0)
    const3 = lambda b: (0, 0, 0)
    return pl.pallas_call(
        functools.partial(_cmp_kernel, nsub=nsub),
        grid=(bsz,),
        in_specs=[pl.BlockSpec((nsub, CMP_STRIDE * 256), lambda b: (b, 0)),
                  pl.BlockSpec((CMP_STRIDE, 256), const2),
                  pl.BlockSpec((CMP_STRIDE, 256), const2),
                  pl.BlockSpec((CMP_STRIDE, 256, 256), const3),
                  pl.BlockSpec((CMP_STRIDE, 256, 256), const3),
                  pl.BlockSpec((256, 256), const2)],
        out_specs=pl.BlockSpec((1, nsub, 256), lambda b: (b, 0, 0)),
        out_shape=jax.ShapeDtypeStruct((bsz, nsub, 256), F32),
        compiler_params=_cparams(("parallel",)),
        name="nsa_compress",
    )(kcmp_sub, cw["pea"], cw["peb"], cw["w1a"], cw["w1b"], cw["w2"])


def _topk_mask(score, nblk, k):
    if score.shape[0] == LANES:
        return _topk_mask_square(score, k)
    lane = lax.broadcasted_iota(jnp.int32, score.shape, 1)
    cnt = jnp.zeros(score.shape, F32)
    for j in range(nblk):
        cj = score[:, j:j + 1]
        ge = jnp.where(cj >= score, 1.0, 0.0)
        gt = jnp.where(cj > score, 1.0, 0.0)
        cnt = cnt + jnp.where(lane > j, ge, gt)
    return cnt < float(k)


def _topk_mask_square(score, k):
    sc = jnp.transpose(score)
    idx = lax.broadcasted_iota(jnp.int32, sc.shape, 0).astype(F32)
    picked = jnp.zeros(sc.shape, F32)
    for _ in range(k):
        mx = jnp.max(sc, axis=0, keepdims=True)
        first = jnp.min(jnp.where(sc == mx, idx, float(LANES)), axis=0, keepdims=True)
        hit = idx == first
        picked = jnp.where(hit, 1.0, picked)
        sc = jnp.where(hit, 3.0 * NEG, sc)
    return jnp.transpose(picked) > 0.5


def _tile_rows(a, reps):
    return jnp.concatenate([a] * reps, axis=0)


def _nsa_queries(zq_ref, g):
    scale = NSA_HD ** -0.5
    q = jnp.concatenate([zq_ref[:, (NSA_GRP * g + h) * LANES:(NSA_GRP * g + h + 1) * LANES]
                         for h in range(NSA_GRP)], axis=0)
    return (q * scale).astype(BF16)


def _nsa_compressed_and_select(qg, kc, vc, m_c, ov, pos1, ns, r):
    s = _dot_nt(qg, kc) + _tile_rows(jnp.where(m_c, 0.0, NEG), NSA_GRP)
    mx = jnp.max(s, axis=-1, keepdims=True)
    p = jnp.exp(s - mx)
    p = p / jnp.sum(p, axis=-1, keepdims=True) * jnp.where(mx > 0.5 * NEG, 1.0, 0.0)
    o_c = _dot(p.astype(BF16), vc)
    psum = (p[0:r] + p[r:2 * r]) + (p[2 * r:3 * r] + p[3 * r:4 * r])
    imp = _dot_mask_left_t(psum, ov)
    blk = lax.broadcasted_iota(jnp.int32, (r, LANES), 1)
    valid = blk * SEL_LEN <= pos1
    cur = pos1 // SEL_LEN
    forced = (blk == 0) | (blk == cur) | (blk == cur - 1)
    score = jnp.where(valid, imp + jnp.where(forced, FORCE_BONUS, 0.0), NEG)
    sel = _topk_mask(score, min(ns, LANES), min(SEL_TOPK, ns)) & valid
    return o_c, sel


def _dot_mask_left_t(x, m01):
    m = m01.astype(BF16)
    x1 = x.astype(BF16)
    r1 = x - x1.astype(F32)
    x2 = r1.astype(BF16)
    x3 = (r1 - x2.astype(F32)).astype(BF16)
    return _dot(x1, m) + (_dot(x2, m) + _dot(x3, m))


def _gate_cols(gates, g, c, r):
    return jnp.concatenate([gates[:, 3 * (NSA_GRP * g + h) + c:3 * (NSA_GRP * g + h) + c + 1]
                            for h in range(NSA_GRP)], axis=0)


def _nsa_assemble(o0, o1, r):
    lane = lax.broadcasted_iota(jnp.int32, (r, LANES), 1)
    return jnp.concatenate([jnp.where(lane < NSA_HD, o0[j * r:(j + 1) * r], o1[j * r:(j + 1) * r])
                            for j in range(NSA_GRP)], axis=1)


SEL_TILE = 512
Q_TILE = 128


def _nsa_prompt_kernel(zq_ref, zg_ref, kcvc_ref, kv_ref, e_ref, ov_ref, out_ref, *, t):
    r = Q_TILE
    qb = pl.program_id(1)
    s0 = qb * r
    pos1 = s0 + lax.broadcasted_iota(jnp.int32, (r, 1), 0)
    nc = t // CMP_STRIDE
    ns = t // SEL_LEN
    kc = kcvc_ref[0][:, 0:LANES].astype(BF16)
    vc = kcvc_ref[0][:, LANES:2 * LANES].astype(BF16)
    ncol = lax.broadcasted_iota(jnp.int32, (1, nc), 1)
    m_c = ((ncol * CMP_STRIDE + (CMP_LEN - 1)) <= pos1) & (ncol < nc - 1)
    gates = _sigmoid(zg_ref[...])
    ov = ov_ref[...]

    qs, ocs, sels = [], [], []
    for g in range(NSA_KV):
        qg = _nsa_queries(zq_ref, g)
        o_c, sel = _nsa_compressed_and_select(qg, kc, vc, m_c, ov, pos1, ns, r)
        qs.append(qg)
        ocs.append(o_c)
        sels.append(jnp.where(sel, 1.0, 0.0).astype(BF16))

    def body(kt, carry):
        k0 = pl.multiple_of(kt * SEL_TILE, SEL_TILE)
        kpos = k0 + lax.broadcasted_iota(jnp.int32, (1, SEL_TILE), 1)
        kk = kv_ref[pl.ds(k0, SEL_TILE), 0:LANES]
        vv = kv_ref[pl.ds(k0, SEL_TILE), LANES:2 * LANES]
        ekt = e_ref[kt]
        new = []
        for g in range(NSA_KV):
            m, l, acc = carry[g]
            allow = (_dot(sels[g], ekt) > 0.5) & (kpos <= pos1)
            s = _dot_nt(qs[g], kk) + _tile_rows(jnp.where(allow, 0.0, NEG), NSA_GRP)
            m_new = jnp.maximum(m, jnp.max(s, axis=-1, keepdims=True))
            alpha = jnp.exp(m - m_new)
            p = jnp.exp(s - m_new)
            l = alpha * l + jnp.sum(p, axis=-1, keepdims=True)
            acc = alpha * acc + _dot(p.astype(BF16), vv)
            new.append((m_new, l, acc))
        return tuple(new)

    init = tuple((jnp.full((NSA_GRP * r, 1), NEG, F32), jnp.zeros((NSA_GRP * r, 1), F32),
                  jnp.zeros((NSA_GRP * r, LANES), F32)) for _ in range(NSA_KV))
    n_tiles = (s0 + r + SEL_TILE - 1) // SEL_TILE
    fin = lax.fori_loop(0, n_tiles, body, init)

    span = WINDOW + r
    start = pl.multiple_of(jnp.maximum(s0 - WINDOW, 0), r)
    kw = kv_ref[pl.ds(start, span), 2 * LANES:3 * LANES]
    vw = kv_ref[pl.ds(start, span), 3 * LANES:4 * LANES]
    kposw = start + lax.broadcasted_iota(jnp.int32, (1, span), 1)
    bias_w = _tile_rows(jnp.where((kposw <= pos1) & (kposw > pos1 - WINDOW), 0.0, NEG), NSA_GRP)

    outs = []
    for g in range(NSA_KV):
        m, l, acc = fin[g]
        o_s = acc / l
        s = _dot_nt(qs[g], kw) + bias_w
        mx = jnp.max(s, axis=-1, keepdims=True)
        p = jnp.exp(s - mx)
        o_w = _dot(p.astype(BF16), vw) / jnp.sum(p, axis=-1, keepdims=True)
        outs.append(_gate_cols(gates, g, 0, r) * ocs[g] + _gate_cols(gates, g, 1, r) * o_s
                    + _gate_cols(gates, g, 2, r) * o_w)
    out_ref[...] = _nsa_assemble(outs[0], outs[1], r)


def nsa_prompt(z, kcvc, kvb, e_mat, ov, bsz, t):
    nqb = t // Q_TILE
    return pl.pallas_call(
        functools.partial(_nsa_prompt_kernel, t=t),
        grid=(bsz, nqb),
        in_specs=[pl.BlockSpec((Q_TILE, NSA_HEADS * LANES), lambda b, q: (b * nqb + q, 0)),
                  pl.BlockSpec((Q_TILE, LANES), lambda b, q: (b * nqb + q, ZNG // LANES)),
                  pl.BlockSpec((1, t // CMP_STRIDE, 256), lambda b, q: (b, 0, 0)),
                  pl.BlockSpec((t, 4 * LANES), lambda b, q: (b, 0)),
                  pl.BlockSpec((t // SEL_TILE, LANES, SEL_TILE), lambda b, q: (0, 0, 0)),
                  pl.BlockSpec((t // CMP_STRIDE, LANES), lambda b, q: (0, 0))],
        out_specs=pl.BlockSpec((Q_TILE, MIX_WIDTH), lambda b, q: (b * nqb + q, 0)),
        out_shape=jax.ShapeDtypeStruct((bsz * t, MIX_WIDTH), F32),
        compiler_params=_cparams(("parallel", "arbitrary")),
        name="nsa_prompt",
    )(z, z, kcvc, kvb, e_mat, ov)


def _nsa_sample_kernel(pt_ref, zq_ref, zkv_ref, cwin_ref, cache_ref, pea_ref, peb_ref, w1a_ref, w1b_ref,
                       w2_ref, e_ref, ov_ref, out_ref, buf_ref, tokm_ref, sem_ref, *, layer, past, n_pages, dec):
    r = SEQ_PAD
    b = pl.program_id(0)
    nb = pl.num_programs(0)
    slot = lax.rem(b, 2)

    def page_copy(seq, sl, p):
        return pltpu.make_async_copy(cache_ref.at[layer, pt_ref[seq * n_pages + p]],
                                     buf_ref.at[sl, :, pl.ds(p * PAGE_SIZE, PAGE_SIZE)],
                                     sem_ref.at[sl])

    def start_all(seq, sl):
        for p in range(n_pages):
            page_copy(seq, sl, p).start()

    @pl.when(b == 0)
    def _():
        start_all(b, slot)

    @pl.when(b + 1 < nb)
    def _():
        start_all(b + 1, 1 - slot)

    for p in range(n_pages):
        page_copy(b, slot, p).wait()

    nsub = past // CMP_STRIDE
    nc = nsub - 1
    ns = -(-(past + dec) // SEL_LEN)
    for c0 in range(2):
        for p in range(past // LANES):
            tokm_ref[c0, p * LANES:(p + 1) * LANES, :] = jnp.transpose(
                buf_ref[slot, c0 * LANES:(c0 + 1) * LANES, p * LANES:(p + 1) * LANES])
    load = lambda j: jnp.concatenate([tokm_ref[0, pl.ds(j, nsub, stride=CMP_STRIDE), :],
                                      tokm_ref[1, pl.ds(j, nsub, stride=CMP_STRIDE), :]], axis=1)
    kcvc = _compress_math(load, pea_ref, peb_ref, w1a_ref, w1b_ref, w2_ref, nsub)
    kc = kcvc[:, 0:LANES].astype(BF16)
    vc = kcvc[:, LANES:2 * LANES].astype(BF16)

    row = lax.broadcasted_iota(jnp.int32, (r, 1), 0)
    pos1 = past + row
    ncol = lax.broadcasted_iota(jnp.int32, (1, nsub), 1)
    m_c = ((ncol * CMP_STRIDE + (CMP_LEN - 1)) <= pos1) & (ncol < nc)
    gates = _sigmoid(zkv_ref[:, ZNG - ZKV:ZNG - ZKV + LANES])
    ov = ov_ref[...]

    k_sel = buf_ref[slot, 2 * LANES:3 * LANES, :].astype(BF16)
    v_sel = buf_ref[slot, 3 * LANES:4 * LANES, :].astype(BF16)
    kn_sel = zkv_ref[:, 2 * LANES:3 * LANES].astype(BF16)
    vn_sel = zkv_ref[:, 3 * LANES:4 * LANES].astype(BF16)
    k_win = cwin_ref[0, 0, 0:LANES, :].astype(BF16)
    v_win = cwin_ref[0, 0, LANES:2 * LANES, :].astype(BF16)
    kn_win = zkv_ref[:, ZKW - ZKV:ZKW - ZKV + LANES].astype(BF16)
    vn_win = zkv_ref[:, ZKW - ZKV + LANES:ZKW - ZKV + 2 * LANES].astype(BF16)

    kpos_p = lax.broadcasted_iota(jnp.int32, (1, past), 1)
    tnew = lax.broadcasted_iota(jnp.int32, (1, r), 1)
    kpos_n = past + tnew
    wk = cwin_ref.shape[3]
    kpos_w = past - wk + lax.broadcasted_iota(jnp.int32, (1, wk), 1)
    in_win = lambda kp: (kp <= pos1) & (kp > pos1 - WINDOW)
    bias_wp = _tile_rows(jnp.where(in_win(kpos_w), 0.0, NEG), NSA_GRP)
    bias_wn = _tile_rows(jnp.where(in_win(kpos_n) & (tnew < dec), 0.0, NEG), NSA_GRP)
    new_blk = past // SEL_LEN

    def two_part_attention(qg, kp_t, vp_t, bias_p, kn, vn, bias_n):
        s_p = _dot(qg, kp_t) + bias_p
        s_n = _dot_nt(qg, kn) + bias_n
        mx = jnp.maximum(jnp.max(s_p, axis=-1, keepdims=True), jnp.max(s_n, axis=-1, keepdims=True))
        p_p = jnp.exp(s_p - mx)
        p_n = jnp.exp(s_n - mx)
        den = jnp.sum(p_p, axis=-1, keepdims=True) + jnp.sum(p_n, axis=-1, keepdims=True)
        return (_dot_nt(p_p.astype(BF16), vp_t) + _dot(p_n.astype(BF16), vn)) / den

    outs = []
    for g in range(NSA_KV):
        qg = _nsa_queries(zq_ref, g)
        o_c, sel = _nsa_compressed_and_select(qg, kc, vc, m_c, ov, pos1, ns, r)
        self = jnp.where(sel, 1.0, 0.0)
        allow_p = (_dot(self.astype(BF16), e_ref[...]) > 0.5) & (kpos_p <= pos1)
        allow_n = (self[:, new_blk:new_blk + 1] > 0.5) & (kpos_n <= pos1) & (tnew < dec)
        o_s = two_part_attention(qg, k_sel, v_sel, _tile_rows(jnp.where(allow_p, 0.0, NEG), NSA_GRP),
                                 kn_sel, vn_sel, _tile_rows(jnp.where(allow_n, 0.0, NEG), NSA_GRP))
        o_w = two_part_attention(qg, k_win, v_win, bias_wp, kn_win, vn_win, bias_wn)
        outs.append(_gate_cols(gates, g, 0, r) * o_c + _gate_cols(gates, g, 1, r) * o_s
                    + _gate_cols(gates, g, 2, r) * o_w)
    out_ref[...] = _nsa_assemble(outs[0], outs[1], r)


def nsa_sample(z, page_table, cache_t, cache_win_t, cw, e_mat, ov, layer, row0, dbsz, past, dec):
    n_pages = past // PAGE_SIZE
    rb0 = row0 // SEQ_PAD
    wk = cache_win_t.shape[3]
    c2 = lambda b, pt: (0, 0)
    c3 = lambda b, pt: (0, 0, 0)
    grid_spec = pltpu.PrefetchScalarGridSpec(
        num_scalar_prefetch=1,
        grid=(dbsz,),
        in_specs=[pl.BlockSpec((SEQ_PAD, NSA_HEADS * LANES), lambda b, pt: (rb0 + b, 0)),
                  pl.BlockSpec((SEQ_PAD, 1024), lambda b, pt: (rb0 + b, ZKV // 1024)),
                  pl.BlockSpec((1, 1, 2 * LANES, wk), lambda b, pt: (layer, b, 0, 0)),
                  pl.BlockSpec(memory_space=pl.ANY),
                  pl.BlockSpec((CMP_STRIDE, 256), c2),
                  pl.BlockSpec((CMP_STRIDE, 256), c2),
                  pl.BlockSpec((CMP_STRIDE, 256, 256), c3),
                  pl.BlockSpec((CMP_STRIDE, 256, 256), c3),
                  pl.BlockSpec((256, 256), c2),
                  pl.BlockSpec((LANES, past), c2),
                  pl.BlockSpec((past // CMP_STRIDE, LANES), c2)],
        out_specs=pl.BlockSpec((SEQ_PAD, MIX_WIDTH), lambda b, pt: (b, 0)),
        scratch_shapes=[pltpu.VMEM((2, 4 * LANES, past), F32), pltpu.VMEM((2, past, LANES), F32),
                        pltpu.SemaphoreType.DMA((2,))],
    )
    return pl.pallas_call(
        functools.partial(_nsa_sample_kernel, layer=layer, past=past, n_pages=n_pages, dec=dec),
        grid_spec=grid_spec,
        out_shape=jax.ShapeDtypeStruct((dbsz * SEQ_PAD, MIX_WIDTH), F32),
        compiler_params=_cparams(("arbitrary",)),
        name="nsa_sample",
    )(page_table.reshape(-1), z, z, cache_win_t, cache_t, cw["pea"], cw["peb"], cw["w1a"], cw["w1b"], cw["w2"],
      e_mat, ov)


def _seg_masks(r, c):
    ri = lax.broadcasted_iota(jnp.int32, (r, r), 0)
    ci = lax.broadcasted_iota(jnp.int32, (r, r), 1)
    same = (ri // c) == (ci // c)
    return same, same & (ri >= ci), same & (ri > ci)


def _gdn_prep(y, ab, par_ref, valid, c, u_ref, w_ref, qe_ref, qk_ref, kdt_ref, eg_ref):
    r = y.shape[0]
    same, incl, strict = _seg_masks(r, c)
    y = _silu(y)
    beta_all = _sigmoid(ab)
    g_all = -jnp.exp(par_ref[0:1, :]) * _softplus(ab + par_ref[1:2, :])
    if valid is not None:
        y = jnp.where(valid, y, 0.0)
        beta_all = jnp.where(valid, beta_all, 0.0)
        g_all = jnp.where(valid, g_all, 0.0)
    gcum_all = _dot_mask_left(jnp.where(incl, 1.0, 0.0), g_all)
    glast_all = _dot_mask_left(jnp.where(same, 1.0, 0.0), g_all)
    gcum_t = jnp.transpose(gcum_all)
    eye = jnp.where(lax.broadcasted_iota(jnp.int32, (r, r), 0) == lax.broadcasted_iota(jnp.int32, (r, r), 1),
                    1.0, 0.0)
    n_double = int(math.log2(c))
    for h in range(GDN_HEADS):
        q = y[:, h * GDN_HD:(h + 1) * GDN_HD]
        k = y[:, MIX_WIDTH + h * GDN_HD:MIX_WIDTH + (h + 1) * GDN_HD]
        v = y[:, 2 * MIX_WIDTH + h * GDN_HD:2 * MIX_WIDTH + (h + 1) * GDN_HD]
        q = q * lax.rsqrt(jnp.sum(q * q, axis=-1, keepdims=True) + EPS) * (GDN_HD ** -0.5)
        k = k * lax.rsqrt(jnp.sum(k * k, axis=-1, keepdims=True) + EPS)
        g1 = jnp.broadcast_to(gcum_all[:, h:h + 1], (r, GDN_HD))
        g2 = jnp.broadcast_to(gcum_t[h:h + 1, :], (r, r))
        gl = jnp.broadcast_to(glast_all[:, h:h + 1], (r, GDN_HD))
        beta = jnp.broadcast_to(beta_all[:, GDN_HEADS + h:GDN_HEADS + h + 1], (r, GDN_HD))
        g1r = g1 if r == GDN_HD else jnp.broadcast_to(gcum_all[:, h:h + 1], (r, r))
        decay = jnp.where(incl, jnp.exp(jnp.where(incl, g1r - g2, 0.0)), 0.0)
        betar = beta if r == GDN_HD else jnp.broadcast_to(beta_all[:, GDN_HEADS + h:GDN_HEADS + h + 1], (r, r))
        a = jnp.where(strict, _bdot_nt(k, k) * decay * betar, 0.0)
        tm = eye - a
        pw = a
        for _ in range(n_double - 1):
            pw = _dot3(pw, pw)
            tm = tm + _dot3(tm, pw)
        eg1 = jnp.exp(g1)
        rhs = jnp.concatenate([v * beta, k * (beta * eg1)], axis=1)
        sol = _dot3(tm, rhs)
        u_ref[h] = sol[:, 0:GDN_HD]
        w_ref[h] = sol[:, GDN_HD:2 * GDN_HD]
        qe_ref[h] = q * eg1
        qk_ref[h] = jnp.where(incl, _bdot_nt(q, k) * decay, 0.0)
        kdt_ref[h] = jnp.transpose(k * jnp.exp(gl - g1))
        eg_ref[h] = jnp.exp(gl)


def _gdn_segment(h, row0, seg, c, s, u_ref, w_ref, qe_ref, qk_ref, kdt_ref, eg_ref, vn_ref, col_seg):
    rows = pl.ds(row0, c)
    vn = u_ref[h, rows, :] - _bdot(w_ref[h, rows, :], s)
    vn_ref[h, rows, :] = vn
    vn_all = vn_ref[h].astype(BF16)
    o = _bdot(qe_ref[h, rows, :], s) + _dot(qk_ref[h, rows, :].astype(BF16), vn_all)
    kdt = jnp.where(col_seg == seg, kdt_ref[h], 0.0)
    s_new = s * eg_ref[h, pl.ds(row0, 1), :] + _dot(kdt.astype(BF16), vn_all)
    return o, s_new


def _gdn_finish(o, zg, ng_ref):
    return _rms(o, ng_ref[...]) * _silu(zg)


GDN_TILE = 128


def _gdn_prompt_kernel(x_ref, ab_ref, zg_ref, cw_ref, par_ref, ng_ref, out_ref, st_ref,
                       halo_ref, u_ref, w_ref, qe_ref, qk_ref, kdt_ref, eg_ref, vn_ref):
    i = pl.program_id(1)
    r = GDN_TILE
    c = GDN_CHUNK

    @pl.when(i == 0)
    def _():
        halo_ref[...] = jnp.zeros_like(halo_ref)
        st_ref[...] = jnp.zeros_like(st_ref)

    x = x_ref[...]
    xx = jnp.concatenate([halo_ref[...], x], axis=0)
    halo_ref[...] = x[r - SUBLANES:r, :]
    y = None
    for j in range(CONV_W):
        sh = CONV_W - 1 - j
        xs = xx if sh == 0 else pltpu.roll(xx, sh, axis=0)
        t = xs[SUBLANES:, :] * cw_ref[j:j + 1, :]
        y = t if y is None else y + t
    _gdn_prep(y, ab_ref[...], par_ref, None, c, u_ref, w_ref, qe_ref, qk_ref, kdt_ref, eg_ref)
    vn_ref[...] = jnp.zeros_like(vn_ref)
    col_seg = lax.broadcasted_iota(jnp.int32, (GDN_HD, r), 1) // c
    for h in range(GDN_HEADS):
        s = st_ref[0, h]
        os_ = []
        for ck in range(r // c):
            o, s = _gdn_segment(h, ck * c, ck, c, s, u_ref, w_ref, qe_ref, qk_ref, kdt_ref, eg_ref, vn_ref,
                                col_seg)
            os_.append(o)
        st_ref[0, h] = s
        o = jnp.concatenate(os_, axis=0)
        out_ref[:, h * GDN_HD:(h + 1) * GDN_HD] = _gdn_finish(o, zg_ref[:, h * GDN_HD:(h + 1) * GDN_HD], ng_ref)


def _gdn_scratch(r):
    hs = (GDN_HEADS, r, GDN_HD)
    return [pltpu.VMEM(hs, F32), pltpu.VMEM(hs, F32), pltpu.VMEM(hs, F32), pltpu.VMEM((GDN_HEADS, r, r), F32),
            pltpu.VMEM((GDN_HEADS, GDN_HD, r), F32), pltpu.VMEM(hs, F32), pltpu.VMEM(hs, F32)]


def gdn_prompt(z, conv_w, par, norm_g, bsz, t):
    nt = t // GDN_TILE
    r = GDN_TILE
    c2 = lambda b, i: (0, 0)
    return pl.pallas_call(
        _gdn_prompt_kernel,
        grid=(bsz, nt),
        in_specs=[pl.BlockSpec((r, 3 * MIX_WIDTH), lambda b, i: (b * nt + i, ZGQ // (3 * MIX_WIDTH))),
                  pl.BlockSpec((r, LANES), lambda b, i: (b * nt + i, ZGAB // LANES)),
                  pl.BlockSpec((r, MIX_WIDTH), lambda b, i: (b * nt + i, ZGZ // MIX_WIDTH)),
                  pl.BlockSpec((CONV_W, 3 * MIX_WIDTH), c2),
                  pl.BlockSpec((SUBLANES, LANES), c2),
                  pl.BlockSpec((1, GDN_HD), c2)],
        out_specs=[pl.BlockSpec((r, MIX_WIDTH), lambda b, i: (b * nt + i, 0)),
                   pl.BlockSpec((1, GDN_HEADS, GDN_HD, GDN_HD), lambda b, i: (b, 0, 0, 0))],
        out_shape=[jax.ShapeDtypeStruct((bsz * t, MIX_WIDTH), F32),
                   jax.ShapeDtypeStruct((bsz, GDN_HEADS, GDN_HD, GDN_HD), F32)],
        scratch_shapes=[pltpu.VMEM((SUBLANES, 3 * MIX_WIDTH), F32)] + _gdn_scratch(r),
        compiler_params=_cparams(("parallel", "arbitrary")),
        name="gdn_prompt",
    )(z, z, z, conv_w, par, norm_g)


SEQ_PER_STEP = 16


def _gdn_sample_kernel(x_ref, ab_ref, zg_ref, prev_ref, s0_ref, cw_ref, par_ref, ng_ref, out_ref, st_ref,
                       u_ref, w_ref, qe_ref, qk_ref, kdt_ref, eg_ref, vn_ref, o_ref, *, dec):
    r = SEQ_PER_STEP * SEQ_PAD
    c = SEQ_PAD
    row = lax.broadcasted_iota(jnp.int32, (r, 1), 0)
    tin = row % c
    valid = tin < dec
    xx = jnp.where(tin < CONV_W - 1, prev_ref[...], pltpu.roll(x_ref[...], CONV_W - 1, axis=0))
    y = None
    for j in range(CONV_W):
        xs = xx if j == 0 else pltpu.roll(xx, r - j, axis=0)
        t = xs * cw_ref[j:j + 1, :]
        y = t if y is None else y + t
    _gdn_prep(y, ab_ref[...], par_ref, valid, c, u_ref, w_ref, qe_ref, qk_ref, kdt_ref, eg_ref)
    vn_ref[...] = jnp.zeros_like(vn_ref)
    col_seg = lax.broadcasted_iota(jnp.int32, (GDN_HD, r), 1) // c

    def body(sq, carry):
        row0 = pl.multiple_of(sq * c, c)
        for h in range(GDN_HEADS):
            o, s = _gdn_segment(h, row0, sq, c, s0_ref[sq, h], u_ref, w_ref, qe_ref, qk_ref, kdt_ref, eg_ref,
                                vn_ref, col_seg)
            st_ref[sq, h] = s
            o_ref[h, pl.ds(row0, c), :] = o
        return carry

    lax.fori_loop(0, SEQ_PER_STEP, body, 0)
    for h in range(GDN_HEADS):
        out_ref[:, h * GDN_HD:(h + 1) * GDN_HD] = _gdn_finish(o_ref[h], zg_ref[:, h * GDN_HD:(h + 1) * GDN_HD],
                                                              ng_ref)


def gdn_sample(z, prev_pad, s0, conv_w, par, norm_g, row0, dbsz, dec):
    r = SEQ_PER_STEP * SEQ_PAD
    rb0 = row0 // r
    c2 = lambda i: (0, 0)
    return pl.pallas_call(
        functools.partial(_gdn_sample_kernel, dec=dec),
        grid=(dbsz // SEQ_PER_STEP,),
        in_specs=[pl.BlockSpec((r, 3 * MIX_WIDTH), lambda i: (rb0 + i, ZGQ // (3 * MIX_WIDTH))),
                  pl.BlockSpec((r, LANES), lambda i: (rb0 + i, ZGAB // LANES)),
                  pl.BlockSpec((r, MIX_WIDTH), lambda i: (rb0 + i, ZGZ // MIX_WIDTH)),
                  pl.BlockSpec((r, 3 * MIX_WIDTH), lambda i: (i, 0)),
                  pl.BlockSpec((SEQ_PER_STEP, GDN_HEADS, GDN_HD, GDN_HD), lambda i: (i, 0, 0, 0)),
                  pl.BlockSpec((CONV_W, 3 * MIX_WIDTH), c2),
                  pl.BlockSpec((SUBLANES, LANES), c2),
                  pl.BlockSpec((1, GDN_HD), c2)],
        out_specs=[pl.BlockSpec((r, MIX_WIDTH), lambda i: (i, 0)),
                   pl.BlockSpec((SEQ_PER_STEP, GDN_HEADS, GDN_HD, GDN_HD), lambda i: (i, 0, 0, 0))],
        out_shape=[jax.ShapeDtypeStruct((dbsz * SEQ_PAD, MIX_WIDTH), F32),
                   jax.ShapeDtypeStruct((dbsz, GDN_HEADS, GDN_HD, GDN_HD), F32)],
        scratch_shapes=_gdn_scratch(r) + [pltpu.VMEM((GDN_HEADS, r, GDN_HD), F32)],
        compiler_params=_cparams(("parallel",)),
        name="gdn_sample",
    )(z, z, z, prev_pad, s0, conv_w, par, norm_g)


def _ret_log_gamma(h):
    return math.log1p(-(2.0 ** (-5.0 - h)))


def _ret_prep(x, cs, sn, valid, h):
    qh = x[:, h * RET_HD:(h + 1) * RET_HD]
    kh = x[:, MIX_WIDTH + h * RET_HD:MIX_WIDTH + (h + 1) * RET_HD]
    vh = x[:, 2 * MIX_WIDTH + h * RET_HD:2 * MIX_WIDTH + (h + 1) * RET_HD]
    qr = qh * cs + pltpu.roll(qh, RET_HD // 2, axis=1) * sn
    kr = (kh * cs + pltpu.roll(kh, RET_HD // 2, axis=1) * sn) * (RET_HD ** -0.5)
    if valid is not None:
        kr = jnp.where(valid, kr, 0.0)
        vh = jnp.where(valid, vh, 0.0)
    return qr, kr, vh


def _ret_finish(o, gate):
    o = o * lax.rsqrt(jnp.mean(o * o, axis=-1, keepdims=True) + EPS)
    return o * _silu(gate)


def _ret_prompt_kernel(x_ref, gt_ref, cs_ref, sn_ref, out_ref, st_ref):
    i = pl.program_id(1)
    c = RET_CHUNK

    @pl.when(i == 0)
    def _():
        st_ref[...] = jnp.zeros_like(st_ref)

    x = x_ref[...]
    cs = cs_ref[...]
    sn = sn_ref[...]
    ri = lax.broadcasted_iota(jnp.int32, (c, c), 0)
    ci = lax.broadcasted_iota(jnp.int32, (c, c), 1)
    diff = (ri - ci).astype(F32)
    n = lax.broadcasted_iota(jnp.int32, (c, 1), 0).astype(F32)
    for h in range(RET_HEADS):
        lg = _ret_log_gamma(h)
        qr, kr, vh = _ret_prep(x, cs, sn, None, h)
        dmat = jnp.where(diff >= 0.0, jnp.exp(jnp.maximum(diff, 0.0) * lg), 0.0)
        s = st_ref[0, h]
        o = _bdot(_bdot_nt(qr, kr) * dmat, vh) + _bdot(qr, s) * jnp.exp((n + 1.0) * lg)
        kd = kr * jnp.exp((c - 1.0 - n) * lg)
        st_ref[0, h] = s * math.exp(c * lg) + _dot(jnp.transpose(kd).astype(BF16), vh.astype(BF16))
        out_ref[:, h * RET_HD:(h + 1) * RET_HD] = _ret_finish(o, gt_ref[:, h * RET_HD:(h + 1) * RET_HD])


def ret_prompt(z, cs, sn, bsz, t):
    c = RET_CHUNK
    nt = t // c
    return pl.pallas_call(
        _ret_prompt_kernel,
        grid=(bsz, nt),
        in_specs=[pl.BlockSpec((c, 3 * MIX_WIDTH), lambda b, i: (b * nt + i, ZRQ // (3 * MIX_WIDTH))),
                  pl.BlockSpec((c, MIX_WIDTH), lambda b, i: (b * nt + i, ZRG // MIX_WIDTH)),
                  pl.BlockSpec((c, RET_HD), lambda b, i: (i, 0)),
                  pl.BlockSpec((c, RET_HD), lambda b, i: (i, 0))],
        out_specs=[pl.BlockSpec((c, MIX_WIDTH), lambda b, i: (b * nt + i, 0)),
                   pl.BlockSpec((1, RET_HEADS, RET_HD, RET_HD), lambda b, i: (b, 0, 0, 0))],
        out_shape=[jax.ShapeDtypeStruct((bsz * t, MIX_WIDTH), F32),
                   jax.ShapeDtypeStruct((bsz, RET_HEADS, RET_HD, RET_HD), F32)],
        compiler_params=_cparams(("parallel", "arbitrary")),
        name="ret_prompt",
    )(z, z, cs, sn)


def _ret_sample_kernel(x_ref, gt_ref, cs_ref, sn_ref, s0_ref, out_ref, st_ref, q_ref, kdt_ref, v_ref, o_ref,
                       *, dec):
    r = SEQ_PER_STEP * SEQ_PAD
    c = SEQ_PAD
    row = lax.broadcasted_iota(jnp.int32, (r, 1), 0)
    tin = row % c
    valid = tin < dec
    n = tin.astype(F32)
    x = x_ref[...]
    cs = cs_ref[...]
    sn = sn_ref[...]
    same, incl, _ = _seg_masks(r, c)
    ri = lax.broadcasted_iota(jnp.int32, (r, r), 0)
    ci = lax.broadcasted_iota(jnp.int32, (r, r), 1)
    diff = (ri - ci).astype(F32)
    for h in range(RET_HEADS):
        lg = _ret_log_gamma(h)
        qr, kr, vh = _ret_prep(x, cs, sn, valid, h)
        dmat = jnp.where(incl, jnp.exp(jnp.maximum(diff, 0.0) * lg), 0.0)
        o_ref[h] = _bdot(_bdot_nt(qr, kr) * dmat, vh)
        q_ref[h] = qr
        kdt_ref[h] = jnp.transpose(kr * jnp.exp((dec - 1.0 - n) * lg))
        v_ref[h] = vh
    col_seg = lax.broadcasted_iota(jnp.int32, (RET_HD, r), 1) // c
    qdec = [jnp.exp((lax.broadcasted_iota(jnp.int32, (c, 1), 0).astype(F32) + 1.0) * _ret_log_gamma(h))
            for h in range(RET_HEADS)]

    def body(sq, carry):
        row0 = pl.multiple_of(sq * c, c)
        rows = pl.ds(row0, c)
        for h in range(RET_HEADS):
            s = s0_ref[sq, h]
            o_ref[h, rows, :] = o_ref[h, rows, :] + _bdot(q_ref[h, rows, :], s) * qdec[h]
            kdt = jnp.where(col_seg == sq, kdt_ref[h], 0.0)
            st_ref[sq, h] = s * math.exp(dec * _ret_log_gamma(h)) + _dot(kdt.astype(BF16), v_ref[h].astype(BF16))
        return carry

    lax.fori_loop(0, SEQ_PER_STEP, body, 0)
    for h in range(RET_HEADS):
        out_ref[:, h * RET_HD:(h + 1) * RET_HD] = _ret_finish(o_ref[h], gt_ref[:, h * RET_HD:(h + 1) * RET_HD])


def ret_sample(z, cs, sn, s0, row0, dbsz, dec):
    r = SEQ_PER_STEP * SEQ_PAD
    rb0 = row0 // r
    hs = (RET_HEADS, r, RET_HD)
    return pl.pallas_call(
        functools.partial(_ret_sample_kernel, dec=dec),
        grid=(dbsz // SEQ_PER_STEP,),
        in_specs=[pl.BlockSpec((r, 3 * MIX_WIDTH), lambda i: (rb0 + i, ZRQ // (3 * MIX_WIDTH))),
                  pl.BlockSpec((r, MIX_WIDTH), lambda i: (rb0 + i, ZRG // MIX_WIDTH)),
                  pl.BlockSpec((r, RET_HD), lambda i: (0, 0)),
                  pl.BlockSpec((r, RET_HD), lambda i: (0, 0)),
                  pl.BlockSpec((SEQ_PER_STEP, RET_HEADS, RET_HD, RET_HD), lambda i: (i, 0, 0, 0))],
        out_specs=[pl.BlockSpec((r, MIX_WIDTH), lambda i: (i, 0)),
                   pl.BlockSpec((SEQ_PER_STEP, RET_HEADS, RET_HD, RET_HD), lambda i: (i, 0, 0, 0))],
        out_shape=[jax.ShapeDtypeStruct((dbsz * SEQ_PAD, MIX_WIDTH), F32),
                   jax.ShapeDtypeStruct((dbsz, RET_HEADS, RET_HD, RET_HD), F32)],
        scratch_shapes=[pltpu.VMEM(hs, F32), pltpu.VMEM((RET_HEADS, RET_HD, r), F32), pltpu.VMEM(hs, F32),
                        pltpu.VMEM(hs, F32)],
        compiler_params=_cparams(("parallel",)),
        name="ret_sample",
    )(z, z, cs, sn, s0)


def _prep_w_in(w):
    sizes = (NSA_HEADS * NSA_HD, 6 * NSA_KV * NSA_HD, 3 * NSA_HEADS, 3 * MIX_WIDTH, GDN_HEADS, GDN_HEADS,
             MIX_WIDTH, 3 * MIX_WIDTH, MIX_WIDTH, N_BRANCH * D_MODEL)
    src = np.concatenate([[0], np.cumsum(sizes)])
    s_q, s_kv, s_ng, s_gq, s_ga, s_gb, s_gz, s_rq, s_rg, s_mg = src[:-1].tolist()
    wb = w.astype(BF16)
    out = jnp.zeros((w.shape[0], ZW), BF16)

    def put(o, dst, s0, n):
        return lax.dynamic_update_slice(o, lax.slice_in_dim(wb, s0, s0 + n, axis=1), (0, dst))

    for h in range(NSA_HEADS):
        out = put(out, ZQ + h * LANES + NSA_HD * (h // NSA_GRP), s_q + h * NSA_HD, NSA_HD)
    out = put(out, ZKV, s_kv, sizes[1] + sizes[2])
    out = put(out, ZGAB, s_ga, 2 * GDN_HEADS)
    out = put(out, ZGZ, s_gz, MIX_WIDTH)
    out = put(out, ZRG, s_rg, MIX_WIDTH)
    out = put(out, ZGQ, s_gq, 3 * MIX_WIDTH)
    out = put(out, ZRQ, s_rq, 3 * MIX_WIDTH)
    out = put(out, ZMG, s_mg, N_BRANCH * D_MODEL)
    return out


def _prep_w_branch(wb):
    w0 = wb[0].reshape(NSA_HEADS, NSA_HD, D_MODEL)
    order = [h for j in range(NSA_GRP) for h in (j, j + NSA_GRP)]
    w0 = w0[np.array(order)].reshape(MIX_WIDTH, D_MODEL)
    return jnp.stack([w0, wb[1], wb[2]]).astype(BF16)


def _prep_compress(pe, w1, w2):
    sel_k = jnp.asarray(np.diag([1.0, 1.0, 0.0, 0.0]).astype(np.float32))
    sel_v = jnp.asarray(np.diag([0.0, 0.0, 1.0, 1.0]).astype(np.float32))

    def blockdiag(mk, mv):
        full = (sel_k[:, None, :, None] * mk[..., None, :, None, :]
                + sel_v[:, None, :, None] * mv[..., None, :, None, :])
        return full.reshape(mk.shape[:-2] + (4 * mk.shape[-2], 4 * mk.shape[-1]))

    w1r = w1.reshape(2, CMP_LEN, NSA_HD, CMP_HIDDEN)
    w1a = blockdiag(w1r[0, :CMP_STRIDE], w1r[1, :CMP_STRIDE])
    w1b = blockdiag(w1r[0, CMP_STRIDE:], w1r[1, CMP_STRIDE:])
    pea = jnp.concatenate([pe[0, :CMP_STRIDE], pe[0, :CMP_STRIDE], pe[1, :CMP_STRIDE], pe[1, :CMP_STRIDE]], axis=1)
    peb = jnp.concatenate([pe[0, CMP_STRIDE:], pe[0, CMP_STRIDE:], pe[1, CMP_STRIDE:], pe[1, CMP_STRIDE:]], axis=1)
    w2b = blockdiag(w2[0], w2[1])
    return dict(pea=pea, peb=peb, w1a=w1a.astype(BF16), w1b=w1b.astype(BF16), w2=w2b.astype(BF16))


def _overlap_matrix(nc_rows, nc, ns):
    n = np.arange(nc_rows)[:, None]
    s = np.arange(LANES)[None, :]
    c_start = n * CMP_STRIDE
    ov = (c_start < (s + 1) * SEL_LEN) & (s * SEL_LEN < c_start + CMP_LEN) & (n < nc) & (s < ns)
    return jnp.asarray(ov.astype(np.float32))


def _expand_matrix(nkeys):
    s = np.arange(LANES)[:, None]
    k = np.arange(nkeys)[None, :]
    return (k // SEL_LEN == s).astype(np.float32)


def _rope_tables(pos):
    half = RET_HD // 2
    inv = ROPE_BASE ** (-jnp.linspace(0.0, 1.0, half, dtype=F32))
    ang = pos.astype(F32)[:, None] * inv[None, :]
    cos, sin = jnp.cos(ang), jnp.sin(ang)
    return jnp.concatenate([cos, cos], axis=1), jnp.concatenate([-sin, sin], axis=1)


def _pad_rows(a, n):
    return jnp.pad(a, ((0, 0), (0, n - a.shape[1])) + ((0, 0),) * (a.ndim - 2))


def kernel(x_prompt, x_sample, cache_nsa_kv, cache_nsa_win, state_gdn_conv, state_gdn, state_ret, page_table,
           p_prompt, p_sample, g_mix, w_in, nsa_cmp_pe, nsa_cmp_w1, nsa_cmp_w2, gdn_conv_w, gdn_a_log,
           gdn_dt_bias, gdn_norm_g, w_branch, w_out, g_mlp, w_up, w_down, w_ple, w_ple_gate, g_final):
    bsz, t, d = x_prompt.shape
    dbsz, dec, _ = x_sample.shape
    depth = w_in.shape[0]
    past = page_table.shape[1] * PAGE_SIZE
    n_p = bsz * t
    n_s = dbsz * SEQ_PAD
    assert dec <= SEQ_PAD and dec >= CONV_W - 1 and (past % SEL_LEN) + dec <= SEL_LEN
    assert t % SEL_TILE == 0 and t >= WINDOW + Q_TILE and past % PAGE_SIZE == 0 and dbsz % SEQ_PER_STEP == 0

    x = jnp.concatenate([x_prompt.reshape(n_p, d), _pad_rows(x_sample, SEQ_PAD).reshape(n_s, d)], axis=0)
    p_all = jnp.concatenate([p_prompt.reshape(depth, n_p, PLE_DIM),
                             jnp.pad(p_sample, ((0, 0), (0, 0), (0, SEQ_PAD - dec), (0, 0))).reshape(depth, n_s, PLE_DIM)],
                            axis=1)

    nc_p = t // CMP_STRIDE - 1
    ov_p = _overlap_matrix(t // CMP_STRIDE, nc_p, t // SEL_LEN)
    e_p = jnp.asarray(_expand_matrix(t).reshape(LANES, t // SEL_TILE, SEL_TILE).transpose(1, 0, 2)).astype(BF16)
    nc_s = past // CMP_STRIDE - 1
    ov_s = _overlap_matrix(past // CMP_STRIDE, nc_s, -(-(past + dec) // SEL_LEN))
    e_s = jnp.asarray(_expand_matrix(past)).astype(BF16)
    n_phys = cache_nsa_kv.shape[1]
    wk = cache_nsa_win.shape[2]
    cache_t = jnp.transpose(cache_nsa_kv, (0, 1, 3, 4, 5, 2)).reshape(depth, n_phys, 4 * LANES, PAGE_SIZE)
    cwin_t = jnp.transpose(cache_nsa_win, (0, 1, 3, 4, 5, 2)).reshape(depth, dbsz, 2 * LANES, wk)
    cs_p, sn_p = _rope_tables(jnp.arange(t))
    cs_s, sn_s = _rope_tables(past + jnp.arange(SEQ_PAD))
    cs_s = jnp.tile(cs_s, (SEQ_PER_STEP, 1))
    sn_s = jnp.tile(sn_s, (SEQ_PER_STEP, 1))

    outs = [[] for _ in range(10)]
    for i in range(depth):
        w_in_p = _prep_w_in(w_in[i])
        wb = _prep_w_branch(w_branch[i])
        cw = _prep_compress(nsa_cmp_pe[i], nsa_cmp_w1[i], nsa_cmp_w2[i])
        par = jnp.zeros((SUBLANES, LANES), F32).at[0, :GDN_HEADS].set(gdn_a_log[i]).at[1, :GDN_HEADS].set(gdn_dt_bias[i])
        ng = gdn_norm_g[i].reshape(1, GDN_HD)

        z = in_proj(x, g_mix[i].reshape(1, d), w_in_p)
        kv_all = lax.slice(z, (0, ZKV), (n_p + n_s, ZNG))
        kv_p = kv_all[:n_p]
        kv_s = kv_all[n_p:].reshape(dbsz, SEQ_PAD, ZNG - ZKV)[:, :dec]

        kcmp_sub = kv_p[:, :2 * LANES].reshape(n_p // CMP_STRIDE, CMP_STRIDE * 2 * LANES)
        kcvc = compress_prompt(kcmp_sub, cw, bsz, t)
        kvb = kv_p[:, 2 * LANES:].astype(BF16)
        o_nsa_p = nsa_prompt(z, kcvc, kvb, e_p, ov_p, bsz, t)
        o_nsa_s = nsa_sample(z, page_table, cache_t, cwin_t, cw, e_s, ov_s, i, n_p, dbsz, past, dec)

        o_gdn_p, gs_p = gdn_prompt(z, gdn_conv_w[i], par, ng, bsz, t)
        prev_pad = _pad_rows(state_gdn_conv[i], SEQ_PAD).reshape(n_s, 3 * MIX_WIDTH)
        o_gdn_s, gs_s = gdn_sample(z, prev_pad, state_gdn[i], gdn_conv_w[i], par, ng, n_p, dbsz, dec)

        o_ret_p, rs_p = ret_prompt(z, cs_p, sn_p, bsz, t)
        o_ret_s, rs_s = ret_sample(z, cs_s, sn_s, state_ret[i], n_p, dbsz, dec)

        x = merge(x, (o_nsa_p, o_gdn_p, o_ret_p), (o_nsa_s, o_gdn_s, o_ret_s), z, wb, w_out[i].astype(BF16))
        x = mlp_ple(x, g_mlp[i].reshape(1, d), w_up[i].astype(BF16), w_down[i].astype(BF16), p_all[i],
                    w_ple[i].astype(BF16), w_ple_gate[i].astype(BF16), g_final.reshape(1, d),
                    final_norm=(i == depth - 1))

        nkv = ZKW - ZKV
        outs[0].append(kv_p[:, :nkv].reshape(bsz, t, 4, NSA_KV, NSA_HD))
        outs[1].append(kv_s[:, :, :nkv].reshape(dbsz, dec, 4, NSA_KV, NSA_HD))
        wlen = min(WINDOW, t)
        outs[2].append(kv_p.reshape(bsz, t, ZNG - ZKV)[:, t - wlen:, nkv:].reshape(bsz, wlen, 2, NSA_KV, NSA_HD))
        win = jnp.concatenate([cache_nsa_win[i], kv_s[:, :, nkv:].reshape(dbsz, dec, 2, NSA_KV, NSA_HD)], axis=1)
        wlen_s = min(WINDOW, past + dec)
        outs[3].append(win[:, win.shape[1] - wlen_s:])
        nconv = CONV_W - 1
        outs[4].append(jnp.stack([lax.slice(z, (b * t + t - nconv, ZGQ), (b * t + t, ZGQ + 3 * MIX_WIDTH))
                                  for b in range(bsz)]))
        gq_s = lax.slice(z, (n_p, ZGQ), (n_p + n_s, ZGQ + 3 * MIX_WIDTH)).reshape(dbsz, SEQ_PAD, 3 * MIX_WIDTH)
        outs[5].append(gq_s[:, dec - nconv:dec])
        outs[6].append(gs_p)
        outs[7].append(gs_s)
        outs[8].append(rs_p)
        outs[9].append(rs_s)

    y_prompt = x[:n_p].reshape(bsz, t, d)
    y_sample = x[n_p:].reshape(dbsz, SEQ_PAD, d)[:, :dec]
    return (y_prompt, y_sample) + tuple(jnp.stack(o) for o in outs)
```

```python
import functools
import math

import numpy as np
import jax
import jax.numpy as jnp
from jax import lax
from jax.experimental import pallas as pl
from jax.experimental.pallas import tpu as pltpu

F32 = jnp.float32
BF16 = jnp.bfloat16

D_MODEL = 1024
MIX_WIDTH = D_MODEL // 2
N_BRANCH = 3
NSA_HEADS = 8
NSA_HD = 64
NSA_KV = 2
NSA_GRP = NSA_HEADS // NSA_KV
CMP_LEN = 32
CMP_STRIDE = 16
CMP_HIDDEN = 64
SEL_LEN = 64
SEL_TOPK = 16
WINDOW = 512
FORCE_BONUS = 1000.0
GDN_HEADS = 4
GDN_HD = 128
CONV_W = 4
GDN_CHUNK = 64
RET_HEADS = 4
RET_HD = 128
RET_CHUNK = 128
ROPE_BASE = 10000.0
D_FF = 4 * D_MODEL
PLE_DIM = 256
EPS = 1e-6
NEG = -1e30
PAGE_SIZE = 128

LANES = 128
SUBLANES = 8
SEQ_PAD = SUBLANES
VMEM_LIMIT = 56 * 1024 * 1024

ZQ = 0
ZKV = 1024
ZKW = 1536
ZNG = 1792
ZGAB = 1920
ZGZ = 2048
ZRG = 2560
ZGQ = 3072
ZRQ = 4608
ZMG = 6144
ZW = 9216


def _sigmoid(x):
    return 1.0 / (1.0 + jnp.exp(-x))


def _silu(x):
    return x * _sigmoid(x)


def _softplus(x):
    return jnp.maximum(x, 0.0) + jnp.log1p(jnp.exp(-jnp.abs(x)))


def _gelu_tanh(x):
    return 0.5 * x * (1.0 + jnp.tanh(math.sqrt(2.0 / math.pi) * (x + 0.044715 * (x * x * x))))


def _rms(x, g):
    return x * lax.rsqrt(jnp.mean(x * x, axis=-1, keepdims=True) + EPS) * g


def _dot(a, b):
    return jnp.dot(a, b, preferred_element_type=F32)


def _dot_nt(a, b):
    return lax.dot_general(a, b, (((1,), (1,)), ((), ())), preferred_element_type=F32)


def _bdot(a, b):
    return _dot(a.astype(BF16), b.astype(BF16))


def _bdot_nt(a, b):
    return _dot_nt(a.astype(BF16), b.astype(BF16))


def _split2(a):
    hi = a.astype(BF16)
    lo = (a - hi.astype(F32)).astype(BF16)
    return hi, lo


def _dot3(a, b):
    ah, al = _split2(a)
    bh, bl = _split2(b)
    return _dot(ah, bh) + (_dot(ah, bl) + _dot(al, bh))


def _dot_mask_left(m01, x):
    m = m01.astype(BF16)
    x1 = x.astype(BF16)
    r1 = x - x1.astype(F32)
    x2 = r1.astype(BF16)
    x3 = (r1 - x2.astype(F32)).astype(BF16)
    return _dot(m, x1) + (_dot(m, x2) + _dot(m, x3))


def _pick_tile(n, prefs):
    for p in prefs:
        if n % p == 0:
            return p
    raise ValueError(f"no tile in {prefs} divides {n}")


def _cparams(sem):
    return pltpu.CompilerParams(dimension_semantics=sem, vmem_limit_bytes=VMEM_LIMIT)


def _in_kernel(x_ref, g_ref, w_ref, z_ref, h_ref):
    @pl.when(pl.program_id(1) == 0)
    def _():
        h_ref[...] = _rms(x_ref[...], g_ref[...]).astype(BF16)

    z_ref[...] = _dot(h_ref[...], w_ref[...])


def in_proj(x, g, w):
    n = x.shape[0]
    tm = _pick_tile(n, (1024, 512, 256, 128))
    tn = 1024
    return pl.pallas_call(
        _in_kernel,
        grid=(n // tm, ZW // tn),
        in_specs=[pl.BlockSpec((tm, D_MODEL), lambda i, j: (i, 0)),
                  pl.BlockSpec((1, D_MODEL), lambda i, j: (0, 0)),
                  pl.BlockSpec((D_MODEL, tn), lambda i, j: (0, j))],
        out_specs=pl.BlockSpec((tm, tn), lambda i, j: (i, j)),
        out_shape=jax.ShapeDtypeStruct((n, ZW), F32),
        scratch_shapes=[pltpu.VMEM((tm, D_MODEL), BF16)],
        compiler_params=_cparams(("parallel", "arbitrary")),
        name="in_proj",
    )(x, g, w)


def _merge_kernel(x_ref, onp_ref, ogp_ref, orp_ref, ons_ref, ogs_ref, ors_ref, mg_ref, wb_ref, wo_ref, out_ref,
                  *, n_prompt_tiles):
    is_prompt = pl.program_id(0) < n_prompt_tiles
    m = None
    for b, (op_ref, os_ref) in enumerate(((onp_ref, ons_ref), (ogp_ref, ogs_ref), (orp_ref, ors_ref))):
        o = jnp.where(is_prompt, op_ref[...], os_ref[...])
        br = _dot(o.astype(BF16), wb_ref[b])
        t = _sigmoid(mg_ref[:, b * D_MODEL:(b + 1) * D_MODEL]) * br
        m = t if m is None else m + t
    out_ref[...] = x_ref[...] + _dot(m.astype(BF16), wo_ref[...])


def merge(x, o_prompt, o_sample, z, wb, wo):
    n = x.shape[0]
    n_p = o_prompt[0].shape[0]
    n_s = o_sample[0].shape[0]
    tm = _pick_tile(math.gcd(n_p, n_s), (512, 256, 128))
    npt = n_p // tm
    row = lambda i: (i, 0)
    prow = lambda i: (jnp.minimum(i, npt - 1), 0)
    srow = lambda i: (jnp.maximum(i - npt, 0), 0)
    return pl.pallas_call(
        functools.partial(_merge_kernel, n_prompt_tiles=npt),
        grid=(n // tm,),
        in_specs=[pl.BlockSpec((tm, D_MODEL), row)]
                 + [pl.BlockSpec((tm, MIX_WIDTH), prow)] * N_BRANCH
                 + [pl.BlockSpec((tm, MIX_WIDTH), srow)] * N_BRANCH
                 + [
                  pl.BlockSpec((tm, N_BRANCH * D_MODEL), lambda i: (i, ZMG // (N_BRANCH * D_MODEL))),
                  pl.BlockSpec((N_BRANCH, MIX_WIDTH, D_MODEL), lambda i: (0, 0, 0)),
                  pl.BlockSpec((D_MODEL, D_MODEL), lambda i: (0, 0))],
        out_specs=pl.BlockSpec((tm, D_MODEL), row),
        out_shape=jax.ShapeDtypeStruct((n, D_MODEL), F32),
        compiler_params=_cparams(("parallel",)),
        name="merge",
    )(x, *o_prompt, *o_sample, z, wb, wo)


def _mlp_kernel(x_ref, g_ref, wu_ref, wd_ref, p_ref, wp_ref, wpg_ref, gf_ref, out_ref, h_ref, acc_ref,
                *, final_norm):
    f = pl.program_id(1)

    @pl.when(f == 0)
    def _():
        h_ref[...] = _rms(x_ref[...], g_ref[...]).astype(BF16)
        acc_ref[...] = jnp.zeros_like(acc_ref)

    up = jnp.maximum(_dot(h_ref[...], wu_ref[...]), 0.0)
    acc_ref[...] += _dot((up * up).astype(BF16), wd_ref[...])

    @pl.when(f == pl.num_programs(1) - 1)
    def _():
        x2 = x_ref[...] + acc_ref[...]
        ple = _dot(p_ref[...].astype(BF16), wp_ref[...])
        x3 = x2 + ple * _sigmoid(_dot(x2.astype(BF16), wpg_ref[...]))
        if final_norm:
            x3 = _rms(x3, gf_ref[...])
        out_ref[...] = x3


def mlp_ple(x, g, wu, wd, p, wp, wpg, gf, final_norm):
    n = x.shape[0]
    tm = _pick_tile(n, (512, 256, 128))
    tf = 1024
    return pl.pallas_call(
        functools.partial(_mlp_kernel, final_norm=final_norm),
        grid=(n // tm, D_FF // tf),
        in_specs=[pl.BlockSpec((tm, D_MODEL), lambda i, f: (i, 0)),
                  pl.BlockSpec((1, D_MODEL), lambda i, f: (0, 0)),
                  pl.BlockSpec((D_MODEL, tf), lambda i, f: (0, f)),
                  pl.BlockSpec((tf, D_MODEL), lambda i, f: (f, 0)),
                  pl.BlockSpec((tm, PLE_DIM), lambda i, f: (i, 0)),
                  pl.BlockSpec((PLE_DIM, D_MODEL), lambda i, f: (0, 0)),
                  pl.BlockSpec((D_MODEL, D_MODEL), lambda i, f: (0, 0)),
                  pl.BlockSpec((1, D_MODEL), lambda i, f: (0, 0))],
        out_specs=pl.BlockSpec((tm, D_MODEL), lambda i, f: (i, 0)),
        out_shape=jax.ShapeDtypeStruct((n, D_MODEL), F32),
        scratch_shapes=[pltpu.VMEM((tm, D_MODEL), BF16), pltpu.VMEM((tm, D_MODEL), F32)],
        compiler_params=_cparams(("parallel", "arbitrary")),
        name="mlp_ple",
    )(x, g, wu, wd, p, wp, wpg, gf)


def _compress_math(load_sub, pea_ref, peb_ref, w1a_ref, w1b_ref, w2_ref, nsub):
    acc_a = jnp.zeros((nsub, 4 * CMP_HIDDEN), F32)
    acc_b = jnp.zeros((nsub, 4 * CMP_HIDDEN), F32)
    for j in range(CMP_STRIDE):
        xj = load_sub(j)
        acc_a = acc_a + _dot((xj + pea_ref[j:j + 1, :]).astype(BF16), w1a_ref[j])
        acc_b = acc_b + _dot((xj + peb_ref[j:j + 1, :]).astype(BF16), w1b_ref[j])
    hid = _gelu_tanh(acc_a + pltpu.roll(acc_b, nsub - 1, axis=0))
    return _dot(hid.astype(BF16), w2_ref[...])


def _cmp_kernel(x_ref, pea_ref, peb_ref, w1a_ref, w1b_ref, w2_ref, out_ref, *, nsub):
    load = lambda j: x_ref[:, j * 256:(j + 1) * 256]
    out_ref[0] = _compress_math(load, pea_ref, peb_ref, w1a_ref, w1b_ref, w2_ref, nsub)


def compress_prompt(kcmp_sub, cw, bsz, t):
    nsub = t // CMP_STRIDE
    const2 = lambda b: (0, 0)
    const3 = lambda b: (0, 0, 0)
    return pl.pallas_call(
        functools.partial(_cmp_kernel, nsub=nsub),
        grid=(bsz,),
        in_specs=[pl.BlockSpec((nsub, CMP_STRIDE * 256), lambda b: (b, 0)),
                  pl.BlockSpec((CMP_STRIDE, 256), const2),
                  pl.BlockSpec((CMP_STRIDE, 256), const2),
                  pl.BlockSpec((CMP_STRIDE, 256, 256), const3),
                  pl.BlockSpec((CMP_STRIDE, 256, 256), const3),
                  pl.BlockSpec((256, 256), const2)],
        out_specs=pl.BlockSpec((1, nsub, 256), lambda b: (b, 0, 0)),
        out_shape=jax.ShapeDtypeStruct((bsz, nsub, 256), F32),
        compiler_params=_cparams(("parallel",)),
        name="nsa_compress",
    )(kcmp_sub, cw["pea"], cw["peb"], cw["w1a"], cw["w1b"], cw["w2"])


def _topk_mask(scores, nblk, k):
    shape = scores[0].shape
    if shape[0] == LANES:
        return _topk_mask_square(scores, k)
    lane = lax.broadcasted_iota(jnp.int32, shape, 1)
    cnts = [jnp.zeros(shape, F32) for _ in scores]
    for j in range(nblk):
        for i, score in enumerate(scores):
            cj = score[:, j:j + 1]
            ge = jnp.where(cj >= score, 1.0, 0.0)
            gt = jnp.where(cj > score, 1.0, 0.0)
            cnts[i] = cnts[i] + jnp.where(lane > j, ge, gt)
    return [cnt < float(k) for cnt in cnts]


def _topk_cols(scs, k):
    shape = scs[0].shape
    idx = lax.broadcasted_iota(jnp.int32, shape, 0).astype(F32)
    picked = [jnp.zeros(shape, F32) for _ in scs]
    for _ in range(k):
        mxs = [jnp.max(sc, axis=0, keepdims=True) for sc in scs]
        firsts = [jnp.min(jnp.where(sc == mx, idx, float(shape[0])), axis=0, keepdims=True)
                  for sc, mx in zip(scs, mxs)]
        hits = [idx == first for first in firsts]
        picked = [jnp.where(hit, 1.0, pk) for hit, pk in zip(hits, picked)]
        scs = [jnp.where(hit, 3.0 * NEG, sc) for hit, sc in zip(hits, scs)]
    return picked


def _topk_mask_square(scores, k):
    picked = _topk_cols([jnp.transpose(s) for s in scores], k)
    return [jnp.transpose(p) > 0.5 for p in picked]


def _tile_rows(a, reps):
    return jnp.concatenate([a] * reps, axis=0)


def _nsa_queries(zq_ref, g):
    scale = NSA_HD ** -0.5
    q = jnp.concatenate([zq_ref[:, (NSA_GRP * g + h) * LANES:(NSA_GRP * g + h + 1) * LANES]
                         for h in range(NSA_GRP)], axis=0)
    return (q * scale).astype(BF16)


def _nsa_compressed_and_select(qgs, kc, vc, m_c, ov, pos1, ns, r):
    bias = _tile_rows(jnp.where(m_c, 0.0, NEG), NSA_GRP)
    ss = [_dot_nt(qg, kc) + bias for qg in qgs]
    mxs = [jnp.max(s, axis=-1, keepdims=True) for s in ss]
    ps = [jnp.exp(s - mx) for s, mx in zip(ss, mxs)]
    ps = [p / jnp.sum(p, axis=-1, keepdims=True) * jnp.where(mx > 0.5 * NEG, 1.0, 0.0) for p, mx in zip(ps, mxs)]
    o_cs = [_dot(p.astype(BF16), vc) for p in ps]
    psums = [(p[0:r] + p[r:2 * r]) + (p[2 * r:3 * r] + p[3 * r:4 * r]) for p in ps]
    imps = [_dot_mask_left_t(psum, ov) for psum in psums]
    blk = lax.broadcasted_iota(jnp.int32, (r, LANES), 1)
    valid = blk * SEL_LEN <= pos1
    cur = pos1 // SEL_LEN
    bonus = jnp.where((blk == 0) | (blk == cur) | (blk == cur - 1), FORCE_BONUS, 0.0)
    scores = [jnp.where(valid, imp + bonus, NEG) for imp in imps]
    sels = [sel & valid for sel in _topk_mask(scores, min(ns, LANES), min(SEL_TOPK, ns))]
    return o_cs, sels


def _dot_mask_left_t(x, m01):
    m = m01.astype(BF16)
    x1 = x.astype(BF16)
    r1 = x - x1.astype(F32)
    x2 = r1.astype(BF16)
    x3 = (r1 - x2.astype(F32)).astype(BF16)
    return _dot(x1, m) + (_dot(x2, m) + _dot(x3, m))


def _gate_cols(gates, g, c, r):
    return jnp.concatenate([gates[:, 3 * (NSA_GRP * g + h) + c:3 * (NSA_GRP * g + h) + c + 1]
                            for h in range(NSA_GRP)], axis=0)


def _nsa_assemble(o0, o1, r):
    lane = lax.broadcasted_iota(jnp.int32, (r, LANES), 1)
    return jnp.concatenate([jnp.where(lane < NSA_HD, o0[j * r:(j + 1) * r], o1[j * r:(j + 1) * r])
                            for j in range(NSA_GRP)], axis=1)


SEL_TILE = 512
Q_TILE = 128


def _nsa_prompt_kernel(zq_ref, zg_ref, kcvc_ref, kv_ref, e_ref, ov_ref, out_ref, *, t):
    r = Q_TILE
    qb = pl.program_id(1)
    s0 = qb * r
    pos1 = s0 + lax.broadcasted_iota(jnp.int32, (r, 1), 0)
    nc = t // CMP_STRIDE
    ns = t // SEL_LEN
    kc = kcvc_ref[0][:, 0:LANES].astype(BF16)
    vc = kcvc_ref[0][:, LANES:2 * LANES].astype(BF16)
    ncol = lax.broadcasted_iota(jnp.int32, (1, nc), 1)
    m_c = ((ncol * CMP_STRIDE + (CMP_LEN - 1)) <= pos1) & (ncol < nc - 1)
    gates = _sigmoid(zg_ref[...])
    ov = ov_ref[...]

    qs = [_nsa_queries(zq_ref, g) for g in range(NSA_KV)]
    ocs, sels = _nsa_compressed_and_select(qs, kc, vc, m_c, ov, pos1, ns, r)
    sels = [jnp.where(sel, 1.0, 0.0).astype(BF16) for sel in sels]

    def body(kt, carry):
        k0 = pl.multiple_of(kt * SEL_TILE, SEL_TILE)
        kpos = k0 + lax.broadcasted_iota(jnp.int32, (1, SEL_TILE), 1)
        kk = kv_ref[pl.ds(k0, SEL_TILE), 0:LANES]
        vv = kv_ref[pl.ds(k0, SEL_TILE), LANES:2 * LANES]
        ekt = e_ref[kt]
        groups = range(NSA_KV)
        causal = kpos <= pos1
        allows = [(_dot(sels[g], ekt) > 0.5) & causal for g in groups]
        ss = [_dot_nt(qs[g], kk) + _tile_rows(jnp.where(allows[g], 0.0, NEG), NSA_GRP) for g in groups]
        m_news = [jnp.maximum(carry[g][0], jnp.max(ss[g], axis=-1, keepdims=True)) for g in groups]
        alphas = [jnp.exp(carry[g][0] - m_news[g]) for g in groups]
        ps = [jnp.exp(ss[g] - m_news[g]) for g in groups]
        ls = [alphas[g] * carry[g][1] + jnp.sum(ps[g], axis=-1, keepdims=True) for g in groups]
        accs = [alphas[g] * carry[g][2] + _dot(ps[g].astype(BF16), vv) for g in groups]
        return tuple((m_news[g], ls[g], accs[g]) for g in groups)

    init = tuple((jnp.full((NSA_GRP * r, 1), NEG, F32), jnp.zeros((NSA_GRP * r, 1), F32),
                  jnp.zeros((NSA_GRP * r, LANES), F32)) for _ in range(NSA_KV))
    n_tiles = (s0 + r + SEL_TILE - 1) // SEL_TILE
    fin = lax.fori_loop(0, n_tiles, body, init)

    span = WINDOW + r
    start = pl.multiple_of(jnp.maximum(s0 - WINDOW, 0), r)
    kw = kv_ref[pl.ds(start, span), 2 * LANES:3 * LANES]
    vw = kv_ref[pl.ds(start, span), 3 * LANES:4 * LANES]
    kposw = start + lax.broadcasted_iota(jnp.int32, (1, span), 1)
    bias_w = _tile_rows(jnp.where((kposw <= pos1) & (kposw > pos1 - WINDOW), 0.0, NEG), NSA_GRP)

    groups = range(NSA_KV)
    o_ss = [fin[g][2] / fin[g][1] for g in groups]
    ss = [_dot_nt(qs[g], kw) + bias_w for g in groups]
    ps = [jnp.exp(s - jnp.max(s, axis=-1, keepdims=True)) for s in ss]
    o_ws = [_dot(p.astype(BF16), vw) / jnp.sum(p, axis=-1, keepdims=True) for p in ps]
    outs = [_gate_cols(gates, g, 0, r) * ocs[g] + _gate_cols(gates, g, 1, r) * o_ss[g]
            + _gate_cols(gates, g, 2, r) * o_ws[g] for g in groups]
    out_ref[...] = _nsa_assemble(outs[0], outs[1], r)


def nsa_prompt(z, kcvc, kvb, e_mat, ov, bsz, t):
    nqb = t // Q_TILE
    return pl.pallas_call(
        functools.partial(_nsa_prompt_kernel, t=t),
        grid=(bsz, nqb),
        in_specs=[pl.BlockSpec((Q_TILE, NSA_HEADS * LANES), lambda b, q: (b * nqb + q, 0)),
                  pl.BlockSpec((Q_TILE, LANES), lambda b, q: (b * nqb + q, ZNG // LANES)),
                  pl.BlockSpec((1, t // CMP_STRIDE, 256), lambda b, q: (b, 0, 0)),
                  pl.BlockSpec((t, 4 * LANES), lambda b, q: (b, 0)),
                  pl.BlockSpec((t // SEL_TILE, LANES, SEL_TILE), lambda b, q: (0, 0, 0)),
                  pl.BlockSpec((t // CMP_STRIDE, LANES), lambda b, q: (0, 0))],
        out_specs=pl.BlockSpec((Q_TILE, MIX_WIDTH), lambda b, q: (b * nqb + q, 0)),
        out_shape=jax.ShapeDtypeStruct((bsz * t, MIX_WIDTH), F32),
        compiler_params=_cparams(("parallel", "arbitrary")),
        name="nsa_prompt",
    )(z, z, kcvc, kvb, e_mat, ov)


def _nsa_sample_kernel(pt_ref, zq_ref, zkv_ref, cwin_ref, cache_ref, pea_ref, peb_ref, w1a_ref, w1b_ref,
                       w2_ref, e_ref, ov_ref, out_ref, buf_ref, tokm_ref, sem_ref, *, layer, past, n_pages, dec):
    r = SEQ_PAD
    b = pl.program_id(0)
    nb = pl.num_programs(0)
    slot = lax.rem(b, 2)

    def page_copy(seq, sl, p):
        return pltpu.make_async_copy(cache_ref.at[layer, pt_ref[seq * n_pages + p]],
                                     buf_ref.at[sl, :, pl.ds(p * PAGE_SIZE, PAGE_SIZE)],
                                     sem_ref.at[sl])

    def start_all(seq, sl):
        for p in range(n_pages):
            page_copy(seq, sl, p).start()

    @pl.when(b == 0)
    def _():
        start_all(b, slot)

    @pl.when(b + 1 < nb)
    def _():
        start_all(b + 1, 1 - slot)

    for p in range(n_pages):
        page_copy(b, slot, p).wait()

    nsub = past // CMP_STRIDE
    nc = nsub - 1
    ns = -(-(past + dec) // SEL_LEN)
    for c0 in range(2):
        for p in range(past // LANES):
            tokm_ref[c0, p * LANES:(p + 1) * LANES, :] = jnp.transpose(
                buf_ref[slot, c0 * LANES:(c0 + 1) * LANES, p * LANES:(p + 1) * LANES])
    load = lambda j: jnp.concatenate([tokm_ref[0, pl.ds(j, nsub, stride=CMP_STRIDE), :],
                                      tokm_ref[1, pl.ds(j, nsub, stride=CMP_STRIDE), :]], axis=1)
    kcvc = _compress_math(load, pea_ref, peb_ref, w1a_ref, w1b_ref, w2_ref, nsub)
    kc = kcvc[:, 0:LANES].astype(BF16)
    vc = kcvc[:, LANES:2 * LANES].astype(BF16)

    row = lax.broadcasted_iota(jnp.int32, (r, 1), 0)
    pos1 = past + row
    ncol = lax.broadcasted_iota(jnp.int32, (1, nsub), 1)
    m_c = ((ncol * CMP_STRIDE + (CMP_LEN - 1)) <= pos1) & (ncol < nc)
    gates = _sigmoid(zkv_ref[:, ZNG - ZKV:ZNG - ZKV + LANES])
    ov = ov_ref[...]

    k_sel = buf_ref[slot, 2 * LANES:3 * LANES, :].astype(BF16)
    v_sel = buf_ref[slot, 3 * LANES:4 * LANES, :].astype(BF16)
    kn_sel = zkv_ref[:, 2 * LANES:3 * LANES].astype(BF16)
    vn_sel = zkv_ref[:, 3 * LANES:4 * LANES].astype(BF16)
    k_win = cwin_ref[0, 0, 0:LANES, :].astype(BF16)
    v_win = cwin_ref[0, 0, LANES:2 * LANES, :].astype(BF16)
    kn_win = zkv_ref[:, ZKW - ZKV:ZKW - ZKV + LANES].astype(BF16)
    vn_win = zkv_ref[:, ZKW - ZKV + LANES:ZKW - ZKV + 2 * LANES].astype(BF16)

    kpos_p = lax.broadcasted_iota(jnp.int32, (1, past), 1)
    tnew = lax.broadcasted_iota(jnp.int32, (1, r), 1)
    kpos_n = past + tnew
    wk = cwin_ref.shape[3]
    kpos_w = past - wk + lax.broadcasted_iota(jnp.int32, (1, wk), 1)
    in_win = lambda kp: (kp <= pos1) & (kp > pos1 - WINDOW)
    bias_wp = _tile_rows(jnp.where(in_win(kpos_w), 0.0, NEG), NSA_GRP)
    bias_wn = _tile_rows(jnp.where(in_win(kpos_n) & (tnew < dec), 0.0, NEG), NSA_GRP)
    new_blk = past // SEL_LEN

    def two_part_attention(qg, kp_t, vp_t, bias_p, kn, vn, bias_n):
        s_p = _dot(qg, kp_t) + bias_p
        s_n = _dot_nt(qg, kn) + bias_n
        mx = jnp.maximum(jnp.max(s_p, axis=-1, keepdims=True), jnp.max(s_n, axis=-1, keepdims=True))
        p_p = jnp.exp(s_p - mx)
        p_n = jnp.exp(s_n - mx)
        den = jnp.sum(p_p, axis=-1, keepdims=True) + jnp.sum(p_n, axis=-1, keepdims=True)
        return (_dot_nt(p_p.astype(BF16), vp_t) + _dot(p_n.astype(BF16), vn)) / den

    outs = []
    qgs = [_nsa_queries(zq_ref, g) for g in range(NSA_KV)]
    o_cs, sels = _nsa_compressed_and_select(qgs, kc, vc, m_c, ov, pos1, ns, r)
    for g in range(NSA_KV):
        qg, o_c = qgs[g], o_cs[g]
        self = jnp.where(sels[g], 1.0, 0.0)
        allow_p = (_dot(self.astype(BF16), e_ref[...]) > 0.5) & (kpos_p <= pos1)
        allow_n = (self[:, new_blk:new_blk + 1] > 0.5) & (kpos_n <= pos1) & (tnew < dec)
        o_s = two_part_attention(qg, k_sel, v_sel, _tile_rows(jnp.where(allow_p, 0.0, NEG), NSA_GRP),
                                 kn_sel, vn_sel, _tile_rows(jnp.where(allow_n, 0.0, NEG), NSA_GRP))
        o_w = two_part_attention(qg, k_win, v_win, bias_wp, kn_win, vn_win, bias_wn)
        outs.append(_gate_cols(gates, g, 0, r) * o_c + _gate_cols(gates, g, 1, r) * o_s
                    + _gate_cols(gates, g, 2, r) * o_w)
    out_ref[...] = _nsa_assemble(outs[0], outs[1], r)


def nsa_sample(z, page_table, cache_t, cache_win_t, cw, e_mat, ov, layer, row0, dbsz, past, dec):
    n_pages = past // PAGE_SIZE
    rb0 = row0 // SEQ_PAD
    wk = cache_win_t.shape[3]
    c2 = lambda b, pt: (0, 0)
    c3 = lambda b, pt: (0, 0, 0)
    grid_spec = pltpu.PrefetchScalarGridSpec(
        num_scalar_prefetch=1,
        grid=(dbsz,),
        in_specs=[pl.BlockSpec((SEQ_PAD, NSA_HEADS * LANES), lambda b, pt: (rb0 + b, 0)),
                  pl.BlockSpec((SEQ_PAD, 1024), lambda b, pt: (rb0 + b, ZKV // 1024)),
                  pl.BlockSpec((1, 1, 2 * LANES, wk), lambda b, pt: (layer, b, 0, 0)),
                  pl.BlockSpec(memory_space=pl.ANY),
                  pl.BlockSpec((CMP_STRIDE, 256), c2),
                  pl.BlockSpec((CMP_STRIDE, 256), c2),
                  pl.BlockSpec((CMP_STRIDE, 256, 256), c3),
                  pl.BlockSpec((CMP_STRIDE, 256, 256), c3),
                  pl.BlockSpec((256, 256), c2),
                  pl.BlockSpec((LANES, past), c2),
                  pl.BlockSpec((past // CMP_STRIDE, LANES), c2)],
        out_specs=pl.BlockSpec((SEQ_PAD, MIX_WIDTH), lambda b, pt: (b, 0)),
        scratch_shapes=[pltpu.VMEM((2, 4 * LANES, past), F32), pltpu.VMEM((2, past, LANES), F32),
                        pltpu.SemaphoreType.DMA((2,))],
    )
    return pl.pallas_call(
        functools.partial(_nsa_sample_kernel, layer=layer, past=past, n_pages=n_pages, dec=dec),
        grid_spec=grid_spec,
        out_shape=jax.ShapeDtypeStruct((dbsz * SEQ_PAD, MIX_WIDTH), F32),
        compiler_params=_cparams(("arbitrary",)),
        name="nsa_sample",
    )(page_table.reshape(-1), z, z, cache_win_t, cache_t, cw["pea"], cw["peb"], cw["w1a"], cw["w1b"], cw["w2"],
      e_mat, ov)


def _seg_masks(r, c):
    ri = lax.broadcasted_iota(jnp.int32, (r, r), 0)
    ci = lax.broadcasted_iota(jnp.int32, (r, r), 1)
    same = (ri // c) == (ci // c)
    return same, same & (ri >= ci), same & (ri > ci)


def _gdn_prep(y, ab, par_ref, valid, c, u_ref, w_ref, qe_ref, qk_ref, kdt_ref, eg_ref):
    r = y.shape[0]
    same, incl, strict = _seg_masks(r, c)
    y = _silu(y)
    beta_all = _sigmoid(ab)
    g_all = -jnp.exp(par_ref[0:1, :]) * _softplus(ab + par_ref[1:2, :])
    if valid is not None:
        y = jnp.where(valid, y, 0.0)
        beta_all = jnp.where(valid, beta_all, 0.0)
        g_all = jnp.where(valid, g_all, 0.0)
    gcum_all = _dot_mask_left(jnp.where(incl, 1.0, 0.0), g_all)
    glast_all = _dot_mask_left(jnp.where(same, 1.0, 0.0), g_all)
    gcum_t = jnp.transpose(gcum_all)
    eye = jnp.where(lax.broadcasted_iota(jnp.int32, (r, r), 0) == lax.broadcasted_iota(jnp.int32, (r, r), 1),
                    1.0, 0.0)
    n_double = int(math.log2(c))
    tms, pws, rhss = [], [], []
    for h in range(GDN_HEADS):
        q = y[:, h * GDN_HD:(h + 1) * GDN_HD]
        k = y[:, MIX_WIDTH + h * GDN_HD:MIX_WIDTH + (h + 1) * GDN_HD]
        v = y[:, 2 * MIX_WIDTH + h * GDN_HD:2 * MIX_WIDTH + (h + 1) * GDN_HD]
        q = q * lax.rsqrt(jnp.sum(q * q, axis=-1, keepdims=True) + EPS) * (GDN_HD ** -0.5)
        k = k * lax.rsqrt(jnp.sum(k * k, axis=-1, keepdims=True) + EPS)
        g1 = jnp.broadcast_to(gcum_all[:, h:h + 1], (r, GDN_HD))
        g2 = jnp.broadcast_to(gcum_t[h:h + 1, :], (r, r))
        gl = jnp.broadcast_to(glast_all[:, h:h + 1], (r, GDN_HD))
        beta = jnp.broadcast_to(beta_all[:, GDN_HEADS + h:GDN_HEADS + h + 1], (r, GDN_HD))
        g1r = g1 if r == GDN_HD else jnp.broadcast_to(gcum_all[:, h:h + 1], (r, r))
        decay = jnp.where(incl, jnp.exp(jnp.where(incl, g1r - g2, 0.0)), 0.0)
        betar = beta if r == GDN_HD else jnp.broadcast_to(beta_all[:, GDN_HEADS + h:GDN_HEADS + h + 1], (r, r))
        a = jnp.where(strict, _bdot_nt(k, k) * decay * betar, 0.0)
        tms.append(eye - a)
        pws.append(a)
        eg1 = jnp.exp(g1)
        rhss.append(jnp.concatenate([v * beta, k * (beta * eg1)], axis=1))
        qe_ref[h] = q * eg1
        qk_ref[h] = jnp.where(incl, _bdot_nt(q, k) * decay, 0.0)
        kdt_ref[h] = jnp.transpose(k * jnp.exp(gl - g1))
        eg_ref[h] = jnp.exp(gl)
    for _ in range(n_double - 1):
        pws = [_dot3(pw, pw) for pw in pws]
        tms = [tm + _dot3(tm, pw) for tm, pw in zip(tms, pws)]
    for h in range(GDN_HEADS):
        sol = _dot3(tms[h], rhss[h])
        u_ref[h] = sol[:, 0:GDN_HD]
        w_ref[h] = sol[:, GDN_HD:2 * GDN_HD]


def _gdn_segment(h, row0, seg, c, s, u_ref, w_ref, qe_ref, qk_ref, kdt_ref, eg_ref, vn_ref, col_seg):
    rows = pl.ds(row0, c)
    vn = u_ref[h, rows, :] - _bdot(w_ref[h, rows, :], s)
    vn_ref[h, rows, :] = vn
    vn_all = vn_ref[h].astype(BF16)
    o = _bdot(qe_ref[h, rows, :], s) + _dot(qk_ref[h, rows, :].astype(BF16), vn_all)
    kdt = jnp.where(col_seg == seg, kdt_ref[h], 0.0)
    s_new = s * eg_ref[h, pl.ds(row0, 1), :] + _dot(kdt.astype(BF16), vn_all)
    return o, s_new


def _gdn_finish(o, zg, ng_ref):
    return _rms(o, ng_ref[...]) * _silu(zg)


GDN_TILE = 128


def _gdn_prompt_kernel(x_ref, ab_ref, zg_ref, cw_ref, par_ref, ng_ref, out_ref, st_ref,
                       halo_ref, u_ref, w_ref, qe_ref, qk_ref, kdt_ref, eg_ref, vn_ref):
    i = pl.program_id(1)
    r = GDN_TILE
    c = GDN_CHUNK

    @pl.when(i == 0)
    def _():
        halo_ref[...] = jnp.zeros_like(halo_ref)
        st_ref[...] = jnp.zeros_like(st_ref)

    x = x_ref[...]
    xx = jnp.concatenate([halo_ref[...], x], axis=0)
    halo_ref[...] = x[r - SUBLANES:r, :]
    y = None
    for j in range(CONV_W):
        sh = CONV_W - 1 - j
        xs = xx if sh == 0 else pltpu.roll(xx, sh, axis=0)
        t = xs[SUBLANES:, :] * cw_ref[j:j + 1, :]
        y = t if y is None else y + t
    _gdn_prep(y, ab_ref[...], par_ref, None, c, u_ref, w_ref, qe_ref, qk_ref, kdt_ref, eg_ref)
    vn_ref[...] = jnp.zeros_like(vn_ref)
    col_seg = lax.broadcasted_iota(jnp.int32, (GDN_HD, r), 1) // c
    states = [st_ref[0, h] for h in range(GDN_HEADS)]
    os_ = [[] for _ in range(GDN_HEADS)]
    for ck in range(r // c):
        for h in range(GDN_HEADS):
            o, states[h] = _gdn_segment(h, ck * c, ck, c, states[h], u_ref, w_ref, qe_ref, qk_ref, kdt_ref,
                                        eg_ref, vn_ref, col_seg)
            os_[h].append(o)
    for h in range(GDN_HEADS):
        st_ref[0, h] = states[h]
        o = jnp.concatenate(os_[h], axis=0)
        out_ref[:, h * GDN_HD:(h + 1) * GDN_HD] = _gdn_finish(o, zg_ref[:, h * GDN_HD:(h + 1) * GDN_HD], ng_ref)


def _gdn_scratch(r):
    hs = (GDN_HEADS, r, GDN_HD)
    return [pltpu.VMEM(hs, F32), pltpu.VMEM(hs, F32), pltpu.VMEM(hs, F32), pltpu.VMEM((GDN_HEADS, r, r), F32),
            pltpu.VMEM((GDN_HEADS, GDN_HD, r), F32), pltpu.VMEM(hs, F32), pltpu.VMEM(hs, F32)]


def gdn_prompt(z, conv_w, par, norm_g, bsz, t):
    nt = t // GDN_TILE
    r = GDN_TILE
    c2 = lambda b, i: (0, 0)
    return pl.pallas_call(
        _gdn_prompt_kernel,
        grid=(bsz, nt),
        in_specs=[pl.BlockSpec((r, 3 * MIX_WIDTH), lambda b, i: (b * nt + i, ZGQ // (3 * MIX_WIDTH))),
                  pl.BlockSpec((r, LANES), lambda b, i: (b * nt + i, ZGAB // LANES)),
                  pl.BlockSpec((r, MIX_WIDTH), lambda b, i: (b * nt + i, ZGZ // MIX_WIDTH)),
                  pl.BlockSpec((CONV_W, 3 * MIX_WIDTH), c2),
                  pl.BlockSpec((SUBLANES, LANES), c2),
                  pl.BlockSpec((1, GDN_HD), c2)],
        out_specs=[pl.BlockSpec((r, MIX_WIDTH), lambda b, i: (b * nt + i, 0)),
                   pl.BlockSpec((1, GDN_HEADS, GDN_HD, GDN_HD), lambda b, i: (b, 0, 0, 0))],
        out_shape=[jax.ShapeDtypeStruct((bsz * t, MIX_WIDTH), F32),
                   jax.ShapeDtypeStruct((bsz, GDN_HEADS, GDN_HD, GDN_HD), F32)],
        scratch_shapes=[pltpu.VMEM((SUBLANES, 3 * MIX_WIDTH), F32)] + _gdn_scratch(r),
        compiler_params=_cparams(("parallel", "arbitrary")),
        name="gdn_prompt",
    )(z, z, z, conv_w, par, norm_g)


SEQ_PER_STEP = 16


def _gdn_sample_kernel(x_ref, ab_ref, zg_ref, prev_ref, s0_ref, cw_ref, par_ref, ng_ref, out_ref, st_ref,
                       u_ref, w_ref, qe_ref, qk_ref, kdt_ref, eg_ref, vn_ref, o_ref, *, dec):
    r = SEQ_PER_STEP * SEQ_PAD
    c = SEQ_PAD
    row = lax.broadcasted_iota(jnp.int32, (r, 1), 0)
    tin = row % c
    valid = tin < dec
    xx = jnp.where(tin < CONV_W - 1, prev_ref[...], pltpu.roll(x_ref[...], CONV_W - 1, axis=0))
    y = None
    for j in range(CONV_W):
        xs = xx if j == 0 else pltpu.roll(xx, r - j, axis=0)
        t = xs * cw_ref[j:j + 1, :]
        y = t if y is None else y + t
    _gdn_prep(y, ab_ref[...], par_ref, valid, c, u_ref, w_ref, qe_ref, qk_ref, kdt_ref, eg_ref)
    vn_ref[...] = jnp.zeros_like(vn_ref)
    col_seg = lax.broadcasted_iota(jnp.int32, (GDN_HD, r), 1) // c

    def body(sq, carry):
        row0 = pl.multiple_of(sq * c, c)
        for h in range(GDN_HEADS):
            o, s = _gdn_segment(h, row0, sq, c, s0_ref[sq, h], u_ref, w_ref, qe_ref, qk_ref, kdt_ref, eg_ref,
                                vn_ref, col_seg)
            st_ref[sq, h] = s
            o_ref[h, pl.ds(row0, c), :] = o
        return carry

    lax.fori_loop(0, SEQ_PER_STEP, body, 0)
    for h in range(GDN_HEADS):
        out_ref[:, h * GDN_HD:(h + 1) * GDN_HD] = _gdn_finish(o_ref[h], zg_ref[:, h * GDN_HD:(h + 1) * GDN_HD],
                                                              ng_ref)


def gdn_sample(z, prev_pad, s0, conv_w, par, norm_g, row0, dbsz, dec):
    r = SEQ_PER_STEP * SEQ_PAD
    rb0 = row0 // r
    c2 = lambda i: (0, 0)
    return pl.pallas_call(
        functools.partial(_gdn_sample_kernel, dec=dec),
        grid=(dbsz // SEQ_PER_STEP,),
        in_specs=[pl.BlockSpec((r, 3 * MIX_WIDTH), lambda i: (rb0 + i, ZGQ // (3 * MIX_WIDTH))),
                  pl.BlockSpec((r, LANES), lambda i: (rb0 + i, ZGAB // LANES)),
                  pl.BlockSpec((r, MIX_WIDTH), lambda i: (rb0 + i, ZGZ // MIX_WIDTH)),
                  pl.BlockSpec((r, 3 * MIX_WIDTH), lambda i: (i, 0)),
                  pl.BlockSpec((SEQ_PER_STEP, GDN_HEADS, GDN_HD, GDN_HD), lambda i: (i, 0, 0, 0)),
                  pl.BlockSpec((CONV_W, 3 * MIX_WIDTH), c2),
                  pl.BlockSpec((SUBLANES, LANES), c2),
                  pl.BlockSpec((1, GDN_HD), c2)],
        out_specs=[pl.BlockSpec((r, MIX_WIDTH), lambda i: (i, 0)),
                   pl.BlockSpec((SEQ_PER_STEP, GDN_HEADS, GDN_HD, GDN_HD), lambda i: (i, 0, 0, 0))],
        out_shape=[jax.ShapeDtypeStruct((dbsz * SEQ_PAD, MIX_WIDTH), F32),
                   jax.ShapeDtypeStruct((dbsz, GDN_HEADS, GDN_HD, GDN_HD), F32)],
        scratch_shapes=_gdn_scratch(r) + [pltpu.VMEM((GDN_HEADS, r, GDN_HD), F32)],
        compiler_params=_cparams(("parallel",)),
        name="gdn_sample",
    )(z, z, z, prev_pad, s0, conv_w, par, norm_g)


def _ret_log_gamma(h):
    return math.log1p(-(2.0 ** (-5.0 - h)))


def _ret_prep(x, cs, sn, valid, h):
    qh = x[:, h * RET_HD:(h + 1) * RET_HD]
    kh = x[:, MIX_WIDTH + h * RET_HD:MIX_WIDTH + (h + 1) * RET_HD]
    vh = x[:, 2 * MIX_WIDTH + h * RET_HD:2 * MIX_WIDTH + (h + 1) * RET_HD]
    qr = qh * cs + pltpu.roll(qh, RET_HD // 2, axis=1) * sn
    kr = (kh * cs + pltpu.roll(kh, RET_HD // 2, axis=1) * sn) * (RET_HD ** -0.5)
    if valid is not None:
        kr = jnp.where(valid, kr, 0.0)
        vh = jnp.where(valid, vh, 0.0)
    return qr, kr, vh


def _ret_finish(o, gate):
    o = o * lax.rsqrt(jnp.mean(o * o, axis=-1, keepdims=True) + EPS)
    return o * _silu(gate)


def _ret_prompt_kernel(x_ref, gt_ref, cs_ref, sn_ref, out_ref, st_ref):
    i = pl.program_id(1)
    c = RET_CHUNK

    @pl.when(i == 0)
    def _():
        st_ref[...] = jnp.zeros_like(st_ref)

    x = x_ref[...]
    cs = cs_ref[...]
    sn = sn_ref[...]
    ri = lax.broadcasted_iota(jnp.int32, (c, c), 0)
    ci = lax.broadcasted_iota(jnp.int32, (c, c), 1)
    diff = (ri - ci).astype(F32)
    n = lax.broadcasted_iota(jnp.int32, (c, 1), 0).astype(F32)
    heads = range(RET_HEADS)
    lgs = [_ret_log_gamma(h) for h in heads]
    qkv = [_ret_prep(x, cs, sn, None, h) for h in heads]
    states = [st_ref[0, h] for h in heads]
    scores = [_bdot_nt(qkv[h][0], qkv[h][1]) * jnp.where(diff >= 0.0, jnp.exp(jnp.maximum(diff, 0.0) * lgs[h]), 0.0)
              for h in heads]
    cross = [_bdot(qkv[h][0], states[h]) * jnp.exp((n + 1.0) * lgs[h]) for h in heads]
    outs = [_bdot(scores[h], qkv[h][2]) + cross[h] for h in heads]
    kdts = [jnp.transpose(qkv[h][1] * jnp.exp((c - 1.0 - n) * lgs[h])).astype(BF16) for h in heads]
    for h in heads:
        st_ref[0, h] = states[h] * math.exp(c * lgs[h]) + _dot(kdts[h], qkv[h][2].astype(BF16))
        out_ref[:, h * RET_HD:(h + 1) * RET_HD] = _ret_finish(outs[h], gt_ref[:, h * RET_HD:(h + 1) * RET_HD])


def ret_prompt(z, cs, sn, bsz, t):
    c = RET_CHUNK
    nt = t // c
    return pl.pallas_call(
        _ret_prompt_kernel,
        grid=(bsz, nt),
        in_specs=[pl.BlockSpec((c, 3 * MIX_WIDTH), lambda b, i: (b * nt + i, ZRQ // (3 * MIX_WIDTH))),
                  pl.BlockSpec((c, MIX_WIDTH), lambda b, i: (b * nt + i, ZRG // MIX_WIDTH)),
                  pl.BlockSpec((c, RET_HD), lambda b, i: (i, 0)),
                  pl.BlockSpec((c, RET_HD), lambda b, i: (i, 0))],
        out_specs=[pl.BlockSpec((c, MIX_WIDTH), lambda b, i: (b * nt + i, 0)),
                   pl.BlockSpec((1, RET_HEADS, RET_HD, RET_HD), lambda b, i: (b, 0, 0, 0))],
        out_shape=[jax.ShapeDtypeStruct((bsz * t, MIX_WIDTH), F32),
                   jax.ShapeDtypeStruct((bsz, RET_HEADS, RET_HD, RET_HD), F32)],
        compiler_params=_cparams(("parallel", "arbitrary")),
        name="ret_prompt",
    )(z, z, cs, sn)


def _ret_sample_kernel(x_ref, gt_ref, cs_ref, sn_ref, s0_ref, out_ref, st_ref, q_ref, kdt_ref, v_ref, o_ref,
                       *, dec):
    r = SEQ_PER_STEP * SEQ_PAD
    c = SEQ_PAD
    row = lax.broadcasted_iota(jnp.int32, (r, 1), 0)
    tin = row % c
    valid = tin < dec
    n = tin.astype(F32)
    x = x_ref[...]
    cs = cs_ref[...]
    sn = sn_ref[...]
    same, incl, _ = _seg_masks(r, c)
    ri = lax.broadcasted_iota(jnp.int32, (r, r), 0)
    ci = lax.broadcasted_iota(jnp.int32, (r, r), 1)
    diff = (ri - ci).astype(F32)
    for h in range(RET_HEADS):
        lg = _ret_log_gamma(h)
        qr, kr, vh = _ret_prep(x, cs, sn, valid, h)
        dmat = jnp.where(incl, jnp.exp(jnp.maximum(diff, 0.0) * lg), 0.0)
        o_ref[h] = _bdot(_bdot_nt(qr, kr) * dmat, vh)
        q_ref[h] = qr
        kdt_ref[h] = jnp.transpose(kr * jnp.exp((dec - 1.0 - n) * lg))
        v_ref[h] = vh
    col_seg = lax.broadcasted_iota(jnp.int32, (RET_HD, r), 1) // c
    qdec = [jnp.exp((lax.broadcasted_iota(jnp.int32, (c, 1), 0).astype(F32) + 1.0) * _ret_log_gamma(h))
            for h in range(RET_HEADS)]

    def body(sq, carry):
        row0 = pl.multiple_of(sq * c, c)
        rows = pl.ds(row0, c)
        for h in range(RET_HEADS):
            s = s0_ref[sq, h]
            o_ref[h, rows, :] = o_ref[h, rows, :] + _bdot(q_ref[h, rows, :], s) * qdec[h]
            kdt = jnp.where(col_seg == sq, kdt_ref[h], 0.0)
            st_ref[sq, h] = s * math.exp(dec * _ret_log_gamma(h)) + _dot(kdt.astype(BF16), v_ref[h].astype(BF16))
        return carry

    lax.fori_loop(0, SEQ_PER_STEP, body, 0)
    for h in range(RET_HEADS):
        out_ref[:, h * RET_HD:(h + 1) * RET_HD] = _ret_finish(o_ref[h], gt_ref[:, h * RET_HD:(h + 1) * RET_HD])


def ret_sample(z, cs, sn, s0, row0, dbsz, dec):
    r = SEQ_PER_STEP * SEQ_PAD
    rb0 = row0 // r
    hs = (RET_HEADS, r, RET_HD)
    return pl.pallas_call(
        functools.partial(_ret_sample_kernel, dec=dec),
        grid=(dbsz // SEQ_PER_STEP,),
        in_specs=[pl.BlockSpec((r, 3 * MIX_WIDTH), lambda i: (rb0 + i, ZRQ // (3 * MIX_WIDTH))),
                  pl.BlockSpec((r, MIX_WIDTH), lambda i: (rb0 + i, ZRG // MIX_WIDTH)),
                  pl.BlockSpec((r, RET_HD), lambda i: (0, 0)),
                  pl.BlockSpec((r, RET_HD), lambda i: (0, 0)),
                  pl.BlockSpec((SEQ_PER_STEP, RET_HEADS, RET_HD, RET_HD), lambda i: (i, 0, 0, 0))],
        out_specs=[pl.BlockSpec((r, MIX_WIDTH), lambda i: (i, 0)),
                   pl.BlockSpec((SEQ_PER_STEP, RET_HEADS, RET_HD, RET_HD), lambda i: (i, 0, 0, 0))],
        out_shape=[jax.ShapeDtypeStruct((dbsz * SEQ_PAD, MIX_WIDTH), F32),
                   jax.ShapeDtypeStruct((dbsz, RET_HEADS, RET_HD, RET_HD), F32)],
        scratch_shapes=[pltpu.VMEM(hs, F32), pltpu.VMEM((RET_HEADS, RET_HD, r), F32), pltpu.VMEM(hs, F32),
                        pltpu.VMEM(hs, F32)],
        compiler_params=_cparams(("parallel",)),
        name="ret_sample",
    )(z, z, cs, sn, s0)


def _prep_w_in(w):
    sizes = (NSA_HEADS * NSA_HD, 6 * NSA_KV * NSA_HD, 3 * NSA_HEADS, 3 * MIX_WIDTH, GDN_HEADS, GDN_HEADS,
             MIX_WIDTH, 3 * MIX_WIDTH, MIX_WIDTH, N_BRANCH * D_MODEL)
    src = np.concatenate([[0], np.cumsum(sizes)])
    s_q, s_kv, s_ng, s_gq, s_ga, s_gb, s_gz, s_rq, s_rg, s_mg = src[:-1].tolist()
    wb = w.astype(BF16)
    out = jnp.zeros((w.shape[0], ZW), BF16)

    def put(o, dst, s0, n):
        return lax.dynamic_update_slice(o, lax.slice_in_dim(wb, s0, s0 + n, axis=1), (0, dst))

    for h in range(NSA_HEADS):
        out = put(out, ZQ + h * LANES + NSA_HD * (h // NSA_GRP), s_q + h * NSA_HD, NSA_HD)
    out = put(out, ZKV, s_kv, sizes[1] + sizes[2])
    out = put(out, ZGAB, s_ga, 2 * GDN_HEADS)
    out = put(out, ZGZ, s_gz, MIX_WIDTH)
    out = put(out, ZRG, s_rg, MIX_WIDTH)
    out = put(out, ZGQ, s_gq, 3 * MIX_WIDTH)
    out = put(out, ZRQ, s_rq, 3 * MIX_WIDTH)
    out = put(out, ZMG, s_mg, N_BRANCH * D_MODEL)
    return out


def _prep_w_branch(wb):
    w0 = wb[0].reshape(NSA_HEADS, NSA_HD, D_MODEL)
    order = [h for j in range(NSA_GRP) for h in (j, j + NSA_GRP)]
    w0 = w0[np.array(order)].reshape(MIX_WIDTH, D_MODEL)
    return jnp.stack([w0, wb[1], wb[2]]).astype(BF16)


def _prep_compress(pe, w1, w2):
    sel_k = jnp.asarray(np.diag([1.0, 1.0, 0.0, 0.0]).astype(np.float32))
    sel_v = jnp.asarray(np.diag([0.0, 0.0, 1.0, 1.0]).astype(np.float32))

    def blockdiag(mk, mv):
        full = (sel_k[:, None, :, None] * mk[..., None, :, None, :]
                + sel_v[:, None, :, None] * mv[..., None, :, None, :])
        return full.reshape(mk.shape[:-2] + (4 * mk.shape[-2], 4 * mk.shape[-1]))

    w1r = w1.reshape(2, CMP_LEN, NSA_HD, CMP_HIDDEN)
    w1a = blockdiag(w1r[0, :CMP_STRIDE], w1r[1, :CMP_STRIDE])
    w1b = blockdiag(w1r[0, CMP_STRIDE:], w1r[1, CMP_STRIDE:])
    pea = jnp.concatenate([pe[0, :CMP_STRIDE], pe[0, :CMP_STRIDE], pe[1, :CMP_STRIDE], pe[1, :CMP_STRIDE]], axis=1)
    peb = jnp.concatenate([pe[0, CMP_STRIDE:], pe[0, CMP_STRIDE:], pe[1, CMP_STRIDE:], pe[1, CMP_STRIDE:]], axis=1)
    w2b = blockdiag(w2[0], w2[1])
    return dict(pea=pea, peb=peb, w1a=w1a.astype(BF16), w1b=w1b.astype(BF16), w2=w2b.astype(BF16))


def _overlap_matrix(nc_rows, nc, ns):
    n = np.arange(nc_rows)[:, None]
    s = np.arange(LANES)[None, :]
    c_start = n * CMP_STRIDE
    ov = (c_start < (s + 1) * SEL_LEN) & (s * SEL_LEN < c_start + CMP_LEN) & (n < nc) & (s < ns)
    return jnp.asarray(ov.astype(np.float32))


def _expand_matrix(nkeys):
    s = np.arange(LANES)[:, None]
    k = np.arange(nkeys)[None, :]
    return (k // SEL_LEN == s).astype(np.float32)


def _rope_tables(pos):
    half = RET_HD // 2
    inv = ROPE_BASE ** (-jnp.linspace(0.0, 1.0, half, dtype=F32))
    ang = pos.astype(F32)[:, None] * inv[None, :]
    cos, sin = jnp.cos(ang), jnp.sin(ang)
    return jnp.concatenate([cos, cos], axis=1), jnp.concatenate([-sin, sin], axis=1)


def _pad_rows(a, n):
    return jnp.pad(a, ((0, 0), (0, n - a.shape[1])) + ((0, 0),) * (a.ndim - 2))


def kernel(x_prompt, x_sample, cache_nsa_kv, cache_nsa_win, state_gdn_conv, state_gdn, state_ret, page_table,
           p_prompt, p_sample, g_mix, w_in, nsa_cmp_pe, nsa_cmp_w1, nsa_cmp_w2, gdn_conv_w, gdn_a_log,
           gdn_dt_bias, gdn_norm_g, w_branch, w_out, g_mlp, w_up, w_down, w_ple, w_ple_gate, g_final):
    bsz, t, d = x_prompt.shape
    dbsz, dec, _ = x_sample.shape
    depth = w_in.shape[0]
    past = page_table.shape[1] * PAGE_SIZE
    n_p = bsz * t
    n_s = dbsz * SEQ_PAD
    assert dec <= SEQ_PAD and dec >= CONV_W - 1 and (past % SEL_LEN) + dec <= SEL_LEN
    assert t % SEL_TILE == 0 and t >= WINDOW + Q_TILE and past % PAGE_SIZE == 0 and dbsz % SEQ_PER_STEP == 0

    x = jnp.concatenate([x_prompt.reshape(n_p, d), _pad_rows(x_sample, SEQ_PAD).reshape(n_s, d)], axis=0)
    p_all = jnp.concatenate([p_prompt.reshape(depth, n_p, PLE_DIM),
                             jnp.pad(p_sample, ((0, 0), (0, 0), (0, SEQ_PAD - dec), (0, 0))).reshape(depth, n_s, PLE_DIM)],
                            axis=1)

    nc_p = t // CMP_STRIDE - 1
    ov_p = _overlap_matrix(t // CMP_STRIDE, nc_p, t // SEL_LEN)
    e_p = jnp.asarray(_expand_matrix(t).reshape(LANES, t // SEL_TILE, SEL_TILE).transpose(1, 0, 2)).astype(BF16)
    nc_s = past // CMP_STRIDE - 1
    ov_s = _overlap_matrix(past // CMP_STRIDE, nc_s, -(-(past + dec) // SEL_LEN))
    e_s = jnp.asarray(_expand_matrix(past)).astype(BF16)
    n_phys = cache_nsa_kv.shape[1]
    wk = cache_nsa_win.shape[2]
    cache_t = jnp.transpose(cache_nsa_kv, (0, 1, 3, 4, 5, 2)).reshape(depth, n_phys, 4 * LANES, PAGE_SIZE)
    cwin_t = jnp.transpose(cache_nsa_win, (0, 1, 3, 4, 5, 2)).reshape(depth, dbsz, 2 * LANES, wk)
    cs_p, sn_p = _rope_tables(jnp.arange(t))
    cs_s, sn_s = _rope_tables(past + jnp.arange(SEQ_PAD))
    cs_s = jnp.tile(cs_s, (SEQ_PER_STEP, 1))
    sn_s = jnp.tile(sn_s, (SEQ_PER_STEP, 1))

    outs = [[] for _ in range(10)]
    for i in range(depth):
        w_in_p = _prep_w_in(w_in[i])
        wb = _prep_w_branch(w_branch[i])
        cw = _prep_compress(nsa_cmp_pe[i], nsa_cmp_w1[i], nsa_cmp_w2[i])
        par = jnp.zeros((SUBLANES, LANES), F32).at[0, :GDN_HEADS].set(gdn_a_log[i]).at[1, :GDN_HEADS].set(gdn_dt_bias[i])
        ng = gdn_norm_g[i].reshape(1, GDN_HD)

        z = in_proj(x, g_mix[i].reshape(1, d), w_in_p)
        kv_all = lax.slice(z, (0, ZKV), (n_p + n_s, ZNG))
        kv_p = kv_all[:n_p]
        kv_s = kv_all[n_p:].reshape(dbsz, SEQ_PAD, ZNG - ZKV)[:, :dec]

        kcmp_sub = kv_p[:, :2 * LANES].reshape(n_p // CMP_STRIDE, CMP_STRIDE * 2 * LANES)
        kcvc = compress_prompt(kcmp_sub, cw, bsz, t)
        kvb = kv_p[:, 2 * LANES:].astype(BF16)
        o_nsa_p = nsa_prompt(z, kcvc, kvb, e_p, ov_p, bsz, t)
        o_nsa_s = nsa_sample(z, page_table, cache_t, cwin_t, cw, e_s, ov_s, i, n_p, dbsz, past, dec)

        o_gdn_p, gs_p = gdn_prompt(z, gdn_conv_w[i], par, ng, bsz, t)
        prev_pad = _pad_rows(state_gdn_conv[i], SEQ_PAD).reshape(n_s, 3 * MIX_WIDTH)
        o_gdn_s, gs_s = gdn_sample(z, prev_pad, state_gdn[i], gdn_conv_w[i], par, ng, n_p, dbsz, dec)

        o_ret_p, rs_p = ret_prompt(z, cs_p, sn_p, bsz, t)
        o_ret_s, rs_s = ret_sample(z, cs_s, sn_s, state_ret[i], n_p, dbsz, dec)

        x = merge(x, (o_nsa_p, o_gdn_p, o_ret_p), (o_nsa_s, o_gdn_s, o_ret_s), z, wb, w_out[i].astype(BF16))
        x = mlp_ple(x, g_mlp[i].reshape(1, d), w_up[i].astype(BF16), w_down[i].astype(BF16), p_all[i],
                    w_ple[i].astype(BF16), w_ple_gate[i].astype(BF16), g_final.reshape(1, d),
                    final_norm=(i == depth - 1))

        nkv = ZKW - ZKV
        outs[0].append(kv_p[:, :nkv].reshape(bsz, t, 4, NSA_KV, NSA_HD))
        outs[1].append(kv_s[:, :, :nkv].reshape(dbsz, dec, 4, NSA_KV, NSA_HD))
        wlen = min(WINDOW, t)
        outs[2].append(kv_p.reshape(bsz, t, ZNG - ZKV)[:, t - wlen:, nkv:].reshape(bsz, wlen, 2, NSA_KV, NSA_HD))
        win = jnp.concatenate([cache_nsa_win[i], kv_s[:, :, nkv:].reshape(dbsz, dec, 2, NSA_KV, NSA_HD)], axis=1)
        wlen_s = min(WINDOW, past + dec)
        outs[3].append(win[:, win.shape[1] - wlen_s:])
        nconv = CONV_W - 1
        outs[4].append(jnp.stack([lax.slice(z, (b * t + t - nconv, ZGQ), (b * t + t, ZGQ + 3 * MIX_WIDTH))
                                  for b in range(bsz)]))
        gq_s = lax.slice(z, (n_p, ZGQ), (n_p + n_s, ZGQ + 3 * MIX_WIDTH)).reshape(dbsz, SEQ_PAD, 3 * MIX_WIDTH)
        outs[5].append(gq_s[:, dec - nconv:dec])
        outs[6].append(gs_p)
        outs[7].append(gs_s)
        outs[8].append(rs_p)
        outs[9].append(rs_s)

    y_prompt = x[:n_p].reshape(bsz, t, d)
    y_sample = x[n_p:].reshape(dbsz, SEQ_PAD, d)[:, :dec]
    return (y_prompt, y_sample) + tuple(jnp.stack(o) for o in outs)
```

```python
import functools
import math

import numpy as np
import jax
import jax.numpy as jnp
from jax import lax
from jax.experimental import pallas as pl
from jax.experimental.pallas import tpu as pltpu

F32 = jnp.float32
BF16 = jnp.bfloat16

D_MODEL = 1024
MIX_WIDTH = D_MODEL // 2
N_BRANCH = 3
NSA_HEADS = 8
NSA_HD = 64
NSA_KV = 2
NSA_GRP = NSA_HEADS // NSA_KV
CMP_LEN = 32
CMP_STRIDE = 16
CMP_HIDDEN = 64
SEL_LEN = 64
SEL_TOPK = 16
WINDOW = 512
FORCE_BONUS = 1000.0
GDN_HEADS = 4
GDN_HD = 128
CONV_W = 4
GDN_CHUNK = 64
RET_HEADS = 4
RET_HD = 128
RET_CHUNK = 128
ROPE_BASE = 10000.0
D_FF = 4 * D_MODEL
PLE_DIM = 256
EPS = 1e-6
NEG = -1e30
PAGE_SIZE = 128

LANES = 128
SUBLANES = 8
SEQ_PAD = SUBLANES
VMEM_LIMIT = 56 * 1024 * 1024

ZQ = 0
ZKV = 1024
ZKW = 1536
ZNG = 1792
ZGAB = 1920
ZGZ = 2048
ZRG = 2560
ZGQ = 3072
ZRQ = 4608
ZMG = 6144
ZW = 9216


def _sigmoid(x):
    return 1.0 / (1.0 + jnp.exp(-x))


def _silu(x):
    return x * _sigmoid(x)


def _softplus(x):
    return jnp.maximum(x, 0.0) + jnp.log1p(jnp.exp(-jnp.abs(x)))


def _gelu_tanh(x):
    return 0.5 * x * (1.0 + jnp.tanh(math.sqrt(2.0 / math.pi) * (x + 0.044715 * (x * x * x))))


def _rms(x, g):
    return x * lax.rsqrt(jnp.mean(x * x, axis=-1, keepdims=True) + EPS) * g


def _dot(a, b):
    return jnp.dot(a, b, preferred_element_type=F32)


def _dot_nt(a, b):
    return lax.dot_general(a, b, (((1,), (1,)), ((), ())), preferred_element_type=F32)


def _bdot(a, b):
    return _dot(a.astype(BF16), b.astype(BF16))


def _bdot_nt(a, b):
    return _dot_nt(a.astype(BF16), b.astype(BF16))


def _split2(a):
    hi = a.astype(BF16)
    lo = (a - hi.astype(F32)).astype(BF16)
    return hi, lo


def _dot3(a, b):
    ah, al = _split2(a)
    bh, bl = _split2(b)
    return _dot(ah, bh) + (_dot(ah, bl) + _dot(al, bh))


def _dot_mask_left(m01, x):
    m = m01.astype(BF16)
    x1 = x.astype(BF16)
    r1 = x - x1.astype(F32)
    x2 = r1.astype(BF16)
    x3 = (r1 - x2.astype(F32)).astype(BF16)
    return _dot(m, x1) + (_dot(m, x2) + _dot(m, x3))


def _pick_tile(n, prefs):
    for p in prefs:
        if n % p == 0:
            return p
    raise ValueError(f"no tile in {prefs} divides {n}")


def _cparams(sem):
    return pltpu.CompilerParams(dimension_semantics=sem, vmem_limit_bytes=VMEM_LIMIT)


def _in_kernel(x_ref, g_ref, w_ref, z_ref, h_ref):
    @pl.when(pl.program_id(1) == 0)
    def _():
        h_ref[...] = _rms(x_ref[...], g_ref[...]).astype(BF16)

    z_ref[...] = _dot(h_ref[...], w_ref[...])


def in_proj(x, g, w):
    n = x.shape[0]
    tm = _pick_tile(n, (1024, 512, 256, 128))
    tn = 1024
    return pl.pallas_call(
        _in_kernel,
        grid=(n // tm, ZW // tn),
        in_specs=[pl.BlockSpec((tm, D_MODEL), lambda i, j: (i, 0)),
                  pl.BlockSpec((1, D_MODEL), lambda i, j: (0, 0)),
                  pl.BlockSpec((D_MODEL, tn), lambda i, j: (0, j))],
        out_specs=pl.BlockSpec((tm, tn), lambda i, j: (i, j)),
        out_shape=jax.ShapeDtypeStruct((n, ZW), F32),
        scratch_shapes=[pltpu.VMEM((tm, D_MODEL), BF16)],
        compiler_params=_cparams(("parallel", "arbitrary")),
        name="in_proj",
    )(x, g, w)


def _merge_kernel(x_ref, onp_ref, ogp_ref, orp_ref, ons_ref, ogs_ref, ors_ref, mg_ref, wb_ref, wo_ref, out_ref,
                  *, n_prompt_tiles):
    is_prompt = pl.program_id(0) < n_prompt_tiles
    m = None
    for b, (op_ref, os_ref) in enumerate(((onp_ref, ons_ref), (ogp_ref, ogs_ref), (orp_ref, ors_ref))):
        o = jnp.where(is_prompt, op_ref[...], os_ref[...])
        br = _dot(o.astype(BF16), wb_ref[b])
        t = _sigmoid(mg_ref[:, b * D_MODEL:(b + 1) * D_MODEL]) * br
        m = t if m is None else m + t
    out_ref[...] = x_ref[...] + _dot(m.astype(BF16), wo_ref[...])


def merge(x, o_prompt, o_sample, z, wb, wo):
    n = x.shape[0]
    n_p = o_prompt[0].shape[0]
    n_s = o_sample[0].shape[0]
    tm = _pick_tile(math.gcd(n_p, n_s), (512, 256, 128))
    npt = n_p // tm
    row = lambda i: (i, 0)
    prow = lambda i: (jnp.minimum(i, npt - 1), 0)
    srow = lambda i: (jnp.maximum(i - npt, 0), 0)
    return pl.pallas_call(
        functools.partial(_merge_kernel, n_prompt_tiles=npt),
        grid=(n // tm,),
        in_specs=[pl.BlockSpec((tm, D_MODEL), row)]
                 + [pl.BlockSpec((tm, MIX_WIDTH), prow)] * N_BRANCH
                 + [pl.BlockSpec((tm, MIX_WIDTH), srow)] * N_BRANCH
                 + [
                  pl.BlockSpec((tm, N_BRANCH * D_MODEL), lambda i: (i, ZMG // (N_BRANCH * D_MODEL))),
                  pl.BlockSpec((N_BRANCH, MIX_WIDTH, D_MODEL), lambda i: (0, 0, 0)),
                  pl.BlockSpec((D_MODEL, D_MODEL), lambda i: (0, 0))],
        out_specs=pl.BlockSpec((tm, D_MODEL), row),
        out_shape=jax.ShapeDtypeStruct((n, D_MODEL), F32),
        compiler_params=_cparams(("parallel",)),
        name="merge",
    )(x, *o_prompt, *o_sample, z, wb, wo)


def _mlp_kernel(x_ref, g_ref, wu_ref, wd_ref, p_ref, wp_ref, wpg_ref, gf_ref, out_ref, h_ref, acc_ref,
                *, final_norm):
    f = pl.program_id(1)

    @pl.when(f == 0)
    def _():
        h_ref[...] = _rms(x_ref[...], g_ref[...]).astype(BF16)
        acc_ref[...] = jnp.zeros_like(acc_ref)

    up = jnp.maximum(_dot(h_ref[...], wu_ref[...]), 0.0)
    acc_ref[...] += _dot((up * up).astype(BF16), wd_ref[...])

    @pl.when(f == pl.num_programs(1) - 1)
    def _():
        x2 = x_ref[...] + acc_ref[...]
        ple = _dot(p_ref[...].astype(BF16), wp_ref[...])
        x3 = x2 + ple * _sigmoid(_dot(x2.astype(BF16), wpg_ref[...]))
        if final_norm:
            x3 = _rms(x3, gf_ref[...])
        out_ref[...] = x3


def mlp_ple(x, g, wu, wd, p, wp, wpg, gf, final_norm):
    n = x.shape[0]
    tm = _pick_tile(n, (512, 256, 128))
    tf = 1024
    return pl.pallas_call(
        functools.partial(_mlp_kernel, final_norm=final_norm),
        grid=(n // tm, D_FF // tf),
        in_specs=[pl.BlockSpec((tm, D_MODEL), lambda i, f: (i, 0)),
                  pl.BlockSpec((1, D_MODEL), lambda i, f: (0, 0)),
                  pl.BlockSpec((D_MODEL, tf), lambda i, f: (0, f)),
                  pl.BlockSpec((tf, D_MODEL), lambda i, f: (f, 0)),
                  pl.BlockSpec((tm, PLE_DIM), lambda i, f: (i, 0)),
                  pl.BlockSpec((PLE_DIM, D_MODEL), lambda i, f: (0, 0)),
                  pl.BlockSpec((D_MODEL, D_MODEL), lambda i, f: (0, 0)),
                  pl.BlockSpec((1, D_MODEL), lambda i, f: (0, 0))],
        out_specs=pl.BlockSpec((tm, D_MODEL), lambda i, f: (i, 0)),
        out_shape=jax.ShapeDtypeStruct((n, D_MODEL), F32),
        scratch_shapes=[pltpu.VMEM((tm, D_MODEL), BF16), pltpu.VMEM((tm, D_MODEL), F32)],
        compiler_params=_cparams(("parallel", "arbitrary")),
        name="mlp_ple",
    )(x, g, wu, wd, p, wp, wpg, gf)


def _compress_math(load_sub, pea_ref, peb_ref, w1a_ref, w1b_ref, w2_ref, nsub):
    acc_a = jnp.zeros((nsub, 4 * CMP_HIDDEN), F32)
    acc_b = jnp.zeros((nsub, 4 * CMP_HIDDEN), F32)
    for j in range(CMP_STRIDE):
        xj = load_sub(j)
        acc_a = acc_a + _dot((xj + pea_ref[j:j + 1, :]).astype(BF16), w1a_ref[j])
        acc_b = acc_b + _dot((xj + peb_ref[j:j + 1, :]).astype(BF16), w1b_ref[j])
    hid = _gelu_tanh(acc_a + pltpu.roll(acc_b, nsub - 1, axis=0))
    return _dot(hid.astype(BF16), w2_ref[...])


def _cmp_kernel(x_ref, pea_ref, peb_ref, w1a_ref, w1b_ref, w2_ref, out_ref, *, nsub):
    load = lambda j: x_ref[:, j * 256:(j + 1) * 256]
    out_ref[0] = _compress_math(load, pea_ref, peb_ref, w1a_ref, w1b_ref, w2_ref, nsub)


def compress_prompt(kcmp_sub, cw, bsz, t):
    nsub = t // CMP_STRIDE
    const2 = lambda b: (0, 0)
    const3 = lambda b: (0, 0, 0)
    return pl.pallas_call(
        functools.partial(_cmp_kernel, nsub=nsub),
        grid=(bsz,),
        in_specs=[pl.BlockSpec((nsub, CMP_STRIDE * 256), lambda b: (b, 0)),
                  pl.BlockSpec((CMP_STRIDE, 256), const2),
                  pl.BlockSpec((CMP_STRIDE, 256), const2),
                  pl.BlockSpec((CMP_STRIDE, 256, 256), const3),
                  pl.BlockSpec((CMP_STRIDE, 256, 256), const3),
                  pl.BlockSpec((256, 256), const2)],
        out_specs=pl.BlockSpec((1, nsub, 256), lambda b: (b, 0, 0)),
        out_shape=jax.ShapeDtypeStruct((bsz, nsub, 256), F32),
        compiler_params=_cparams(("parallel",)),
        name="nsa_compress",
    )(kcmp_sub, cw["pea"], cw["peb"], cw["w1a"], cw["w1b"], cw["w2"])


def _topk_mask(scores, nblk, k):
    shape = scores[0].shape
    if shape[0] == LANES:
        return _topk_mask_square(scores, k)
    lane = lax.broadcasted_iota(jnp.int32, shape, 1)
    cnts = [jnp.zeros(shape, F32) for _ in scores]
    for j in range(nblk):
        for i, score in enumerate(scores):
            cj = score[:, j:j + 1]
            ge = jnp.where(cj >= score, 1.0, 0.0)
            gt = jnp.where(cj > score, 1.0, 0.0)
            cnts[i] = cnts[i] + jnp.where(lane > j, ge, gt)
    return [cnt < float(k) for cnt in cnts]


def _topk_cols(scs, k):
    shape = scs[0].shape
    idx = lax.broadcasted_iota(jnp.int32, shape, 0).astype(F32)
    picked = [jnp.zeros(shape, F32) for _ in scs]
    for _ in range(k):
        mxs = [jnp.max(sc, axis=0, keepdims=True) for sc in scs]
        firsts = [jnp.min(jnp.where(sc == mx, idx, float(shape[0])), axis=0, keepdims=True)
                  for sc, mx in zip(scs, mxs)]
        hits = [idx == first for first in firsts]
        picked = [jnp.where(hit, 1.0, pk) for hit, pk in zip(hits, picked)]
        scs = [jnp.where(hit, 3.0 * NEG, sc) for hit, sc in zip(hits, scs)]
    return picked


def _topk_mask_square(scores, k):
    picked = _topk_cols([jnp.transpose(s) for s in scores], k)
    return [jnp.transpose(p) > 0.5 for p in picked]


def _tile_rows(a, reps):
    return jnp.concatenate([a] * reps, axis=0)


def _nsa_queries(zq_ref, g):
    scale = NSA_HD ** -0.5
    q = jnp.concatenate([zq_ref[:, (NSA_GRP * g + h) * LANES:(NSA_GRP * g + h + 1) * LANES]
                         for h in range(NSA_GRP)], axis=0)
    return (q * scale).astype(BF16)


def _nsa_compressed_and_select(qgs, kc, vc, m_c, ov, pos1, ns, r):
    bias = _tile_rows(jnp.where(m_c, 0.0, NEG), NSA_GRP)
    ss = [_dot_nt(qg, kc) + bias for qg in qgs]
    mxs = [jnp.max(s, axis=-1, keepdims=True) for s in ss]
    ps = [jnp.exp(s - mx) for s, mx in zip(ss, mxs)]
    ps = [p / jnp.sum(p, axis=-1, keepdims=True) * jnp.where(mx > 0.5 * NEG, 1.0, 0.0) for p, mx in zip(ps, mxs)]
    o_cs = [_dot(p.astype(BF16), vc) for p in ps]
    psums = [(p[0:r] + p[r:2 * r]) + (p[2 * r:3 * r] + p[3 * r:4 * r]) for p in ps]
    imps = [_dot_mask_left_t(psum, ov) for psum in psums]
    blk = lax.broadcasted_iota(jnp.int32, (r, LANES), 1)
    valid = blk * SEL_LEN <= pos1
    cur = pos1 // SEL_LEN
    bonus = jnp.where((blk == 0) | (blk == cur) | (blk == cur - 1), FORCE_BONUS, 0.0)
    scores = [jnp.where(valid, imp + bonus, NEG) for imp in imps]
    sels = [sel & valid for sel in _topk_mask(scores, min(ns, LANES), min(SEL_TOPK, ns))]
    return o_cs, sels


def _dot_mask_left_t(x, m01):
    m = m01.astype(BF16)
    x1 = x.astype(BF16)
    r1 = x - x1.astype(F32)
    x2 = r1.astype(BF16)
    x3 = (r1 - x2.astype(F32)).astype(BF16)
    return _dot(x1, m) + (_dot(x2, m) + _dot(x3, m))


def _gate_cols(gates, g, c, r):
    return jnp.concatenate([gates[:, 3 * (NSA_GRP * g + h) + c:3 * (NSA_GRP * g + h) + c + 1]
                            for h in range(NSA_GRP)], axis=0)


def _nsa_assemble(o0, o1, r):
    lane = lax.broadcasted_iota(jnp.int32, (r, LANES), 1)
    return jnp.concatenate([jnp.where(lane < NSA_HD, o0[j * r:(j + 1) * r], o1[j * r:(j + 1) * r])
                            for j in range(NSA_GRP)], axis=1)


SEL_TILE = 1024
Q_TILE = 128


KV_KA = 0
KV_VA = 2 * LANES
KV_KW = 4 * LANES
KV_VWA = 5 * LANES
KV_COLS = 7 * LANES
SEL_LANE = NSA_HD


def _nsa_prompt_kernel(zq_ref, zg_ref, kcvc_ref, kv_ref, shift_ref, ov_ref, out_ref, *, t):
    r = Q_TILE
    groups = range(NSA_KV)
    qb = pl.program_id(1)
    s0 = qb * r
    pos1 = s0 + lax.broadcasted_iota(jnp.int32, (r, 1), 0)
    nc = t // CMP_STRIDE
    ns = t // SEL_LEN
    kc = kcvc_ref[0][:, 0:LANES].astype(BF16)
    vc = kcvc_ref[0][:, LANES:2 * LANES].astype(BF16)
    ncol = lax.broadcasted_iota(jnp.int32, (1, nc), 1)
    m_c = ((ncol * CMP_STRIDE + (CMP_LEN - 1)) <= pos1) & (ncol < nc - 1)
    gates = _sigmoid(zg_ref[...])
    ov = ov_ref[...]

    qs = [_nsa_queries(zq_ref, g) for g in groups]
    ocs, sels = _nsa_compressed_and_select(qs, kc, vc, m_c, ov, pos1, ns, r)
    sels = [jnp.where(sel, 1.0, 0.0).astype(BF16) for sel in sels]

    scale = NSA_HD ** -0.5
    qsel = []
    for g in groups:
        heads = [zq_ref[:, (NSA_GRP * g + h) * LANES:(NSA_GRP * g + h + 1) * LANES] for h in range(NSA_GRP)]
        if g == 1:
            heads = [pltpu.roll(q, NSA_HD, axis=1) for q in heads]
        qsel.append(jnp.concatenate(heads, axis=0) * scale)
    lane = lax.broadcasted_iota(jnp.int32, (1, LANES), 1)
    ind = jnp.where((lane >= SEL_LANE) & (lane < SEL_LANE + SEL_TILE // SEL_LEN), 1.0, 0.0)

    def tile_step(kt, carry, diagonal):
        k0 = pl.multiple_of(kt * SEL_TILE, SEL_TILE)
        shift = shift_ref[kt]
        va = kv_ref[pl.ds(k0, SEL_TILE), KV_VA:KV_VA + 2 * LANES]
        qas = [(qsel[g] + _tile_rows(NEG * (ind - _dot(sels[g], shift)), NSA_GRP)).astype(BF16) for g in groups]
        ss = [_dot_nt(qas[g], kv_ref[pl.ds(k0, SEL_TILE), KV_KA + g * LANES:KV_KA + (g + 1) * LANES])
              for g in groups]
        if diagonal:
            kpos = k0 + lax.broadcasted_iota(jnp.int32, (1, SEL_TILE), 1)
            causal = _tile_rows(jnp.where(kpos <= pos1, 0.0, NEG), NSA_GRP)
            ss = [s + causal for s in ss]
        m_news = [jnp.maximum(carry[g][0], jnp.max(ss[g], axis=-1, keepdims=True)) for g in groups]
        alphas = [jnp.exp(carry[g][0] - m_news[g]) for g in groups]
        ps = [jnp.exp((ss[g] - m_news[g]).astype(BF16)) for g in groups]
        accs = [alphas[g] * carry[g][1] + _dot(ps[g], va) for g in groups]
        return tuple((m_news[g], accs[g]) for g in groups)

    init = tuple((jnp.full((NSA_GRP * r, 1), NEG, F32), jnp.zeros((NSA_GRP * r, 2 * LANES), F32)) for _ in groups)
    n_full = s0 // SEL_TILE
    mid = lax.fori_loop(0, n_full, lambda kt, c: tile_step(kt, c, False), init)
    fin = tile_step(n_full, mid, True)

    span = WINDOW + r
    start = pl.multiple_of(jnp.maximum(s0 - WINDOW, 0), r)
    kw = kv_ref[pl.ds(start, span), KV_KW:KV_KW + LANES]
    vwa = kv_ref[pl.ds(start, span), KV_VWA:KV_VWA + 2 * LANES]
    kposw = start + lax.broadcasted_iota(jnp.int32, (1, span), 1)
    bias_w = _tile_rows(jnp.where((kposw <= pos1) & (kposw > pos1 - WINDOW), 0.0, NEG), NSA_GRP)

    o_ss = [fin[g][1][:, 0:LANES] / fin[g][1][:, LANES:LANES + 1] for g in groups]
    ss = [_dot_nt(qs[g], kw) + bias_w for g in groups]
    ps = [jnp.exp((s - jnp.max(s, axis=-1, keepdims=True)).astype(BF16)) for s in ss]
    ows = [_dot(p, vwa) for p in ps]
    o_ws = [ow[:, 0:LANES] / ow[:, LANES:LANES + 1] for ow in ows]
    outs = [_gate_cols(gates, g, 0, r) * ocs[g] + _gate_cols(gates, g, 1, r) * o_ss[g]
            + _gate_cols(gates, g, 2, r) * o_ws[g] for g in groups]
    out_ref[...] = _nsa_assemble(outs[0], outs[1], r)


def nsa_prompt(z, kcvc, kvb, shift, ov, bsz, t):
    nqb = t // Q_TILE
    return pl.pallas_call(
        functools.partial(_nsa_prompt_kernel, t=t),
        grid=(bsz, nqb),
        in_specs=[pl.BlockSpec((Q_TILE, NSA_HEADS * LANES), lambda b, q: (b * nqb + q, 0)),
                  pl.BlockSpec((Q_TILE, LANES), lambda b, q: (b * nqb + q, ZNG // LANES)),
                  pl.BlockSpec((1, t // CMP_STRIDE, 256), lambda b, q: (b, 0, 0)),
                  pl.BlockSpec((t, KV_COLS), lambda b, q: (b, 0)),
                  pl.BlockSpec((t // SEL_TILE, LANES, LANES), lambda b, q: (0, 0, 0)),
                  pl.BlockSpec((t // CMP_STRIDE, LANES), lambda b, q: (0, 0))],
        out_specs=pl.BlockSpec((Q_TILE, MIX_WIDTH), lambda b, q: (b * nqb + q, 0)),
        out_shape=jax.ShapeDtypeStruct((bsz * t, MIX_WIDTH), F32),
        compiler_params=_cparams(("parallel", "arbitrary")),
        name="nsa_prompt",
    )(z, z, kcvc, kvb, shift, ov)


def _prompt_kv_slab(kv_p, t):
    n = kv_p.shape[0]
    blocks_per_tile = SEL_TILE // SEL_LEN
    blk = ((jnp.arange(n, dtype=jnp.int32) % t) // SEL_LEN) % blocks_per_tile
    onehot = (jnp.arange(LANES - NSA_HD, dtype=jnp.int32)[None, :] == blk[:, None]).astype(F32)
    ones_col = jnp.broadcast_to((jnp.arange(LANES, dtype=jnp.int32) == 0).astype(F32)[None, :], (n, LANES))
    k_slc = kv_p[:, 2 * LANES:3 * LANES]
    parts = [k_slc[:, 0:NSA_HD], onehot, k_slc[:, NSA_HD:2 * NSA_HD], onehot,
             kv_p[:, 3 * LANES:4 * LANES], ones_col, kv_p[:, 4 * LANES:5 * LANES],
             kv_p[:, 5 * LANES:6 * LANES], ones_col]
    return jnp.concatenate(parts, axis=1).astype(BF16)


def _shift_matrices(t):
    blocks_per_tile = SEL_TILE // SEL_LEN
    out = np.zeros((t // SEL_TILE, LANES, LANES), np.float32)
    for b in range(min(t // SEL_LEN, LANES)):
        out[b // blocks_per_tile, b, SEL_LANE + b % blocks_per_tile] = 1.0
    return jnp.asarray(out).astype(BF16)


def _nsa_sample_kernel(pt_ref, zq_ref, zkv_ref, cwin_ref, cache_ref, pea_ref, peb_ref, w1a_ref, w1b_ref,
                       w2_ref, e_ref, ov_ref, out_ref, buf_ref, tokm_ref, sem_ref, *, layer, past, n_pages, dec):
    r = SEQ_PAD
    b = pl.program_id(0)
    nb = pl.num_programs(0)
    slot = lax.rem(b, 2)

    def page_copy(seq, sl, p):
        return pltpu.make_async_copy(cache_ref.at[layer, pt_ref[seq * n_pages + p]],
                                     buf_ref.at[sl, :, pl.ds(p * PAGE_SIZE, PAGE_SIZE)],
                                     sem_ref.at[sl])

    def start_all(seq, sl):
        for p in range(n_pages):
            page_copy(seq, sl, p).start()

    @pl.when(b == 0)
    def _():
        start_all(b, slot)

    @pl.when(b + 1 < nb)
    def _():
        start_all(b + 1, 1 - slot)

    for p in range(n_pages):
        page_copy(b, slot, p).wait()

    nsub = past // CMP_STRIDE
    nc = nsub - 1
    ns = -(-(past + dec) // SEL_LEN)
    for c0 in range(2):
        for p in range(past // LANES):
            tokm_ref[c0, p * LANES:(p + 1) * LANES, :] = jnp.transpose(
                buf_ref[slot, c0 * LANES:(c0 + 1) * LANES, p * LANES:(p + 1) * LANES])
    load = lambda j: jnp.concatenate([tokm_ref[0, pl.ds(j, nsub, stride=CMP_STRIDE), :],
                                      tokm_ref[1, pl.ds(j, nsub, stride=CMP_STRIDE), :]], axis=1)
    kcvc = _compress_math(load, pea_ref, peb_ref, w1a_ref, w1b_ref, w2_ref, nsub)
    kc = kcvc[:, 0:LANES].astype(BF16)
    vc = kcvc[:, LANES:2 * LANES].astype(BF16)

    row = lax.broadcasted_iota(jnp.int32, (r, 1), 0)
    pos1 = past + row
    ncol = lax.broadcasted_iota(jnp.int32, (1, nsub), 1)
    m_c = ((ncol * CMP_STRIDE + (CMP_LEN - 1)) <= pos1) & (ncol < nc)
    gates = _sigmoid(zkv_ref[:, ZNG - ZKV:ZNG - ZKV + LANES])
    ov = ov_ref[...]

    k_sel = buf_ref[slot, 2 * LANES:3 * LANES, :].astype(BF16)
    v_sel = buf_ref[slot, 3 * LANES:4 * LANES, :].astype(BF16)
    kn_sel = zkv_ref[:, 2 * LANES:3 * LANES].astype(BF16)
    vn_sel = zkv_ref[:, 3 * LANES:4 * LANES].astype(BF16)
    k_win = cwin_ref[0, 0, 0:LANES, :].astype(BF16)
    v_win = cwin_ref[0, 0, LANES:2 * LANES, :].astype(BF16)
    kn_win = zkv_ref[:, ZKW - ZKV:ZKW - ZKV + LANES].astype(BF16)
    vn_win = zkv_ref[:, ZKW - ZKV + LANES:ZKW - ZKV + 2 * LANES].astype(BF16)

    kpos_p = lax.broadcasted_iota(jnp.int32, (1, past), 1)
    tnew = lax.broadcasted_iota(jnp.int32, (1, r), 1)
    kpos_n = past + tnew
    wk = cwin_ref.shape[3]
    kpos_w = past - wk + lax.broadcasted_iota(jnp.int32, (1, wk), 1)
    in_win = lambda kp: (kp <= pos1) & (kp > pos1 - WINDOW)
    bias_wp = _tile_rows(jnp.where(in_win(kpos_w), 0.0, NEG), NSA_GRP)
    bias_wn = _tile_rows(jnp.where(in_win(kpos_n) & (tnew < dec), 0.0, NEG), NSA_GRP)
    new_blk = past // SEL_LEN

    def two_part_attention(qg, kp_t, vp_t, bias_p, kn, vn, bias_n):
        s_p = _dot(qg, kp_t) + bias_p
        s_n = _dot_nt(qg, kn) + bias_n
        mx = jnp.maximum(jnp.max(s_p, axis=-1, keepdims=True), jnp.max(s_n, axis=-1, keepdims=True))
        p_p = jnp.exp(s_p - mx)
        p_n = jnp.exp(s_n - mx)
        den = jnp.sum(p_p, axis=-1, keepdims=True) + jnp.sum(p_n, axis=-1, keepdims=True)
        return (_dot_nt(p_p.astype(BF16), vp_t) + _dot(p_n.astype(BF16), vn)) / den

    outs = []
    qgs = [_nsa_queries(zq_ref, g) for g in range(NSA_KV)]
    o_cs, sels = _nsa_compressed_and_select(qgs, kc, vc, m_c, ov, pos1, ns, r)
    for g in range(NSA_KV):
        qg, o_c = qgs[g], o_cs[g]
        self = jnp.where(sels[g], 1.0, 0.0)
        allow_p = (_dot(self.astype(BF16), e_ref[...]) > 0.5) & (kpos_p <= pos1)
        allow_n = (self[:, new_blk:new_blk + 1] > 0.5) & (kpos_n <= pos1) & (tnew < dec)
        o_s = two_part_attention(qg, k_sel, v_sel, _tile_rows(jnp.where(allow_p, 0.0, NEG), NSA_GRP),
                                 kn_sel, vn_sel, _tile_rows(jnp.where(allow_n, 0.0, NEG), NSA_GRP))
        o_w = two_part_attention(qg, k_win, v_win, bias_wp, kn_win, vn_win, bias_wn)
        outs.append(_gate_cols(gates, g, 0, r) * o_c + _gate_cols(gates, g, 1, r) * o_s
                    + _gate_cols(gates, g, 2, r) * o_w)
    out_ref[...] = _nsa_assemble(outs[0], outs[1], r)


def nsa_sample(z, page_table, cache_t, cache_win_t, cw, e_mat, ov, layer, row0, dbsz, past, dec):
    n_pages = past // PAGE_SIZE
    rb0 = row0 // SEQ_PAD
    wk = cache_win_t.shape[3]
    c2 = lambda b, pt: (0, 0)
    c3 = lambda b, pt: (0, 0, 0)
    grid_spec = pltpu.PrefetchScalarGridSpec(
        num_scalar_prefetch=1,
        grid=(dbsz,),
        in_specs=[pl.BlockSpec((SEQ_PAD, NSA_HEADS * LANES), lambda b, pt: (rb0 + b, 0)),
                  pl.BlockSpec((SEQ_PAD, 1024), lambda b, pt: (rb0 + b, ZKV // 1024)),
                  pl.BlockSpec((1, 1, 2 * LANES, wk), lambda b, pt: (layer, b, 0, 0)),
                  pl.BlockSpec(memory_space=pl.ANY),
                  pl.BlockSpec((CMP_STRIDE, 256), c2),
                  pl.BlockSpec((CMP_STRIDE, 256), c2),
                  pl.BlockSpec((CMP_STRIDE, 256, 256), c3),
                  pl.BlockSpec((CMP_STRIDE, 256, 256), c3),
                  pl.BlockSpec((256, 256), c2),
                  pl.BlockSpec((LANES, past), c2),
                  pl.BlockSpec((past // CMP_STRIDE, LANES), c2)],
        out_specs=pl.BlockSpec((SEQ_PAD, MIX_WIDTH), lambda b, pt: (b, 0)),
        scratch_shapes=[pltpu.VMEM((2, 4 * LANES, past), F32), pltpu.VMEM((2, past, LANES), F32),
                        pltpu.SemaphoreType.DMA((2,))],
    )
    return pl.pallas_call(
        functools.partial(_nsa_sample_kernel, layer=layer, past=past, n_pages=n_pages, dec=dec),
        grid_spec=grid_spec,
        out_shape=jax.ShapeDtypeStruct((dbsz * SEQ_PAD, MIX_WIDTH), F32),
        compiler_params=_cparams(("arbitrary",)),
        name="nsa_sample",
    )(page_table.reshape(-1), z, z, cache_win_t, cache_t, cw["pea"], cw["peb"], cw["w1a"], cw["w1b"], cw["w2"],
      e_mat, ov)


def _seg_masks(r, c):
    ri = lax.broadcasted_iota(jnp.int32, (r, r), 0)
    ci = lax.broadcasted_iota(jnp.int32, (r, r), 1)
    same = (ri // c) == (ci // c)
    return same, same & (ri >= ci), same & (ri > ci)


def _gdn_prep(y, ab, par_ref, valid, c, u_ref, w_ref, qe_ref, qk_ref, kdt_ref, eg_ref):
    r = y.shape[0]
    same, incl, strict = _seg_masks(r, c)
    y = _silu(y)
    beta_all = _sigmoid(ab)
    g_all = -jnp.exp(par_ref[0:1, :]) * _softplus(ab + par_ref[1:2, :])
    if valid is not None:
        y = jnp.where(valid, y, 0.0)
        beta_all = jnp.where(valid, beta_all, 0.0)
        g_all = jnp.where(valid, g_all, 0.0)
    gcum_all = _dot_mask_left(jnp.where(incl, 1.0, 0.0), g_all)
    glast_all = _dot_mask_left(jnp.where(same, 1.0, 0.0), g_all)
    gcum_t = jnp.transpose(gcum_all)
    eye = jnp.where(lax.broadcasted_iota(jnp.int32, (r, r), 0) == lax.broadcasted_iota(jnp.int32, (r, r), 1),
                    1.0, 0.0)
    n_double = int(math.log2(c))
    tms, pws, rhss = [], [], []
    for h in range(GDN_HEADS):
        q = y[:, h * GDN_HD:(h + 1) * GDN_HD]
        k = y[:, MIX_WIDTH + h * GDN_HD:MIX_WIDTH + (h + 1) * GDN_HD]
        v = y[:, 2 * MIX_WIDTH + h * GDN_HD:2 * MIX_WIDTH + (h + 1) * GDN_HD]
        q = q * lax.rsqrt(jnp.sum(q * q, axis=-1, keepdims=True) + EPS) * (GDN_HD ** -0.5)
        k = k * lax.rsqrt(jnp.sum(k * k, axis=-1, keepdims=True) + EPS)
        g1 = jnp.broadcast_to(gcum_all[:, h:h + 1], (r, GDN_HD))
        g2 = jnp.broadcast_to(gcum_t[h:h + 1, :], (r, r))
        gl = jnp.broadcast_to(glast_all[:, h:h + 1], (r, GDN_HD))
        beta = jnp.broadcast_to(beta_all[:, GDN_HEADS + h:GDN_HEADS + h + 1], (r, GDN_HD))
        g1r = g1 if r == GDN_HD else jnp.broadcast_to(gcum_all[:, h:h + 1], (r, r))
        decay = jnp.where(incl, jnp.exp(jnp.where(incl, g1r - g2, 0.0)), 0.0)
        betar = beta if r == GDN_HD else jnp.broadcast_to(beta_all[:, GDN_HEADS + h:GDN_HEADS + h + 1], (r, r))
        a = jnp.where(strict, _bdot_nt(k, k) * decay * betar, 0.0)
        tms.append(eye - a)
        pws.append(a)
        eg1 = jnp.exp(g1)
        rhss.append(jnp.concatenate([v * beta, k * (beta * eg1)], axis=1))
        qe_ref[h] = q * eg1
        qk_ref[h] = jnp.where(incl, _bdot_nt(q, k) * decay, 0.0)
        kdt_ref[h] = jnp.transpose(k * jnp.exp(gl - g1))
        eg_ref[h] = jnp.exp(gl)
    for _ in range(n_double - 1):
        pws = [_dot3(pw, pw) for pw in pws]
        tms = [tm + _dot3(tm, pw) for tm, pw in zip(tms, pws)]
    for h in range(GDN_HEADS):
        sol = _dot3(tms[h], rhss[h])
        u_ref[h] = sol[:, 0:GDN_HD]
        w_ref[h] = sol[:, GDN_HD:2 * GDN_HD]


def _gdn_segment(h, row0, seg, c, s, u_ref, w_ref, qe_ref, qk_ref, kdt_ref, eg_ref, vn_ref, col_seg):
    rows = pl.ds(row0, c)
    vn = u_ref[h, rows, :] - _bdot(w_ref[h, rows, :], s)
    vn_ref[h, rows, :] = vn
    vn_all = vn_ref[h].astype(BF16)
    o = _bdot(qe_ref[h, rows, :], s) + _dot(qk_ref[h, rows, :].astype(BF16), vn_all)
    kdt = jnp.where(col_seg == seg, kdt_ref[h], 0.0)
    s_new = s * eg_ref[h, pl.ds(row0, 1), :] + _dot(kdt.astype(BF16), vn_all)
    return o, s_new


def _gdn_finish(o, zg, ng_ref):
    return _rms(o, ng_ref[...]) * _silu(zg)


GDN_TILE = 128


def _gdn_prompt_kernel(x_ref, ab_ref, zg_ref, cw_ref, par_ref, ng_ref, out_ref, st_ref,
                       halo_ref, u_ref, w_ref, qe_ref, qk_ref, kdt_ref, eg_ref, vn_ref):
    i = pl.program_id(1)
    r = GDN_TILE
    c = GDN_CHUNK

    @pl.when(i == 0)
    def _():
        halo_ref[...] = jnp.zeros_like(halo_ref)
        st_ref[...] = jnp.zeros_like(st_ref)

    x = x_ref[...]
    xx = jnp.concatenate([halo_ref[...], x], axis=0)
    halo_ref[...] = x[r - SUBLANES:r, :]
    y = None
    for j in range(CONV_W):
        sh = CONV_W - 1 - j
        xs = xx if sh == 0 else pltpu.roll(xx, sh, axis=0)
        t = xs[SUBLANES:, :] * cw_ref[j:j + 1, :]
        y = t if y is None else y + t
    _gdn_prep(y, ab_ref[...], par_ref, None, c, u_ref, w_ref, qe_ref, qk_ref, kdt_ref, eg_ref)
    vn_ref[...] = jnp.zeros_like(vn_ref)
    col_seg = lax.broadcasted_iota(jnp.int32, (GDN_HD, r), 1) // c
    states = [st_ref[0, h] for h in range(GDN_HEADS)]
    os_ = [[] for _ in range(GDN_HEADS)]
    for ck in range(r // c):
        for h in range(GDN_HEADS):
            o, states[h] = _gdn_segment(h, ck * c, ck, c, states[h], u_ref, w_ref, qe_ref, qk_ref, kdt_ref,
                                        eg_ref, vn_ref, col_seg)
            os_[h].append(o)
    for h in range(GDN_HEADS):
        st_ref[0, h] = states[h]
        o = jnp.concatenate(os_[h], axis=0)
        out_ref[:, h * GDN_HD:(h + 1) * GDN_HD] = _gdn_finish(o, zg_ref[:, h * GDN_HD:(h + 1) * GDN_HD], ng_ref)


def _gdn_scratch(r):
    hs = (GDN_HEADS, r, GDN_HD)
    return [pltpu.VMEM(hs, F32), pltpu.VMEM(hs, F32), pltpu.VMEM(hs, F32), pltpu.VMEM((GDN_HEADS, r, r), F32),
            pltpu.VMEM((GDN_HEADS, GDN_HD, r), F32), pltpu.VMEM(hs, F32), pltpu.VMEM(hs, F32)]


def gdn_prompt(z, conv_w, par, norm_g, bsz, t):
    nt = t // GDN_TILE
    r = GDN_TILE
    c2 = lambda b, i: (0, 0)
    return pl.pallas_call(
        _gdn_prompt_kernel,
        grid=(bsz, nt),
        in_specs=[pl.BlockSpec((r, 3 * MIX_WIDTH), lambda b, i: (b * nt + i, ZGQ // (3 * MIX_WIDTH))),
                  pl.BlockSpec((r, LANES), lambda b, i: (b * nt + i, ZGAB // LANES)),
                  pl.BlockSpec((r, MIX_WIDTH), lambda b, i: (b * nt + i, ZGZ // MIX_WIDTH)),
                  pl.BlockSpec((CONV_W, 3 * MIX_WIDTH), c2),
                  pl.BlockSpec((SUBLANES, LANES), c2),
                  pl.BlockSpec((1, GDN_HD), c2)],
        out_specs=[pl.BlockSpec((r, MIX_WIDTH), lambda b, i: (b * nt + i, 0)),
                   pl.BlockSpec((1, GDN_HEADS, GDN_HD, GDN_HD), lambda b, i: (b, 0, 0, 0))],
        out_shape=[jax.ShapeDtypeStruct((bsz * t, MIX_WIDTH), F32),
                   jax.ShapeDtypeStruct((bsz, GDN_HEADS, GDN_HD, GDN_HD), F32)],
        scratch_shapes=[pltpu.VMEM((SUBLANES, 3 * MIX_WIDTH), F32)] + _gdn_scratch(r),
        compiler_params=_cparams(("parallel", "arbitrary")),
        name="gdn_prompt",
    )(z, z, z, conv_w, par, norm_g)


SEQ_PER_STEP = 16


def _gdn_sample_kernel(x_ref, ab_ref, zg_ref, prev_ref, s0_ref, cw_ref, par_ref, ng_ref, out_ref, st_ref,
                       u_ref, w_ref, qe_ref, qk_ref, kdt_ref, eg_ref, vn_ref, o_ref, *, dec):
    r = SEQ_PER_STEP * SEQ_PAD
    c = SEQ_PAD
    row = lax.broadcasted_iota(jnp.int32, (r, 1), 0)
    tin = row % c
    valid = tin < dec
    xx = jnp.where(tin < CONV_W - 1, prev_ref[...], pltpu.roll(x_ref[...], CONV_W - 1, axis=0))
    y = None
    for j in range(CONV_W):
        xs = xx if j == 0 else pltpu.roll(xx, r - j, axis=0)
        t = xs * cw_ref[j:j + 1, :]
        y = t if y is None else y + t
    _gdn_prep(y, ab_ref[...], par_ref, valid, c, u_ref, w_ref, qe_ref, qk_ref, kdt_ref, eg_ref)
    vn_ref[...] = jnp.zeros_like(vn_ref)
    col_seg = lax.broadcasted_iota(jnp.int32, (GDN_HD, r), 1) // c

    def body(sq, carry):
        row0 = pl.multiple_of(sq * c, c)
        for h in range(GDN_HEADS):
            o, s = _gdn_segment(h, row0, sq, c, s0_ref[sq, h], u_ref, w_ref, qe_ref, qk_ref, kdt_ref, eg_ref,
                                vn_ref, col_seg)
            st_ref[sq, h] = s
            o_ref[h, pl.ds(row0, c), :] = o
        return carry

    lax.fori_loop(0, SEQ_PER_STEP, body, 0)
    for h in range(GDN_HEADS):
        out_ref[:, h * GDN_HD:(h + 1) * GDN_HD] = _gdn_finish(o_ref[h], zg_ref[:, h * GDN_HD:(h + 1) * GDN_HD],
                                                              ng_ref)


def gdn_sample(z, prev_pad, s0, conv_w, par, norm_g, row0, dbsz, dec):
    r = SEQ_PER_STEP * SEQ_PAD
    rb0 = row0 // r
    c2 = lambda i: (0, 0)
    return pl.pallas_call(
        functools.partial(_gdn_sample_kernel, dec=dec),
        grid=(dbsz // SEQ_PER_STEP,),
        in_specs=[pl.BlockSpec((r, 3 * MIX_WIDTH), lambda i: (rb0 + i, ZGQ // (3 * MIX_WIDTH))),
                  pl.BlockSpec((r, LANES), lambda i: (rb0 + i, ZGAB // LANES)),
                  pl.BlockSpec((r, MIX_WIDTH), lambda i: (rb0 + i, ZGZ // MIX_WIDTH)),
                  pl.BlockSpec((r, 3 * MIX_WIDTH), lambda i: (i, 0)),
                  pl.BlockSpec((SEQ_PER_STEP, GDN_HEADS, GDN_HD, GDN_HD), lambda i: (i, 0, 0, 0)),
                  pl.BlockSpec((CONV_W, 3 * MIX_WIDTH), c2),
                  pl.BlockSpec((SUBLANES, LANES), c2),
                  pl.BlockSpec((1, GDN_HD), c2)],
        out_specs=[pl.BlockSpec((r, MIX_WIDTH), lambda i: (i, 0)),
                   pl.BlockSpec((SEQ_PER_STEP, GDN_HEADS, GDN_HD, GDN_HD), lambda i: (i, 0, 0, 0))],
        out_shape=[jax.ShapeDtypeStruct((dbsz * SEQ_PAD, MIX_WIDTH), F32),
                   jax.ShapeDtypeStruct((dbsz, GDN_HEADS, GDN_HD, GDN_HD), F32)],
        scratch_shapes=_gdn_scratch(r) + [pltpu.VMEM((GDN_HEADS, r, GDN_HD), F32)],
        compiler_params=_cparams(("parallel",)),
        name="gdn_sample",
    )(z, z, z, prev_pad, s0, conv_w, par, norm_g)


def _ret_log_gamma(h):
    return math.log1p(-(2.0 ** (-5.0 - h)))


def _ret_prep(x, cs, sn, valid, h):
    qh = x[:, h * RET_HD:(h + 1) * RET_HD]
    kh = x[:, MIX_WIDTH + h * RET_HD:MIX_WIDTH + (h + 1) * RET_HD]
    vh = x[:, 2 * MIX_WIDTH + h * RET_HD:2 * MIX_WIDTH + (h + 1) * RET_HD]
    qr = qh * cs + pltpu.roll(qh, RET_HD // 2, axis=1) * sn
    kr = (kh * cs + pltpu.roll(kh, RET_HD // 2, axis=1) * sn) * (RET_HD ** -0.5)
    if valid is not None:
        kr = jnp.where(valid, kr, 0.0)
        vh = jnp.where(valid, vh, 0.0)
    return qr, kr, vh


def _ret_finish(o, gate):
    o = o * lax.rsqrt(jnp.mean(o * o, axis=-1, keepdims=True) + EPS)
    return o * _silu(gate)


def _ret_prompt_kernel(x_ref, gt_ref, cs_ref, sn_ref, out_ref, st_ref):
    i = pl.program_id(1)
    c = RET_CHUNK

    @pl.when(i == 0)
    def _():
        st_ref[...] = jnp.zeros_like(st_ref)

    x = x_ref[...]
    cs = cs_ref[...]
    sn = sn_ref[...]
    ri = lax.broadcasted_iota(jnp.int32, (c, c), 0)
    ci = lax.broadcasted_iota(jnp.int32, (c, c), 1)
    diff = (ri - ci).astype(F32)
    n = lax.broadcasted_iota(jnp.int32, (c, 1), 0).astype(F32)
    heads = range(RET_HEADS)
    lgs = [_ret_log_gamma(h) for h in heads]
    qkv = [_ret_prep(x, cs, sn, None, h) for h in heads]
    states = [st_ref[0, h] for h in heads]
    scores = [_bdot_nt(qkv[h][0], qkv[h][1]) * jnp.where(diff >= 0.0, jnp.exp(jnp.maximum(diff, 0.0) * lgs[h]), 0.0)
              for h in heads]
    cross = [_bdot(qkv[h][0], states[h]) * jnp.exp((n + 1.0) * lgs[h]) for h in heads]
    outs = [_bdot(scores[h], qkv[h][2]) + cross[h] for h in heads]
    kdts = [jnp.transpose(qkv[h][1] * jnp.exp((c - 1.0 - n) * lgs[h])).astype(BF16) for h in heads]
    for h in heads:
        st_ref[0, h] = states[h] * math.exp(c * lgs[h]) + _dot(kdts[h], qkv[h][2].astype(BF16))
        out_ref[:, h * RET_HD:(h + 1) * RET_HD] = _ret_finish(outs[h], gt_ref[:, h * RET_HD:(h + 1) * RET_HD])


def ret_prompt(z, cs, sn, bsz, t):
    c = RET_CHUNK
    nt = t // c
    return pl.pallas_call(
        _ret_prompt_kernel,
        grid=(bsz, nt),
        in_specs=[pl.BlockSpec((c, 3 * MIX_WIDTH), lambda b, i: (b * nt + i, ZRQ // (3 * MIX_WIDTH))),
                  pl.BlockSpec((c, MIX_WIDTH), lambda b, i: (b * nt + i, ZRG // MIX_WIDTH)),
                  pl.BlockSpec((c, RET_HD), lambda b, i: (i, 0)),
                  pl.BlockSpec((c, RET_HD), lambda b, i: (i, 0))],
        out_specs=[pl.BlockSpec((c, MIX_WIDTH), lambda b, i: (b * nt + i, 0)),
                   pl.BlockSpec((1, RET_HEADS, RET_HD, RET_HD), lambda b, i: (b, 0, 0, 0))],
        out_shape=[jax.ShapeDtypeStruct((bsz * t, MIX_WIDTH), F32),
                   jax.ShapeDtypeStruct((bsz, RET_HEADS, RET_HD, RET_HD), F32)],
        compiler_params=_cparams(("parallel", "arbitrary")),
        name="ret_prompt",
    )(z, z, cs, sn)


def _ret_sample_kernel(x_ref, gt_ref, cs_ref, sn_ref, s0_ref, out_ref, st_ref, q_ref, kdt_ref, v_ref, o_ref,
                       *, dec):
    r = SEQ_PER_STEP * SEQ_PAD
    c = SEQ_PAD
    row = lax.broadcasted_iota(jnp.int32, (r, 1), 0)
    tin = row % c
    valid = tin < dec
    n = tin.astype(F32)
    x = x_ref[...]
    cs = cs_ref[...]
    sn = sn_ref[...]
    same, incl, _ = _seg_masks(r, c)
    ri = lax.broadcasted_iota(jnp.int32, (r, r), 0)
    ci = lax.broadcasted_iota(jnp.int32, (r, r), 1)
    diff = (ri - ci).astype(F32)
    for h in range(RET_HEADS):
        lg = _ret_log_gamma(h)
        qr, kr, vh = _ret_prep(x, cs, sn, valid, h)
        dmat = jnp.where(incl, jnp.exp(jnp.maximum(diff, 0.0) * lg), 0.0)
        o_ref[h] = _bdot(_bdot_nt(qr, kr) * dmat, vh)
        q_ref[h] = qr
        kdt_ref[h] = jnp.transpose(kr * jnp.exp((dec - 1.0 - n) * lg))
        v_ref[h] = vh
    col_seg = lax.broadcasted_iota(jnp.int32, (RET_HD, r), 1) // c
    qdec = [jnp.exp((lax.broadcasted_iota(jnp.int32, (c, 1), 0).astype(F32) + 1.0) * _ret_log_gamma(h))
            for h in range(RET_HEADS)]

    def body(sq, carry):
        row0 = pl.multiple_of(sq * c, c)
        rows = pl.ds(row0, c)
        for h in range(RET_HEADS):
            s = s0_ref[sq, h]
            o_ref[h, rows, :] = o_ref[h, rows, :] + _bdot(q_ref[h, rows, :], s) * qdec[h]
            kdt = jnp.where(col_seg == sq, kdt_ref[h], 0.0)
            st_ref[sq, h] = s * math.exp(dec * _ret_log_gamma(h)) + _dot(kdt.astype(BF16), v_ref[h].astype(BF16))
        return carry

    lax.fori_loop(0, SEQ_PER_STEP, body, 0)
    for h in range(RET_HEADS):
        out_ref[:, h * RET_HD:(h + 1) * RET_HD] = _ret_finish(o_ref[h], gt_ref[:, h * RET_HD:(h + 1) * RET_HD])


def ret_sample(z, cs, sn, s0, row0, dbsz, dec):
    r = SEQ_PER_STEP * SEQ_PAD
    rb0 = row0 // r
    hs = (RET_HEADS, r, RET_HD)
    return pl.pallas_call(
        functools.partial(_ret_sample_kernel, dec=dec),
        grid=(dbsz // SEQ_PER_STEP,),
        in_specs=[pl.BlockSpec((r, 3 * MIX_WIDTH), lambda i: (rb0 + i, ZRQ // (3 * MIX_WIDTH))),
                  pl.BlockSpec((r, MIX_WIDTH), lambda i: (rb0 + i, ZRG // MIX_WIDTH)),
                  pl.BlockSpec((r, RET_HD), lambda i: (0, 0)),
                  pl.BlockSpec((r, RET_HD), lambda i: (0, 0)),
                  pl.BlockSpec((SEQ_PER_STEP, RET_HEADS, RET_HD, RET_HD), lambda i: (i, 0, 0, 0))],
        out_specs=[pl.BlockSpec((r, MIX_WIDTH), lambda i: (i, 0)),
                   pl.BlockSpec((SEQ_PER_STEP, RET_HEADS, RET_HD, RET_HD), lambda i: (i, 0, 0, 0))],
        out_shape=[jax.ShapeDtypeStruct((dbsz * SEQ_PAD, MIX_WIDTH), F32),
                   jax.ShapeDtypeStruct((dbsz, RET_HEADS, RET_HD, RET_HD), F32)],
        scratch_shapes=[pltpu.VMEM(hs, F32), pltpu.VMEM((RET_HEADS, RET_HD, r), F32), pltpu.VMEM(hs, F32),
                        pltpu.VMEM(hs, F32)],
        compiler_params=_cparams(("parallel",)),
        name="ret_sample",
    )(z, z, cs, sn, s0)


def _prep_w_in(w):
    sizes = (NSA_HEADS * NSA_HD, 6 * NSA_KV * NSA_HD, 3 * NSA_HEADS, 3 * MIX_WIDTH, GDN_HEADS, GDN_HEADS,
             MIX_WIDTH, 3 * MIX_WIDTH, MIX_WIDTH, N_BRANCH * D_MODEL)
    src = np.concatenate([[0], np.cumsum(sizes)])
    s_q, s_kv, s_ng, s_gq, s_ga, s_gb, s_gz, s_rq, s_rg, s_mg = src[:-1].tolist()
    wb = w.astype(BF16)
    out = jnp.zeros((w.shape[0], ZW), BF16)

    def put(o, dst, s0, n):
        return lax.dynamic_update_slice(o, lax.slice_in_dim(wb, s0, s0 + n, axis=1), (0, dst))

    for h in range(NSA_HEADS):
        out = put(out, ZQ + h * LANES + NSA_HD * (h // NSA_GRP), s_q + h * NSA_HD, NSA_HD)
    out = put(out, ZKV, s_kv, sizes[1] + sizes[2])
    out = put(out, ZGAB, s_ga, 2 * GDN_HEADS)
    out = put(out, ZGZ, s_gz, MIX_WIDTH)
    out = put(out, ZRG, s_rg, MIX_WIDTH)
    out = put(out, ZGQ, s_gq, 3 * MIX_WIDTH)
    out = put(out, ZRQ, s_rq, 3 * MIX_WIDTH)
    out = put(out, ZMG, s_mg, N_BRANCH * D_MODEL)
    return out


def _prep_w_branch(wb):
    w0 = wb[0].reshape(NSA_HEADS, NSA_HD, D_MODEL)
    order = [h for j in range(NSA_GRP) for h in (j, j + NSA_GRP)]
    w0 = w0[np.array(order)].reshape(MIX_WIDTH, D_MODEL)
    return jnp.stack([w0, wb[1], wb[2]]).astype(BF16)


def _prep_compress(pe, w1, w2):
    sel_k = jnp.asarray(np.diag([1.0, 1.0, 0.0, 0.0]).astype(np.float32))
    sel_v = jnp.asarray(np.diag([0.0, 0.0, 1.0, 1.0]).astype(np.float32))

    def blockdiag(mk, mv):
        full = (sel_k[:, None, :, None] * mk[..., None, :, None, :]
                + sel_v[:, None, :, None] * mv[..., None, :, None, :])
        return full.reshape(mk.shape[:-2] + (4 * mk.shape[-2], 4 * mk.shape[-1]))

    w1r = w1.reshape(2, CMP_LEN, NSA_HD, CMP_HIDDEN)
    w1a = blockdiag(w1r[0, :CMP_STRIDE], w1r[1, :CMP_STRIDE])
    w1b = blockdiag(w1r[0, CMP_STRIDE:], w1r[1, CMP_STRIDE:])
    pea = jnp.concatenate([pe[0, :CMP_STRIDE], pe[0, :CMP_STRIDE], pe[1, :CMP_STRIDE], pe[1, :CMP_STRIDE]], axis=1)
    peb = jnp.concatenate([pe[0, CMP_STRIDE:], pe[0, CMP_STRIDE:], pe[1, CMP_STRIDE:], pe[1, CMP_STRIDE:]], axis=1)
    w2b = blockdiag(w2[0], w2[1])
    return dict(pea=pea, peb=peb, w1a=w1a.astype(BF16), w1b=w1b.astype(BF16), w2=w2b.astype(BF16))


def _overlap_matrix(nc_rows, nc, ns):
    n = np.arange(nc_rows)[:, None]
    s = np.arange(LANES)[None, :]
    c_start = n * CMP_STRIDE
    ov = (c_start < (s + 1) * SEL_LEN) & (s * SEL_LEN < c_start + CMP_LEN) & (n < nc) & (s < ns)
    return jnp.asarray(ov.astype(np.float32))


def _expand_matrix(nkeys):
    s = np.arange(LANES)[:, None]
    k = np.arange(nkeys)[None, :]
    return (k // SEL_LEN == s).astype(np.float32)


def _rope_tables(pos):
    half = RET_HD // 2
    inv = ROPE_BASE ** (-jnp.linspace(0.0, 1.0, half, dtype=F32))
    ang = pos.astype(F32)[:, None] * inv[None, :]
    cos, sin = jnp.cos(ang), jnp.sin(ang)
    return jnp.concatenate([cos, cos], axis=1), jnp.concatenate([-sin, sin], axis=1)


def _pad_rows(a, n):
    return jnp.pad(a, ((0, 0), (0, n - a.shape[1])) + ((0, 0),) * (a.ndim - 2))


def kernel(x_prompt, x_sample, cache_nsa_kv, cache_nsa_win, state_gdn_conv, state_gdn, state_ret, page_table,
           p_prompt, p_sample, g_mix, w_in, nsa_cmp_pe, nsa_cmp_w1, nsa_cmp_w2, gdn_conv_w, gdn_a_log,
           gdn_dt_bias, gdn_norm_g, w_branch, w_out, g_mlp, w_up, w_down, w_ple, w_ple_gate, g_final):
    bsz, t, d = x_prompt.shape
    dbsz, dec, _ = x_sample.shape
    depth = w_in.shape[0]
    past = page_table.shape[1] * PAGE_SIZE
    n_p = bsz * t
    n_s = dbsz * SEQ_PAD
    assert dec <= SEQ_PAD and dec >= CONV_W - 1 and (past % SEL_LEN) + dec <= SEL_LEN
    assert t % SEL_TILE == 0 and t >= WINDOW + Q_TILE and past % PAGE_SIZE == 0 and dbsz % SEQ_PER_STEP == 0

    x = jnp.concatenate([x_prompt.reshape(n_p, d), _pad_rows(x_sample, SEQ_PAD).reshape(n_s, d)], axis=0)
    p_all = jnp.concatenate([p_prompt.reshape(depth, n_p, PLE_DIM),
                             jnp.pad(p_sample, ((0, 0), (0, 0), (0, SEQ_PAD - dec), (0, 0))).reshape(depth, n_s, PLE_DIM)],
                            axis=1)

    nc_p = t // CMP_STRIDE - 1
    ov_p = _overlap_matrix(t // CMP_STRIDE, nc_p, t // SEL_LEN)
    shift_p = _shift_matrices(t)
    nc_s = past // CMP_STRIDE - 1
    ov_s = _overlap_matrix(past // CMP_STRIDE, nc_s, -(-(past + dec) // SEL_LEN))
    e_s = jnp.asarray(_expand_matrix(past)).astype(BF16)
    n_phys = cache_nsa_kv.shape[1]
    wk = cache_nsa_win.shape[2]
    cache_t = jnp.transpose(cache_nsa_kv, (0, 1, 3, 4, 5, 2)).reshape(depth, n_phys, 4 * LANES, PAGE_SIZE)
    cwin_t = jnp.transpose(cache_nsa_win, (0, 1, 3, 4, 5, 2)).reshape(depth, dbsz, 2 * LANES, wk)
    cs_p, sn_p = _rope_tables(jnp.arange(t))
    cs_s, sn_s = _rope_tables(past + jnp.arange(SEQ_PAD))
    cs_s = jnp.tile(cs_s, (SEQ_PER_STEP, 1))
    sn_s = jnp.tile(sn_s, (SEQ_PER_STEP, 1))

    outs = [[] for _ in range(10)]
    for i in range(depth):
        w_in_p = _prep_w_in(w_in[i])
        wb = _prep_w_branch(w_branch[i])
        cw = _prep_compress(nsa_cmp_pe[i], nsa_cmp_w1[i], nsa_cmp_w2[i])
        par = jnp.zeros((SUBLANES, LANES), F32).at[0, :GDN_HEADS].set(gdn_a_log[i]).at[1, :GDN_HEADS].set(gdn_dt_bias[i])
        ng = gdn_norm_g[i].reshape(1, GDN_HD)

        z = in_proj(x, g_mix[i].reshape(1, d), w_in_p)
        kv_all = lax.slice(z, (0, ZKV), (n_p + n_s, ZNG))
        kv_p = kv_all[:n_p]
        kv_s = kv_all[n_p:].reshape(dbsz, SEQ_PAD, ZNG - ZKV)[:, :dec]

        kcmp_sub = kv_p[:, :2 * LANES].reshape(n_p // CMP_STRIDE, CMP_STRIDE * 2 * LANES)
        kcvc = compress_prompt(kcmp_sub, cw, bsz, t)
        o_nsa_p = nsa_prompt(z, kcvc, _prompt_kv_slab(kv_p, t), shift_p, ov_p, bsz, t)
        o_nsa_s = nsa_sample(z, page_table, cache_t, cwin_t, cw, e_s, ov_s, i, n_p, dbsz, past, dec)

        o_gdn_p, gs_p = gdn_prompt(z, gdn_conv_w[i], par, ng, bsz, t)
        prev_pad = _pad_rows(state_gdn_conv[i], SEQ_PAD).reshape(n_s, 3 * MIX_WIDTH)
        o_gdn_s, gs_s = gdn_sample(z, prev_pad, state_gdn[i], gdn_conv_w[i], par, ng, n_p, dbsz, dec)

        o_ret_p, rs_p = ret_prompt(z, cs_p, sn_p, bsz, t)
        o_ret_s, rs_s = ret_sample(z, cs_s, sn_s, state_ret[i], n_p, dbsz, dec)

        x = merge(x, (o_nsa_p, o_gdn_p, o_ret_p), (o_nsa_s, o_gdn_s, o_ret_s), z, wb, w_out[i].astype(BF16))
        x = mlp_ple(x, g_mlp[i].reshape(1, d), w_up[i].astype(BF16), w_down[i].astype(BF16), p_all[i],
                    w_ple[i].astype(BF16), w_ple_gate[i].astype(BF16), g_final.reshape(1, d),
                    final_norm=(i == depth - 1))

        nkv = ZKW - ZKV
        outs[0].append(kv_p[:, :nkv].reshape(bsz, t, 4, NSA_KV, NSA_HD))
        outs[1].append(kv_s[:, :, :nkv].reshape(dbsz, dec, 4, NSA_KV, NSA_HD))
        wlen = min(WINDOW, t)
        outs[2].append(kv_p.reshape(bsz, t, ZNG - ZKV)[:, t - wlen:, nkv:].reshape(bsz, wlen, 2, NSA_KV, NSA_HD))
        win = jnp.concatenate([cache_nsa_win[i], kv_s[:, :, nkv:].reshape(dbsz, dec, 2, NSA_KV, NSA_HD)], axis=1)
        wlen_s = min(WINDOW, past + dec)
        outs[3].append(win[:, win.shape[1] - wlen_s:])
        nconv = CONV_W - 1
        outs[4].append(jnp.stack([lax.slice(z, (b * t + t - nconv, ZGQ), (b * t + t, ZGQ + 3 * MIX_WIDTH))
                                  for b in range(bsz)]))
        gq_s = lax.slice(z, (n_p, ZGQ), (n_p + n_s, ZGQ + 3 * MIX_WIDTH)).reshape(dbsz, SEQ_PAD, 3 * MIX_WIDTH)
        outs[5].append(gq_s[:, dec - nconv:dec])
        outs[6].append(gs_p)
        outs[7].append(gs_s)
        outs[8].append(rs_p)
        outs[9].append(rs_s)

    y_prompt = x[:n_p].reshape(bsz, t, d)
    y_sample = x[n_p:].reshape(dbsz, SEQ_PAD, d)[:, :dec]
    return (y_prompt, y_sample) + tuple(jnp.stack(o) for o in outs)
```

```python
import functools
import math

import numpy as np
import jax
import jax.numpy as jnp
from jax import lax
from jax.experimental import pallas as pl
from jax.experimental.pallas import tpu as pltpu

F32 = jnp.float32
BF16 = jnp.bfloat16

D_MODEL = 1024
MIX_WIDTH = D_MODEL // 2
N_BRANCH = 3
NSA_HEADS = 8
NSA_HD = 64
NSA_KV = 2
NSA_GRP = NSA_HEADS // NSA_KV
CMP_LEN = 32
CMP_STRIDE = 16
CMP_HIDDEN = 64
SEL_LEN = 64
SEL_TOPK = 16
WINDOW = 512
FORCE_BONUS = 1000.0
GDN_HEADS = 4
GDN_HD = 128
CONV_W = 4
GDN_CHUNK = 64
RET_HEADS = 4
RET_HD = 128
RET_CHUNK = 128
ROPE_BASE = 10000.0
D_FF = 4 * D_MODEL
PLE_DIM = 256
EPS = 1e-6
NEG = -1e30
PAGE_SIZE = 128

LANES = 128
SUBLANES = 8
SEQ_PAD = SUBLANES
VMEM_LIMIT = 56 * 1024 * 1024

ZQ = 0
ZKV = 1024
ZKW = 1536
ZNG = 1792
ZGAB = 1920
ZGZ = 2048
ZRG = 2560
ZGQ = 3072
ZRQ = 4608
ZMG = 6144
ZW = 9216


def _sigmoid(x):
    return 1.0 / (1.0 + jnp.exp(-x))


def _silu(x):
    return x * _sigmoid(x)


def _softplus(x):
    return jnp.maximum(x, 0.0) + jnp.log1p(jnp.exp(-jnp.abs(x)))


def _gelu_tanh(x):
    return 0.5 * x * (1.0 + jnp.tanh(math.sqrt(2.0 / math.pi) * (x + 0.044715 * (x * x * x))))


def _rms(x, g):
    return x * lax.rsqrt(jnp.mean(x * x, axis=-1, keepdims=True) + EPS) * g


def _dot(a, b):
    return jnp.dot(a, b, preferred_element_type=F32)


def _dot_nt(a, b):
    return lax.dot_general(a, b, (((1,), (1,)), ((), ())), preferred_element_type=F32)


def _bdot(a, b):
    return _dot(a.astype(BF16), b.astype(BF16))


def _bdot_nt(a, b):
    return _dot_nt(a.astype(BF16), b.astype(BF16))


def _split2(a):
    hi = a.astype(BF16)
    lo = (a - hi.astype(F32)).astype(BF16)
    return hi, lo


def _dot3(a, b):
    ah, al = _split2(a)
    bh, bl = _split2(b)
    return _dot(ah, bh) + (_dot(ah, bl) + _dot(al, bh))


def _dot_mask_left(m01, x):
    m = m01.astype(BF16)
    x1 = x.astype(BF16)
    r1 = x - x1.astype(F32)
    x2 = r1.astype(BF16)
    x3 = (r1 - x2.astype(F32)).astype(BF16)
    return _dot(m, x1) + (_dot(m, x2) + _dot(m, x3))


def _pick_tile(n, prefs):
    for p in prefs:
        if n % p == 0:
            return p
    raise ValueError(f"no tile in {prefs} divides {n}")


def _cparams(sem):
    return pltpu.CompilerParams(dimension_semantics=sem, vmem_limit_bytes=VMEM_LIMIT)


def _in_kernel(x_ref, g_ref, w_ref, z_ref, h_ref):
    @pl.when(pl.program_id(1) == 0)
    def _():
        h_ref[...] = _rms(x_ref[...], g_ref[...]).astype(BF16)

    z_ref[...] = _dot(h_ref[...], w_ref[...])


def in_proj(x, g, w):
    n = x.shape[0]
    tm = _pick_tile(n, (1024, 512, 256, 128))
    tn = 1024
    return pl.pallas_call(
        _in_kernel,
        grid=(n // tm, ZW // tn),
        in_specs=[pl.BlockSpec((tm, D_MODEL), lambda i, j: (i, 0)),
                  pl.BlockSpec((1, D_MODEL), lambda i, j: (0, 0)),
                  pl.BlockSpec((D_MODEL, tn), lambda i, j: (0, j))],
        out_specs=pl.BlockSpec((tm, tn), lambda i, j: (i, j)),
        out_shape=jax.ShapeDtypeStruct((n, ZW), F32),
        scratch_shapes=[pltpu.VMEM((tm, D_MODEL), BF16)],
        compiler_params=_cparams(("parallel", "arbitrary")),
        name="in_proj",
    )(x, g, w)


def _merge_kernel(x_ref, onp_ref, ogp_ref, orp_ref, ons_ref, ogs_ref, ors_ref, mg_ref, wb_ref, wo_ref, out_ref,
                  *, n_prompt_tiles):
    is_prompt = pl.program_id(0) < n_prompt_tiles
    m = None
    for b, (op_ref, os_ref) in enumerate(((onp_ref, ons_ref), (ogp_ref, ogs_ref), (orp_ref, ors_ref))):
        o = jnp.where(is_prompt, op_ref[...], os_ref[...])
        br = _dot(o.astype(BF16), wb_ref[b])
        t = _sigmoid(mg_ref[:, b * D_MODEL:(b + 1) * D_MODEL]) * br
        m = t if m is None else m + t
    out_ref[...] = x_ref[...] + _dot(m.astype(BF16), wo_ref[...])


def merge(x, o_prompt, o_sample, z, wb, wo):
    n = x.shape[0]
    n_p = o_prompt[0].shape[0]
    n_s = o_sample[0].shape[0]
    tm = _pick_tile(math.gcd(n_p, n_s), (512, 256, 128))
    npt = n_p // tm
    row = lambda i: (i, 0)
    prow = lambda i: (jnp.minimum(i, npt - 1), 0)
    srow = lambda i: (jnp.maximum(i - npt, 0), 0)
    return pl.pallas_call(
        functools.partial(_merge_kernel, n_prompt_tiles=npt),
        grid=(n // tm,),
        in_specs=[pl.BlockSpec((tm, D_MODEL), row)]
                 + [pl.BlockSpec((tm, MIX_WIDTH), prow)] * N_BRANCH
                 + [pl.BlockSpec((tm, MIX_WIDTH), srow)] * N_BRANCH
                 + [
                  pl.BlockSpec((tm, N_BRANCH * D_MODEL), lambda i: (i, ZMG // (N_BRANCH * D_MODEL))),
                  pl.BlockSpec((N_BRANCH, MIX_WIDTH, D_MODEL), lambda i: (0, 0, 0)),
                  pl.BlockSpec((D_MODEL, D_MODEL), lambda i: (0, 0))],
        out_specs=pl.BlockSpec((tm, D_MODEL), row),
        out_shape=jax.ShapeDtypeStruct((n, D_MODEL), F32),
        compiler_params=_cparams(("parallel",)),
        name="merge",
    )(x, *o_prompt, *o_sample, z, wb, wo)


def _mlp_kernel(x_ref, g_ref, wu_ref, wd_ref, p_ref, wp_ref, wpg_ref, gf_ref, out_ref, h_ref, acc_ref,
                *, final_norm):
    f = pl.program_id(1)

    @pl.when(f == 0)
    def _():
        h_ref[...] = _rms(x_ref[...], g_ref[...]).astype(BF16)
        acc_ref[...] = jnp.zeros_like(acc_ref)

    up = jnp.maximum(_dot(h_ref[...], wu_ref[...]), 0.0)
    acc_ref[...] += _dot((up * up).astype(BF16), wd_ref[...])

    @pl.when(f == pl.num_programs(1) - 1)
    def _():
        x2 = x_ref[...] + acc_ref[...]
        ple = _dot(p_ref[...].astype(BF16), wp_ref[...])
        x3 = x2 + ple * _sigmoid(_dot(x2.astype(BF16), wpg_ref[...]))
        if final_norm:
            x3 = _rms(x3, gf_ref[...])
        out_ref[...] = x3


def mlp_ple(x, g, wu, wd, p, wp, wpg, gf, final_norm):
    n = x.shape[0]
    tm = _pick_tile(n, (512, 256, 128))
    tf = 1024
    return pl.pallas_call(
        functools.partial(_mlp_kernel, final_norm=final_norm),
        grid=(n // tm, D_FF // tf),
        in_specs=[pl.BlockSpec((tm, D_MODEL), lambda i, f: (i, 0)),
                  pl.BlockSpec((1, D_MODEL), lambda i, f: (0, 0)),
                  pl.BlockSpec((D_MODEL, tf), lambda i, f: (0, f)),
                  pl.BlockSpec((tf, D_MODEL), lambda i, f: (f, 0)),
                  pl.BlockSpec((tm, PLE_DIM), lambda i, f: (i, 0)),
                  pl.BlockSpec((PLE_DIM, D_MODEL), lambda i, f: (0, 0)),
                  pl.BlockSpec((D_MODEL, D_MODEL), lambda i, f: (0, 0)),
                  pl.BlockSpec((1, D_MODEL), lambda i, f: (0, 0))],
        out_specs=pl.BlockSpec((tm, D_MODEL), lambda i, f: (i, 0)),
        out_shape=jax.ShapeDtypeStruct((n, D_MODEL), F32),
        scratch_shapes=[pltpu.VMEM((tm, D_MODEL), BF16), pltpu.VMEM((tm, D_MODEL), F32)],
        compiler_params=_cparams(("parallel", "arbitrary")),
        name="mlp_ple",
    )(x, g, wu, wd, p, wp, wpg, gf)


def _compress_math(load_sub, pea_ref, peb_ref, w1a_ref, w1b_ref, w2_ref, nsub):
    acc_a = jnp.zeros((nsub, 4 * CMP_HIDDEN), F32)
    acc_b = jnp.zeros((nsub, 4 * CMP_HIDDEN), F32)
    for j in range(CMP_STRIDE):
        xj = load_sub(j)
        acc_a = acc_a + _dot((xj + pea_ref[j:j + 1, :]).astype(BF16), w1a_ref[j])
        acc_b = acc_b + _dot((xj + peb_ref[j:j + 1, :]).astype(BF16), w1b_ref[j])
    hid = _gelu_tanh(acc_a + pltpu.roll(acc_b, nsub - 1, axis=0))
    return _dot(hid.astype(BF16), w2_ref[...])


def _cmp_kernel(x_ref, pea_ref, peb_ref, w1a_ref, w1b_ref, w2_ref, out_ref, *, nsub):
    load = lambda j: x_ref[:, j * 256:(j + 1) * 256]
    out_ref[0] = _compress_math(load, pea_ref, peb_ref, w1a_ref, w1b_ref, w2_ref, nsub)


def compress_prompt(kcmp_sub, cw, bsz, t):
    nsub = t // CMP_STRIDE
    const2 = lambda b: (0, 0)
    const3 = lambda b: (0, 0, 0)
    return pl.pallas_call(
        functools.partial(_cmp_kernel, nsub=nsub),
        grid=(bsz,),
        in_specs=[pl.BlockSpec((nsub, CMP_STRIDE * 256), lambda b: (b, 0)),
                  pl.BlockSpec((CMP_STRIDE, 256), const2),
                  pl.BlockSpec((CMP_STRIDE, 256), const2),
                  pl.BlockSpec((CMP_STRIDE, 256, 256), const3),
                  pl.BlockSpec((CMP_STRIDE, 256, 256), const3),
                  pl.BlockSpec((256, 256), const2)],
        out_specs=pl.BlockSpec((1, nsub, 256), lambda b: (b, 0, 0)),
        out_shape=jax.ShapeDtypeStruct((bsz, nsub, 256), F32),
        compiler_params=_cparams(("parallel",)),
        name="nsa_compress",
    )(kcmp_sub, cw["pea"], cw["peb"], cw["w1a"], cw["w1b"], cw["w2"])


def _topk_mask(scores, nblk, k):
    shape = scores[0].shape
    if shape[0] == LANES:
        return _topk_mask_square(scores, k)
    lane = lax.broadcasted_iota(jnp.int32, shape, 1)
    cnts = [jnp.zeros(shape, F32) for _ in scores]
    for j in range(nblk):
        for i, score in enumerate(scores):
            cj = score[:, j:j + 1]
            ge = jnp.where(cj >= score, 1.0, 0.0)
            gt = jnp.where(cj > score, 1.0, 0.0)
            cnts[i] = cnts[i] + jnp.where(lane > j, ge, gt)
    return [cnt < float(k) for cnt in cnts]


def _topk_cols(scs, k):
    shape = scs[0].shape
    idx = lax.broadcasted_iota(jnp.int32, shape, 0).astype(F32)
    picked = [jnp.zeros(shape, F32) for _ in scs]
    for _ in range(k):
        mxs = [jnp.max(sc, axis=0, keepdims=True) for sc in scs]
        firsts = [jnp.min(jnp.where(sc == mx, idx, float(shape[0])), axis=0, keepdims=True)
                  for sc, mx in zip(scs, mxs)]
        hits = [idx == first for first in firsts]
        picked = [jnp.where(hit, 1.0, pk) for hit, pk in zip(hits, picked)]
        scs = [jnp.where(hit, 3.0 * NEG, sc) for hit, sc in zip(hits, scs)]
    return picked


def _topk_mask_square(scores, k):
    picked = _topk_cols([jnp.transpose(s) for s in scores], k)
    return [jnp.transpose(p) > 0.5 for p in picked]


def _tile_rows(a, reps):
    return jnp.concatenate([a] * reps, axis=0)


def _nsa_queries(zq_ref, g):
    scale = NSA_HD ** -0.5
    q = jnp.concatenate([zq_ref[:, (NSA_GRP * g + h) * LANES:(NSA_GRP * g + h + 1) * LANES]
                         for h in range(NSA_GRP)], axis=0)
    return (q * scale).astype(BF16)


def _nsa_compressed_and_select(qgs, kc, vc, m_c, ov, pos1, ns, r):
    bias = _tile_rows(jnp.where(m_c, 0.0, NEG), NSA_GRP)
    ss = [_dot_nt(qg, kc) + bias for qg in qgs]
    mxs = [jnp.max(s, axis=-1, keepdims=True) for s in ss]
    ps = [jnp.exp(s - mx) for s, mx in zip(ss, mxs)]
    ps = [p / jnp.sum(p, axis=-1, keepdims=True) * jnp.where(mx > 0.5 * NEG, 1.0, 0.0) for p, mx in zip(ps, mxs)]
    o_cs = [_dot(p.astype(BF16), vc) for p in ps]
    psums = [(p[0:r] + p[r:2 * r]) + (p[2 * r:3 * r] + p[3 * r:4 * r]) for p in ps]
    imps = [_dot_mask_left_t(psum, ov) for psum in psums]
    blk = lax.broadcasted_iota(jnp.int32, (r, LANES), 1)
    valid = blk * SEL_LEN <= pos1
    cur = pos1 // SEL_LEN
    bonus = jnp.where((blk == 0) | (blk == cur) | (blk == cur - 1), FORCE_BONUS, 0.0)
    scores = [jnp.where(valid, imp + bonus, NEG) for imp in imps]
    sels = [sel & valid for sel in _topk_mask(scores, min(ns, LANES), min(SEL_TOPK, ns))]
    return o_cs, sels


def _dot_mask_left_t(x, m01):
    m = m01.astype(BF16)
    x1 = x.astype(BF16)
    r1 = x - x1.astype(F32)
    x2 = r1.astype(BF16)
    x3 = (r1 - x2.astype(F32)).astype(BF16)
    return _dot(x1, m) + (_dot(x2, m) + _dot(x3, m))


def _gate_cols(gates, g, c, r):
    return jnp.concatenate([gates[:, 3 * (NSA_GRP * g + h) + c:3 * (NSA_GRP * g + h) + c + 1]
                            for h in range(NSA_GRP)], axis=0)


def _nsa_assemble(o0, o1, r):
    lane = lax.broadcasted_iota(jnp.int32, (r, LANES), 1)
    return jnp.concatenate([jnp.where(lane < NSA_HD, o0[j * r:(j + 1) * r], o1[j * r:(j + 1) * r])
                            for j in range(NSA_GRP)], axis=1)


SEL_TILE = 1024
Q_TILE = 128


KV_KA = 0
KV_VA = 2 * LANES
KV_KW = 4 * LANES
KV_VWA = 5 * LANES
KV_COLS = 7 * LANES
SEL_LANE = NSA_HD


def _nsa_prompt_kernel(zq_ref, zg_ref, kcvc_ref, kv_ref, shift_ref, ov_ref, out_ref, *, t):
    r = Q_TILE
    groups = range(NSA_KV)
    qb = pl.program_id(1)
    s0 = qb * r
    pos1 = s0 + lax.broadcasted_iota(jnp.int32, (r, 1), 0)
    nc = t // CMP_STRIDE
    ns = t // SEL_LEN
    kc = kcvc_ref[0][:, 0:LANES].astype(BF16)
    vc = kcvc_ref[0][:, LANES:2 * LANES].astype(BF16)
    ncol = lax.broadcasted_iota(jnp.int32, (1, nc), 1)
    m_c = ((ncol * CMP_STRIDE + (CMP_LEN - 1)) <= pos1) & (ncol < nc - 1)
    gates = _sigmoid(zg_ref[...])
    ov = ov_ref[...]

    qs = [_nsa_queries(zq_ref, g) for g in groups]
    ocs, sels = _nsa_compressed_and_select(qs, kc, vc, m_c, ov, pos1, ns, r)
    sels = [jnp.where(sel, 1.0, 0.0).astype(BF16) for sel in sels]

    scale = NSA_HD ** -0.5
    qsel = []
    for g in groups:
        heads = [zq_ref[:, (NSA_GRP * g + h) * LANES:(NSA_GRP * g + h + 1) * LANES] for h in range(NSA_GRP)]
        if g == 1:
            heads = [pltpu.roll(q, NSA_HD, axis=1) for q in heads]
        qsel.append(jnp.concatenate(heads, axis=0) * scale)
    lane = lax.broadcasted_iota(jnp.int32, (1, LANES), 1)
    ind = jnp.where((lane >= SEL_LANE) & (lane < SEL_LANE + SEL_TILE // SEL_LEN), 1.0, 0.0)

    def tile_step(kt, carry, diagonal):
        k0 = pl.multiple_of(kt * SEL_TILE, SEL_TILE)
        shift = shift_ref[kt]
        va = kv_ref[pl.ds(k0, SEL_TILE), KV_VA:KV_VA + 2 * LANES]
        qas = [(qsel[g] + _tile_rows(NEG * (ind - _dot(sels[g], shift)), NSA_GRP)).astype(BF16) for g in groups]
        ss = [_dot_nt(qas[g], kv_ref[pl.ds(k0, SEL_TILE), KV_KA + g * LANES:KV_KA + (g + 1) * LANES])
              for g in groups]
        if diagonal:
            kpos = k0 + lax.broadcasted_iota(jnp.int32, (1, SEL_TILE), 1)
            causal = _tile_rows(jnp.where(kpos <= pos1, 0.0, NEG), NSA_GRP)
            ss = [s + causal for s in ss]
        m_news = [jnp.maximum(carry[g][0], jnp.max(ss[g], axis=-1, keepdims=True)) for g in groups]
        alphas = [jnp.exp(carry[g][0] - m_news[g]) for g in groups]
        ps = [jnp.exp((ss[g] - m_news[g]).astype(BF16)) for g in groups]
        accs = [alphas[g] * carry[g][1] + _dot(ps[g], va) for g in groups]
        return tuple((m_news[g], accs[g]) for g in groups)

    init = tuple((jnp.full((NSA_GRP * r, 1), NEG, F32), jnp.zeros((NSA_GRP * r, 2 * LANES), F32)) for _ in groups)
    n_full = s0 // SEL_TILE
    mid = lax.fori_loop(0, n_full, lambda kt, c: tile_step(kt, c, False), init)
    fin = tile_step(n_full, mid, True)

    span = WINDOW + r
    start = pl.multiple_of(jnp.maximum(s0 - WINDOW, 0), r)
    kw = kv_ref[pl.ds(start, span), KV_KW:KV_KW + LANES]
    vwa = kv_ref[pl.ds(start, span), KV_VWA:KV_VWA + 2 * LANES]
    kposw = start + lax.broadcasted_iota(jnp.int32, (1, span), 1)
    bias_w = _tile_rows(jnp.where((kposw <= pos1) & (kposw > pos1 - WINDOW), 0.0, NEG), NSA_GRP)

    o_ss = [fin[g][1][:, 0:LANES] / fin[g][1][:, LANES:LANES + 1] for g in groups]
    ss = [_dot_nt(qs[g], kw) + bias_w for g in groups]
    ps = [jnp.exp((s - jnp.max(s, axis=-1, keepdims=True)).astype(BF16)) for s in ss]
    ows = [_dot(p, vwa) for p in ps]
    o_ws = [ow[:, 0:LANES] / ow[:, LANES:LANES + 1] for ow in ows]
    outs = [_gate_cols(gates, g, 0, r) * ocs[g] + _gate_cols(gates, g, 1, r) * o_ss[g]
            + _gate_cols(gates, g, 2, r) * o_ws[g] for g in groups]
    out_ref[...] = _nsa_assemble(outs[0], outs[1], r)


def nsa_prompt(z, kcvc, kvb, shift, ov, bsz, t):
    nqb = t // Q_TILE
    return pl.pallas_call(
        functools.partial(_nsa_prompt_kernel, t=t),
        grid=(bsz, nqb),
        in_specs=[pl.BlockSpec((Q_TILE, NSA_HEADS * LANES), lambda b, q: (b * nqb + q, 0)),
                  pl.BlockSpec((Q_TILE, LANES), lambda b, q: (b * nqb + q, ZNG // LANES)),
                  pl.BlockSpec((1, t // CMP_STRIDE, 256), lambda b, q: (b, 0, 0)),
                  pl.BlockSpec((t, KV_COLS), lambda b, q: (b, 0)),
                  pl.BlockSpec((t // SEL_TILE, LANES, LANES), lambda b, q: (0, 0, 0)),
                  pl.BlockSpec((t // CMP_STRIDE, LANES), lambda b, q: (0, 0))],
        out_specs=pl.BlockSpec((Q_TILE, MIX_WIDTH), lambda b, q: (b * nqb + q, 0)),
        out_shape=jax.ShapeDtypeStruct((bsz * t, MIX_WIDTH), F32),
        compiler_params=_cparams(("parallel", "arbitrary")),
        name="nsa_prompt",
    )(z, z, kcvc, kvb, shift, ov)


SLAB_ROWS = 1024


def _kv_slab_kernel(slc_ref, win_ref, out_ref, *, t):
    rows = slc_ref.shape[0]
    blocks_per_tile = SEL_TILE // SEL_LEN
    lane = lax.broadcasted_iota(jnp.int32, (rows, LANES), 1)
    tok = (pl.program_id(0) * rows + lax.broadcasted_iota(jnp.int32, (rows, LANES), 0)) % t
    onehot = jnp.where(lane == SEL_LANE + (tok // SEL_LEN) % blocks_per_tile, 1.0, 0.0)
    ones_col = jnp.where(lane == 0, 1.0, 0.0)
    k_slc = slc_ref[:, 0:LANES]
    is_key = lane < NSA_HD
    pieces = [jnp.where(is_key, k_slc, onehot), jnp.where(is_key, pltpu.roll(k_slc, NSA_HD, axis=1), onehot),
              slc_ref[:, LANES:2 * LANES], ones_col, win_ref[:, 0:LANES], win_ref[:, LANES:2 * LANES], ones_col]
    for i, piece in enumerate(pieces):
        out_ref[:, i * LANES:(i + 1) * LANES] = piece.astype(BF16)


def prompt_kv_slab(z, n_p, t):
    rows = _pick_tile(n_p, (SLAB_ROWS, 512, 256, 128))
    return pl.pallas_call(
        functools.partial(_kv_slab_kernel, t=t),
        grid=(n_p // rows,),
        in_specs=[pl.BlockSpec((rows, 2 * LANES), lambda i: (i, (ZKV + 2 * LANES) // (2 * LANES))),
                  pl.BlockSpec((rows, 2 * LANES), lambda i: (i, ZKW // (2 * LANES)))],
        out_specs=pl.BlockSpec((rows, KV_COLS), lambda i: (i, 0)),
        out_shape=jax.ShapeDtypeStruct((n_p, KV_COLS), BF16),
        compiler_params=_cparams(("parallel",)),
        name="nsa_kv_slab",
    )(z, z)


def _shift_matrices(t):
    blocks_per_tile = SEL_TILE // SEL_LEN
    out = np.zeros((t // SEL_TILE, LANES, LANES), np.float32)
    for b in range(min(t // SEL_LEN, LANES)):
        out[b // blocks_per_tile, b, SEL_LANE + b % blocks_per_tile] = 1.0
    return jnp.asarray(out).astype(BF16)


def _nsa_sample_kernel(pt_ref, zq_ref, zkv_ref, cwin_ref, cache_ref, pea_ref, peb_ref, w1a_ref, w1b_ref,
                       w2_ref, e_ref, ov_ref, out_ref, buf_ref, tokm_ref, sem_ref, *, layer, past, n_pages, dec):
    r = SEQ_PAD
    b = pl.program_id(0)
    nb = pl.num_programs(0)
    slot = lax.rem(b, 2)

    def page_copy(seq, sl, p):
        return pltpu.make_async_copy(cache_ref.at[layer, pt_ref[seq * n_pages + p]],
                                     buf_ref.at[sl, :, pl.ds(p * PAGE_SIZE, PAGE_SIZE)],
                                     sem_ref.at[sl])

    def start_all(seq, sl):
        for p in range(n_pages):
            page_copy(seq, sl, p).start()

    @pl.when(b == 0)
    def _():
        start_all(b, slot)

    @pl.when(b + 1 < nb)
    def _():
        start_all(b + 1, 1 - slot)

    for p in range(n_pages):
        page_copy(b, slot, p).wait()

    nsub = past // CMP_STRIDE
    nc = nsub - 1
    ns = -(-(past + dec) // SEL_LEN)
    for c0 in range(2):
        for p in range(past // LANES):
            tokm_ref[c0, p * LANES:(p + 1) * LANES, :] = jnp.transpose(
                buf_ref[slot, c0 * LANES:(c0 + 1) * LANES, p * LANES:(p + 1) * LANES])
    load = lambda j: jnp.concatenate([tokm_ref[0, pl.ds(j, nsub, stride=CMP_STRIDE), :],
                                      tokm_ref[1, pl.ds(j, nsub, stride=CMP_STRIDE), :]], axis=1)
    kcvc = _compress_math(load, pea_ref, peb_ref, w1a_ref, w1b_ref, w2_ref, nsub)
    kc = kcvc[:, 0:LANES].astype(BF16)
    vc = kcvc[:, LANES:2 * LANES].astype(BF16)

    row = lax.broadcasted_iota(jnp.int32, (r, 1), 0)
    pos1 = past + row
    ncol = lax.broadcasted_iota(jnp.int32, (1, nsub), 1)
    m_c = ((ncol * CMP_STRIDE + (CMP_LEN - 1)) <= pos1) & (ncol < nc)
    gates = _sigmoid(zkv_ref[:, ZNG - ZKV:ZNG - ZKV + LANES])
    ov = ov_ref[...]

    k_sel = buf_ref[slot, 2 * LANES:3 * LANES, :].astype(BF16)
    v_sel = buf_ref[slot, 3 * LANES:4 * LANES, :].astype(BF16)
    kn_sel = zkv_ref[:, 2 * LANES:3 * LANES].astype(BF16)
    vn_sel = zkv_ref[:, 3 * LANES:4 * LANES].astype(BF16)
    k_win = cwin_ref[0, 0, 0:LANES, :].astype(BF16)
    v_win = cwin_ref[0, 0, LANES:2 * LANES, :].astype(BF16)
    kn_win = zkv_ref[:, ZKW - ZKV:ZKW - ZKV + LANES].astype(BF16)
    vn_win = zkv_ref[:, ZKW - ZKV + LANES:ZKW - ZKV + 2 * LANES].astype(BF16)

    kpos_p = lax.broadcasted_iota(jnp.int32, (1, past), 1)
    tnew = lax.broadcasted_iota(jnp.int32, (1, r), 1)
    kpos_n = past + tnew
    wk = cwin_ref.shape[3]
    kpos_w = past - wk + lax.broadcasted_iota(jnp.int32, (1, wk), 1)
    in_win = lambda kp: (kp <= pos1) & (kp > pos1 - WINDOW)
    bias_wp = _tile_rows(jnp.where(in_win(kpos_w), 0.0, NEG), NSA_GRP)
    bias_wn = _tile_rows(jnp.where(in_win(kpos_n) & (tnew < dec), 0.0, NEG), NSA_GRP)
    new_blk = past // SEL_LEN

    def two_part_attention(qg, kp_t, vp_t, bias_p, kn, vn, bias_n):
        s_p = _dot(qg, kp_t) + bias_p
        s_n = _dot_nt(qg, kn) + bias_n
        mx = jnp.maximum(jnp.max(s_p, axis=-1, keepdims=True), jnp.max(s_n, axis=-1, keepdims=True))
        p_p = jnp.exp(s_p - mx)
        p_n = jnp.exp(s_n - mx)
        den = jnp.sum(p_p, axis=-1, keepdims=True) + jnp.sum(p_n, axis=-1, keepdims=True)
        return (_dot_nt(p_p.astype(BF16), vp_t) + _dot(p_n.astype(BF16), vn)) / den

    outs = []
    qgs = [_nsa_queries(zq_ref, g) for g in range(NSA_KV)]
    o_cs, sels = _nsa_compressed_and_select(qgs, kc, vc, m_c, ov, pos1, ns, r)
    for g in range(NSA_KV):
        qg, o_c = qgs[g], o_cs[g]
        self = jnp.where(sels[g], 1.0, 0.0)
        allow_p = (_dot(self.astype(BF16), e_ref[...]) > 0.5) & (kpos_p <= pos1)
        allow_n = (self[:, new_blk:new_blk + 1] > 0.5) & (kpos_n <= pos1) & (tnew < dec)
        o_s = two_part_attention(qg, k_sel, v_sel, _tile_rows(jnp.where(allow_p, 0.0, NEG), NSA_GRP),
                                 kn_sel, vn_sel, _tile_rows(jnp.where(allow_n, 0.0, NEG), NSA_GRP))
        o_w = two_part_attention(qg, k_win, v_win, bias_wp, kn_win, vn_win, bias_wn)
        outs.append(_gate_cols(gates, g, 0, r) * o_c + _gate_cols(gates, g, 1, r) * o_s
                    + _gate_cols(gates, g, 2, r) * o_w)
    out_ref[...] = _nsa_assemble(outs[0], outs[1], r)


def nsa_sample(z, page_table, cache_t, cache_win_t, cw, e_mat, ov, layer, row0, dbsz, past, dec):
    n_pages = past // PAGE_SIZE
    rb0 = row0 // SEQ_PAD
    wk = cache_win_t.shape[3]
    c2 = lambda b, pt: (0, 0)
    c3 = lambda b, pt: (0, 0, 0)
    grid_spec = pltpu.PrefetchScalarGridSpec(
        num_scalar_prefetch=1,
        grid=(dbsz,),
        in_specs=[pl.BlockSpec((SEQ_PAD, NSA_HEADS * LANES), lambda b, pt: (rb0 + b, 0)),
                  pl.BlockSpec((SEQ_PAD, 1024), lambda b, pt: (rb0 + b, ZKV // 1024)),
                  pl.BlockSpec((1, 1, 2 * LANES, wk), lambda b, pt: (layer, b, 0, 0)),
                  pl.BlockSpec(memory_space=pl.ANY),
                  pl.BlockSpec((CMP_STRIDE, 256), c2),
                  pl.BlockSpec((CMP_STRIDE, 256), c2),
                  pl.BlockSpec((CMP_STRIDE, 256, 256), c3),
                  pl.BlockSpec((CMP_STRIDE, 256, 256), c3),
                  pl.BlockSpec((256, 256), c2),
                  pl.BlockSpec((LANES, past), c2),
                  pl.BlockSpec((past // CMP_STRIDE, LANES), c2)],
        out_specs=pl.BlockSpec((SEQ_PAD, MIX_WIDTH), lambda b, pt: (b, 0)),
        scratch_shapes=[pltpu.VMEM((2, 4 * LANES, past), F32), pltpu.VMEM((2, past, LANES), F32),
                        pltpu.SemaphoreType.DMA((2,))],
    )
    return pl.pallas_call(
        functools.partial(_nsa_sample_kernel, layer=layer, past=past, n_pages=n_pages, dec=dec),
        grid_spec=grid_spec,
        out_shape=jax.ShapeDtypeStruct((dbsz * SEQ_PAD, MIX_WIDTH), F32),
        compiler_params=_cparams(("arbitrary",)),
        name="nsa_sample",
    )(page_table.reshape(-1), z, z, cache_win_t, cache_t, cw["pea"], cw["peb"], cw["w1a"], cw["w1b"], cw["w2"],
      e_mat, ov)


def _seg_masks(r, c):
    ri = lax.broadcasted_iota(jnp.int32, (r, r), 0)
    ci = lax.broadcasted_iota(jnp.int32, (r, r), 1)
    same = (ri // c) == (ci // c)
    return same, same & (ri >= ci), same & (ri > ci)


def _gdn_prep(y, ab, par_ref, valid, c, u_ref, w_ref, qe_ref, qk_ref, kdt_ref, eg_ref):
    r = y.shape[0]
    same, incl, strict = _seg_masks(r, c)
    y = _silu(y)
    beta_all = _sigmoid(ab)
    g_all = -jnp.exp(par_ref[0:1, :]) * _softplus(ab + par_ref[1:2, :])
    if valid is not None:
        y = jnp.where(valid, y, 0.0)
        beta_all = jnp.where(valid, beta_all, 0.0)
        g_all = jnp.where(valid, g_all, 0.0)
    gcum_all = _dot_mask_left(jnp.where(incl, 1.0, 0.0), g_all)
    glast_all = _dot_mask_left(jnp.where(same, 1.0, 0.0), g_all)
    gcum_t = jnp.transpose(gcum_all)
    eye = jnp.where(lax.broadcasted_iota(jnp.int32, (r, r), 0) == lax.broadcasted_iota(jnp.int32, (r, r), 1),
                    1.0, 0.0)
    n_double = int(math.log2(c))
    tms, pws, rhss = [], [], []
    for h in range(GDN_HEADS):
        q = y[:, h * GDN_HD:(h + 1) * GDN_HD]
        k = y[:, MIX_WIDTH + h * GDN_HD:MIX_WIDTH + (h + 1) * GDN_HD]
        v = y[:, 2 * MIX_WIDTH + h * GDN_HD:2 * MIX_WIDTH + (h + 1) * GDN_HD]
        q = q * lax.rsqrt(jnp.sum(q * q, axis=-1, keepdims=True) + EPS) * (GDN_HD ** -0.5)
        k = k * lax.rsqrt(jnp.sum(k * k, axis=-1, keepdims=True) + EPS)
        g1 = jnp.broadcast_to(gcum_all[:, h:h + 1], (r, GDN_HD))
        g2 = jnp.broadcast_to(gcum_t[h:h + 1, :], (r, r))
        gl = jnp.broadcast_to(glast_all[:, h:h + 1], (r, GDN_HD))
        beta = jnp.broadcast_to(beta_all[:, GDN_HEADS + h:GDN_HEADS + h + 1], (r, GDN_HD))
        g1r = g1 if r == GDN_HD else jnp.broadcast_to(gcum_all[:, h:h + 1], (r, r))
        decay = jnp.where(incl, jnp.exp(jnp.where(incl, g1r - g2, 0.0)), 0.0)
        betar = beta if r == GDN_HD else jnp.broadcast_to(beta_all[:, GDN_HEADS + h:GDN_HEADS + h + 1], (r, r))
        a = jnp.where(strict, _bdot_nt(k, k) * decay * betar, 0.0)
        tms.append(eye - a)
        pws.append(a)
        eg1 = jnp.exp(g1)
        rhss.append(jnp.concatenate([v * beta, k * (beta * eg1)], axis=1))
        qe_ref[h] = q * eg1
        qk_ref[h] = jnp.where(incl, _bdot_nt(q, k) * decay, 0.0)
        kdt_ref[h] = jnp.transpose(k * jnp.exp(gl - g1))
        eg_ref[h] = jnp.exp(gl)
    for _ in range(n_double - 1):
        pws = [_dot3(pw, pw) for pw in pws]
        tms = [tm + _dot3(tm, pw) for tm, pw in zip(tms, pws)]
    for h in range(GDN_HEADS):
        sol = _dot3(tms[h], rhss[h])
        u_ref[h] = sol[:, 0:GDN_HD]
        w_ref[h] = sol[:, GDN_HD:2 * GDN_HD]


def _gdn_segment(h, row0, seg, c, s, u_ref, w_ref, qe_ref, qk_ref, kdt_ref, eg_ref, vn_ref, col_seg):
    rows = pl.ds(row0, c)
    vn = u_ref[h, rows, :] - _bdot(w_ref[h, rows, :], s)
    vn_ref[h, rows, :] = vn
    vn_all = vn_ref[h].astype(BF16)
    o = _bdot(qe_ref[h, rows, :], s) + _dot(qk_ref[h, rows, :].astype(BF16), vn_all)
    kdt = jnp.where(col_seg == seg, kdt_ref[h], 0.0)
    s_new = s * eg_ref[h, pl.ds(row0, 1), :] + _dot(kdt.astype(BF16), vn_all)
    return o, s_new


def _gdn_finish(o, zg, ng_ref):
    return _rms(o, ng_ref[...]) * _silu(zg)


GDN_TILE = 128


def _gdn_prompt_kernel(x_ref, ab_ref, zg_ref, cw_ref, par_ref, ng_ref, out_ref, st_ref,
                       halo_ref, u_ref, w_ref, qe_ref, qk_ref, kdt_ref, eg_ref, vn_ref):
    i = pl.program_id(1)
    r = GDN_TILE
    c = GDN_CHUNK

    @pl.when(i == 0)
    def _():
        halo_ref[...] = jnp.zeros_like(halo_ref)
        st_ref[...] = jnp.zeros_like(st_ref)

    x = x_ref[...]
    xx = jnp.concatenate([halo_ref[...], x], axis=0)
    halo_ref[...] = x[r - SUBLANES:r, :]
    y = None
    for j in range(CONV_W):
        sh = CONV_W - 1 - j
        xs = xx if sh == 0 else pltpu.roll(xx, sh, axis=0)
        t = xs[SUBLANES:, :] * cw_ref[j:j + 1, :]
        y = t if y is None else y + t
    _gdn_prep(y, ab_ref[...], par_ref, None, c, u_ref, w_ref, qe_ref, qk_ref, kdt_ref, eg_ref)
    vn_ref[...] = jnp.zeros_like(vn_ref)
    col_seg = lax.broadcasted_iota(jnp.int32, (GDN_HD, r), 1) // c
    states = [st_ref[0, h] for h in range(GDN_HEADS)]
    os_ = [[] for _ in range(GDN_HEADS)]
    for ck in range(r // c):
        for h in range(GDN_HEADS):
            o, states[h] = _gdn_segment(h, ck * c, ck, c, states[h], u_ref, w_ref, qe_ref, qk_ref, kdt_ref,
                                        eg_ref, vn_ref, col_seg)
            os_[h].append(o)
    for h in range(GDN_HEADS):
        st_ref[0, h] = states[h]
        o = jnp.concatenate(os_[h], axis=0)
        out_ref[:, h * GDN_HD:(h + 1) * GDN_HD] = _gdn_finish(o, zg_ref[:, h * GDN_HD:(h + 1) * GDN_HD], ng_ref)


def _gdn_scratch(r):
    hs = (GDN_HEADS, r, GDN_HD)
    return [pltpu.VMEM(hs, F32), pltpu.VMEM(hs, F32), pltpu.VMEM(hs, F32), pltpu.VMEM((GDN_HEADS, r, r), F32),
            pltpu.VMEM((GDN_HEADS, GDN_HD, r), F32), pltpu.VMEM(hs, F32), pltpu.VMEM(hs, F32)]


def gdn_prompt(z, conv_w, par, norm_g, bsz, t):
    nt = t // GDN_TILE
    r = GDN_TILE
    c2 = lambda b, i: (0, 0)
    return pl.pallas_call(
        _gdn_prompt_kernel,
        grid=(bsz, nt),
        in_specs=[pl.BlockSpec((r, 3 * MIX_WIDTH), lambda b, i: (b * nt + i, ZGQ // (3 * MIX_WIDTH))),
                  pl.BlockSpec((r, LANES), lambda b, i: (b * nt + i, ZGAB // LANES)),
                  pl.BlockSpec((r, MIX_WIDTH), lambda b, i: (b * nt + i, ZGZ // MIX_WIDTH)),
                  pl.BlockSpec((CONV_W, 3 * MIX_WIDTH), c2),
                  pl.BlockSpec((SUBLANES, LANES), c2),
                  pl.BlockSpec((1, GDN_HD), c2)],
        out_specs=[pl.BlockSpec((r, MIX_WIDTH), lambda b, i: (b * nt + i, 0)),
                   pl.BlockSpec((1, GDN_HEADS, GDN_HD, GDN_HD), lambda b, i: (b, 0, 0, 0))],
        out_shape=[jax.ShapeDtypeStruct((bsz * t, MIX_WIDTH), F32),
                   jax.ShapeDtypeStruct((bsz, GDN_HEADS, GDN_HD, GDN_HD), F32)],
        scratch_shapes=[pltpu.VMEM((SUBLANES, 3 * MIX_WIDTH), F32)] + _gdn_scratch(r),
        compiler_params=_cparams(("parallel", "arbitrary")),
        name="gdn_prompt",
    )(z, z, z, conv_w, par, norm_g)


SEQ_PER_STEP = 16


def _gdn_sample_kernel(x_ref, ab_ref, zg_ref, prev_ref, s0_ref, cw_ref, par_ref, ng_ref, out_ref, st_ref,
                       u_ref, w_ref, qe_ref, qk_ref, kdt_ref, eg_ref, vn_ref, o_ref, *, dec):
    r = SEQ_PER_STEP * SEQ_PAD
    c = SEQ_PAD
    row = lax.broadcasted_iota(jnp.int32, (r, 1), 0)
    tin = row % c
    valid = tin < dec
    xx = jnp.where(tin < CONV_W - 1, prev_ref[...], pltpu.roll(x_ref[...], CONV_W - 1, axis=0))
    y = None
    for j in range(CONV_W):
        xs = xx if j == 0 else pltpu.roll(xx, r - j, axis=0)
        t = xs * cw_ref[j:j + 1, :]
        y = t if y is None else y + t
    _gdn_prep(y, ab_ref[...], par_ref, valid, c, u_ref, w_ref, qe_ref, qk_ref, kdt_ref, eg_ref)
    vn_ref[...] = jnp.zeros_like(vn_ref)
    col_seg = lax.broadcasted_iota(jnp.int32, (GDN_HD, r), 1) // c

    def body(sq, carry):
        row0 = pl.multiple_of(sq * c, c)
        for h in range(GDN_HEADS):
            o, s = _gdn_segment(h, row0, sq, c, s0_ref[0, sq, h], u_ref, w_ref, qe_ref, qk_ref, kdt_ref, eg_ref,
                                vn_ref, col_seg)
            st_ref[sq, h] = s
            o_ref[h, pl.ds(row0, c), :] = o
        return carry

    lax.fori_loop(0, SEQ_PER_STEP, body, 0, unroll=4)
    for h in range(GDN_HEADS):
        out_ref[:, h * GDN_HD:(h + 1) * GDN_HD] = _gdn_finish(o_ref[h], zg_ref[:, h * GDN_HD:(h + 1) * GDN_HD],
                                                              ng_ref)


def gdn_sample(z, prev_pad, s0_all, layer, conv_w, par, norm_g, row0, dbsz, dec):
    r = SEQ_PER_STEP * SEQ_PAD
    rb0 = row0 // r
    c2 = lambda i: (0, 0)
    return pl.pallas_call(
        functools.partial(_gdn_sample_kernel, dec=dec),
        grid=(dbsz // SEQ_PER_STEP,),
        in_specs=[pl.BlockSpec((r, 3 * MIX_WIDTH), lambda i: (rb0 + i, ZGQ // (3 * MIX_WIDTH))),
                  pl.BlockSpec((r, LANES), lambda i: (rb0 + i, ZGAB // LANES)),
                  pl.BlockSpec((r, MIX_WIDTH), lambda i: (rb0 + i, ZGZ // MIX_WIDTH)),
                  pl.BlockSpec((r, 3 * MIX_WIDTH), lambda i: (i, 0)),
                  pl.BlockSpec((1, SEQ_PER_STEP, GDN_HEADS, GDN_HD, GDN_HD), lambda i: (layer, i, 0, 0, 0)),
                  pl.BlockSpec((CONV_W, 3 * MIX_WIDTH), c2),
                  pl.BlockSpec((SUBLANES, LANES), c2),
                  pl.BlockSpec((1, GDN_HD), c2)],
        out_specs=[pl.BlockSpec((r, MIX_WIDTH), lambda i: (i, 0)),
                   pl.BlockSpec((SEQ_PER_STEP, GDN_HEADS, GDN_HD, GDN_HD), lambda i: (i, 0, 0, 0))],
        out_shape=[jax.ShapeDtypeStruct((dbsz * SEQ_PAD, MIX_WIDTH), F32),
                   jax.ShapeDtypeStruct((dbsz, GDN_HEADS, GDN_HD, GDN_HD), F32)],
        scratch_shapes=_gdn_scratch(r) + [pltpu.VMEM((GDN_HEADS, r, GDN_HD), F32)],
        compiler_params=_cparams(("parallel",)),
        name="gdn_sample",
    )(z, z, z, prev_pad, s0_all, conv_w, par, norm_g)


def _ret_log_gamma(h):
    return math.log1p(-(2.0 ** (-5.0 - h)))


def _ret_prep(x, cs, sn, valid, h):
    qh = x[:, h * RET_HD:(h + 1) * RET_HD]
    kh = x[:, MIX_WIDTH + h * RET_HD:MIX_WIDTH + (h + 1) * RET_HD]
    vh = x[:, 2 * MIX_WIDTH + h * RET_HD:2 * MIX_WIDTH + (h + 1) * RET_HD]
    qr = qh * cs + pltpu.roll(qh, RET_HD // 2, axis=1) * sn
    kr = (kh * cs + pltpu.roll(kh, RET_HD // 2, axis=1) * sn) * (RET_HD ** -0.5)
    if valid is not None:
        kr = jnp.where(valid, kr, 0.0)
        vh = jnp.where(valid, vh, 0.0)
    return qr, kr, vh


def _ret_finish(o, gate):
    o = o * lax.rsqrt(jnp.mean(o * o, axis=-1, keepdims=True) + EPS)
    return o * _silu(gate)


def _ret_prompt_kernel(x_ref, gt_ref, cs_ref, sn_ref, out_ref, st_ref):
    i = pl.program_id(1)
    c = RET_CHUNK

    @pl.when(i == 0)
    def _():
        st_ref[...] = jnp.zeros_like(st_ref)

    x = x_ref[...]
    cs = cs_ref[...]
    sn = sn_ref[...]
    ri = lax.broadcasted_iota(jnp.int32, (c, c), 0)
    ci = lax.broadcasted_iota(jnp.int32, (c, c), 1)
    diff = (ri - ci).astype(F32)
    n = lax.broadcasted_iota(jnp.int32, (c, 1), 0).astype(F32)
    heads = range(RET_HEADS)
    lgs = [_ret_log_gamma(h) for h in heads]
    qkv = [_ret_prep(x, cs, sn, None, h) for h in heads]
    states = [st_ref[0, h] for h in heads]
    scores = [_bdot_nt(qkv[h][0], qkv[h][1]) * jnp.where(diff >= 0.0, jnp.exp(jnp.maximum(diff, 0.0) * lgs[h]), 0.0)
              for h in heads]
    cross = [_bdot(qkv[h][0], states[h]) * jnp.exp((n + 1.0) * lgs[h]) for h in heads]
    outs = [_bdot(scores[h], qkv[h][2]) + cross[h] for h in heads]
    kdts = [jnp.transpose(qkv[h][1] * jnp.exp((c - 1.0 - n) * lgs[h])).astype(BF16) for h in heads]
    for h in heads:
        st_ref[0, h] = states[h] * math.exp(c * lgs[h]) + _dot(kdts[h], qkv[h][2].astype(BF16))
        out_ref[:, h * RET_HD:(h + 1) * RET_HD] = _ret_finish(outs[h], gt_ref[:, h * RET_HD:(h + 1) * RET_HD])


def ret_prompt(z, cs, sn, bsz, t):
    c = RET_CHUNK
    nt = t // c
    return pl.pallas_call(
        _ret_prompt_kernel,
        grid=(bsz, nt),
        in_specs=[pl.BlockSpec((c, 3 * MIX_WIDTH), lambda b, i: (b * nt + i, ZRQ // (3 * MIX_WIDTH))),
                  pl.BlockSpec((c, MIX_WIDTH), lambda b, i: (b * nt + i, ZRG // MIX_WIDTH)),
                  pl.BlockSpec((c, RET_HD), lambda b, i: (i, 0)),
                  pl.BlockSpec((c, RET_HD), lambda b, i: (i, 0))],
        out_specs=[pl.BlockSpec((c, MIX_WIDTH), lambda b, i: (b * nt + i, 0)),
                   pl.BlockSpec((1, RET_HEADS, RET_HD, RET_HD), lambda b, i: (b, 0, 0, 0))],
        out_shape=[jax.ShapeDtypeStruct((bsz * t, MIX_WIDTH), F32),
                   jax.ShapeDtypeStruct((bsz, RET_HEADS, RET_HD, RET_HD), F32)],
        compiler_params=_cparams(("parallel", "arbitrary")),
        name="ret_prompt",
    )(z, z, cs, sn)


def _ret_sample_kernel(x_ref, gt_ref, cs_ref, sn_ref, s0_ref, out_ref, st_ref, q_ref, kdt_ref, v_ref, o_ref,
                       *, dec):
    r = SEQ_PER_STEP * SEQ_PAD
    c = SEQ_PAD
    row = lax.broadcasted_iota(jnp.int32, (r, 1), 0)
    tin = row % c
    valid = tin < dec
    n = tin.astype(F32)
    x = x_ref[...]
    cs = cs_ref[...]
    sn = sn_ref[...]
    same, incl, _ = _seg_masks(r, c)
    ri = lax.broadcasted_iota(jnp.int32, (r, r), 0)
    ci = lax.broadcasted_iota(jnp.int32, (r, r), 1)
    diff = (ri - ci).astype(F32)
    for h in range(RET_HEADS):
        lg = _ret_log_gamma(h)
        qr, kr, vh = _ret_prep(x, cs, sn, valid, h)
        dmat = jnp.where(incl, jnp.exp(jnp.maximum(diff, 0.0) * lg), 0.0)
        o_ref[h] = _bdot(_bdot_nt(qr, kr) * dmat, vh)
        q_ref[h] = qr
        kdt_ref[h] = jnp.transpose(kr * jnp.exp((dec - 1.0 - n) * lg))
        v_ref[h] = vh
    col_seg = lax.broadcasted_iota(jnp.int32, (RET_HD, r), 1) // c
    qdec = [jnp.exp((lax.broadcasted_iota(jnp.int32, (c, 1), 0).astype(F32) + 1.0) * _ret_log_gamma(h))
            for h in range(RET_HEADS)]

    def body(sq, carry):
        row0 = pl.multiple_of(sq * c, c)
        rows = pl.ds(row0, c)
        for h in range(RET_HEADS):
            s = s0_ref[0, sq, h]
            o_ref[h, rows, :] = o_ref[h, rows, :] + _bdot(q_ref[h, rows, :], s) * qdec[h]
            kdt = jnp.where(col_seg == sq, kdt_ref[h], 0.0)
            st_ref[sq, h] = s * math.exp(dec * _ret_log_gamma(h)) + _dot(kdt.astype(BF16), v_ref[h].astype(BF16))
        return carry

    lax.fori_loop(0, SEQ_PER_STEP, body, 0, unroll=4)
    for h in range(RET_HEADS):
        out_ref[:, h * RET_HD:(h + 1) * RET_HD] = _ret_finish(o_ref[h], gt_ref[:, h * RET_HD:(h + 1) * RET_HD])


def ret_sample(z, cs, sn, s0_all, layer, row0, dbsz, dec):
    r = SEQ_PER_STEP * SEQ_PAD
    rb0 = row0 // r
    hs = (RET_HEADS, r, RET_HD)
    return pl.pallas_call(
        functools.partial(_ret_sample_kernel, dec=dec),
        grid=(dbsz // SEQ_PER_STEP,),
        in_specs=[pl.BlockSpec((r, 3 * MIX_WIDTH), lambda i: (rb0 + i, ZRQ // (3 * MIX_WIDTH))),
                  pl.BlockSpec((r, MIX_WIDTH), lambda i: (rb0 + i, ZRG // MIX_WIDTH)),
                  pl.BlockSpec((r, RET_HD), lambda i: (0, 0)),
                  pl.BlockSpec((r, RET_HD), lambda i: (0, 0)),
                  pl.BlockSpec((1, SEQ_PER_STEP, RET_HEADS, RET_HD, RET_HD), lambda i: (layer, i, 0, 0, 0))],
        out_specs=[pl.BlockSpec((r, MIX_WIDTH), lambda i: (i, 0)),
                   pl.BlockSpec((SEQ_PER_STEP, RET_HEADS, RET_HD, RET_HD), lambda i: (i, 0, 0, 0))],
        out_shape=[jax.ShapeDtypeStruct((dbsz * SEQ_PAD, MIX_WIDTH), F32),
                   jax.ShapeDtypeStruct((dbsz, RET_HEADS, RET_HD, RET_HD), F32)],
        scratch_shapes=[pltpu.VMEM(hs, F32), pltpu.VMEM((RET_HEADS, RET_HD, r), F32), pltpu.VMEM(hs, F32),
                        pltpu.VMEM(hs, F32)],
        compiler_params=_cparams(("parallel",)),
        name="ret_sample",
    )(z, z, cs, sn, s0_all)


def _prep_w_in(w):
    sizes = (NSA_HEADS * NSA_HD, 6 * NSA_KV * NSA_HD, 3 * NSA_HEADS, 3 * MIX_WIDTH, GDN_HEADS, GDN_HEADS,
             MIX_WIDTH, 3 * MIX_WIDTH, MIX_WIDTH, N_BRANCH * D_MODEL)
    src = np.concatenate([[0], np.cumsum(sizes)])
    s_q, s_kv, s_ng, s_gq, s_ga, s_gb, s_gz, s_rq, s_rg, s_mg = src[:-1].tolist()
    wb = w.astype(BF16)
    out = jnp.zeros((w.shape[0], ZW), BF16)

    def put(o, dst, s0, n):
        return lax.dynamic_update_slice(o, lax.slice_in_dim(wb, s0, s0 + n, axis=1), (0, dst))

    for h in range(NSA_HEADS):
        out = put(out, ZQ + h * LANES + NSA_HD * (h // NSA_GRP), s_q + h * NSA_HD, NSA_HD)
    out = put(out, ZKV, s_kv, sizes[1] + sizes[2])
    out = put(out, ZGAB, s_ga, 2 * GDN_HEADS)
    out = put(out, ZGZ, s_gz, MIX_WIDTH)
    out = put(out, ZRG, s_rg, MIX_WIDTH)
    out = put(out, ZGQ, s_gq, 3 * MIX_WIDTH)
    out = put(out, ZRQ, s_rq, 3 * MIX_WIDTH)
    out = put(out, ZMG, s_mg, N_BRANCH * D_MODEL)
    return out


def _prep_w_branch(wb):
    w0 = wb[0].reshape(NSA_HEADS, NSA_HD, D_MODEL)
    order = [h for j in range(NSA_GRP) for h in (j, j + NSA_GRP)]
    w0 = w0[np.array(order)].reshape(MIX_WIDTH, D_MODEL)
    return jnp.stack([w0, wb[1], wb[2]]).astype(BF16)


def _prep_compress(pe, w1, w2):
    sel_k = jnp.asarray(np.diag([1.0, 1.0, 0.0, 0.0]).astype(np.float32))
    sel_v = jnp.asarray(np.diag([0.0, 0.0, 1.0, 1.0]).astype(np.float32))

    def blockdiag(mk, mv):
        full = (sel_k[:, None, :, None] * mk[..., None, :, None, :]
                + sel_v[:, None, :, None] * mv[..., None, :, None, :])
        return full.reshape(mk.shape[:-2] + (4 * mk.shape[-2], 4 * mk.shape[-1]))

    w1r = w1.reshape(2, CMP_LEN, NSA_HD, CMP_HIDDEN)
    w1a = blockdiag(w1r[0, :CMP_STRIDE], w1r[1, :CMP_STRIDE])
    w1b = blockdiag(w1r[0, CMP_STRIDE:], w1r[1, CMP_STRIDE:])
    pea = jnp.concatenate([pe[0, :CMP_STRIDE], pe[0, :CMP_STRIDE], pe[1, :CMP_STRIDE], pe[1, :CMP_STRIDE]], axis=1)
    peb = jnp.concatenate([pe[0, CMP_STRIDE:], pe[0, CMP_STRIDE:], pe[1, CMP_STRIDE:], pe[1, CMP_STRIDE:]], axis=1)
    w2b = blockdiag(w2[0], w2[1])
    return dict(pea=pea, peb=peb, w1a=w1a.astype(BF16), w1b=w1b.astype(BF16), w2=w2b.astype(BF16))


def _overlap_matrix(nc_rows, nc, ns):
    n = np.arange(nc_rows)[:, None]
    s = np.arange(LANES)[None, :]
    c_start = n * CMP_STRIDE
    ov = (c_start < (s + 1) * SEL_LEN) & (s * SEL_LEN < c_start + CMP_LEN) & (n < nc) & (s < ns)
    return jnp.asarray(ov.astype(np.float32))


def _expand_matrix(nkeys):
    s = np.arange(LANES)[:, None]
    k = np.arange(nkeys)[None, :]
    return (k // SEL_LEN == s).astype(np.float32)


def _rope_tables(pos):
    half = RET_HD // 2
    inv = ROPE_BASE ** (-jnp.linspace(0.0, 1.0, half, dtype=F32))
    ang = pos.astype(F32)[:, None] * inv[None, :]
    cos, sin = jnp.cos(ang), jnp.sin(ang)
    return jnp.concatenate([cos, cos], axis=1), jnp.concatenate([-sin, sin], axis=1)


def _pad_rows(a, n):
    return jnp.pad(a, ((0, 0), (0, n - a.shape[1])) + ((0, 0),) * (a.ndim - 2))


def kernel(x_prompt, x_sample, cache_nsa_kv, cache_nsa_win, state_gdn_conv, state_gdn, state_ret, page_table,
           p_prompt, p_sample, g_mix, w_in, nsa_cmp_pe, nsa_cmp_w1, nsa_cmp_w2, gdn_conv_w, gdn_a_log,
           gdn_dt_bias, gdn_norm_g, w_branch, w_out, g_mlp, w_up, w_down, w_ple, w_ple_gate, g_final):
    bsz, t, d = x_prompt.shape
    dbsz, dec, _ = x_sample.shape
    depth = w_in.shape[0]
    past = page_table.shape[1] * PAGE_SIZE
    n_p = bsz * t
    n_s = dbsz * SEQ_PAD
    assert dec <= SEQ_PAD and dec >= CONV_W - 1 and (past % SEL_LEN) + dec <= SEL_LEN
    assert t % SEL_TILE == 0 and t >= WINDOW + Q_TILE and past % PAGE_SIZE == 0 and dbsz % SEQ_PER_STEP == 0

    x = jnp.concatenate([x_prompt.reshape(n_p, d), _pad_rows(x_sample, SEQ_PAD).reshape(n_s, d)], axis=0)
    p_all = jnp.concatenate([p_prompt.reshape(depth, n_p, PLE_DIM),
                             jnp.pad(p_sample, ((0, 0), (0, 0), (0, SEQ_PAD - dec), (0, 0))).reshape(depth, n_s, PLE_DIM)],
                            axis=1)

    nc_p = t // CMP_STRIDE - 1
    ov_p = _overlap_matrix(t // CMP_STRIDE, nc_p, t // SEL_LEN)
    shift_p = _shift_matrices(t)
    nc_s = past // CMP_STRIDE - 1
    ov_s = _overlap_matrix(past // CMP_STRIDE, nc_s, -(-(past + dec) // SEL_LEN))
    e_s = jnp.asarray(_expand_matrix(past)).astype(BF16)
    n_phys = cache_nsa_kv.shape[1]
    wk = cache_nsa_win.shape[2]
    cache_t = jnp.transpose(cache_nsa_kv, (0, 1, 3, 4, 5, 2)).reshape(depth, n_phys, 4 * LANES, PAGE_SIZE)
    cwin_t = jnp.transpose(cache_nsa_win, (0, 1, 3, 4, 5, 2)).reshape(depth, dbsz, 2 * LANES, wk)
    cs_p, sn_p = _rope_tables(jnp.arange(t))
    cs_s, sn_s = _rope_tables(past + jnp.arange(SEQ_PAD))
    cs_s = jnp.tile(cs_s, (SEQ_PER_STEP, 1))
    sn_s = jnp.tile(sn_s, (SEQ_PER_STEP, 1))

    outs = [[] for _ in range(10)]
    for i in range(depth):
        w_in_p = _prep_w_in(w_in[i])
        wb = _prep_w_branch(w_branch[i])
        cw = _prep_compress(nsa_cmp_pe[i], nsa_cmp_w1[i], nsa_cmp_w2[i])
        par = jnp.zeros((SUBLANES, LANES), F32).at[0, :GDN_HEADS].set(gdn_a_log[i]).at[1, :GDN_HEADS].set(gdn_dt_bias[i])
        ng = gdn_norm_g[i].reshape(1, GDN_HD)

        z = in_proj(x, g_mix[i].reshape(1, d), w_in_p)
        kv_all = lax.slice(z, (0, ZKV), (n_p + n_s, ZNG))
        kv_p = kv_all[:n_p]
        kv_s = kv_all[n_p:].reshape(dbsz, SEQ_PAD, ZNG - ZKV)[:, :dec]

        kcmp_sub = kv_p[:, :2 * LANES].reshape(n_p // CMP_STRIDE, CMP_STRIDE * 2 * LANES)
        kcvc = compress_prompt(kcmp_sub, cw, bsz, t)
        o_nsa_p = nsa_prompt(z, kcvc, prompt_kv_slab(z, n_p, t), shift_p, ov_p, bsz, t)
        o_nsa_s = nsa_sample(z, page_table, cache_t, cwin_t, cw, e_s, ov_s, i, n_p, dbsz, past, dec)

        o_gdn_p, gs_p = gdn_prompt(z, gdn_conv_w[i], par, ng, bsz, t)
        prev_pad = _pad_rows(state_gdn_conv[i], SEQ_PAD).reshape(n_s, 3 * MIX_WIDTH)
        o_gdn_s, gs_s = gdn_sample(z, prev_pad, state_gdn, i, gdn_conv_w[i], par, ng, n_p, dbsz, dec)

        o_ret_p, rs_p = ret_prompt(z, cs_p, sn_p, bsz, t)
        o_ret_s, rs_s = ret_sample(z, cs_s, sn_s, state_ret, i, n_p, dbsz, dec)

        x = merge(x, (o_nsa_p, o_gdn_p, o_ret_p), (o_nsa_s, o_gdn_s, o_ret_s), z, wb, w_out[i].astype(BF16))
        x = mlp_ple(x, g_mlp[i].reshape(1, d), w_up[i].astype(BF16), w_down[i].astype(BF16), p_all[i],
                    w_ple[i].astype(BF16), w_ple_gate[i].astype(BF16), g_final.reshape(1, d),
                    final_norm=(i == depth - 1))

        nkv = ZKW - ZKV
        outs[0].append(kv_p[:, :nkv].reshape(bsz, t, 4, NSA_KV, NSA_HD))
        outs[1].append(kv_s[:, :, :nkv].reshape(dbsz, dec, 4, NSA_KV, NSA_HD))
        wlen = min(WINDOW, t)
        outs[2].append(kv_p.reshape(bsz, t, ZNG - ZKV)[:, t - wlen:, nkv:].reshape(bsz, wlen, 2, NSA_KV, NSA_HD))
        win = jnp.concatenate([cache_nsa_win[i], kv_s[:, :, nkv:].reshape(dbsz, dec, 2, NSA_KV, NSA_HD)], axis=1)
        wlen_s = min(WINDOW, past + dec)
        outs[3].append(win[:, win.shape[1] - wlen_s:])
        nconv = CONV_W - 1
        outs[4].append(jnp.stack([lax.slice(z, (b * t + t - nconv, ZGQ), (b * t + t, ZGQ + 3 * MIX_WIDTH))
                                  for b in range(bsz)]))
        gq_s = lax.slice(z, (n_p, ZGQ), (n_p + n_s, ZGQ + 3 * MIX_WIDTH)).reshape(dbsz, SEQ_PAD, 3 * MIX_WIDTH)
        outs[5].append(gq_s[:, dec - nconv:dec])
        outs[6].append(gs_p)
        outs[7].append(gs_s)
        outs[8].append(rs_p)
        outs[9].append(rs_s)

    y_prompt = x[:n_p].reshape(bsz, t, d)
    y_sample = x[n_p:].reshape(dbsz, SEQ_PAD, d)[:, :dec]
    return (y_prompt, y_sample) + tuple(jnp.stack(o) for o in outs)
```

```python
import functools
import math

import numpy as np
import jax
import jax.numpy as jnp
from jax import lax
from jax.experimental import pallas as pl
from jax.experimental.pallas import tpu as pltpu

F32 = jnp.float32
BF16 = jnp.bfloat16

D_MODEL = 1024
MIX_WIDTH = D_MODEL // 2
N_BRANCH = 3
NSA_HEADS = 8
NSA_HD = 64
NSA_KV = 2
NSA_GRP = NSA_HEADS // NSA_KV
CMP_LEN = 32
CMP_STRIDE = 16
CMP_HIDDEN = 64
SEL_LEN = 64
SEL_TOPK = 16
WINDOW = 512
FORCE_BONUS = 1000.0
GDN_HEADS = 4
GDN_HD = 128
CONV_W = 4
GDN_CHUNK = 64
RET_HEADS = 4
RET_HD = 128
RET_CHUNK = 128
ROPE_BASE = 10000.0
D_FF = 4 * D_MODEL
PLE_DIM = 256
EPS = 1e-6
NEG = -1e30
PAGE_SIZE = 128

LANES = 128
SUBLANES = 8
SEQ_PAD = SUBLANES
VMEM_LIMIT = 56 * 1024 * 1024

ZQ = 0
ZKV = 1024
ZKW = 1536
ZNG = 1792
ZGAB = 1920
ZGZ = 2048
ZRG = 2560
ZGQ = 3072
ZRQ = 4608
ZMG = 6144
ZW = 9216


def _sigmoid(x):
    return 1.0 / (1.0 + jnp.exp(-x))


def _silu(x):
    return x * _sigmoid(x)


def _softplus(x):
    return jnp.maximum(x, 0.0) + jnp.log1p(jnp.exp(-jnp.abs(x)))


def _gelu_tanh(x):
    return 0.5 * x * (1.0 + jnp.tanh(math.sqrt(2.0 / math.pi) * (x + 0.044715 * (x * x * x))))


def _rms(x, g):
    return x * lax.rsqrt(jnp.mean(x * x, axis=-1, keepdims=True) + EPS) * g


def _dot(a, b):
    return jnp.dot(a, b, preferred_element_type=F32)


def _dot_nt(a, b):
    return lax.dot_general(a, b, (((1,), (1,)), ((), ())), preferred_element_type=F32)


def _bdot(a, b):
    return _dot(a.astype(BF16), b.astype(BF16))


def _bdot_nt(a, b):
    return _dot_nt(a.astype(BF16), b.astype(BF16))


def _split2(a):
    hi = a.astype(BF16)
    lo = (a - hi.astype(F32)).astype(BF16)
    return hi, lo


def _dot3(a, b):
    ah, al = _split2(a)
    bh, bl = _split2(b)
    return _dot(ah, bh) + (_dot(ah, bl) + _dot(al, bh))


def _dot_mask_left(m01, x):
    m = m01.astype(BF16)
    x1 = x.astype(BF16)
    r1 = x - x1.astype(F32)
    x2 = r1.astype(BF16)
    x3 = (r1 - x2.astype(F32)).astype(BF16)
    return _dot(m, x1) + (_dot(m, x2) + _dot(m, x3))


def _pick_tile(n, prefs):
    for p in prefs:
        if n % p == 0:
            return p
    raise ValueError(f"no tile in {prefs} divides {n}")


def _cparams(sem):
    return pltpu.CompilerParams(dimension_semantics=sem, vmem_limit_bytes=VMEM_LIMIT)


def _in_kernel(x_ref, g_ref, w_ref, z_ref, h_ref):
    @pl.when(pl.program_id(1) == 0)
    def _():
        h_ref[...] = _rms(x_ref[...], g_ref[...]).astype(BF16)

    z_ref[...] = _dot(h_ref[...], w_ref[...])


def in_proj(x, g, w):
    n = x.shape[0]
    tm = _pick_tile(n, (1024, 512, 256, 128))
    tn = 1024
    return pl.pallas_call(
        _in_kernel,
        grid=(n // tm, ZW // tn),
        in_specs=[pl.BlockSpec((tm, D_MODEL), lambda i, j: (i, 0)),
                  pl.BlockSpec((1, D_MODEL), lambda i, j: (0, 0)),
                  pl.BlockSpec((D_MODEL, tn), lambda i, j: (0, j))],
        out_specs=pl.BlockSpec((tm, tn), lambda i, j: (i, j)),
        out_shape=jax.ShapeDtypeStruct((n, ZW), F32),
        scratch_shapes=[pltpu.VMEM((tm, D_MODEL), BF16)],
        compiler_params=_cparams(("parallel", "arbitrary")),
        name="in_proj",
    )(x, g, w)


def _merge_kernel(x_ref, onp_ref, ogp_ref, orp_ref, ons_ref, ogs_ref, ors_ref, mg_ref, wb_ref, wo_ref, out_ref,
                  *, n_prompt_tiles):
    is_prompt = pl.program_id(0) < n_prompt_tiles
    m = None
    for b, (op_ref, os_ref) in enumerate(((onp_ref, ons_ref), (ogp_ref, ogs_ref), (orp_ref, ors_ref))):
        o = jnp.where(is_prompt, op_ref[...], os_ref[...])
        br = _dot(o.astype(BF16), wb_ref[b])
        t = _sigmoid(mg_ref[:, b * D_MODEL:(b + 1) * D_MODEL]) * br
        m = t if m is None else m + t
    out_ref[...] = x_ref[...] + _dot(m.astype(BF16), wo_ref[...])


def merge(x, o_prompt, o_sample, z, wb, wo):
    n = x.shape[0]
    n_p = o_prompt[0].shape[0]
    n_s = o_sample[0].shape[0]
    tm = _pick_tile(math.gcd(n_p, n_s), (512, 256, 128))
    npt = n_p // tm
    row = lambda i: (i, 0)
    prow = lambda i: (jnp.minimum(i, npt - 1), 0)
    srow = lambda i: (jnp.maximum(i - npt, 0), 0)
    return pl.pallas_call(
        functools.partial(_merge_kernel, n_prompt_tiles=npt),
        grid=(n // tm,),
        in_specs=[pl.BlockSpec((tm, D_MODEL), row)]
                 + [pl.BlockSpec((tm, MIX_WIDTH), prow)] * N_BRANCH
                 + [pl.BlockSpec((tm, MIX_WIDTH), srow)] * N_BRANCH
                 + [
                  pl.BlockSpec((tm, N_BRANCH * D_MODEL), lambda i: (i, ZMG // (N_BRANCH * D_MODEL))),
                  pl.BlockSpec((N_BRANCH, MIX_WIDTH, D_MODEL), lambda i: (0, 0, 0)),
                  pl.BlockSpec((D_MODEL, D_MODEL), lambda i: (0, 0))],
        out_specs=pl.BlockSpec((tm, D_MODEL), row),
        out_shape=jax.ShapeDtypeStruct((n, D_MODEL), F32),
        compiler_params=_cparams(("parallel",)),
        name="merge",
    )(x, *o_prompt, *o_sample, z, wb, wo)


def _mlp_kernel(x_ref, g_ref, wu_ref, wd_ref, p_ref, wp_ref, wpg_ref, gf_ref, out_ref, h_ref, acc_ref,
                *, final_norm):
    f = pl.program_id(1)

    @pl.when(f == 0)
    def _():
        h_ref[...] = _rms(x_ref[...], g_ref[...]).astype(BF16)
        acc_ref[...] = jnp.zeros_like(acc_ref)

    up = jnp.maximum(_dot(h_ref[...], wu_ref[...]), 0.0)
    acc_ref[...] += _dot((up * up).astype(BF16), wd_ref[...])

    @pl.when(f == pl.num_programs(1) - 1)
    def _():
        x2 = x_ref[...] + acc_ref[...]
        ple = _dot(p_ref[...].astype(BF16), wp_ref[...])
        x3 = x2 + ple * _sigmoid(_dot(x2.astype(BF16), wpg_ref[...]))
        if final_norm:
            x3 = _rms(x3, gf_ref[...])
        out_ref[...] = x3


def mlp_ple(x, g, wu, wd, p, wp, wpg, gf, final_norm):
    n = x.shape[0]
    tm = _pick_tile(n, (512, 256, 128))
    tf = 1024
    return pl.pallas_call(
        functools.partial(_mlp_kernel, final_norm=final_norm),
        grid=(n // tm, D_FF // tf),
        in_specs=[pl.BlockSpec((tm, D_MODEL), lambda i, f: (i, 0)),
                  pl.BlockSpec((1, D_MODEL), lambda i, f: (0, 0)),
                  pl.BlockSpec((D_MODEL, tf), lambda i, f: (0, f)),
                  pl.BlockSpec((tf, D_MODEL), lambda i, f: (f, 0)),
                  pl.BlockSpec((tm, PLE_DIM), lambda i, f: (i, 0)),
                  pl.BlockSpec((PLE_DIM, D_MODEL), lambda i, f: (0, 0)),
                  pl.BlockSpec((D_MODEL, D_MODEL), lambda i, f: (0, 0)),
                  pl.BlockSpec((1, D_MODEL), lambda i, f: (0, 0))],
        out_specs=pl.BlockSpec((tm, D_MODEL), lambda i, f: (i, 0)),
        out_shape=jax.ShapeDtypeStruct((n, D_MODEL), F32),
        scratch_shapes=[pltpu.VMEM((tm, D_MODEL), BF16), pltpu.VMEM((tm, D_MODEL), F32)],
        compiler_params=_cparams(("parallel", "arbitrary")),
        name="mlp_ple",
    )(x, g, wu, wd, p, wp, wpg, gf)


def _compress_math(load_sub, pea_ref, peb_ref, w1a_ref, w1b_ref, w2_ref, nsub):
    n = len(load_sub)
    acc_a = [jnp.zeros((nsub, 4 * CMP_HIDDEN), F32) for _ in range(n)]
    acc_b = [jnp.zeros((nsub, 4 * CMP_HIDDEN), F32) for _ in range(n)]
    for j in range(CMP_STRIDE):
        for i in range(n):
            xj = load_sub[i](j)
            acc_a[i] = acc_a[i] + _dot((xj + pea_ref[j:j + 1, :]).astype(BF16), w1a_ref[j])
            acc_b[i] = acc_b[i] + _dot((xj + peb_ref[j:j + 1, :]).astype(BF16), w1b_ref[j])
    hids = [_gelu_tanh(a + pltpu.roll(b, nsub - 1, axis=0)) for a, b in zip(acc_a, acc_b)]
    return [_dot(hid.astype(BF16), w2_ref[...]) for hid in hids]


def _cmp_kernel(x_ref, pea_ref, peb_ref, w1a_ref, w1b_ref, w2_ref, out_ref, *, nsub):
    load = lambda j: x_ref[:, j * 256:(j + 1) * 256]
    out_ref[0] = _compress_math([load], pea_ref, peb_ref, w1a_ref, w1b_ref, w2_ref, nsub)[0]


def compress_prompt(kcmp_sub, cw, bsz, t):
    nsub = t // CMP_STRIDE
    const2 = lambda b: (0, 0)
    const3 = lambda b: (0, 0, 0)
    return pl.pallas_call(
        functools.partial(_cmp_kernel, nsub=nsub),
        grid=(bsz,),
        in_specs=[pl.BlockSpec((nsub, CMP_STRIDE * 256), lambda b: (b, 0)),
                  pl.BlockSpec((CMP_STRIDE, 256), const2),
                  pl.BlockSpec((CMP_STRIDE, 256), const2),
                  pl.BlockSpec((CMP_STRIDE, 256, 256), const3),
                  pl.BlockSpec((CMP_STRIDE, 256, 256), const3),
                  pl.BlockSpec((256, 256), const2)],
        out_specs=pl.BlockSpec((1, nsub, 256), lambda b: (b, 0, 0)),
        out_shape=jax.ShapeDtypeStruct((bsz, nsub, 256), F32),
        compiler_params=_cparams(("parallel",)),
        name="nsa_compress",
    )(kcmp_sub, cw["pea"], cw["peb"], cw["w1a"], cw["w1b"], cw["w2"])


def _topk_mask(scores, nblk, k):
    shape = scores[0].shape
    if shape[0] == LANES:
        return _topk_mask_square(scores, k)
    lane = lax.broadcasted_iota(jnp.int32, shape, 1)
    cnts = [jnp.zeros(shape, F32) for _ in scores]
    for j in range(nblk):
        for i, score in enumerate(scores):
            cj = score[:, j:j + 1]
            ge = jnp.where(cj >= score, 1.0, 0.0)
            gt = jnp.where(cj > score, 1.0, 0.0)
            cnts[i] = cnts[i] + jnp.where(lane > j, ge, gt)
    return [cnt < float(k) for cnt in cnts]


def _topk_cols(scs, k):
    shape = scs[0].shape
    idx = lax.broadcasted_iota(jnp.int32, shape, 0).astype(F32)
    picked = [jnp.zeros(shape, F32) for _ in scs]
    for _ in range(k):
        mxs = [jnp.max(sc, axis=0, keepdims=True) for sc in scs]
        firsts = [jnp.min(jnp.where(sc == mx, idx, float(shape[0])), axis=0, keepdims=True)
                  for sc, mx in zip(scs, mxs)]
        hits = [idx == first for first in firsts]
        picked = [jnp.where(hit, 1.0, pk) for hit, pk in zip(hits, picked)]
        scs = [jnp.where(hit, 3.0 * NEG, sc) for hit, sc in zip(hits, scs)]
    return picked


def _topk_mask_square(scores, k):
    picked = _topk_cols([jnp.transpose(s) for s in scores], k)
    return [jnp.transpose(p) > 0.5 for p in picked]


def _tile_rows(a, reps):
    return jnp.concatenate([a] * reps, axis=0)


def _nsa_queries(zq_ref, g, rows=slice(None)):
    scale = NSA_HD ** -0.5
    q = jnp.concatenate([zq_ref[rows, (NSA_GRP * g + h) * LANES:(NSA_GRP * g + h + 1) * LANES]
                         for h in range(NSA_GRP)], axis=0)
    return (q * scale).astype(BF16)


def _nsa_compressed_and_select(qgs, kcs, vcs, m_c, ov, pos1, ns, r):
    bias = _tile_rows(jnp.where(m_c, 0.0, NEG), NSA_GRP)
    ss = [_dot_nt(qg, kc) + bias for qg, kc in zip(qgs, kcs)]
    mxs = [jnp.max(s, axis=-1, keepdims=True) for s in ss]
    ps = [jnp.exp(s - mx) for s, mx in zip(ss, mxs)]
    ps = [p / jnp.sum(p, axis=-1, keepdims=True) * jnp.where(mx > 0.5 * NEG, 1.0, 0.0) for p, mx in zip(ps, mxs)]
    o_cs = [_dot(p.astype(BF16), vc) for p, vc in zip(ps, vcs)]
    psums = [(p[0:r] + p[r:2 * r]) + (p[2 * r:3 * r] + p[3 * r:4 * r]) for p in ps]
    imps = [_dot_mask_left_t(psum, ov) for psum in psums]
    blk = lax.broadcasted_iota(jnp.int32, (r, LANES), 1)
    valid = blk * SEL_LEN <= pos1
    cur = pos1 // SEL_LEN
    bonus = jnp.where((blk == 0) | (blk == cur) | (blk == cur - 1), FORCE_BONUS, 0.0)
    scores = [jnp.where(valid, imp + bonus, NEG) for imp in imps]
    sels = [sel & valid for sel in _topk_mask(scores, min(ns, LANES), min(SEL_TOPK, ns))]
    return o_cs, sels


def _dot_mask_left_t(x, m01):
    m = m01.astype(BF16)
    x1 = x.astype(BF16)
    r1 = x - x1.astype(F32)
    x2 = r1.astype(BF16)
    x3 = (r1 - x2.astype(F32)).astype(BF16)
    return _dot(x1, m) + (_dot(x2, m) + _dot(x3, m))


def _gate_cols(gates, g, c, r):
    return jnp.concatenate([gates[:, 3 * (NSA_GRP * g + h) + c:3 * (NSA_GRP * g + h) + c + 1]
                            for h in range(NSA_GRP)], axis=0)


def _nsa_assemble(o0, o1, r):
    lane = lax.broadcasted_iota(jnp.int32, (r, LANES), 1)
    return jnp.concatenate([jnp.where(lane < NSA_HD, o0[j * r:(j + 1) * r], o1[j * r:(j + 1) * r])
                            for j in range(NSA_GRP)], axis=1)


SEL_TILE = 1024
Q_TILE = 128


KV_KA = 0
KV_VA = 2 * LANES
KV_KW = 4 * LANES
KV_VWA = 5 * LANES
KV_COLS = 7 * LANES
SEL_LANE = NSA_HD


def _nsa_prompt_kernel(zq_ref, zg_ref, kcvc_ref, kv_ref, shift_ref, ov_ref, out_ref, *, t):
    r = Q_TILE
    groups = range(NSA_KV)
    qb = pl.program_id(1)
    s0 = qb * r
    pos1 = s0 + lax.broadcasted_iota(jnp.int32, (r, 1), 0)
    nc = t // CMP_STRIDE
    ns = t // SEL_LEN
    kc = kcvc_ref[0][:, 0:LANES].astype(BF16)
    vc = kcvc_ref[0][:, LANES:2 * LANES].astype(BF16)
    ncol = lax.broadcasted_iota(jnp.int32, (1, nc), 1)
    m_c = ((ncol * CMP_STRIDE + (CMP_LEN - 1)) <= pos1) & (ncol < nc - 1)
    gates = _sigmoid(zg_ref[...])
    ov = ov_ref[...]

    qs = [_nsa_queries(zq_ref, g) for g in groups]
    ocs, sels = _nsa_compressed_and_select(qs, [kc] * NSA_KV, [vc] * NSA_KV, m_c, ov, pos1, ns, r)
    sels = [jnp.where(sel, 1.0, 0.0).astype(BF16) for sel in sels]

    scale = NSA_HD ** -0.5
    qsel = []
    for g in groups:
        heads = [zq_ref[:, (NSA_GRP * g + h) * LANES:(NSA_GRP * g + h + 1) * LANES] for h in range(NSA_GRP)]
        if g == 1:
            heads = [pltpu.roll(q, NSA_HD, axis=1) for q in heads]
        qsel.append(jnp.concatenate(heads, axis=0) * scale)
    lane = lax.broadcasted_iota(jnp.int32, (1, LANES), 1)
    ind = jnp.where((lane >= SEL_LANE) & (lane < SEL_LANE + SEL_TILE // SEL_LEN), 1.0, 0.0)

    def tile_step(kt, carry, diagonal):
        k0 = pl.multiple_of(kt * SEL_TILE, SEL_TILE)
        shift = shift_ref[kt]
        va = kv_ref[pl.ds(k0, SEL_TILE), KV_VA:KV_VA + 2 * LANES]
        qas = [(qsel[g] + _tile_rows(NEG * (ind - _dot(sels[g], shift)), NSA_GRP)).astype(BF16) for g in groups]
        ss = [_dot_nt(qas[g], kv_ref[pl.ds(k0, SEL_TILE), KV_KA + g * LANES:KV_KA + (g + 1) * LANES])
              for g in groups]
        if diagonal:
            kpos = k0 + lax.broadcasted_iota(jnp.int32, (1, SEL_TILE), 1)
            causal = _tile_rows(jnp.where(kpos <= pos1, 0.0, NEG), NSA_GRP)
            ss = [s + causal for s in ss]
        m_news = [jnp.maximum(carry[g][0], jnp.max(ss[g], axis=-1, keepdims=True)) for g in groups]
        alphas = [jnp.exp(carry[g][0] - m_news[g]) for g in groups]
        ps = [jnp.exp((ss[g] - m_news[g]).astype(BF16)) for g in groups]
        accs = [alphas[g] * carry[g][1] + _dot(ps[g], va) for g in groups]
        return tuple((m_news[g], accs[g]) for g in groups)

    init = tuple((jnp.full((NSA_GRP * r, 1), NEG, F32), jnp.zeros((NSA_GRP * r, 2 * LANES), F32)) for _ in groups)
    n_full = s0 // SEL_TILE
    mid = lax.fori_loop(0, n_full, lambda kt, c: tile_step(kt, c, False), init)
    fin = tile_step(n_full, mid, True)

    span = WINDOW + r
    start = pl.multiple_of(jnp.maximum(s0 - WINDOW, 0), r)
    kw = kv_ref[pl.ds(start, span), KV_KW:KV_KW + LANES]
    vwa = kv_ref[pl.ds(start, span), KV_VWA:KV_VWA + 2 * LANES]
    kposw = start + lax.broadcasted_iota(jnp.int32, (1, span), 1)
    bias_w = _tile_rows(jnp.where((kposw <= pos1) & (kposw > pos1 - WINDOW), 0.0, NEG), NSA_GRP)

    o_ss = [fin[g][1][:, 0:LANES] / fin[g][1][:, LANES:LANES + 1] for g in groups]
    ss = [_dot_nt(qs[g], kw) + bias_w for g in groups]
    ps = [jnp.exp((s - jnp.max(s, axis=-1, keepdims=True)).astype(BF16)) for s in ss]
    ows = [_dot(p, vwa) for p in ps]
    o_ws = [ow[:, 0:LANES] / ow[:, LANES:LANES + 1] for ow in ows]
    outs = [_gate_cols(gates, g, 0, r) * ocs[g] + _gate_cols(gates, g, 1, r) * o_ss[g]
            + _gate_cols(gates, g, 2, r) * o_ws[g] for g in groups]
    out_ref[...] = _nsa_assemble(outs[0], outs[1], r)


def nsa_prompt(z, kcvc, kvb, shift, ov, bsz, t):
    nqb = t // Q_TILE
    return pl.pallas_call(
        functools.partial(_nsa_prompt_kernel, t=t),
        grid=(bsz, nqb),
        in_specs=[pl.BlockSpec((Q_TILE, NSA_HEADS * LANES), lambda b, q: (b * nqb + q, 0)),
                  pl.BlockSpec((Q_TILE, LANES), lambda b, q: (b * nqb + q, ZNG // LANES)),
                  pl.BlockSpec((1, t // CMP_STRIDE, 256), lambda b, q: (b, 0, 0)),
                  pl.BlockSpec((t, KV_COLS), lambda b, q: (b, 0)),
                  pl.BlockSpec((t // SEL_TILE, LANES, LANES), lambda b, q: (0, 0, 0)),
                  pl.BlockSpec((t // CMP_STRIDE, LANES), lambda b, q: (0, 0))],
        out_specs=pl.BlockSpec((Q_TILE, MIX_WIDTH), lambda b, q: (b * nqb + q, 0)),
        out_shape=jax.ShapeDtypeStruct((bsz * t, MIX_WIDTH), F32),
        compiler_params=_cparams(("parallel", "arbitrary")),
        name="nsa_prompt",
    )(z, z, kcvc, kvb, shift, ov)


SLAB_ROWS = 1024


def _kv_slab_kernel(slc_ref, win_ref, out_ref, *, t):
    rows = slc_ref.shape[0]
    blocks_per_tile = SEL_TILE // SEL_LEN
    lane = lax.broadcasted_iota(jnp.int32, (rows, LANES), 1)
    tok = (pl.program_id(0) * rows + lax.broadcasted_iota(jnp.int32, (rows, LANES), 0)) % t
    onehot = jnp.where(lane == SEL_LANE + (tok // SEL_LEN) % blocks_per_tile, 1.0, 0.0)
    ones_col = jnp.where(lane == 0, 1.0, 0.0)
    k_slc = slc_ref[:, 0:LANES]
    is_key = lane < NSA_HD
    pieces = [jnp.where(is_key, k_slc, onehot), jnp.where(is_key, pltpu.roll(k_slc, NSA_HD, axis=1), onehot),
              slc_ref[:, LANES:2 * LANES], ones_col, win_ref[:, 0:LANES], win_ref[:, LANES:2 * LANES], ones_col]
    for i, piece in enumerate(pieces):
        out_ref[:, i * LANES:(i + 1) * LANES] = piece.astype(BF16)


def prompt_kv_slab(z, n_p, t):
    rows = _pick_tile(n_p, (SLAB_ROWS, 512, 256, 128))
    return pl.pallas_call(
        functools.partial(_kv_slab_kernel, t=t),
        grid=(n_p // rows,),
        in_specs=[pl.BlockSpec((rows, 2 * LANES), lambda i: (i, (ZKV + 2 * LANES) // (2 * LANES))),
                  pl.BlockSpec((rows, 2 * LANES), lambda i: (i, ZKW // (2 * LANES)))],
        out_specs=pl.BlockSpec((rows, KV_COLS), lambda i: (i, 0)),
        out_shape=jax.ShapeDtypeStruct((n_p, KV_COLS), BF16),
        compiler_params=_cparams(("parallel",)),
        name="nsa_kv_slab",
    )(z, z)


def _shift_matrices(t):
    blocks_per_tile = SEL_TILE // SEL_LEN
    out = np.zeros((t // SEL_TILE, LANES, LANES), np.float32)
    for b in range(min(t // SEL_LEN, LANES)):
        out[b // blocks_per_tile, b, SEL_LANE + b % blocks_per_tile] = 1.0
    return jnp.asarray(out).astype(BF16)


NSA_SEQ_PER_STEP = 2


def _two_part_attention(args):
    s_ps = [_dot(a[0], a[1]) + a[3] for a in args]
    s_ns = [_dot_nt(a[0], a[4]) + a[6] for a in args]
    mxs = [jnp.maximum(jnp.max(sp, axis=-1, keepdims=True), jnp.max(sn, axis=-1, keepdims=True))
           for sp, sn in zip(s_ps, s_ns)]
    p_ps = [jnp.exp(sp - mx) for sp, mx in zip(s_ps, mxs)]
    p_ns = [jnp.exp(sn - mx) for sn, mx in zip(s_ns, mxs)]
    dens = [jnp.sum(pp, axis=-1, keepdims=True) + jnp.sum(pn, axis=-1, keepdims=True) for pp, pn in zip(p_ps, p_ns)]
    return [(_dot_nt(pp.astype(BF16), a[2]) + _dot(pn.astype(BF16), a[5])) / den
            for pp, pn, a, den in zip(p_ps, p_ns, args, dens)]


def _nsa_sample_kernel(pt_ref, zq_ref, zkv_ref, cwin_ref, cache_ref, pea_ref, peb_ref, w1a_ref, w1b_ref,
                       w2_ref, e_ref, ov_ref, out_ref, buf_ref, tokm_ref, sem_ref, *, layer, past, n_pages, dec):
    r = SEQ_PAD
    nseq = NSA_SEQ_PER_STEP
    seqs = range(nseq)
    groups = range(NSA_KV)
    b = pl.program_id(0)
    nb = pl.num_programs(0)
    slot = lax.rem(b, 2)

    def page_copy(step, sl, q, p):
        return pltpu.make_async_copy(cache_ref.at[layer, pt_ref[(step * nseq + q) * n_pages + p]],
                                     buf_ref.at[sl, q, :, pl.ds(p * PAGE_SIZE, PAGE_SIZE)],
                                     sem_ref.at[sl])

    def start_all(step, sl):
        for q in seqs:
            for p in range(n_pages):
                page_copy(step, sl, q, p).start()

    @pl.when(b == 0)
    def _():
        start_all(b, slot)

    @pl.when(b + 1 < nb)
    def _():
        start_all(b + 1, 1 - slot)

    for q in seqs:
        for p in range(n_pages):
            page_copy(b, slot, q, p).wait()

    nsub = past // CMP_STRIDE
    nc = nsub - 1
    ns = -(-(past + dec) // SEL_LEN)
    for c0 in range(2):
        for p in range(past // LANES):
            for q in seqs:
                tokm_ref[q, c0, p * LANES:(p + 1) * LANES, :] = jnp.transpose(
                    buf_ref[slot, q, c0 * LANES:(c0 + 1) * LANES, p * LANES:(p + 1) * LANES])
    loads = [lambda j, q=q: jnp.concatenate([tokm_ref[q, 0, pl.ds(j, nsub, stride=CMP_STRIDE), :],
                                             tokm_ref[q, 1, pl.ds(j, nsub, stride=CMP_STRIDE), :]], axis=1)
             for q in seqs]
    kcvcs = _compress_math(loads, pea_ref, peb_ref, w1a_ref, w1b_ref, w2_ref, nsub)

    row = lax.broadcasted_iota(jnp.int32, (r, 1), 0)
    pos1 = past + row
    ncol = lax.broadcasted_iota(jnp.int32, (1, nsub), 1)
    m_c = ((ncol * CMP_STRIDE + (CMP_LEN - 1)) <= pos1) & (ncol < nc)
    ov = ov_ref[...]
    kpos_p = lax.broadcasted_iota(jnp.int32, (1, past), 1)
    tnew = lax.broadcasted_iota(jnp.int32, (1, r), 1)
    kpos_n = past + tnew
    wk = cwin_ref.shape[3]
    kpos_w = past - wk + lax.broadcasted_iota(jnp.int32, (1, wk), 1)
    in_win = lambda kp: (kp <= pos1) & (kp > pos1 - WINDOW)
    bias_wp = _tile_rows(jnp.where(in_win(kpos_w), 0.0, NEG), NSA_GRP)
    bias_wn = _tile_rows(jnp.where(in_win(kpos_n) & (tnew < dec), 0.0, NEG), NSA_GRP)
    new_blk = past // SEL_LEN

    probs = [(q, g) for q in seqs for g in groups]
    rows = lambda q: slice(q * r, (q + 1) * r)
    zkv = lambda q, c0, c1: zkv_ref[rows(q), c0:c1]
    gates = [_sigmoid(zkv(q, ZNG - ZKV, ZNG - ZKV + LANES)) for q in seqs]
    qgs = [_nsa_queries(zq_ref, g, rows(q)) for q, g in probs]
    o_cs, sels = _nsa_compressed_and_select(
        qgs, [kcvcs[q][:, 0:LANES].astype(BF16) for q, g in probs],
        [kcvcs[q][:, LANES:2 * LANES].astype(BF16) for q, g in probs], m_c, ov, pos1, ns, r)
    selfs = [jnp.where(sel, 1.0, 0.0) for sel in sels]
    allow_ps = [(_dot(sf.astype(BF16), e_ref[...]) > 0.5) & (kpos_p <= pos1) for sf in selfs]
    allow_ns = [(sf[:, new_blk:new_blk + 1] > 0.5) & (kpos_n <= pos1) & (tnew < dec) for sf in selfs]
    o_ss = _two_part_attention([
        (qgs[i], buf_ref[slot, q, 2 * LANES:3 * LANES, :].astype(BF16),
         buf_ref[slot, q, 3 * LANES:4 * LANES, :].astype(BF16),
         _tile_rows(jnp.where(allow_ps[i], 0.0, NEG), NSA_GRP),
         zkv(q, 2 * LANES, 3 * LANES).astype(BF16), zkv(q, 3 * LANES, 4 * LANES).astype(BF16),
         _tile_rows(jnp.where(allow_ns[i], 0.0, NEG), NSA_GRP)) for i, (q, g) in enumerate(probs)])
    o_ws = _two_part_attention([
        (qgs[i], cwin_ref[0, q, 0:LANES, :].astype(BF16), cwin_ref[0, q, LANES:2 * LANES, :].astype(BF16), bias_wp,
         zkv(q, ZKW - ZKV, ZKW - ZKV + LANES).astype(BF16),
         zkv(q, ZKW - ZKV + LANES, ZKW - ZKV + 2 * LANES).astype(BF16), bias_wn)
        for i, (q, g) in enumerate(probs)])
    outs = [_gate_cols(gates[q], g, 0, r) * o_cs[i] + _gate_cols(gates[q], g, 1, r) * o_ss[i]
            + _gate_cols(gates[q], g, 2, r) * o_ws[i] for i, (q, g) in enumerate(probs)]
    for q in seqs:
        out_ref[rows(q), :] = _nsa_assemble(outs[NSA_KV * q], outs[NSA_KV * q + 1], r)


def nsa_sample(z, page_table, cache_t, cache_win_t, cw, e_mat, ov, layer, row0, dbsz, past, dec):
    n_pages = past // PAGE_SIZE
    nseq = NSA_SEQ_PER_STEP
    rows = nseq * SEQ_PAD
    rb0 = row0 // rows
    wk = cache_win_t.shape[3]
    c2 = lambda b, pt: (0, 0)
    c3 = lambda b, pt: (0, 0, 0)
    grid_spec = pltpu.PrefetchScalarGridSpec(
        num_scalar_prefetch=1,
        grid=(dbsz // nseq,),
        in_specs=[pl.BlockSpec((rows, NSA_HEADS * LANES), lambda b, pt: (rb0 + b, 0)),
                  pl.BlockSpec((rows, 1024), lambda b, pt: (rb0 + b, ZKV // 1024)),
                  pl.BlockSpec((1, nseq, 2 * LANES, wk), lambda b, pt: (layer, b, 0, 0)),
                  pl.BlockSpec(memory_space=pl.ANY),
                  pl.BlockSpec((CMP_STRIDE, 256), c2),
                  pl.BlockSpec((CMP_STRIDE, 256), c2),
                  pl.BlockSpec((CMP_STRIDE, 256, 256), c3),
                  pl.BlockSpec((CMP_STRIDE, 256, 256), c3),
                  pl.BlockSpec((256, 256), c2),
                  pl.BlockSpec((LANES, past), c2),
                  pl.BlockSpec((past // CMP_STRIDE, LANES), c2)],
        out_specs=pl.BlockSpec((rows, MIX_WIDTH), lambda b, pt: (b, 0)),
        scratch_shapes=[pltpu.VMEM((2, nseq, 4 * LANES, past), F32), pltpu.VMEM((nseq, 2, past, LANES), F32),
                        pltpu.SemaphoreType.DMA((2,))],
    )
    return pl.pallas_call(
        functools.partial(_nsa_sample_kernel, layer=layer, past=past, n_pages=n_pages, dec=dec),
        grid_spec=grid_spec,
        out_shape=jax.ShapeDtypeStruct((dbsz * SEQ_PAD, MIX_WIDTH), F32),
        compiler_params=_cparams(("arbitrary",)),
        name="nsa_sample",
    )(page_table.reshape(-1), z, z, cache_win_t, cache_t, cw["pea"], cw["peb"], cw["w1a"], cw["w1b"], cw["w2"],
      e_mat, ov)


def _seg_masks(r, c):
    ri = lax.broadcasted_iota(jnp.int32, (r, r), 0)
    ci = lax.broadcasted_iota(jnp.int32, (r, r), 1)
    same = (ri // c) == (ci // c)
    return same, same & (ri >= ci), same & (ri > ci)


def _gdn_prep(y, ab, par_ref, valid, c, u_ref, w_ref, qe_ref, qk_ref, kdt_ref, eg_ref):
    r = y.shape[0]
    same, incl, strict = _seg_masks(r, c)
    y = _silu(y)
    beta_all = _sigmoid(ab)
    g_all = -jnp.exp(par_ref[0:1, :]) * _softplus(ab + par_ref[1:2, :])
    if valid is not None:
        y = jnp.where(valid, y, 0.0)
        beta_all = jnp.where(valid, beta_all, 0.0)
        g_all = jnp.where(valid, g_all, 0.0)
    gcum_all = _dot_mask_left(jnp.where(incl, 1.0, 0.0), g_all)
    glast_all = _dot_mask_left(jnp.where(same, 1.0, 0.0), g_all)
    gcum_t = jnp.transpose(gcum_all)
    eye = jnp.where(lax.broadcasted_iota(jnp.int32, (r, r), 0) == lax.broadcasted_iota(jnp.int32, (r, r), 1),
                    1.0, 0.0)
    n_double = int(math.log2(c))
    tms, pws, rhss = [], [], []
    for h in range(GDN_HEADS):
        q = y[:, h * GDN_HD:(h + 1) * GDN_HD]
        k = y[:, MIX_WIDTH + h * GDN_HD:MIX_WIDTH + (h + 1) * GDN_HD]
        v = y[:, 2 * MIX_WIDTH + h * GDN_HD:2 * MIX_WIDTH + (h + 1) * GDN_HD]
        q = q * lax.rsqrt(jnp.sum(q * q, axis=-1, keepdims=True) + EPS) * (GDN_HD ** -0.5)
        k = k * lax.rsqrt(jnp.sum(k * k, axis=-1, keepdims=True) + EPS)
        g1 = jnp.broadcast_to(gcum_all[:, h:h + 1], (r, GDN_HD))
        g2 = jnp.broadcast_to(gcum_t[h:h + 1, :], (r, r))
        gl = jnp.broadcast_to(glast_all[:, h:h + 1], (r, GDN_HD))
        beta = jnp.broadcast_to(beta_all[:, GDN_HEADS + h:GDN_HEADS + h + 1], (r, GDN_HD))
        g1r = g1 if r == GDN_HD else jnp.broadcast_to(gcum_all[:, h:h + 1], (r, r))
        decay = jnp.where(incl, jnp.exp(jnp.where(incl, g1r - g2, 0.0)), 0.0)
        betar = beta if r == GDN_HD else jnp.broadcast_to(beta_all[:, GDN_HEADS + h:GDN_HEADS + h + 1], (r, r))
        a = jnp.where(strict, _bdot_nt(k, k) * decay * betar, 0.0)
        tms.append(eye - a)
        pws.append(a)
        eg1 = jnp.exp(g1)
        rhss.append(jnp.concatenate([v * beta, k * (beta * eg1)], axis=1))
        qe_ref[h] = q * eg1
        qk_ref[h] = jnp.where(incl, _bdot_nt(q, k) * decay, 0.0)
        kdt_ref[h] = jnp.transpose(k * jnp.exp(gl - g1))
        eg_ref[h] = jnp.exp(gl)
    for _ in range(n_double - 1):
        pws = [_dot3(pw, pw) for pw in pws]
        tms = [tm + _dot3(tm, pw) for tm, pw in zip(tms, pws)]
    for h in range(GDN_HEADS):
        sol = _dot3(tms[h], rhss[h])
        u_ref[h] = sol[:, 0:GDN_HD]
        w_ref[h] = sol[:, GDN_HD:2 * GDN_HD]


def _gdn_segment(h, row0, seg, c, s, u_ref, w_ref, qe_ref, qk_ref, kdt_ref, eg_ref, vn_ref, col_seg):
    rows = pl.ds(row0, c)
    vn = u_ref[h, rows, :] - _bdot(w_ref[h, rows, :], s)
    vn_ref[h, rows, :] = vn
    vn_all = vn_ref[h].astype(BF16)
    o = _bdot(qe_ref[h, rows, :], s) + _dot(qk_ref[h, rows, :].astype(BF16), vn_all)
    kdt = jnp.where(col_seg == seg, kdt_ref[h], 0.0)
    s_new = s * eg_ref[h, pl.ds(row0, 1), :] + _dot(kdt.astype(BF16), vn_all)
    return o, s_new


def _gdn_finish(o, zg, ng_ref):
    return _rms(o, ng_ref[...]) * _silu(zg)


GDN_TILE = 128


def _gdn_prompt_kernel(x_ref, ab_ref, zg_ref, cw_ref, par_ref, ng_ref, out_ref, st_ref,
                       halo_ref, u_ref, w_ref, qe_ref, qk_ref, kdt_ref, eg_ref, vn_ref):
    i = pl.program_id(1)
    r = GDN_TILE
    c = GDN_CHUNK

    @pl.when(i == 0)
    def _():
        halo_ref[...] = jnp.zeros_like(halo_ref)
        st_ref[...] = jnp.zeros_like(st_ref)

    x = x_ref[...]
    xx = jnp.concatenate([halo_ref[...], x], axis=0)
    halo_ref[...] = x[r - SUBLANES:r, :]
    y = None
    for j in range(CONV_W):
        sh = CONV_W - 1 - j
        xs = xx if sh == 0 else pltpu.roll(xx, sh, axis=0)
        t = xs[SUBLANES:, :] * cw_ref[j:j + 1, :]
        y = t if y is None else y + t
    _gdn_prep(y, ab_ref[...], par_ref, None, c, u_ref, w_ref, qe_ref, qk_ref, kdt_ref, eg_ref)
    vn_ref[...] = jnp.zeros_like(vn_ref)
    col_seg = lax.broadcasted_iota(jnp.int32, (GDN_HD, r), 1) // c
    states = [st_ref[0, h] for h in range(GDN_HEADS)]
    os_ = [[] for _ in range(GDN_HEADS)]
    for ck in range(r // c):
        for h in range(GDN_HEADS):
            o, states[h] = _gdn_segment(h, ck * c, ck, c, states[h], u_ref, w_ref, qe_ref, qk_ref, kdt_ref,
                                        eg_ref, vn_ref, col_seg)
            os_[h].append(o)
    for h in range(GDN_HEADS):
        st_ref[0, h] = states[h]
        o = jnp.concatenate(os_[h], axis=0)
        out_ref[:, h * GDN_HD:(h + 1) * GDN_HD] = _gdn_finish(o, zg_ref[:, h * GDN_HD:(h + 1) * GDN_HD], ng_ref)


def _gdn_scratch(r):
    hs = (GDN_HEADS, r, GDN_HD)
    return [pltpu.VMEM(hs, F32), pltpu.VMEM(hs, F32), pltpu.VMEM(hs, F32), pltpu.VMEM((GDN_HEADS, r, r), F32),
            pltpu.VMEM((GDN_HEADS, GDN_HD, r), F32), pltpu.VMEM(hs, F32), pltpu.VMEM(hs, F32)]


def gdn_prompt(z, conv_w, par, norm_g, bsz, t):
    nt = t // GDN_TILE
    r = GDN_TILE
    c2 = lambda b, i: (0, 0)
    return pl.pallas_call(
        _gdn_prompt_kernel,
        grid=(bsz, nt),
        in_specs=[pl.BlockSpec((r, 3 * MIX_WIDTH), lambda b, i: (b * nt + i, ZGQ // (3 * MIX_WIDTH))),
                  pl.BlockSpec((r, LANES), lambda b, i: (b * nt + i, ZGAB // LANES)),
                  pl.BlockSpec((r, MIX_WIDTH), lambda b, i: (b * nt + i, ZGZ // MIX_WIDTH)),
                  pl.BlockSpec((CONV_W, 3 * MIX_WIDTH), c2),
                  pl.BlockSpec((SUBLANES, LANES), c2),
                  pl.BlockSpec((1, GDN_HD), c2)],
        out_specs=[pl.BlockSpec((r, MIX_WIDTH), lambda b, i: (b * nt + i, 0)),
                   pl.BlockSpec((1, GDN_HEADS, GDN_HD, GDN_HD), lambda b, i: (b, 0, 0, 0))],
        out_shape=[jax.ShapeDtypeStruct((bsz * t, MIX_WIDTH), F32),
                   jax.ShapeDtypeStruct((bsz, GDN_HEADS, GDN_HD, GDN_HD), F32)],
        scratch_shapes=[pltpu.VMEM((SUBLANES, 3 * MIX_WIDTH), F32)] + _gdn_scratch(r),
        compiler_params=_cparams(("parallel", "arbitrary")),
        name="gdn_prompt",
    )(z, z, z, conv_w, par, norm_g)


SEQ_PER_STEP = 16


def _gdn_sample_kernel(x_ref, ab_ref, zg_ref, prev_ref, s0_ref, cw_ref, par_ref, ng_ref, out_ref, st_ref,
                       u_ref, w_ref, qe_ref, qk_ref, kdt_ref, eg_ref, vn_ref, o_ref, *, dec):
    r = SEQ_PER_STEP * SEQ_PAD
    c = SEQ_PAD
    row = lax.broadcasted_iota(jnp.int32, (r, 1), 0)
    tin = row % c
    valid = tin < dec
    xx = jnp.where(tin < CONV_W - 1, prev_ref[...], pltpu.roll(x_ref[...], CONV_W - 1, axis=0))
    y = None
    for j in range(CONV_W):
        xs = xx if j == 0 else pltpu.roll(xx, r - j, axis=0)
        t = xs * cw_ref[j:j + 1, :]
        y = t if y is None else y + t
    _gdn_prep(y, ab_ref[...], par_ref, valid, c, u_ref, w_ref, qe_ref, qk_ref, kdt_ref, eg_ref)
    vn_ref[...] = jnp.zeros_like(vn_ref)
    col_seg = lax.broadcasted_iota(jnp.int32, (GDN_HD, r), 1) // c

    def body(sq, carry):
        row0 = pl.multiple_of(sq * c, c)
        for h in range(GDN_HEADS):
            o, s = _gdn_segment(h, row0, sq, c, s0_ref[0, sq, h], u_ref, w_ref, qe_ref, qk_ref, kdt_ref, eg_ref,
                                vn_ref, col_seg)
            st_ref[sq, h] = s
            o_ref[h, pl.ds(row0, c), :] = o
        return carry

    lax.fori_loop(0, SEQ_PER_STEP, body, 0, unroll=4)
    for h in range(GDN_HEADS):
        out_ref[:, h * GDN_HD:(h + 1) * GDN_HD] = _gdn_finish(o_ref[h], zg_ref[:, h * GDN_HD:(h + 1) * GDN_HD],
                                                              ng_ref)


def gdn_sample(z, prev_pad, s0_all, layer, conv_w, par, norm_g, row0, dbsz, dec):
    r = SEQ_PER_STEP * SEQ_PAD
    rb0 = row0 // r
    c2 = lambda i: (0, 0)
    return pl.pallas_call(
        functools.partial(_gdn_sample_kernel, dec=dec),
        grid=(dbsz // SEQ_PER_STEP,),
        in_specs=[pl.BlockSpec((r, 3 * MIX_WIDTH), lambda i: (rb0 + i, ZGQ // (3 * MIX_WIDTH))),
                  pl.BlockSpec((r, LANES), lambda i: (rb0 + i, ZGAB // LANES)),
                  pl.BlockSpec((r, MIX_WIDTH), lambda i: (rb0 + i, ZGZ // MIX_WIDTH)),
                  pl.BlockSpec((r, 3 * MIX_WIDTH), lambda i: (i, 0)),
                  pl.BlockSpec((1, SEQ_PER_STEP, GDN_HEADS, GDN_HD, GDN_HD), lambda i: (layer, i, 0, 0, 0)),
                  pl.BlockSpec((CONV_W, 3 * MIX_WIDTH), c2),
                  pl.BlockSpec((SUBLANES, LANES), c2),
                  pl.BlockSpec((1, GDN_HD), c2)],
        out_specs=[pl.BlockSpec((r, MIX_WIDTH), lambda i: (i, 0)),
                   pl.BlockSpec((SEQ_PER_STEP, GDN_HEADS, GDN_HD, GDN_HD), lambda i: (i, 0, 0, 0))],
        out_shape=[jax.ShapeDtypeStruct((dbsz * SEQ_PAD, MIX_WIDTH), F32),
                   jax.ShapeDtypeStruct((dbsz, GDN_HEADS, GDN_HD, GDN_HD), F32)],
        scratch_shapes=_gdn_scratch(r) + [pltpu.VMEM((GDN_HEADS, r, GDN_HD), F32)],
        compiler_params=_cparams(("parallel",)),
        name="gdn_sample",
    )(z, z, z, prev_pad, s0_all, conv_w, par, norm_g)


def _ret_log_gamma(h):
    return math.log1p(-(2.0 ** (-5.0 - h)))


def _ret_prep(x, cs, sn, valid, h):
    qh = x[:, h * RET_HD:(h + 1) * RET_HD]
    kh = x[:, MIX_WIDTH + h * RET_HD:MIX_WIDTH + (h + 1) * RET_HD]
    vh = x[:, 2 * MIX_WIDTH + h * RET_HD:2 * MIX_WIDTH + (h + 1) * RET_HD]
    qr = qh * cs + pltpu.roll(qh, RET_HD // 2, axis=1) * sn
    kr = (kh * cs + pltpu.roll(kh, RET_HD // 2, axis=1) * sn) * (RET_HD ** -0.5)
    if valid is not None:
        kr = jnp.where(valid, kr, 0.0)
        vh = jnp.where(valid, vh, 0.0)
    return qr, kr, vh


def _ret_finish(o, gate):
    o = o * lax.rsqrt(jnp.mean(o * o, axis=-1, keepdims=True) + EPS)
    return o * _silu(gate)


def _ret_prompt_kernel(x_ref, gt_ref, cs_ref, sn_ref, out_ref, st_ref):
    i = pl.program_id(1)
    c = RET_CHUNK

    @pl.when(i == 0)
    def _():
        st_ref[...] = jnp.zeros_like(st_ref)

    x = x_ref[...]
    cs = cs_ref[...]
    sn = sn_ref[...]
    ri = lax.broadcasted_iota(jnp.int32, (c, c), 0)
    ci = lax.broadcasted_iota(jnp.int32, (c, c), 1)
    diff = (ri - ci).astype(F32)
    n = lax.broadcasted_iota(jnp.int32, (c, 1), 0).astype(F32)
    heads = range(RET_HEADS)
    lgs = [_ret_log_gamma(h) for h in heads]
    qkv = [_ret_prep(x, cs, sn, None, h) for h in heads]
    states = [st_ref[0, h] for h in heads]
    scores = [_bdot_nt(qkv[h][0], qkv[h][1]) * jnp.where(diff >= 0.0, jnp.exp(jnp.maximum(diff, 0.0) * lgs[h]), 0.0)
              for h in heads]
    cross = [_bdot(qkv[h][0], states[h]) * jnp.exp((n + 1.0) * lgs[h]) for h in heads]
    outs = [_bdot(scores[h], qkv[h][2]) + cross[h] for h in heads]
    kdts = [jnp.transpose(qkv[h][1] * jnp.exp((c - 1.0 - n) * lgs[h])).astype(BF16) for h in heads]
    for h in heads:
        st_ref[0, h] = states[h] * math.exp(c * lgs[h]) + _dot(kdts[h], qkv[h][2].astype(BF16))
        out_ref[:, h * RET_HD:(h + 1) * RET_HD] = _ret_finish(outs[h], gt_ref[:, h * RET_HD:(h + 1) * RET_HD])


def ret_prompt(z, cs, sn, bsz, t):
    c = RET_CHUNK
    nt = t // c
    return pl.pallas_call(
        _ret_prompt_kernel,
        grid=(bsz, nt),
        in_specs=[pl.BlockSpec((c, 3 * MIX_WIDTH), lambda b, i: (b * nt + i, ZRQ // (3 * MIX_WIDTH))),
                  pl.BlockSpec((c, MIX_WIDTH), lambda b, i: (b * nt + i, ZRG // MIX_WIDTH)),
                  pl.BlockSpec((c, RET_HD), lambda b, i: (i, 0)),
                  pl.BlockSpec((c, RET_HD), lambda b, i: (i, 0))],
        out_specs=[pl.BlockSpec((c, MIX_WIDTH), lambda b, i: (b * nt + i, 0)),
                   pl.BlockSpec((1, RET_HEADS, RET_HD, RET_HD), lambda b, i: (b, 0, 0, 0))],
        out_shape=[jax.ShapeDtypeStruct((bsz * t, MIX_WIDTH), F32),
                   jax.ShapeDtypeStruct((bsz, RET_HEADS, RET_HD, RET_HD), F32)],
        compiler_params=_cparams(("parallel", "arbitrary")),
        name="ret_prompt",
    )(z, z, cs, sn)


def _ret_sample_kernel(x_ref, gt_ref, cs_ref, sn_ref, s0_ref, out_ref, st_ref, q_ref, kdt_ref, v_ref, o_ref,
                       *, dec):
    r = SEQ_PER_STEP * SEQ_PAD
    c = SEQ_PAD
    row = lax.broadcasted_iota(jnp.int32, (r, 1), 0)
    tin = row % c
    valid = tin < dec
    n = tin.astype(F32)
    x = x_ref[...]
    cs = cs_ref[...]
    sn = sn_ref[...]
    same, incl, _ = _seg_masks(r, c)
    ri = lax.broadcasted_iota(jnp.int32, (r, r), 0)
    ci = lax.broadcasted_iota(jnp.int32, (r, r), 1)
    diff = (ri - ci).astype(F32)
    for h in range(RET_HEADS):
        lg = _ret_log_gamma(h)
        qr, kr, vh = _ret_prep(x, cs, sn, valid, h)
        dmat = jnp.where(incl, jnp.exp(jnp.maximum(diff, 0.0) * lg), 0.0)
        o_ref[h] = _bdot(_bdot_nt(qr, kr) * dmat, vh)
        q_ref[h] = qr
        kdt_ref[h] = jnp.transpose(kr * jnp.exp((dec - 1.0 - n) * lg))
        v_ref[h] = vh
    col_seg = lax.broadcasted_iota(jnp.int32, (RET_HD, r), 1) // c
    qdec = [jnp.exp((lax.broadcasted_iota(jnp.int32, (c, 1), 0).astype(F32) + 1.0) * _ret_log_gamma(h))
            for h in range(RET_HEADS)]

    def body(sq, carry):
        row0 = pl.multiple_of(sq * c, c)
        rows = pl.ds(row0, c)
        for h in range(RET_HEADS):
            s = s0_ref[0, sq, h]
            o_ref[h, rows, :] = o_ref[h, rows, :] + _bdot(q_ref[h, rows, :], s) * qdec[h]
            kdt = jnp.where(col_seg == sq, kdt_ref[h], 0.0)
            st_ref[sq, h] = s * math.exp(dec * _ret_log_gamma(h)) + _dot(kdt.astype(BF16), v_ref[h].astype(BF16))
        return carry

    lax.fori_loop(0, SEQ_PER_STEP, body, 0, unroll=4)
    for h in range(RET_HEADS):
        out_ref[:, h * RET_HD:(h + 1) * RET_HD] = _ret_finish(o_ref[h], gt_ref[:, h * RET_HD:(h + 1) * RET_HD])


def ret_sample(z, cs, sn, s0_all, layer, row0, dbsz, dec):
    r = SEQ_PER_STEP * SEQ_PAD
    rb0 = row0 // r
    hs = (RET_HEADS, r, RET_HD)
    return pl.pallas_call(
        functools.partial(_ret_sample_kernel, dec=dec),
        grid=(dbsz // SEQ_PER_STEP,),
        in_specs=[pl.BlockSpec((r, 3 * MIX_WIDTH), lambda i: (rb0 + i, ZRQ // (3 * MIX_WIDTH))),
                  pl.BlockSpec((r, MIX_WIDTH), lambda i: (rb0 + i, ZRG // MIX_WIDTH)),
                  pl.BlockSpec((r, RET_HD), lambda i: (0, 0)),
                  pl.BlockSpec((r, RET_HD), lambda i: (0, 0)),
                  pl.BlockSpec((1, SEQ_PER_STEP, RET_HEADS, RET_HD, RET_HD), lambda i: (layer, i, 0, 0, 0))],
        out_specs=[pl.BlockSpec((r, MIX_WIDTH), lambda i: (i, 0)),
                   pl.BlockSpec((SEQ_PER_STEP, RET_HEADS, RET_HD, RET_HD), lambda i: (i, 0, 0, 0))],
        out_shape=[jax.ShapeDtypeStruct((dbsz * SEQ_PAD, MIX_WIDTH), F32),
                   jax.ShapeDtypeStruct((dbsz, RET_HEADS, RET_HD, RET_HD), F32)],
        scratch_shapes=[pltpu.VMEM(hs, F32), pltpu.VMEM((RET_HEADS, RET_HD, r), F32), pltpu.VMEM(hs, F32),
                        pltpu.VMEM(hs, F32)],
        compiler_params=_cparams(("parallel",)),
        name="ret_sample",
    )(z, z, cs, sn, s0_all)


def _prep_w_in(w):
    sizes = (NSA_HEADS * NSA_HD, 6 * NSA_KV * NSA_HD, 3 * NSA_HEADS, 3 * MIX_WIDTH, GDN_HEADS, GDN_HEADS,
             MIX_WIDTH, 3 * MIX_WIDTH, MIX_WIDTH, N_BRANCH * D_MODEL)
    src = np.concatenate([[0], np.cumsum(sizes)])
    s_q, s_kv, s_ng, s_gq, s_ga, s_gb, s_gz, s_rq, s_rg, s_mg = src[:-1].tolist()
    wb = w.astype(BF16)
    out = jnp.zeros((w.shape[0], ZW), BF16)

    def put(o, dst, s0, n):
        return lax.dynamic_update_slice(o, lax.slice_in_dim(wb, s0, s0 + n, axis=1), (0, dst))

    for h in range(NSA_HEADS):
        out = put(out, ZQ + h * LANES + NSA_HD * (h // NSA_GRP), s_q + h * NSA_HD, NSA_HD)
    out = put(out, ZKV, s_kv, sizes[1] + sizes[2])
    out = put(out, ZGAB, s_ga, 2 * GDN_HEADS)
    out = put(out, ZGZ, s_gz, MIX_WIDTH)
    out = put(out, ZRG, s_rg, MIX_WIDTH)
    out = put(out, ZGQ, s_gq, 3 * MIX_WIDTH)
    out = put(out, ZRQ, s_rq, 3 * MIX_WIDTH)
    out = put(out, ZMG, s_mg, N_BRANCH * D_MODEL)
    return out


def _prep_w_branch(wb):
    w0 = wb[0].reshape(NSA_HEADS, NSA_HD, D_MODEL)
    order = [h for j in range(NSA_GRP) for h in (j, j + NSA_GRP)]
    w0 = w0[np.array(order)].reshape(MIX_WIDTH, D_MODEL)
    return jnp.stack([w0, wb[1], wb[2]]).astype(BF16)


def _prep_compress(pe, w1, w2):
    sel_k = jnp.asarray(np.diag([1.0, 1.0, 0.0, 0.0]).astype(np.float32))
    sel_v = jnp.asarray(np.diag([0.0, 0.0, 1.0, 1.0]).astype(np.float32))

    def blockdiag(mk, mv):
        full = (sel_k[:, None, :, None] * mk[..., None, :, None, :]
                + sel_v[:, None, :, None] * mv[..., None, :, None, :])
        return full.reshape(mk.shape[:-2] + (4 * mk.shape[-2], 4 * mk.shape[-1]))

    w1r = w1.reshape(2, CMP_LEN, NSA_HD, CMP_HIDDEN)
    w1a = blockdiag(w1r[0, :CMP_STRIDE], w1r[1, :CMP_STRIDE])
    w1b = blockdiag(w1r[0, CMP_STRIDE:], w1r[1, CMP_STRIDE:])
    pea = jnp.concatenate([pe[0, :CMP_STRIDE], pe[0, :CMP_STRIDE], pe[1, :CMP_STRIDE], pe[1, :CMP_STRIDE]], axis=1)
    peb = jnp.concatenate([pe[0, CMP_STRIDE:], pe[0, CMP_STRIDE:], pe[1, CMP_STRIDE:], pe[1, CMP_STRIDE:]], axis=1)
    w2b = blockdiag(w2[0], w2[1])
    return dict(pea=pea, peb=peb, w1a=w1a.astype(BF16), w1b=w1b.astype(BF16), w2=w2b.astype(BF16))


def _overlap_matrix(nc_rows, nc, ns):
    n = np.arange(nc_rows)[:, None]
    s = np.arange(LANES)[None, :]
    c_start = n * CMP_STRIDE
    ov = (c_start < (s + 1) * SEL_LEN) & (s * SEL_LEN < c_start + CMP_LEN) & (n < nc) & (s < ns)
    return jnp.asarray(ov.astype(np.float32))


def _expand_matrix(nkeys):
    s = np.arange(LANES)[:, None]
    k = np.arange(nkeys)[None, :]
    return (k // SEL_LEN == s).astype(np.float32)


def _rope_tables(pos):
    half = RET_HD // 2
    inv = ROPE_BASE ** (-jnp.linspace(0.0, 1.0, half, dtype=F32))
    ang = pos.astype(F32)[:, None] * inv[None, :]
    cos, sin = jnp.cos(ang), jnp.sin(ang)
    return jnp.concatenate([cos, cos], axis=1), jnp.concatenate([-sin, sin], axis=1)


def _pad_rows(a, n):
    return jnp.pad(a, ((0, 0), (0, n - a.shape[1])) + ((0, 0),) * (a.ndim - 2))


def kernel(x_prompt, x_sample, cache_nsa_kv, cache_nsa_win, state_gdn_conv, state_gdn, state_ret, page_table,
           p_prompt, p_sample, g_mix, w_in, nsa_cmp_pe, nsa_cmp_w1, nsa_cmp_w2, gdn_conv_w, gdn_a_log,
           gdn_dt_bias, gdn_norm_g, w_branch, w_out, g_mlp, w_up, w_down, w_ple, w_ple_gate, g_final):
    bsz, t, d = x_prompt.shape
    dbsz, dec, _ = x_sample.shape
    depth = w_in.shape[0]
    past = page_table.shape[1] * PAGE_SIZE
    n_p = bsz * t
    n_s = dbsz * SEQ_PAD
    assert dec <= SEQ_PAD and dec >= CONV_W - 1 and (past % SEL_LEN) + dec <= SEL_LEN
    assert t % SEL_TILE == 0 and t >= WINDOW + Q_TILE and past % PAGE_SIZE == 0 and dbsz % SEQ_PER_STEP == 0

    x = jnp.concatenate([x_prompt.reshape(n_p, d), _pad_rows(x_sample, SEQ_PAD).reshape(n_s, d)], axis=0)
    p_all = jnp.concatenate([p_prompt.reshape(depth, n_p, PLE_DIM),
                             jnp.pad(p_sample, ((0, 0), (0, 0), (0, SEQ_PAD - dec), (0, 0))).reshape(depth, n_s, PLE_DIM)],
                            axis=1)

    nc_p = t // CMP_STRIDE - 1
    ov_p = _overlap_matrix(t // CMP_STRIDE, nc_p, t // SEL_LEN)
    shift_p = _shift_matrices(t)
    nc_s = past // CMP_STRIDE - 1
    ov_s = _overlap_matrix(past // CMP_STRIDE, nc_s, -(-(past + dec) // SEL_LEN))
    e_s = jnp.asarray(_expand_matrix(past)).astype(BF16)
    n_phys = cache_nsa_kv.shape[1]
    wk = cache_nsa_win.shape[2]
    cache_t = jnp.transpose(cache_nsa_kv, (0, 1, 3, 4, 5, 2)).reshape(depth, n_phys, 4 * LANES, PAGE_SIZE)
    cwin_t = jnp.transpose(cache_nsa_win, (0, 1, 3, 4, 5, 2)).reshape(depth, dbsz, 2 * LANES, wk)
    cs_p, sn_p = _rope_tables(jnp.arange(t))
    cs_s, sn_s = _rope_tables(past + jnp.arange(SEQ_PAD))
    cs_s = jnp.tile(cs_s, (SEQ_PER_STEP, 1))
    sn_s = jnp.tile(sn_s, (SEQ_PER_STEP, 1))

    outs = [[] for _ in range(10)]
    for i in range(depth):
        w_in_p = _prep_w_in(w_in[i])
        wb = _prep_w_branch(w_branch[i])
        cw = _prep_compress(nsa_cmp_pe[i], nsa_cmp_w1[i], nsa_cmp_w2[i])
        par = jnp.zeros((SUBLANES, LANES), F32).at[0, :GDN_HEADS].set(gdn_a_log[i]).at[1, :GDN_HEADS].set(gdn_dt_bias[i])
        ng = gdn_norm_g[i].reshape(1, GDN_HD)

        z = in_proj(x, g_mix[i].reshape(1, d), w_in_p)
        kv_all = lax.slice(z, (0, ZKV), (n_p + n_s, ZNG))
        kv_p = kv_all[:n_p]
        kv_s = kv_all[n_p:].reshape(dbsz, SEQ_PAD, ZNG - ZKV)[:, :dec]

        kcmp_sub = kv_p[:, :2 * LANES].reshape(n_p // CMP_STRIDE, CMP_STRIDE * 2 * LANES)
        kcvc = compress_prompt(kcmp_sub, cw, bsz, t)
        o_nsa_p = nsa_prompt(z, kcvc, prompt_kv_slab(z, n_p, t), shift_p, ov_p, bsz, t)
        o_nsa_s = nsa_sample(z, page_table, cache_t, cwin_t, cw, e_s, ov_s, i, n_p, dbsz, past, dec)

        o_gdn_p, gs_p = gdn_prompt(z, gdn_conv_w[i], par, ng, bsz, t)
        prev_pad = _pad_rows(state_gdn_conv[i], SEQ_PAD).reshape(n_s, 3 * MIX_WIDTH)
        o_gdn_s, gs_s = gdn_sample(z, prev_pad, state_gdn, i, gdn_conv_w[i], par, ng, n_p, dbsz, dec)

        o_ret_p, rs_p = ret_prompt(z, cs_p, sn_p, bsz, t)
        o_ret_s, rs_s = ret_sample(z, cs_s, sn_s, state_ret, i, n_p, dbsz, dec)

        x = merge(x, (o_nsa_p, o_gdn_p, o_ret_p), (o_nsa_s, o_gdn_s, o_ret_s), z, wb, w_out[i].astype(BF16))
        x = mlp_ple(x, g_mlp[i].reshape(1, d), w_up[i].astype(BF16), w_down[i].astype(BF16), p_all[i],
                    w_ple[i].astype(BF16), w_ple_gate[i].astype(BF16), g_final.reshape(1, d),
                    final_norm=(i == depth - 1))

        nkv = ZKW - ZKV
        outs[0].append(kv_p[:, :nkv].reshape(bsz, t, 4, NSA_KV, NSA_HD))
        outs[1].append(kv_s[:, :, :nkv].reshape(dbsz, dec, 4, NSA_KV, NSA_HD))
        wlen = min(WINDOW, t)
        outs[2].append(kv_p.reshape(bsz, t, ZNG - ZKV)[:, t - wlen:, nkv:].reshape(bsz, wlen, 2, NSA_KV, NSA_HD))
        win = jnp.concatenate([cache_nsa_win[i], kv_s[:, :, nkv:].reshape(dbsz, dec, 2, NSA_KV, NSA_HD)], axis=1)
        wlen_s = min(WINDOW, past + dec)
        outs[3].append(win[:, win.shape[1] - wlen_s:])
        nconv = CONV_W - 1
        outs[4].append(jnp.stack([lax.slice(z, (b * t + t - nconv, ZGQ), (b * t + t, ZGQ + 3 * MIX_WIDTH))
                                  for b in range(bsz)]))
        gq_s = lax.slice(z, (n_p, ZGQ), (n_p + n_s, ZGQ + 3 * MIX_WIDTH)).reshape(dbsz, SEQ_PAD, 3 * MIX_WIDTH)
        outs[5].append(gq_s[:, dec - nconv:dec])
        outs[6].append(gs_p)
        outs[7].append(gs_s)
        outs[8].append(rs_p)
        outs[9].append(rs_s)

    y_prompt = x[:n_p].reshape(bsz, t, d)
    y_sample = x[n_p:].reshape(dbsz, SEQ_PAD, d)[:, :dec]
    return (y_prompt, y_sample) + tuple(jnp.stack(o) for o in outs)
```

```python
import functools
import math

import numpy as np
import jax
import jax.numpy as jnp
from jax import lax
from jax.experimental import pallas as pl
from jax.experimental.pallas import tpu as pltpu

F32 = jnp.float32
BF16 = jnp.bfloat16

D_MODEL = 1024
MIX_WIDTH = D_MODEL // 2
N_BRANCH = 3
NSA_HEADS = 8
NSA_HD = 64
NSA_KV = 2
NSA_GRP = NSA_HEADS // NSA_KV
CMP_LEN = 32
CMP_STRIDE = 16
CMP_HIDDEN = 64
SEL_LEN = 64
SEL_TOPK = 16
WINDOW = 512
FORCE_BONUS = 1000.0
GDN_HEADS = 4
GDN_HD = 128
CONV_W = 4
GDN_CHUNK = 64
RET_HEADS = 4
RET_HD = 128
RET_CHUNK = 128
ROPE_BASE = 10000.0
D_FF = 4 * D_MODEL
PLE_DIM = 256
EPS = 1e-6
NEG = -1e30
PAGE_SIZE = 128

LANES = 128
SUBLANES = 8
SEQ_PAD = SUBLANES
VMEM_LIMIT = 56 * 1024 * 1024

ZQ = 0
ZKV = 1024
ZKW = 1536
ZNG = 1792
ZGAB = 1920
ZGZ = 2048
ZRG = 2560
ZGQ = 3072
ZRQ = 4608
ZMG = 6144
ZW = 9216


def _sigmoid(x):
    return 1.0 / (1.0 + jnp.exp(-x))


def _silu(x):
    return x * _sigmoid(x)


def _softplus(x):
    return jnp.maximum(x, 0.0) + jnp.log1p(jnp.exp(-jnp.abs(x)))


def _gelu_tanh(x):
    return 0.5 * x * (1.0 + jnp.tanh(math.sqrt(2.0 / math.pi) * (x + 0.044715 * (x * x * x))))


def _rms(x, g):
    return x * lax.rsqrt(jnp.mean(x * x, axis=-1, keepdims=True) + EPS) * g


def _dot(a, b):
    return jnp.dot(a, b, preferred_element_type=F32)


def _dot_nt(a, b):
    return lax.dot_general(a, b, (((1,), (1,)), ((), ())), preferred_element_type=F32)


def _bdot(a, b):
    return _dot(a.astype(BF16), b.astype(BF16))


def _bdot_nt(a, b):
    return _dot_nt(a.astype(BF16), b.astype(BF16))


def _split2(a):
    hi = a.astype(BF16)
    lo = (a - hi.astype(F32)).astype(BF16)
    return hi, lo


def _dot3(a, b):
    ah, al = _split2(a)
    bh, bl = _split2(b)
    return _dot(ah, bh) + (_dot(ah, bl) + _dot(al, bh))


def _dot_mask_left(m01, x):
    m = m01.astype(BF16)
    x1 = x.astype(BF16)
    r1 = x - x1.astype(F32)
    x2 = r1.astype(BF16)
    x3 = (r1 - x2.astype(F32)).astype(BF16)
    return _dot(m, x1) + (_dot(m, x2) + _dot(m, x3))


def _pick_tile(n, prefs):
    for p in prefs:
        if n % p == 0:
            return p
    raise ValueError(f"no tile in {prefs} divides {n}")


def _cparams(sem):
    return pltpu.CompilerParams(dimension_semantics=sem, vmem_limit_bytes=VMEM_LIMIT)


def _in_kernel(x_ref, g_ref, w_ref, z_ref, h_ref):
    @pl.when(pl.program_id(1) == 0)
    def _():
        h_ref[...] = _rms(x_ref[...], g_ref[...]).astype(BF16)

    z_ref[...] = _dot(h_ref[...], w_ref[...])


def in_proj(x, g, w):
    n = x.shape[0]
    tm = _pick_tile(n, (1024, 512, 256, 128))
    tn = 1024
    return pl.pallas_call(
        _in_kernel,
        grid=(n // tm, ZW // tn),
        in_specs=[pl.BlockSpec((tm, D_MODEL), lambda i, j: (i, 0)),
                  pl.BlockSpec((1, D_MODEL), lambda i, j: (0, 0)),
                  pl.BlockSpec((D_MODEL, tn), lambda i, j: (0, j))],
        out_specs=pl.BlockSpec((tm, tn), lambda i, j: (i, j)),
        out_shape=jax.ShapeDtypeStruct((n, ZW), F32),
        scratch_shapes=[pltpu.VMEM((tm, D_MODEL), BF16)],
        compiler_params=_cparams(("parallel", "arbitrary")),
        name="in_proj",
    )(x, g, w)


def _merge_kernel(x_ref, onp_ref, ogp_ref, orp_ref, ons_ref, ogs_ref, ors_ref, mg_ref, wb_ref, wo_ref, out_ref,
                  *, n_prompt_tiles):
    is_prompt = pl.program_id(0) < n_prompt_tiles
    m = None
    for b, (op_ref, os_ref) in enumerate(((onp_ref, ons_ref), (ogp_ref, ogs_ref), (orp_ref, ors_ref))):
        o = jnp.where(is_prompt, op_ref[...], os_ref[...])
        br = _dot(o.astype(BF16), wb_ref[b])
        t = _sigmoid(mg_ref[:, b * D_MODEL:(b + 1) * D_MODEL]) * br
        m = t if m is None else m + t
    out_ref[...] = x_ref[...] + _dot(m.astype(BF16), wo_ref[...])


def merge(x, o_prompt, o_sample, z, wb, wo):
    n = x.shape[0]
    n_p = o_prompt[0].shape[0]
    n_s = o_sample[0].shape[0]
    tm = _pick_tile(math.gcd(n_p, n_s), (512, 256, 128))
    npt = n_p // tm
    row = lambda i: (i, 0)
    prow = lambda i: (jnp.minimum(i, npt - 1), 0)
    srow = lambda i: (jnp.maximum(i - npt, 0), 0)
    return pl.pallas_call(
        functools.partial(_merge_kernel, n_prompt_tiles=npt),
        grid=(n // tm,),
        in_specs=[pl.BlockSpec((tm, D_MODEL), row)]
                 + [pl.BlockSpec((tm, MIX_WIDTH), prow)] * N_BRANCH
                 + [pl.BlockSpec((tm, MIX_WIDTH), srow)] * N_BRANCH
                 + [
                  pl.BlockSpec((tm, N_BRANCH * D_MODEL), lambda i: (i, ZMG // (N_BRANCH * D_MODEL))),
                  pl.BlockSpec((N_BRANCH, MIX_WIDTH, D_MODEL), lambda i: (0, 0, 0)),
                  pl.BlockSpec((D_MODEL, D_MODEL), lambda i: (0, 0))],
        out_specs=pl.BlockSpec((tm, D_MODEL), row),
        out_shape=jax.ShapeDtypeStruct((n, D_MODEL), F32),
        compiler_params=_cparams(("parallel",)),
        name="merge",
    )(x, *o_prompt, *o_sample, z, wb, wo)


def _mlp_kernel(x_ref, g_ref, wu_ref, wd_ref, p_ref, wp_ref, wpg_ref, gf_ref, out_ref, h_ref, acc_ref,
                *, final_norm):
    f = pl.program_id(1)

    @pl.when(f == 0)
    def _():
        h_ref[...] = _rms(x_ref[...], g_ref[...]).astype(BF16)
        acc_ref[...] = jnp.zeros_like(acc_ref)

    up = jnp.maximum(_dot(h_ref[...], wu_ref[...]), 0.0)
    acc_ref[...] += _dot((up * up).astype(BF16), wd_ref[...])

    @pl.when(f == pl.num_programs(1) - 1)
    def _():
        x2 = x_ref[...] + acc_ref[...]
        ple = _dot(p_ref[...].astype(BF16), wp_ref[...])
        x3 = x2 + ple * _sigmoid(_dot(x2.astype(BF16), wpg_ref[...]))
        if final_norm:
            x3 = _rms(x3, gf_ref[...])
        out_ref[...] = x3


def mlp_ple(x, g, wu, wd, p, wp, wpg, gf, final_norm):
    n = x.shape[0]
    tm = _pick_tile(n, (512, 256, 128))
    tf = 1024
    return pl.pallas_call(
        functools.partial(_mlp_kernel, final_norm=final_norm),
        grid=(n // tm, D_FF // tf),
        in_specs=[pl.BlockSpec((tm, D_MODEL), lambda i, f: (i, 0)),
                  pl.BlockSpec((1, D_MODEL), lambda i, f: (0, 0)),
                  pl.BlockSpec((D_MODEL, tf), lambda i, f: (0, f)),
                  pl.BlockSpec((tf, D_MODEL), lambda i, f: (f, 0)),
                  pl.BlockSpec((tm, PLE_DIM), lambda i, f: (i, 0)),
                  pl.BlockSpec((PLE_DIM, D_MODEL), lambda i, f: (0, 0)),
                  pl.BlockSpec((D_MODEL, D_MODEL), lambda i, f: (0, 0)),
                  pl.BlockSpec((1, D_MODEL), lambda i, f: (0, 0))],
        out_specs=pl.BlockSpec((tm, D_MODEL), lambda i, f: (i, 0)),
        out_shape=jax.ShapeDtypeStruct((n, D_MODEL), F32),
        scratch_shapes=[pltpu.VMEM((tm, D_MODEL), BF16), pltpu.VMEM((tm, D_MODEL), F32)],
        compiler_params=_cparams(("parallel", "arbitrary")),
        name="mlp_ple",
    )(x, g, wu, wd, p, wp, wpg, gf)


def _compress_math(load_sub, pea_ref, peb_ref, w1a_ref, w1b_ref, w2_ref, nsub):
    n = len(load_sub)
    acc_a = [jnp.zeros((nsub, 4 * CMP_HIDDEN), F32) for _ in range(n)]
    acc_b = [jnp.zeros((nsub, 4 * CMP_HIDDEN), F32) for _ in range(n)]
    for j in range(CMP_STRIDE):
        for i in range(n):
            xj = load_sub[i](j)
            acc_a[i] = acc_a[i] + _dot((xj + pea_ref[j:j + 1, :]).astype(BF16), w1a_ref[j])
            acc_b[i] = acc_b[i] + _dot((xj + peb_ref[j:j + 1, :]).astype(BF16), w1b_ref[j])
    hids = [_gelu_tanh(a + pltpu.roll(b, nsub - 1, axis=0)) for a, b in zip(acc_a, acc_b)]
    return [_dot(hid.astype(BF16), w2_ref[...]) for hid in hids]


def _cmp_kernel(x_ref, pea_ref, peb_ref, w1a_ref, w1b_ref, w2_ref, out_ref, *, nsub):
    load = lambda j: x_ref[:, j * 256:(j + 1) * 256]
    out_ref[0] = _compress_math([load], pea_ref, peb_ref, w1a_ref, w1b_ref, w2_ref, nsub)[0]


def compress_prompt(kcmp_sub, cw, bsz, t):
    nsub = t // CMP_STRIDE
    const2 = lambda b: (0, 0)
    const3 = lambda b: (0, 0, 0)
    return pl.pallas_call(
        functools.partial(_cmp_kernel, nsub=nsub),
        grid=(bsz,),
        in_specs=[pl.BlockSpec((nsub, CMP_STRIDE * 256), lambda b: (b, 0)),
                  pl.BlockSpec((CMP_STRIDE, 256), const2),
                  pl.BlockSpec((CMP_STRIDE, 256), const2),
                  pl.BlockSpec((CMP_STRIDE, 256, 256), const3),
                  pl.BlockSpec((CMP_STRIDE, 256, 256), const3),
                  pl.BlockSpec((256, 256), const2)],
        out_specs=pl.BlockSpec((1, nsub, 256), lambda b: (b, 0, 0)),
        out_shape=jax.ShapeDtypeStruct((bsz, nsub, 256), F32),
        compiler_params=_cparams(("parallel",)),
        name="nsa_compress",
    )(kcmp_sub, cw["pea"], cw["peb"], cw["w1a"], cw["w1b"], cw["w2"])


def _topk_mask(scores, nblk, k):
    shape = scores[0].shape
    if shape[0] == LANES:
        return _topk_mask_square(scores, k)
    lane = lax.broadcasted_iota(jnp.int32, shape, 1)
    cnts = [jnp.zeros(shape, F32) for _ in scores]
    for j in range(nblk):
        for i, score in enumerate(scores):
            cj = score[:, j:j + 1]
            ge = jnp.where(cj >= score, 1.0, 0.0)
            gt = jnp.where(cj > score, 1.0, 0.0)
            cnts[i] = cnts[i] + jnp.where(lane > j, ge, gt)
    return [cnt < float(k) for cnt in cnts]


def _topk_cols(scs, k):
    shape = scs[0].shape
    idx = lax.broadcasted_iota(jnp.int32, shape, 0).astype(F32)
    picked = [jnp.zeros(shape, F32) for _ in scs]
    for _ in range(k):
        mxs = [jnp.max(sc, axis=0, keepdims=True) for sc in scs]
        firsts = [jnp.min(jnp.where(sc == mx, idx, float(shape[0])), axis=0, keepdims=True)
                  for sc, mx in zip(scs, mxs)]
        hits = [idx == first for first in firsts]
        picked = [jnp.where(hit, 1.0, pk) for hit, pk in zip(hits, picked)]
        scs = [jnp.where(hit, 3.0 * NEG, sc) for hit, sc in zip(hits, scs)]
    return picked


def _topk_mask_square(scores, k):
    picked = _topk_cols([jnp.transpose(s) for s in scores], k)
    return [jnp.transpose(p) > 0.5 for p in picked]


def _tile_rows(a, reps):
    return jnp.concatenate([a] * reps, axis=0)


def _nsa_queries(zq_ref, g, rows=slice(None)):
    scale = NSA_HD ** -0.5
    q = jnp.concatenate([zq_ref[rows, (NSA_GRP * g + h) * LANES:(NSA_GRP * g + h + 1) * LANES]
                         for h in range(NSA_GRP)], axis=0)
    return (q * scale).astype(BF16)


def _nsa_compressed_and_select(qgs, kcs, vcs, m_c, ov, pos1, ns, r):
    bias = _tile_rows(jnp.where(m_c, 0.0, NEG), NSA_GRP)
    ss = [_dot_nt(qg, kc) + bias for qg, kc in zip(qgs, kcs)]
    mxs = [jnp.max(s, axis=-1, keepdims=True) for s in ss]
    ps = [jnp.exp(s - mx) for s, mx in zip(ss, mxs)]
    ps = [p / jnp.sum(p, axis=-1, keepdims=True) * jnp.where(mx > 0.5 * NEG, 1.0, 0.0) for p, mx in zip(ps, mxs)]
    o_cs = [_dot(p.astype(BF16), vc) for p, vc in zip(ps, vcs)]
    psums = [(p[0:r] + p[r:2 * r]) + (p[2 * r:3 * r] + p[3 * r:4 * r]) for p in ps]
    imps = [_dot_mask_left_t(psum, ov) for psum in psums]
    blk = lax.broadcasted_iota(jnp.int32, (r, LANES), 1)
    valid = blk * SEL_LEN <= pos1
    cur = pos1 // SEL_LEN
    bonus = jnp.where((blk == 0) | (blk == cur) | (blk == cur - 1), FORCE_BONUS, 0.0)
    scores = [jnp.where(valid, imp + bonus, NEG) for imp in imps]
    sels = [sel & valid for sel in _topk_mask(scores, min(ns, LANES), min(SEL_TOPK, ns))]
    return o_cs, sels


def _dot_mask_left_t(x, m01):
    m = m01.astype(BF16)
    x1 = x.astype(BF16)
    r1 = x - x1.astype(F32)
    x2 = r1.astype(BF16)
    x3 = (r1 - x2.astype(F32)).astype(BF16)
    return _dot(x1, m) + (_dot(x2, m) + _dot(x3, m))


def _gate_cols(gates, g, c, r):
    return jnp.concatenate([gates[:, 3 * (NSA_GRP * g + h) + c:3 * (NSA_GRP * g + h) + c + 1]
                            for h in range(NSA_GRP)], axis=0)


def _nsa_assemble(o0, o1, r):
    lane = lax.broadcasted_iota(jnp.int32, (r, LANES), 1)
    return jnp.concatenate([jnp.where(lane < NSA_HD, o0[j * r:(j + 1) * r], o1[j * r:(j + 1) * r])
                            for j in range(NSA_GRP)], axis=1)


SEL_TILE = 1024
Q_TILE = 128


KV_KA = 0
KV_VA = 2 * LANES
KV_KW = 4 * LANES
KV_VWA = 5 * LANES
KV_COLS = 7 * LANES
SEL_LANE = NSA_HD


def _nsa_prompt_kernel(zq_ref, zg_ref, kcvc_ref, kv_ref, shift_ref, ov_ref, out_ref, *, t):
    r = Q_TILE
    groups = range(NSA_KV)
    qb = pl.program_id(1)
    s0 = qb * r
    pos1 = s0 + lax.broadcasted_iota(jnp.int32, (r, 1), 0)
    nc = t // CMP_STRIDE
    ns = t // SEL_LEN
    kc = kcvc_ref[0][:, 0:LANES].astype(BF16)
    vc = kcvc_ref[0][:, LANES:2 * LANES].astype(BF16)
    ncol = lax.broadcasted_iota(jnp.int32, (1, nc), 1)
    m_c = ((ncol * CMP_STRIDE + (CMP_LEN - 1)) <= pos1) & (ncol < nc - 1)
    gates = _sigmoid(zg_ref[...])
    ov = ov_ref[...]

    qs = [_nsa_queries(zq_ref, g) for g in groups]
    ocs, sels = _nsa_compressed_and_select(qs, [kc] * NSA_KV, [vc] * NSA_KV, m_c, ov, pos1, ns, r)
    sels = [jnp.where(sel, 1.0, 0.0).astype(BF16) for sel in sels]

    scale = NSA_HD ** -0.5
    qsel = []
    for g in groups:
        heads = [zq_ref[:, (NSA_GRP * g + h) * LANES:(NSA_GRP * g + h + 1) * LANES] for h in range(NSA_GRP)]
        if g == 1:
            heads = [pltpu.roll(q, NSA_HD, axis=1) for q in heads]
        qsel.append(jnp.concatenate(heads, axis=0) * scale)
    lane = lax.broadcasted_iota(jnp.int32, (1, LANES), 1)
    ind = jnp.where((lane >= SEL_LANE) & (lane < SEL_LANE + SEL_TILE // SEL_LEN), 1.0, 0.0)

    def tile_step(kt, carry, diagonal):
        k0 = pl.multiple_of(kt * SEL_TILE, SEL_TILE)
        shift = shift_ref[kt]
        va = kv_ref[pl.ds(k0, SEL_TILE), KV_VA:KV_VA + 2 * LANES]
        qas = [(qsel[g] + _tile_rows(NEG * (ind - _dot(sels[g], shift)), NSA_GRP)).astype(BF16) for g in groups]
        ss = [_dot_nt(qas[g], kv_ref[pl.ds(k0, SEL_TILE), KV_KA + g * LANES:KV_KA + (g + 1) * LANES])
              for g in groups]
        if diagonal:
            kpos = k0 + lax.broadcasted_iota(jnp.int32, (1, SEL_TILE), 1)
            causal = _tile_rows(jnp.where(kpos <= pos1, 0.0, NEG), NSA_GRP)
            ss = [s + causal for s in ss]
        m_news = [jnp.maximum(carry[g][0], jnp.max(ss[g], axis=-1, keepdims=True)) for g in groups]
        alphas = [jnp.exp(carry[g][0] - m_news[g]) for g in groups]
        ps = [jnp.exp((ss[g] - m_news[g]).astype(BF16)) for g in groups]
        accs = [alphas[g] * carry[g][1] + _dot(ps[g], va) for g in groups]
        return tuple((m_news[g], accs[g]) for g in groups)

    init = tuple((jnp.full((NSA_GRP * r, 1), NEG, F32), jnp.zeros((NSA_GRP * r, 2 * LANES), F32)) for _ in groups)
    n_full = s0 // SEL_TILE
    mid = lax.fori_loop(0, n_full, lambda kt, c: tile_step(kt, c, False), init)
    fin = tile_step(n_full, mid, True)

    span = WINDOW + r
    start = pl.multiple_of(jnp.maximum(s0 - WINDOW, 0), r)
    kw = kv_ref[pl.ds(start, span), KV_KW:KV_KW + LANES]
    vwa = kv_ref[pl.ds(start, span), KV_VWA:KV_VWA + 2 * LANES]
    kposw = start + lax.broadcasted_iota(jnp.int32, (1, span), 1)
    bias_w = _tile_rows(jnp.where((kposw <= pos1) & (kposw > pos1 - WINDOW), 0.0, NEG), NSA_GRP)

    o_ss = [fin[g][1][:, 0:LANES] / fin[g][1][:, LANES:LANES + 1] for g in groups]
    ss = [_dot_nt(qs[g], kw) + bias_w for g in groups]
    ps = [jnp.exp((s - jnp.max(s, axis=-1, keepdims=True)).astype(BF16)) for s in ss]
    ows = [_dot(p, vwa) for p in ps]
    o_ws = [ow[:, 0:LANES] / ow[:, LANES:LANES + 1] for ow in ows]
    outs = [_gate_cols(gates, g, 0, r) * ocs[g] + _gate_cols(gates, g, 1, r) * o_ss[g]
            + _gate_cols(gates, g, 2, r) * o_ws[g] for g in groups]
    out_ref[...] = _nsa_assemble(outs[0], outs[1], r)


def nsa_prompt(z, kcvc, kvb, shift, ov, bsz, t):
    nqb = t // Q_TILE
    return pl.pallas_call(
        functools.partial(_nsa_prompt_kernel, t=t),
        grid=(bsz, nqb),
        in_specs=[pl.BlockSpec((Q_TILE, NSA_HEADS * LANES), lambda b, q: (b * nqb + q, 0)),
                  pl.BlockSpec((Q_TILE, LANES), lambda b, q: (b * nqb + q, ZNG // LANES)),
                  pl.BlockSpec((1, t // CMP_STRIDE, 256), lambda b, q: (b, 0, 0)),
                  pl.BlockSpec((t, KV_COLS), lambda b, q: (b, 0)),
                  pl.BlockSpec((t // SEL_TILE, LANES, LANES), lambda b, q: (0, 0, 0)),
                  pl.BlockSpec((t // CMP_STRIDE, LANES), lambda b, q: (0, 0))],
        out_specs=pl.BlockSpec((Q_TILE, MIX_WIDTH), lambda b, q: (b * nqb + q, 0)),
        out_shape=jax.ShapeDtypeStruct((bsz * t, MIX_WIDTH), F32),
        compiler_params=_cparams(("parallel", "arbitrary")),
        name="nsa_prompt",
    )(z, z, kcvc, kvb, shift, ov)


SLAB_ROWS = 1024


def _kv_slab_kernel(slc_ref, win_ref, out_ref, *, t):
    rows = slc_ref.shape[0]
    blocks_per_tile = SEL_TILE // SEL_LEN
    lane = lax.broadcasted_iota(jnp.int32, (rows, LANES), 1)
    tok = (pl.program_id(0) * rows + lax.broadcasted_iota(jnp.int32, (rows, LANES), 0)) % t
    onehot = jnp.where(lane == SEL_LANE + (tok // SEL_LEN) % blocks_per_tile, 1.0, 0.0)
    ones_col = jnp.where(lane == 0, 1.0, 0.0)
    k_slc = slc_ref[:, 0:LANES]
    is_key = lane < NSA_HD
    pieces = [jnp.where(is_key, k_slc, onehot), jnp.where(is_key, pltpu.roll(k_slc, NSA_HD, axis=1), onehot),
              slc_ref[:, LANES:2 * LANES], ones_col, win_ref[:, 0:LANES], win_ref[:, LANES:2 * LANES], ones_col]
    for i, piece in enumerate(pieces):
        out_ref[:, i * LANES:(i + 1) * LANES] = piece.astype(BF16)


def prompt_kv_slab(z, n_p, t):
    rows = _pick_tile(n_p, (SLAB_ROWS, 512, 256, 128))
    return pl.pallas_call(
        functools.partial(_kv_slab_kernel, t=t),
        grid=(n_p // rows,),
        in_specs=[pl.BlockSpec((rows, 2 * LANES), lambda i: (i, (ZKV + 2 * LANES) // (2 * LANES))),
                  pl.BlockSpec((rows, 2 * LANES), lambda i: (i, ZKW // (2 * LANES)))],
        out_specs=pl.BlockSpec((rows, KV_COLS), lambda i: (i, 0)),
        out_shape=jax.ShapeDtypeStruct((n_p, KV_COLS), BF16),
        compiler_params=_cparams(("parallel",)),
        name="nsa_kv_slab",
    )(z, z)


def _shift_matrices(t):
    blocks_per_tile = SEL_TILE // SEL_LEN
    out = np.zeros((t // SEL_TILE, LANES, LANES), np.float32)
    for b in range(min(t // SEL_LEN, LANES)):
        out[b // blocks_per_tile, b, SEL_LANE + b % blocks_per_tile] = 1.0
    return jnp.asarray(out).astype(BF16)


NSA_SEQ_PER_STEP = 4


def _two_part_attention(args):
    s_ps = [_dot(a[0], a[1]) + a[3] for a in args]
    s_ns = [_dot_nt(a[0], a[4]) + a[6] for a in args]
    mxs = [jnp.maximum(jnp.max(sp, axis=-1, keepdims=True), jnp.max(sn, axis=-1, keepdims=True))
           for sp, sn in zip(s_ps, s_ns)]
    p_ps = [jnp.exp(sp - mx) for sp, mx in zip(s_ps, mxs)]
    p_ns = [jnp.exp(sn - mx) for sn, mx in zip(s_ns, mxs)]
    dens = [jnp.sum(pp, axis=-1, keepdims=True) + jnp.sum(pn, axis=-1, keepdims=True) for pp, pn in zip(p_ps, p_ns)]
    return [(_dot_nt(pp.astype(BF16), a[2]) + _dot(pn.astype(BF16), a[5])) / den
            for pp, pn, a, den in zip(p_ps, p_ns, args, dens)]


def _nsa_sample_kernel(pt_ref, zq_ref, zkv_ref, cwin_ref, cache_ref, pea_ref, peb_ref, w1a_ref, w1b_ref,
                       w2_ref, e_ref, ov_ref, out_ref, buf_ref, tokm_ref, sem_ref, *, layer, past, n_pages, dec):
    r = SEQ_PAD
    nseq = NSA_SEQ_PER_STEP
    seqs = range(nseq)
    groups = range(NSA_KV)
    b = pl.program_id(0)
    nb = pl.num_programs(0)
    slot = lax.rem(b, 2)

    def page_copy(step, sl, q, p):
        return pltpu.make_async_copy(cache_ref.at[layer, pt_ref[(step * nseq + q) * n_pages + p]],
                                     buf_ref.at[sl, q, :, pl.ds(p * PAGE_SIZE, PAGE_SIZE)],
                                     sem_ref.at[sl])

    def start_all(step, sl):
        for q in seqs:
            for p in range(n_pages):
                page_copy(step, sl, q, p).start()

    @pl.when(b == 0)
    def _():
        start_all(b, slot)

    @pl.when(b + 1 < nb)
    def _():
        start_all(b + 1, 1 - slot)

    for q in seqs:
        for p in range(n_pages):
            page_copy(b, slot, q, p).wait()

    nsub = past // CMP_STRIDE
    nc = nsub - 1
    ns = -(-(past + dec) // SEL_LEN)
    for c0 in range(2):
        for p in range(past // LANES):
            for q in seqs:
                tokm_ref[q, c0, p * LANES:(p + 1) * LANES, :] = jnp.transpose(
                    buf_ref[slot, q, c0 * LANES:(c0 + 1) * LANES, p * LANES:(p + 1) * LANES])
    loads = [lambda j, q=q: jnp.concatenate([tokm_ref[q, 0, pl.ds(j, nsub, stride=CMP_STRIDE), :],
                                             tokm_ref[q, 1, pl.ds(j, nsub, stride=CMP_STRIDE), :]], axis=1)
             for q in seqs]
    kcvcs = _compress_math(loads, pea_ref, peb_ref, w1a_ref, w1b_ref, w2_ref, nsub)

    row = lax.broadcasted_iota(jnp.int32, (r, 1), 0)
    pos1 = past + row
    ncol = lax.broadcasted_iota(jnp.int32, (1, nsub), 1)
    m_c = ((ncol * CMP_STRIDE + (CMP_LEN - 1)) <= pos1) & (ncol < nc)
    ov = ov_ref[...]
    kpos_p = lax.broadcasted_iota(jnp.int32, (1, past), 1)
    tnew = lax.broadcasted_iota(jnp.int32, (1, r), 1)
    kpos_n = past + tnew
    wk = cwin_ref.shape[3]
    kpos_w = past - wk + lax.broadcasted_iota(jnp.int32, (1, wk), 1)
    in_win = lambda kp: (kp <= pos1) & (kp > pos1 - WINDOW)
    bias_wp = _tile_rows(jnp.where(in_win(kpos_w), 0.0, NEG), NSA_GRP)
    bias_wn = _tile_rows(jnp.where(in_win(kpos_n) & (tnew < dec), 0.0, NEG), NSA_GRP)
    new_blk = past // SEL_LEN

    probs = [(q, g) for q in seqs for g in groups]
    rows = lambda q: slice(q * r, (q + 1) * r)
    zkv = lambda q, c0, c1: zkv_ref[rows(q), c0:c1]
    gates = [_sigmoid(zkv(q, ZNG - ZKV, ZNG - ZKV + LANES)) for q in seqs]
    qgs = [_nsa_queries(zq_ref, g, rows(q)) for q, g in probs]
    o_cs, sels = _nsa_compressed_and_select(
        qgs, [kcvcs[q][:, 0:LANES].astype(BF16) for q, g in probs],
        [kcvcs[q][:, LANES:2 * LANES].astype(BF16) for q, g in probs], m_c, ov, pos1, ns, r)
    selfs = [jnp.where(sel, 1.0, 0.0) for sel in sels]
    allow_ps = [(_dot(sf.astype(BF16), e_ref[...]) > 0.5) & (kpos_p <= pos1) for sf in selfs]
    allow_ns = [(sf[:, new_blk:new_blk + 1] > 0.5) & (kpos_n <= pos1) & (tnew < dec) for sf in selfs]
    o_ss = _two_part_attention([
        (qgs[i], buf_ref[slot, q, 2 * LANES:3 * LANES, :].astype(BF16),
         buf_ref[slot, q, 3 * LANES:4 * LANES, :].astype(BF16),
         _tile_rows(jnp.where(allow_ps[i], 0.0, NEG), NSA_GRP),
         zkv(q, 2 * LANES, 3 * LANES).astype(BF16), zkv(q, 3 * LANES, 4 * LANES).astype(BF16),
         _tile_rows(jnp.where(allow_ns[i], 0.0, NEG), NSA_GRP)) for i, (q, g) in enumerate(probs)])
    o_ws = _two_part_attention([
        (qgs[i], cwin_ref[0, q, 0:LANES, :].astype(BF16), cwin_ref[0, q, LANES:2 * LANES, :].astype(BF16), bias_wp,
         zkv(q, ZKW - ZKV, ZKW - ZKV + LANES).astype(BF16),
         zkv(q, ZKW - ZKV + LANES, ZKW - ZKV + 2 * LANES).astype(BF16), bias_wn)
        for i, (q, g) in enumerate(probs)])
    outs = [_gate_cols(gates[q], g, 0, r) * o_cs[i] + _gate_cols(gates[q], g, 1, r) * o_ss[i]
            + _gate_cols(gates[q], g, 2, r) * o_ws[i] for i, (q, g) in enumerate(probs)]
    for q in seqs:
        out_ref[rows(q), :] = _nsa_assemble(outs[NSA_KV * q], outs[NSA_KV * q + 1], r)


def nsa_sample(z, page_table, cache_t, cache_win_t, cw, e_mat, ov, layer, row0, dbsz, past, dec):
    n_pages = past // PAGE_SIZE
    nseq = NSA_SEQ_PER_STEP
    rows = nseq * SEQ_PAD
    rb0 = row0 // rows
    wk = cache_win_t.shape[3]
    c2 = lambda b, pt: (0, 0)
    c3 = lambda b, pt: (0, 0, 0)
    grid_spec = pltpu.PrefetchScalarGridSpec(
        num_scalar_prefetch=1,
        grid=(dbsz // nseq,),
        in_specs=[pl.BlockSpec((rows, NSA_HEADS * LANES), lambda b, pt: (rb0 + b, 0)),
                  pl.BlockSpec((rows, 1024), lambda b, pt: (rb0 + b, ZKV // 1024)),
                  pl.BlockSpec((1, nseq, 2 * LANES, wk), lambda b, pt: (layer, b, 0, 0)),
                  pl.BlockSpec(memory_space=pl.ANY),
                  pl.BlockSpec((CMP_STRIDE, 256), c2),
                  pl.BlockSpec((CMP_STRIDE, 256), c2),
                  pl.BlockSpec((CMP_STRIDE, 256, 256), c3),
                  pl.BlockSpec((CMP_STRIDE, 256, 256), c3),
                  pl.BlockSpec((256, 256), c2),
                  pl.BlockSpec((LANES, past), c2),
                  pl.BlockSpec((past // CMP_STRIDE, LANES), c2)],
        out_specs=pl.BlockSpec((rows, MIX_WIDTH), lambda b, pt: (b, 0)),
        scratch_shapes=[pltpu.VMEM((2, nseq, 4 * LANES, past), F32), pltpu.VMEM((nseq, 2, past, LANES), F32),
                        pltpu.SemaphoreType.DMA((2,))],
    )
    return pl.pallas_call(
        functools.partial(_nsa_sample_kernel, layer=layer, past=past, n_pages=n_pages, dec=dec),
        grid_spec=grid_spec,
        out_shape=jax.ShapeDtypeStruct((dbsz * SEQ_PAD, MIX_WIDTH), F32),
        compiler_params=_cparams(("arbitrary",)),
        name="nsa_sample",
    )(page_table.reshape(-1), z, z, cache_win_t, cache_t, cw["pea"], cw["peb"], cw["w1a"], cw["w1b"], cw["w2"],
      e_mat, ov)


def _seg_masks(r, c):
    ri = lax.broadcasted_iota(jnp.int32, (r, r), 0)
    ci = lax.broadcasted_iota(jnp.int32, (r, r), 1)
    same = (ri // c) == (ci // c)
    return same, same & (ri >= ci), same & (ri > ci)


def _gdn_prep(y, ab, par_ref, valid, c, u_ref, w_ref, qe_ref, qk_ref, kdt_ref, eg_ref):
    r = y.shape[0]
    same, incl, strict = _seg_masks(r, c)
    y = _silu(y)
    beta_all = _sigmoid(ab)
    g_all = -jnp.exp(par_ref[0:1, :]) * _softplus(ab + par_ref[1:2, :])
    if valid is not None:
        y = jnp.where(valid, y, 0.0)
        beta_all = jnp.where(valid, beta_all, 0.0)
        g_all = jnp.where(valid, g_all, 0.0)
    gcum_all = _dot_mask_left(jnp.where(incl, 1.0, 0.0), g_all)
    glast_all = _dot_mask_left(jnp.where(same, 1.0, 0.0), g_all)
    gcum_t = jnp.transpose(gcum_all)
    eye = jnp.where(lax.broadcasted_iota(jnp.int32, (r, r), 0) == lax.broadcasted_iota(jnp.int32, (r, r), 1),
                    1.0, 0.0)
    n_double = int(math.log2(c))
    tms, pws, rhss = [], [], []
    for h in range(GDN_HEADS):
        q = y[:, h * GDN_HD:(h + 1) * GDN_HD]
        k = y[:, MIX_WIDTH + h * GDN_HD:MIX_WIDTH + (h + 1) * GDN_HD]
        v = y[:, 2 * MIX_WIDTH + h * GDN_HD:2 * MIX_WIDTH + (h + 1) * GDN_HD]
        q = q * lax.rsqrt(jnp.sum(q * q, axis=-1, keepdims=True) + EPS) * (GDN_HD ** -0.5)
        k = k * lax.rsqrt(jnp.sum(k * k, axis=-1, keepdims=True) + EPS)
        g1 = jnp.broadcast_to(gcum_all[:, h:h + 1], (r, GDN_HD))
        g2 = jnp.broadcast_to(gcum_t[h:h + 1, :], (r, r))
        gl = jnp.broadcast_to(glast_all[:, h:h + 1], (r, GDN_HD))
        beta = jnp.broadcast_to(beta_all[:, GDN_HEADS + h:GDN_HEADS + h + 1], (r, GDN_HD))
        g1r = g1 if r == GDN_HD else jnp.broadcast_to(gcum_all[:, h:h + 1], (r, r))
        decay = jnp.where(incl, jnp.exp(jnp.where(incl, g1r - g2, 0.0)), 0.0)
        betar = beta if r == GDN_HD else jnp.broadcast_to(beta_all[:, GDN_HEADS + h:GDN_HEADS + h + 1], (r, r))
        a = jnp.where(strict, _bdot_nt(k, k) * decay * betar, 0.0)
        tms.append(eye - a)
        pws.append(a)
        eg1 = jnp.exp(g1)
        rhss.append(jnp.concatenate([v * beta, k * (beta * eg1)], axis=1))
        qe_ref[h] = q * eg1
        qk_ref[h] = jnp.where(incl, _bdot_nt(q, k) * decay, 0.0)
        kdt_ref[h] = jnp.transpose(k * jnp.exp(gl - g1))
        eg_ref[h] = jnp.exp(gl)
    for _ in range(n_double - 1):
        pws = [_bdot(pw, pw) for pw in pws]
        tms = [tm + _bdot(tm, pw) for tm, pw in zip(tms, pws)]
    for h in range(GDN_HEADS):
        sol = _dot3(tms[h], rhss[h])
        u_ref[h] = sol[:, 0:GDN_HD]
        w_ref[h] = sol[:, GDN_HD:2 * GDN_HD]


def _gdn_segment(h, row0, seg, c, s, u_ref, w_ref, qe_ref, qk_ref, kdt_ref, eg_ref, vn_ref, col_seg):
    rows = pl.ds(row0, c)
    vn = u_ref[h, rows, :] - _bdot(w_ref[h, rows, :], s)
    vn_ref[h, rows, :] = vn
    vn_all = vn_ref[h].astype(BF16)
    o = _bdot(qe_ref[h, rows, :], s) + _dot(qk_ref[h, rows, :].astype(BF16), vn_all)
    kdt = jnp.where(col_seg == seg, kdt_ref[h], 0.0)
    s_new = s * eg_ref[h, pl.ds(row0, 1), :] + _dot(kdt.astype(BF16), vn_all)
    return o, s_new


def _gdn_finish(o, zg, ng_ref):
    return _rms(o, ng_ref[...]) * _silu(zg)


GDN_TILE = 128


def _gdn_prompt_kernel(x_ref, ab_ref, zg_ref, cw_ref, par_ref, ng_ref, out_ref, st_ref,
                       halo_ref, u_ref, w_ref, qe_ref, qk_ref, kdt_ref, eg_ref, vn_ref):
    i = pl.program_id(1)
    r = GDN_TILE
    c = GDN_CHUNK

    @pl.when(i == 0)
    def _():
        halo_ref[...] = jnp.zeros_like(halo_ref)
        st_ref[...] = jnp.zeros_like(st_ref)

    x = x_ref[...]
    xx = jnp.concatenate([halo_ref[...], x], axis=0)
    halo_ref[...] = x[r - SUBLANES:r, :]
    y = None
    for j in range(CONV_W):
        sh = CONV_W - 1 - j
        xs = xx if sh == 0 else pltpu.roll(xx, sh, axis=0)
        t = xs[SUBLANES:, :] * cw_ref[j:j + 1, :]
        y = t if y is None else y + t
    _gdn_prep(y, ab_ref[...], par_ref, None, c, u_ref, w_ref, qe_ref, qk_ref, kdt_ref, eg_ref)
    vn_ref[...] = jnp.zeros_like(vn_ref)
    col_seg = lax.broadcasted_iota(jnp.int32, (GDN_HD, r), 1) // c
    states = [st_ref[0, h] for h in range(GDN_HEADS)]
    os_ = [[] for _ in range(GDN_HEADS)]
    for ck in range(r // c):
        for h in range(GDN_HEADS):
            o, states[h] = _gdn_segment(h, ck * c, ck, c, states[h], u_ref, w_ref, qe_ref, qk_ref, kdt_ref,
                                        eg_ref, vn_ref, col_seg)
            os_[h].append(o)
    for h in range(GDN_HEADS):
        st_ref[0, h] = states[h]
        o = jnp.concatenate(os_[h], axis=0)
        out_ref[:, h * GDN_HD:(h + 1) * GDN_HD] = _gdn_finish(o, zg_ref[:, h * GDN_HD:(h + 1) * GDN_HD], ng_ref)


def _gdn_scratch(r):
    hs = (GDN_HEADS, r, GDN_HD)
    return [pltpu.VMEM(hs, F32), pltpu.VMEM(hs, F32), pltpu.VMEM(hs, F32), pltpu.VMEM((GDN_HEADS, r, r), F32),
            pltpu.VMEM((GDN_HEADS, GDN_HD, r), F32), pltpu.VMEM(hs, F32), pltpu.VMEM(hs, F32)]


def gdn_prompt(z, conv_w, par, norm_g, bsz, t):
    nt = t // GDN_TILE
    r = GDN_TILE
    c2 = lambda b, i: (0, 0)
    return pl.pallas_call(
        _gdn_prompt_kernel,
        grid=(bsz, nt),
        in_specs=[pl.BlockSpec((r, 3 * MIX_WIDTH), lambda b, i: (b * nt + i, ZGQ // (3 * MIX_WIDTH))),
                  pl.BlockSpec((r, LANES), lambda b, i: (b * nt + i, ZGAB // LANES)),
                  pl.BlockSpec((r, MIX_WIDTH), lambda b, i: (b * nt + i, ZGZ // MIX_WIDTH)),
                  pl.BlockSpec((CONV_W, 3 * MIX_WIDTH), c2),
                  pl.BlockSpec((SUBLANES, LANES), c2),
                  pl.BlockSpec((1, GDN_HD), c2)],
        out_specs=[pl.BlockSpec((r, MIX_WIDTH), lambda b, i: (b * nt + i, 0)),
                   pl.BlockSpec((1, GDN_HEADS, GDN_HD, GDN_HD), lambda b, i: (b, 0, 0, 0))],
        out_shape=[jax.ShapeDtypeStruct((bsz * t, MIX_WIDTH), F32),
                   jax.ShapeDtypeStruct((bsz, GDN_HEADS, GDN_HD, GDN_HD), F32)],
        scratch_shapes=[pltpu.VMEM((SUBLANES, 3 * MIX_WIDTH), F32)] + _gdn_scratch(r),
        compiler_params=_cparams(("parallel", "arbitrary")),
        name="gdn_prompt",
    )(z, z, z, conv_w, par, norm_g)


SEQ_PER_STEP = 16


def _gdn_sample_kernel(x_ref, ab_ref, zg_ref, prev_ref, s0_ref, cw_ref, par_ref, ng_ref, out_ref, st_ref,
                       u_ref, w_ref, qe_ref, qk_ref, kdt_ref, eg_ref, vn_ref, o_ref, *, dec):
    r = SEQ_PER_STEP * SEQ_PAD
    c = SEQ_PAD
    row = lax.broadcasted_iota(jnp.int32, (r, 1), 0)
    tin = row % c
    valid = tin < dec
    xx = jnp.where(tin < CONV_W - 1, prev_ref[...], pltpu.roll(x_ref[...], CONV_W - 1, axis=0))
    y = None
    for j in range(CONV_W):
        xs = xx if j == 0 else pltpu.roll(xx, r - j, axis=0)
        t = xs * cw_ref[j:j + 1, :]
        y = t if y is None else y + t
    _gdn_prep(y, ab_ref[...], par_ref, valid, c, u_ref, w_ref, qe_ref, qk_ref, kdt_ref, eg_ref)
    vn_ref[...] = jnp.zeros_like(vn_ref)
    col_seg = lax.broadcasted_iota(jnp.int32, (GDN_HD, r), 1) // c

    def body(sq, carry):
        row0 = pl.multiple_of(sq * c, c)
        for h in range(GDN_HEADS):
            o, s = _gdn_segment(h, row0, sq, c, s0_ref[0, sq, h], u_ref, w_ref, qe_ref, qk_ref, kdt_ref, eg_ref,
                                vn_ref, col_seg)
            st_ref[sq, h] = s
            o_ref[h, pl.ds(row0, c), :] = o
        return carry

    lax.fori_loop(0, SEQ_PER_STEP, body, 0, unroll=4)
    for h in range(GDN_HEADS):
        out_ref[:, h * GDN_HD:(h + 1) * GDN_HD] = _gdn_finish(o_ref[h], zg_ref[:, h * GDN_HD:(h + 1) * GDN_HD],
                                                              ng_ref)


def gdn_sample(z, prev_pad, s0_all, layer, conv_w, par, norm_g, row0, dbsz, dec):
    r = SEQ_PER_STEP * SEQ_PAD
    rb0 = row0 // r
    c2 = lambda i: (0, 0)
    return pl.pallas_call(
        functools.partial(_gdn_sample_kernel, dec=dec),
        grid=(dbsz // SEQ_PER_STEP,),
        in_specs=[pl.BlockSpec((r, 3 * MIX_WIDTH), lambda i: (rb0 + i, ZGQ // (3 * MIX_WIDTH))),
                  pl.BlockSpec((r, LANES), lambda i: (rb0 + i, ZGAB // LANES)),
                  pl.BlockSpec((r, MIX_WIDTH), lambda i: (rb0 + i, ZGZ // MIX_WIDTH)),
                  pl.BlockSpec((r, 3 * MIX_WIDTH), lambda i: (i, 0)),
                  pl.BlockSpec((1, SEQ_PER_STEP, GDN_HEADS, GDN_HD, GDN_HD), lambda i: (layer, i, 0, 0, 0)),
                  pl.BlockSpec((CONV_W, 3 * MIX_WIDTH), c2),
                  pl.BlockSpec((SUBLANES, LANES), c2),
                  pl.BlockSpec((1, GDN_HD), c2)],
        out_specs=[pl.BlockSpec((r, MIX_WIDTH), lambda i: (i, 0)),
                   pl.BlockSpec((SEQ_PER_STEP, GDN_HEADS, GDN_HD, GDN_HD), lambda i: (i, 0, 0, 0))],
        out_shape=[jax.ShapeDtypeStruct((dbsz * SEQ_PAD, MIX_WIDTH), F32),
                   jax.ShapeDtypeStruct((dbsz, GDN_HEADS, GDN_HD, GDN_HD), F32)],
        scratch_shapes=_gdn_scratch(r) + [pltpu.VMEM((GDN_HEADS, r, GDN_HD), F32)],
        compiler_params=_cparams(("parallel",)),
        name="gdn_sample",
    )(z, z, z, prev_pad, s0_all, conv_w, par, norm_g)


def _ret_log_gamma(h):
    return math.log1p(-(2.0 ** (-5.0 - h)))


def _ret_prep(x, cs, sn, valid, h):
    qh = x[:, h * RET_HD:(h + 1) * RET_HD]
    kh = x[:, MIX_WIDTH + h * RET_HD:MIX_WIDTH + (h + 1) * RET_HD]
    vh = x[:, 2 * MIX_WIDTH + h * RET_HD:2 * MIX_WIDTH + (h + 1) * RET_HD]
    qr = qh * cs + pltpu.roll(qh, RET_HD // 2, axis=1) * sn
    kr = (kh * cs + pltpu.roll(kh, RET_HD // 2, axis=1) * sn) * (RET_HD ** -0.5)
    if valid is not None:
        kr = jnp.where(valid, kr, 0.0)
        vh = jnp.where(valid, vh, 0.0)
    return qr, kr, vh


def _ret_finish(o, gate):
    o = o * lax.rsqrt(jnp.mean(o * o, axis=-1, keepdims=True) + EPS)
    return o * _silu(gate)


def _ret_prompt_kernel(x_ref, gt_ref, cs_ref, sn_ref, out_ref, st_ref):
    i = pl.program_id(1)
    c = RET_CHUNK

    @pl.when(i == 0)
    def _():
        st_ref[...] = jnp.zeros_like(st_ref)

    x = x_ref[...]
    cs = cs_ref[...]
    sn = sn_ref[...]
    ri = lax.broadcasted_iota(jnp.int32, (c, c), 0)
    ci = lax.broadcasted_iota(jnp.int32, (c, c), 1)
    diff = (ri - ci).astype(F32)
    n = lax.broadcasted_iota(jnp.int32, (c, 1), 0).astype(F32)
    heads = range(RET_HEADS)
    lgs = [_ret_log_gamma(h) for h in heads]
    qkv = [_ret_prep(x, cs, sn, None, h) for h in heads]
    states = [st_ref[0, h] for h in heads]
    scores = [_bdot_nt(qkv[h][0], qkv[h][1]) * jnp.where(diff >= 0.0, jnp.exp(jnp.maximum(diff, 0.0) * lgs[h]), 0.0)
              for h in heads]
    cross = [_bdot(qkv[h][0], states[h]) * jnp.exp((n + 1.0) * lgs[h]) for h in heads]
    outs = [_bdot(scores[h], qkv[h][2]) + cross[h] for h in heads]
    kdts = [jnp.transpose(qkv[h][1] * jnp.exp((c - 1.0 - n) * lgs[h])).astype(BF16) for h in heads]
    for h in heads:
        st_ref[0, h] = states[h] * math.exp(c * lgs[h]) + _dot(kdts[h], qkv[h][2].astype(BF16))
        out_ref[:, h * RET_HD:(h + 1) * RET_HD] = _ret_finish(outs[h], gt_ref[:, h * RET_HD:(h + 1) * RET_HD])


def ret_prompt(z, cs, sn, bsz, t):
    c = RET_CHUNK
    nt = t // c
    return pl.pallas_call(
        _ret_prompt_kernel,
        grid=(bsz, nt),
        in_specs=[pl.BlockSpec((c, 3 * MIX_WIDTH), lambda b, i: (b * nt + i, ZRQ // (3 * MIX_WIDTH))),
                  pl.BlockSpec((c, MIX_WIDTH), lambda b, i: (b * nt + i, ZRG // MIX_WIDTH)),
                  pl.BlockSpec((c, RET_HD), lambda b, i: (i, 0)),
                  pl.BlockSpec((c, RET_HD), lambda b, i: (i, 0))],
        out_specs=[pl.BlockSpec((c, MIX_WIDTH), lambda b, i: (b * nt + i, 0)),
                   pl.BlockSpec((1, RET_HEADS, RET_HD, RET_HD), lambda b, i: (b, 0, 0, 0))],
        out_shape=[jax.ShapeDtypeStruct((bsz * t, MIX_WIDTH), F32),
                   jax.ShapeDtypeStruct((bsz, RET_HEADS, RET_HD, RET_HD), F32)],
        compiler_params=_cparams(("parallel", "arbitrary")),
        name="ret_prompt",
    )(z, z, cs, sn)


def _ret_sample_kernel(x_ref, gt_ref, cs_ref, sn_ref, s0_ref, out_ref, st_ref, q_ref, kdt_ref, v_ref, o_ref,
                       *, dec):
    r = SEQ_PER_STEP * SEQ_PAD
    c = SEQ_PAD
    row = lax.broadcasted_iota(jnp.int32, (r, 1), 0)
    tin = row % c
    valid = tin < dec
    n = tin.astype(F32)
    x = x_ref[...]
    cs = cs_ref[...]
    sn = sn_ref[...]
    same, incl, _ = _seg_masks(r, c)
    ri = lax.broadcasted_iota(jnp.int32, (r, r), 0)
    ci = lax.broadcasted_iota(jnp.int32, (r, r), 1)
    diff = (ri - ci).astype(F32)
    for h in range(RET_HEADS):
        lg = _ret_log_gamma(h)
        qr, kr, vh = _ret_prep(x, cs, sn, valid, h)
        dmat = jnp.where(incl, jnp.exp(jnp.maximum(diff, 0.0) * lg), 0.0)
        o_ref[h] = _bdot(_bdot_nt(qr, kr) * dmat, vh)
        q_ref[h] = qr
        kdt_ref[h] = jnp.transpose(kr * jnp.exp((dec - 1.0 - n) * lg))
        v_ref[h] = vh
    col_seg = lax.broadcasted_iota(jnp.int32, (RET_HD, r), 1) // c
    qdec = [jnp.exp((lax.broadcasted_iota(jnp.int32, (c, 1), 0).astype(F32) + 1.0) * _ret_log_gamma(h))
            for h in range(RET_HEADS)]

    def body(sq, carry):
        row0 = pl.multiple_of(sq * c, c)
        rows = pl.ds(row0, c)
        for h in range(RET_HEADS):
            s = s0_ref[0, sq, h]
            o_ref[h, rows, :] = o_ref[h, rows, :] + _bdot(q_ref[h, rows, :], s) * qdec[h]
            kdt = jnp.where(col_seg == sq, kdt_ref[h], 0.0)
            st_ref[sq, h] = s * math.exp(dec * _ret_log_gamma(h)) + _dot(kdt.astype(BF16), v_ref[h].astype(BF16))
        return carry

    lax.fori_loop(0, SEQ_PER_STEP, body, 0, unroll=4)
    for h in range(RET_HEADS):
        out_ref[:, h * RET_HD:(h + 1) * RET_HD] = _ret_finish(o_ref[h], gt_ref[:, h * RET_HD:(h + 1) * RET_HD])


def ret_sample(z, cs, sn, s0_all, layer, row0, dbsz, dec):
    r = SEQ_PER_STEP * SEQ_PAD
    rb0 = row0 // r
    hs = (RET_HEADS, r, RET_HD)
    return pl.pallas_call(
        functools.partial(_ret_sample_kernel, dec=dec),
        grid=(dbsz // SEQ_PER_STEP,),
        in_specs=[pl.BlockSpec((r, 3 * MIX_WIDTH), lambda i: (rb0 + i, ZRQ // (3 * MIX_WIDTH))),
                  pl.BlockSpec((r, MIX_WIDTH), lambda i: (rb0 + i, ZRG // MIX_WIDTH)),
                  pl.BlockSpec((r, RET_HD), lambda i: (0, 0)),
                  pl.BlockSpec((r, RET_HD), lambda i: (0, 0)),
                  pl.BlockSpec((1, SEQ_PER_STEP, RET_HEADS, RET_HD, RET_HD), lambda i: (layer, i, 0, 0, 0))],
        out_specs=[pl.BlockSpec((r, MIX_WIDTH), lambda i: (i, 0)),
                   pl.BlockSpec((SEQ_PER_STEP, RET_HEADS, RET_HD, RET_HD), lambda i: (i, 0, 0, 0))],
        out_shape=[jax.ShapeDtypeStruct((dbsz * SEQ_PAD, MIX_WIDTH), F32),
                   jax.ShapeDtypeStruct((dbsz, RET_HEADS, RET_HD, RET_HD), F32)],
        scratch_shapes=[pltpu.VMEM(hs, F32), pltpu.VMEM((RET_HEADS, RET_HD, r), F32), pltpu.VMEM(hs, F32),
                        pltpu.VMEM(hs, F32)],
        compiler_params=_cparams(("parallel",)),
        name="ret_sample",
    )(z, z, cs, sn, s0_all)


def _prep_w_in(w):
    sizes = (NSA_HEADS * NSA_HD, 6 * NSA_KV * NSA_HD, 3 * NSA_HEADS, 3 * MIX_WIDTH, GDN_HEADS, GDN_HEADS,
             MIX_WIDTH, 3 * MIX_WIDTH, MIX_WIDTH, N_BRANCH * D_MODEL)
    src = np.concatenate([[0], np.cumsum(sizes)])
    s_q, s_kv, s_ng, s_gq, s_ga, s_gb, s_gz, s_rq, s_rg, s_mg = src[:-1].tolist()
    wb = w.astype(BF16)
    out = jnp.zeros((w.shape[0], ZW), BF16)

    def put(o, dst, s0, n):
        return lax.dynamic_update_slice(o, lax.slice_in_dim(wb, s0, s0 + n, axis=1), (0, dst))

    for h in range(NSA_HEADS):
        out = put(out, ZQ + h * LANES + NSA_HD * (h // NSA_GRP), s_q + h * NSA_HD, NSA_HD)
    out = put(out, ZKV, s_kv, sizes[1] + sizes[2])
    out = put(out, ZGAB, s_ga, 2 * GDN_HEADS)
    out = put(out, ZGZ, s_gz, MIX_WIDTH)
    out = put(out, ZRG, s_rg, MIX_WIDTH)
    out = put(out, ZGQ, s_gq, 3 * MIX_WIDTH)
    out = put(out, ZRQ, s_rq, 3 * MIX_WIDTH)
    out = put(out, ZMG, s_mg, N_BRANCH * D_MODEL)
    return out


def _prep_w_branch(wb):
    w0 = wb[0].reshape(NSA_HEADS, NSA_HD, D_MODEL)
    order = [h for j in range(NSA_GRP) for h in (j, j + NSA_GRP)]
    w0 = w0[np.array(order)].reshape(MIX_WIDTH, D_MODEL)
    return jnp.stack([w0, wb[1], wb[2]]).astype(BF16)


def _prep_compress(pe, w1, w2):
    sel_k = jnp.asarray(np.diag([1.0, 1.0, 0.0, 0.0]).astype(np.float32))
    sel_v = jnp.asarray(np.diag([0.0, 0.0, 1.0, 1.0]).astype(np.float32))

    def blockdiag(mk, mv):
        full = (sel_k[:, None, :, None] * mk[..., None, :, None, :]
                + sel_v[:, None, :, None] * mv[..., None, :, None, :])
        return full.reshape(mk.shape[:-2] + (4 * mk.shape[-2], 4 * mk.shape[-1]))

    w1r = w1.reshape(2, CMP_LEN, NSA_HD, CMP_HIDDEN)
    w1a = blockdiag(w1r[0, :CMP_STRIDE], w1r[1, :CMP_STRIDE])
    w1b = blockdiag(w1r[0, CMP_STRIDE:], w1r[1, CMP_STRIDE:])
    pea = jnp.concatenate([pe[0, :CMP_STRIDE], pe[0, :CMP_STRIDE], pe[1, :CMP_STRIDE], pe[1, :CMP_STRIDE]], axis=1)
    peb = jnp.concatenate([pe[0, CMP_STRIDE:], pe[0, CMP_STRIDE:], pe[1, CMP_STRIDE:], pe[1, CMP_STRIDE:]], axis=1)
    w2b = blockdiag(w2[0], w2[1])
    return dict(pea=pea, peb=peb, w1a=w1a.astype(BF16), w1b=w1b.astype(BF16), w2=w2b.astype(BF16))


def _overlap_matrix(nc_rows, nc, ns):
    n = np.arange(nc_rows)[:, None]
    s = np.arange(LANES)[None, :]
    c_start = n * CMP_STRIDE
    ov = (c_start < (s + 1) * SEL_LEN) & (s * SEL_LEN < c_start + CMP_LEN) & (n < nc) & (s < ns)
    return jnp.asarray(ov.astype(np.float32))


def _expand_matrix(nkeys):
    s = np.arange(LANES)[:, None]
    k = np.arange(nkeys)[None, :]
    return (k // SEL_LEN == s).astype(np.float32)


def _rope_tables(pos):
    half = RET_HD // 2
    inv = ROPE_BASE ** (-jnp.linspace(0.0, 1.0, half, dtype=F32))
    ang = pos.astype(F32)[:, None] * inv[None, :]
    cos, sin = jnp.cos(ang), jnp.sin(ang)
    return jnp.concatenate([cos, cos], axis=1), jnp.concatenate([-sin, sin], axis=1)


def _pad_rows(a, n):
    return jnp.pad(a, ((0, 0), (0, n - a.shape[1])) + ((0, 0),) * (a.ndim - 2))


def kernel(x_prompt, x_sample, cache_nsa_kv, cache_nsa_win, state_gdn_conv, state_gdn, state_ret, page_table,
           p_prompt, p_sample, g_mix, w_in, nsa_cmp_pe, nsa_cmp_w1, nsa_cmp_w2, gdn_conv_w, gdn_a_log,
           gdn_dt_bias, gdn_norm_g, w_branch, w_out, g_mlp, w_up, w_down, w_ple, w_ple_gate, g_final):
    bsz, t, d = x_prompt.shape
    dbsz, dec, _ = x_sample.shape
    depth = w_in.shape[0]
    past = page_table.shape[1] * PAGE_SIZE
    n_p = bsz * t
    n_s = dbsz * SEQ_PAD
    assert dec <= SEQ_PAD and dec >= CONV_W - 1 and (past % SEL_LEN) + dec <= SEL_LEN
    assert t % SEL_TILE == 0 and t >= WINDOW + Q_TILE and past % PAGE_SIZE == 0 and dbsz % SEQ_PER_STEP == 0

    x = jnp.concatenate([x_prompt.reshape(n_p, d), _pad_rows(x_sample, SEQ_PAD).reshape(n_s, d)], axis=0)
    p_all = jnp.concatenate([p_prompt.reshape(depth, n_p, PLE_DIM),
                             jnp.pad(p_sample, ((0, 0), (0, 0), (0, SEQ_PAD - dec), (0, 0))).reshape(depth, n_s, PLE_DIM)],
                            axis=1)

    nc_p = t // CMP_STRIDE - 1
    ov_p = _overlap_matrix(t // CMP_STRIDE, nc_p, t // SEL_LEN)
    shift_p = _shift_matrices(t)
    nc_s = past // CMP_STRIDE - 1
    ov_s = _overlap_matrix(past // CMP_STRIDE, nc_s, -(-(past + dec) // SEL_LEN))
    e_s = jnp.asarray(_expand_matrix(past)).astype(BF16)
    n_phys = cache_nsa_kv.shape[1]
    wk = cache_nsa_win.shape[2]
    cache_t = jnp.transpose(cache_nsa_kv, (0, 1, 3, 4, 5, 2)).reshape(depth, n_phys, 4 * LANES, PAGE_SIZE)
    cwin_t = jnp.transpose(cache_nsa_win, (0, 1, 3, 4, 5, 2)).reshape(depth, dbsz, 2 * LANES, wk)
    cs_p, sn_p = _rope_tables(jnp.arange(t))
    cs_s, sn_s = _rope_tables(past + jnp.arange(SEQ_PAD))
    cs_s = jnp.tile(cs_s, (SEQ_PER_STEP, 1))
    sn_s = jnp.tile(sn_s, (SEQ_PER_STEP, 1))

    outs = [[] for _ in range(10)]
    for i in range(depth):
        w_in_p = _prep_w_in(w_in[i])
        wb = _prep_w_branch(w_branch[i])
        cw = _prep_compress(nsa_cmp_pe[i], nsa_cmp_w1[i], nsa_cmp_w2[i])
        par = jnp.zeros((SUBLANES, LANES), F32).at[0, :GDN_HEADS].set(gdn_a_log[i]).at[1, :GDN_HEADS].set(gdn_dt_bias[i])
        ng = gdn_norm_g[i].reshape(1, GDN_HD)

        z = in_proj(x, g_mix[i].reshape(1, d), w_in_p)
        kv_all = lax.slice(z, (0, ZKV), (n_p + n_s, ZNG))
        kv_p = kv_all[:n_p]
        kv_s = kv_all[n_p:].reshape(dbsz, SEQ_PAD, ZNG - ZKV)[:, :dec]

        kcmp_sub = kv_p[:, :2 * LANES].reshape(n_p // CMP_STRIDE, CMP_STRIDE * 2 * LANES)
        kcvc = compress_prompt(kcmp_sub, cw, bsz, t)
        o_nsa_p = nsa_prompt(z, kcvc, prompt_kv_slab(z, n_p, t), shift_p, ov_p, bsz, t)
        o_nsa_s = nsa_sample(z, page_table, cache_t, cwin_t, cw, e_s, ov_s, i, n_p, dbsz, past, dec)

        o_gdn_p, gs_p = gdn_prompt(z, gdn_conv_w[i], par, ng, bsz, t)
        prev_pad = _pad_rows(state_gdn_conv[i], SEQ_PAD).reshape(n_s, 3 * MIX_WIDTH)
        o_gdn_s, gs_s = gdn_sample(z, prev_pad, state_gdn, i, gdn_conv_w[i], par, ng, n_p, dbsz, dec)

        o_ret_p, rs_p = ret_prompt(z, cs_p, sn_p, bsz, t)
        o_ret_s, rs_s = ret_sample(z, cs_s, sn_s, state_ret, i, n_p, dbsz, dec)

        x = merge(x, (o_nsa_p, o_gdn_p, o_ret_p), (o_nsa_s, o_gdn_s, o_ret_s), z, wb, w_out[i].astype(BF16))
        x = mlp_ple(x, g_mlp[i].reshape(1, d), w_up[i].astype(BF16), w_down[i].astype(BF16), p_all[i],
                    w_ple[i].astype(BF16), w_ple_gate[i].astype(BF16), g_final.reshape(1, d),
                    final_norm=(i == depth - 1))

        nkv = ZKW - ZKV
        outs[0].append(kv_p[:, :nkv].reshape(bsz, t, 4, NSA_KV, NSA_HD))
        outs[1].append(kv_s[:, :, :nkv].reshape(dbsz, dec, 4, NSA_KV, NSA_HD))
        wlen = min(WINDOW, t)
        outs[2].append(kv_p.reshape(bsz, t, ZNG - ZKV)[:, t - wlen:, nkv:].reshape(bsz, wlen, 2, NSA_KV, NSA_HD))
        win = jnp.concatenate([cache_nsa_win[i], kv_s[:, :, nkv:].reshape(dbsz, dec, 2, NSA_KV, NSA_HD)], axis=1)
        wlen_s = min(WINDOW, past + dec)
        outs[3].append(win[:, win.shape[1] - wlen_s:])
        nconv = CONV_W - 1
        outs[4].append(jnp.stack([lax.slice(z, (b * t + t - nconv, ZGQ), (b * t + t, ZGQ + 3 * MIX_WIDTH))
                                  for b in range(bsz)]))
        gq_s = lax.slice(z, (n_p, ZGQ), (n_p + n_s, ZGQ + 3 * MIX_WIDTH)).reshape(dbsz, SEQ_PAD, 3 * MIX_WIDTH)
        outs[5].append(gq_s[:, dec - nconv:dec])
        outs[6].append(gs_p)
        outs[7].append(gs_s)
        outs[8].append(rs_p)
        outs[9].append(rs_s)

    y_prompt = x[:n_p].reshape(bsz, t, d)
    y_sample = x[n_p:].reshape(dbsz, SEQ_PAD, d)[:, :dec]
    return (y_prompt, y_sample) + tuple(jnp.stack(o) for o in outs)
```

```python
import functools
import math

import numpy as np
import jax
import jax.numpy as jnp
from jax import lax
from jax.experimental import pallas as pl
from jax.experimental.pallas import tpu as pltpu

F32 = jnp.float32
BF16 = jnp.bfloat16

D_MODEL = 1024
MIX_WIDTH = D_MODEL // 2
N_BRANCH = 3
NSA_HEADS = 8
NSA_HD = 64
NSA_KV = 2
NSA_GRP = NSA_HEADS // NSA_KV
CMP_LEN = 32
CMP_STRIDE = 16
CMP_HIDDEN = 64
SEL_LEN = 64
SEL_TOPK = 16
WINDOW = 512
FORCE_BONUS = 1000.0
GDN_HEADS = 4
GDN_HD = 128
CONV_W = 4
GDN_CHUNK = 64
RET_HEADS = 4
RET_HD = 128
RET_CHUNK = 128
ROPE_BASE = 10000.0
D_FF = 4 * D_MODEL
PLE_DIM = 256
EPS = 1e-6
NEG = -1e30
PAGE_SIZE = 128

LANES = 128
SUBLANES = 8
SEQ_PAD = SUBLANES
VMEM_LIMIT = 56 * 1024 * 1024

ZQ = 0
ZKV = 1024
ZKW = 1536
ZNG = 1792
ZGAB = 1920
ZGZ = 2048
ZRG = 2560
ZGQ = 3072
ZRQ = 4608
ZMG = 6144
ZW = 9216


def _sigmoid(x):
    return 1.0 / (1.0 + jnp.exp(-x))


def _silu(x):
    return x * _sigmoid(x)


def _softplus(x):
    return jnp.maximum(x, 0.0) + jnp.log1p(jnp.exp(-jnp.abs(x)))


def _gelu_tanh(x):
    return 0.5 * x * (1.0 + jnp.tanh(math.sqrt(2.0 / math.pi) * (x + 0.044715 * (x * x * x))))


def _rms(x, g):
    return x * lax.rsqrt(jnp.mean(x * x, axis=-1, keepdims=True) + EPS) * g


def _dot(a, b):
    return jnp.dot(a, b, preferred_element_type=F32)


def _dot_nt(a, b):
    return lax.dot_general(a, b, (((1,), (1,)), ((), ())), preferred_element_type=F32)


def _bdot(a, b):
    return _dot(a.astype(BF16), b.astype(BF16))


def _bdot_nt(a, b):
    return _dot_nt(a.astype(BF16), b.astype(BF16))


def _split2(a):
    hi = a.astype(BF16)
    lo = (a - hi.astype(F32)).astype(BF16)
    return hi, lo


def _dot3(a, b):
    ah, al = _split2(a)
    bh, bl = _split2(b)
    return _dot(ah, bh) + (_dot(ah, bl) + _dot(al, bh))


def _dot_mask_left(m01, x):
    m = m01.astype(BF16)
    x1 = x.astype(BF16)
    r1 = x - x1.astype(F32)
    x2 = r1.astype(BF16)
    x3 = (r1 - x2.astype(F32)).astype(BF16)
    return _dot(m, x1) + (_dot(m, x2) + _dot(m, x3))


def _pick_tile(n, prefs):
    for p in prefs:
        if n % p == 0:
            return p
    raise ValueError(f"no tile in {prefs} divides {n}")


def _cparams(sem):
    return pltpu.CompilerParams(dimension_semantics=sem, vmem_limit_bytes=VMEM_LIMIT)


def _in_kernel(x_ref, g_ref, w_ref, z_ref, h_ref):
    @pl.when(pl.program_id(1) == 0)
    def _():
        h_ref[...] = _rms(x_ref[...], g_ref[...]).astype(BF16)

    z_ref[...] = _dot(h_ref[...], w_ref[...])


def in_proj(x, g, w):
    n = x.shape[0]
    tm = _pick_tile(n, (1024, 512, 256, 128))
    tn = 1024
    return pl.pallas_call(
        _in_kernel,
        grid=(n // tm, ZW // tn),
        in_specs=[pl.BlockSpec((tm, D_MODEL), lambda i, j: (i, 0)),
                  pl.BlockSpec((1, D_MODEL), lambda i, j: (0, 0)),
                  pl.BlockSpec((D_MODEL, tn), lambda i, j: (0, j))],
        out_specs=pl.BlockSpec((tm, tn), lambda i, j: (i, j)),
        out_shape=jax.ShapeDtypeStruct((n, ZW), F32),
        scratch_shapes=[pltpu.VMEM((tm, D_MODEL), BF16)],
        compiler_params=_cparams(("parallel", "arbitrary")),
        name="in_proj",
    )(x, g, w)


def _merge_kernel(x_ref, onp_ref, ogp_ref, orp_ref, ons_ref, ogs_ref, ors_ref, mg_ref, wb_ref, wo_ref, out_ref,
                  *, n_prompt_tiles):
    is_prompt = pl.program_id(0) < n_prompt_tiles
    m = None
    for b, (op_ref, os_ref) in enumerate(((onp_ref, ons_ref), (ogp_ref, ogs_ref), (orp_ref, ors_ref))):
        o = jnp.where(is_prompt, op_ref[...], os_ref[...])
        br = _dot(o.astype(BF16), wb_ref[b])
        t = _sigmoid(mg_ref[:, b * D_MODEL:(b + 1) * D_MODEL]) * br
        m = t if m is None else m + t
    out_ref[...] = x_ref[...] + _dot(m.astype(BF16), wo_ref[...])


def merge(x, o_prompt, o_sample, z, wb, wo):
    n = x.shape[0]
    n_p = o_prompt[0].shape[0]
    n_s = o_sample[0].shape[0]
    tm = _pick_tile(math.gcd(n_p, n_s), (512, 256, 128))
    npt = n_p // tm
    row = lambda i: (i, 0)
    prow = lambda i: (jnp.minimum(i, npt - 1), 0)
    srow = lambda i: (jnp.maximum(i - npt, 0), 0)
    return pl.pallas_call(
        functools.partial(_merge_kernel, n_prompt_tiles=npt),
        grid=(n // tm,),
        in_specs=[pl.BlockSpec((tm, D_MODEL), row)]
                 + [pl.BlockSpec((tm, MIX_WIDTH), prow)] * N_BRANCH
                 + [pl.BlockSpec((tm, MIX_WIDTH), srow)] * N_BRANCH
                 + [
                  pl.BlockSpec((tm, N_BRANCH * D_MODEL), lambda i: (i, ZMG // (N_BRANCH * D_MODEL))),
                  pl.BlockSpec((N_BRANCH, MIX_WIDTH, D_MODEL), lambda i: (0, 0, 0)),
                  pl.BlockSpec((D_MODEL, D_MODEL), lambda i: (0, 0))],
        out_specs=pl.BlockSpec((tm, D_MODEL), row),
        out_shape=jax.ShapeDtypeStruct((n, D_MODEL), F32),
        compiler_params=_cparams(("parallel",)),
        name="merge",
    )(x, *o_prompt, *o_sample, z, wb, wo)


def _mlp_kernel(x_ref, g_ref, wu_ref, wd_ref, p_ref, wp_ref, wpg_ref, gf_ref, out_ref, h_ref, acc_ref,
                *, final_norm):
    f = pl.program_id(1)

    @pl.when(f == 0)
    def _():
        h_ref[...] = _rms(x_ref[...], g_ref[...]).astype(BF16)
        acc_ref[...] = jnp.zeros_like(acc_ref)

    up = jnp.maximum(_dot(h_ref[...], wu_ref[...]), 0.0)
    acc_ref[...] += _dot((up * up).astype(BF16), wd_ref[...])

    @pl.when(f == pl.num_programs(1) - 1)
    def _():
        x2 = x_ref[...] + acc_ref[...]
        ple = _dot(p_ref[...].astype(BF16), wp_ref[...])
        x3 = x2 + ple * _sigmoid(_dot(x2.astype(BF16), wpg_ref[...]))
        if final_norm:
            x3 = _rms(x3, gf_ref[...])
        out_ref[...] = x3


def mlp_ple(x, g, wu, wd, p, wp, wpg, gf, final_norm):
    n = x.shape[0]
    tm = _pick_tile(n, (512, 256, 128))
    tf = 1024
    return pl.pallas_call(
        functools.partial(_mlp_kernel, final_norm=final_norm),
        grid=(n // tm, D_FF // tf),
        in_specs=[pl.BlockSpec((tm, D_MODEL), lambda i, f: (i, 0)),
                  pl.BlockSpec((1, D_MODEL), lambda i, f: (0, 0)),
                  pl.BlockSpec((D_MODEL, tf), lambda i, f: (0, f)),
                  pl.BlockSpec((tf, D_MODEL), lambda i, f: (f, 0)),
                  pl.BlockSpec((tm, PLE_DIM), lambda i, f: (i, 0)),
                  pl.BlockSpec((PLE_DIM, D_MODEL), lambda i, f: (0, 0)),
                  pl.BlockSpec((D_MODEL, D_MODEL), lambda i, f: (0, 0)),
                  pl.BlockSpec((1, D_MODEL), lambda i, f: (0, 0))],
        out_specs=pl.BlockSpec((tm, D_MODEL), lambda i, f: (i, 0)),
        out_shape=jax.ShapeDtypeStruct((n, D_MODEL), F32),
        scratch_shapes=[pltpu.VMEM((tm, D_MODEL), BF16), pltpu.VMEM((tm, D_MODEL), F32)],
        compiler_params=_cparams(("parallel", "arbitrary")),
        name="mlp_ple",
    )(x, g, wu, wd, p, wp, wpg, gf)


def _compress_math(load_sub, pea_ref, peb_ref, w1a_ref, w1b_ref, w2_ref, nsub):
    n = len(load_sub)
    acc_a = [jnp.zeros((nsub, 4 * CMP_HIDDEN), F32) for _ in range(n)]
    acc_b = [jnp.zeros((nsub, 4 * CMP_HIDDEN), F32) for _ in range(n)]
    for j in range(CMP_STRIDE):
        for i in range(n):
            xj = load_sub[i](j)
            acc_a[i] = acc_a[i] + _dot((xj + pea_ref[j:j + 1, :]).astype(BF16), w1a_ref[j])
            acc_b[i] = acc_b[i] + _dot((xj + peb_ref[j:j + 1, :]).astype(BF16), w1b_ref[j])
    hids = [_gelu_tanh(a + pltpu.roll(b, nsub - 1, axis=0)) for a, b in zip(acc_a, acc_b)]
    return [_dot(hid.astype(BF16), w2_ref[...]) for hid in hids]


def _cmp_kernel(x_ref, pea_ref, peb_ref, w1a_ref, w1b_ref, w2_ref, out_ref, *, nsub):
    load = lambda j: x_ref[:, j * 256:(j + 1) * 256]
    out_ref[0] = _compress_math([load], pea_ref, peb_ref, w1a_ref, w1b_ref, w2_ref, nsub)[0]


def compress_prompt(kcmp_sub, cw, bsz, t):
    nsub = t // CMP_STRIDE
    const2 = lambda b: (0, 0)
    const3 = lambda b: (0, 0, 0)
    return pl.pallas_call(
        functools.partial(_cmp_kernel, nsub=nsub),
        grid=(bsz,),
        in_specs=[pl.BlockSpec((nsub, CMP_STRIDE * 256), lambda b: (b, 0)),
                  pl.BlockSpec((CMP_STRIDE, 256), const2),
                  pl.BlockSpec((CMP_STRIDE, 256), const2),
                  pl.BlockSpec((CMP_STRIDE, 256, 256), const3),
                  pl.BlockSpec((CMP_STRIDE, 256, 256), const3),
                  pl.BlockSpec((256, 256), const2)],
        out_specs=pl.BlockSpec((1, nsub, 256), lambda b: (b, 0, 0)),
        out_shape=jax.ShapeDtypeStruct((bsz, nsub, 256), F32),
        compiler_params=_cparams(("parallel",)),
        name="nsa_compress",
    )(kcmp_sub, cw["pea"], cw["peb"], cw["w1a"], cw["w1b"], cw["w2"])


def _topk_mask(scores, nblk, k):
    shape = scores[0].shape
    if shape[0] == LANES:
        return _topk_mask_square(scores, k)
    lane = lax.broadcasted_iota(jnp.int32, shape, 1)
    cnts = [jnp.zeros(shape, F32) for _ in scores]
    for j in range(nblk):
        for i, score in enumerate(scores):
            cj = score[:, j:j + 1]
            ge = jnp.where(cj >= score, 1.0, 0.0)
            gt = jnp.where(cj > score, 1.0, 0.0)
            cnts[i] = cnts[i] + jnp.where(lane > j, ge, gt)
    return [cnt < float(k) for cnt in cnts]


def _topk_cols(scs, k):
    shape = scs[0].shape
    idx = lax.broadcasted_iota(jnp.int32, shape, 0).astype(F32)
    picked = [jnp.zeros(shape, F32) for _ in scs]
    for _ in range(k):
        mxs = [jnp.max(sc, axis=0, keepdims=True) for sc in scs]
        firsts = [jnp.min(jnp.where(sc == mx, idx, float(shape[0])), axis=0, keepdims=True)
                  for sc, mx in zip(scs, mxs)]
        hits = [idx == first for first in firsts]
        picked = [jnp.where(hit, 1.0, pk) for hit, pk in zip(hits, picked)]
        scs = [jnp.where(hit, 3.0 * NEG, sc) for hit, sc in zip(hits, scs)]
    return picked


def _topk_mask_square(scores, k):
    picked = _topk_cols([jnp.transpose(s) for s in scores], k)
    return [jnp.transpose(p) > 0.5 for p in picked]


def _tile_rows(a, reps):
    return jnp.concatenate([a] * reps, axis=0)


def _nsa_queries(zq_ref, g, rows=slice(None)):
    scale = NSA_HD ** -0.5
    q = jnp.concatenate([zq_ref[rows, (NSA_GRP * g + h) * LANES:(NSA_GRP * g + h + 1) * LANES]
                         for h in range(NSA_GRP)], axis=0)
    return (q * scale).astype(BF16)


def _nsa_compressed_and_select(qgs, kcs, vcs, m_c, ov, pos1, ns, r):
    bias = _tile_rows(jnp.where(m_c, 0.0, NEG), NSA_GRP)
    ss = [_dot_nt(qg, kc) + bias for qg, kc in zip(qgs, kcs)]
    mxs = [jnp.max(s, axis=-1, keepdims=True) for s in ss]
    ps = [jnp.exp(s - mx) for s, mx in zip(ss, mxs)]
    ps = [p / jnp.sum(p, axis=-1, keepdims=True) * jnp.where(mx > 0.5 * NEG, 1.0, 0.0) for p, mx in zip(ps, mxs)]
    o_cs = [_dot(p.astype(BF16), vc) for p, vc in zip(ps, vcs)]
    psums = [(p[0:r] + p[r:2 * r]) + (p[2 * r:3 * r] + p[3 * r:4 * r]) for p in ps]
    imps = [_dot_mask_left_t(psum, ov) for psum in psums]
    blk = lax.broadcasted_iota(jnp.int32, (r, LANES), 1)
    valid = blk * SEL_LEN <= pos1
    cur = pos1 // SEL_LEN
    bonus = jnp.where((blk == 0) | (blk == cur) | (blk == cur - 1), FORCE_BONUS, 0.0)
    scores = [jnp.where(valid, imp + bonus, NEG) for imp in imps]
    sels = [sel & valid for sel in _topk_mask(scores, min(ns, LANES), min(SEL_TOPK, ns))]
    return o_cs, sels


def _dot_mask_left_t(x, m01):
    m = m01.astype(BF16)
    x1 = x.astype(BF16)
    r1 = x - x1.astype(F32)
    x2 = r1.astype(BF16)
    x3 = (r1 - x2.astype(F32)).astype(BF16)
    return _dot(x1, m) + (_dot(x2, m) + _dot(x3, m))


def _gate_cols(gates, g, c, r):
    return jnp.concatenate([gates[:, 3 * (NSA_GRP * g + h) + c:3 * (NSA_GRP * g + h) + c + 1]
                            for h in range(NSA_GRP)], axis=0)


def _nsa_assemble(o0, o1, r):
    lane = lax.broadcasted_iota(jnp.int32, (r, LANES), 1)
    return jnp.concatenate([jnp.where(lane < NSA_HD, o0[j * r:(j + 1) * r], o1[j * r:(j + 1) * r])
                            for j in range(NSA_GRP)], axis=1)


SEL_TILE = 1024
Q_TILE = 128


KV_KA = 0
KV_VA = 2 * LANES
KV_KW = 4 * LANES
KV_VWA = 5 * LANES
KV_COLS = 7 * LANES
SEL_LANE = NSA_HD


def _nsa_prompt_kernel(zq_ref, zg_ref, kcvc_ref, kv_ref, shift_ref, ov_ref, out_ref, *, t):
    r = Q_TILE
    groups = range(NSA_KV)
    qb = pl.program_id(1)
    s0 = qb * r
    pos1 = s0 + lax.broadcasted_iota(jnp.int32, (r, 1), 0)
    nc = t // CMP_STRIDE
    ns = t // SEL_LEN
    kc = kcvc_ref[0][:, 0:LANES].astype(BF16)
    vc = kcvc_ref[0][:, LANES:2 * LANES].astype(BF16)
    ncol = lax.broadcasted_iota(jnp.int32, (1, nc), 1)
    m_c = ((ncol * CMP_STRIDE + (CMP_LEN - 1)) <= pos1) & (ncol < nc - 1)
    gates = _sigmoid(zg_ref[...])
    ov = ov_ref[...]

    qs = [_nsa_queries(zq_ref, g) for g in groups]
    ocs, sels = _nsa_compressed_and_select(qs, [kc] * NSA_KV, [vc] * NSA_KV, m_c, ov, pos1, ns, r)
    sels = [jnp.where(sel, 1.0, 0.0).astype(BF16) for sel in sels]

    scale = NSA_HD ** -0.5
    qsel = []
    for g in groups:
        heads = [zq_ref[:, (NSA_GRP * g + h) * LANES:(NSA_GRP * g + h + 1) * LANES] for h in range(NSA_GRP)]
        if g == 1:
            heads = [pltpu.roll(q, NSA_HD, axis=1) for q in heads]
        qsel.append(jnp.concatenate(heads, axis=0) * scale)
    lane = lax.broadcasted_iota(jnp.int32, (1, LANES), 1)
    ind = jnp.where((lane >= SEL_LANE) & (lane < SEL_LANE + SEL_TILE // SEL_LEN), 1.0, 0.0)

    def tile_step(kt, carry, diagonal):
        k0 = pl.multiple_of(kt * SEL_TILE, SEL_TILE)
        shift = shift_ref[kt]
        va = kv_ref[pl.ds(k0, SEL_TILE), KV_VA:KV_VA + 2 * LANES]
        qas = [(qsel[g] + _tile_rows(NEG * (ind - _dot(sels[g], shift)), NSA_GRP)).astype(BF16) for g in groups]
        ss = [_dot_nt(qas[g], kv_ref[pl.ds(k0, SEL_TILE), KV_KA + g * LANES:KV_KA + (g + 1) * LANES])
              for g in groups]
        if diagonal:
            kpos = k0 + lax.broadcasted_iota(jnp.int32, (1, SEL_TILE), 1)
            causal = _tile_rows(jnp.where(kpos <= pos1, 0.0, NEG), NSA_GRP)
            ss = [s + causal for s in ss]
        m_news = [jnp.maximum(carry[g][0], jnp.max(ss[g], axis=-1, keepdims=True)) for g in groups]
        alphas = [jnp.exp(carry[g][0] - m_news[g]) for g in groups]
        ps = [jnp.exp((ss[g] - m_news[g]).astype(BF16)) for g in groups]
        accs = [alphas[g] * carry[g][1] + _dot(ps[g], va) for g in groups]
        return tuple((m_news[g], accs[g]) for g in groups)

    init = tuple((jnp.full((NSA_GRP * r, 1), NEG, F32), jnp.zeros((NSA_GRP * r, 2 * LANES), F32)) for _ in groups)
    n_full = s0 // SEL_TILE
    mid = lax.fori_loop(0, n_full, lambda kt, c: tile_step(kt, c, False), init)
    fin = tile_step(n_full, mid, True)

    span = WINDOW + r
    start = pl.multiple_of(jnp.maximum(s0 - WINDOW, 0), r)
    kw = kv_ref[pl.ds(start, span), KV_KW:KV_KW + LANES]
    vwa = kv_ref[pl.ds(start, span), KV_VWA:KV_VWA + 2 * LANES]
    kposw = start + lax.broadcasted_iota(jnp.int32, (1, span), 1)
    bias_w = _tile_rows(jnp.where((kposw <= pos1) & (kposw > pos1 - WINDOW), 0.0, NEG), NSA_GRP)

    o_ss = [fin[g][1][:, 0:LANES] / fin[g][1][:, LANES:LANES + 1] for g in groups]
    ss = [_dot_nt(qs[g], kw) + bias_w for g in groups]
    ps = [jnp.exp((s - jnp.max(s, axis=-1, keepdims=True)).astype(BF16)) for s in ss]
    ows = [_dot(p, vwa) for p in ps]
    o_ws = [ow[:, 0:LANES] / ow[:, LANES:LANES + 1] for ow in ows]
    outs = [_gate_cols(gates, g, 0, r) * ocs[g] + _gate_cols(gates, g, 1, r) * o_ss[g]
            + _gate_cols(gates, g, 2, r) * o_ws[g] for g in groups]
    out_ref[...] = _nsa_assemble(outs[0], outs[1], r)


def nsa_prompt(z, kcvc, kvb, shift, ov, bsz, t):
    nqb = t // Q_TILE
    return pl.pallas_call(
        functools.partial(_nsa_prompt_kernel, t=t),
        grid=(bsz, nqb),
        in_specs=[pl.BlockSpec((Q_TILE, NSA_HEADS * LANES), lambda b, q: (b * nqb + q, 0)),
                  pl.BlockSpec((Q_TILE, LANES), lambda b, q: (b * nqb + q, ZNG // LANES)),
                  pl.BlockSpec((1, t // CMP_STRIDE, 256), lambda b, q: (b, 0, 0)),
                  pl.BlockSpec((t, KV_COLS), lambda b, q: (b, 0)),
                  pl.BlockSpec((t // SEL_TILE, LANES, LANES), lambda b, q: (0, 0, 0)),
                  pl.BlockSpec((t // CMP_STRIDE, LANES), lambda b, q: (0, 0))],
        out_specs=pl.BlockSpec((Q_TILE, MIX_WIDTH), lambda b, q: (b * nqb + q, 0)),
        out_shape=jax.ShapeDtypeStruct((bsz * t, MIX_WIDTH), F32),
        compiler_params=_cparams(("parallel", "arbitrary")),
        name="nsa_prompt",
    )(z, z, kcvc, kvb, shift, ov)


SLAB_ROWS = 1024


def _kv_slab_kernel(slc_ref, win_ref, out_ref, *, t):
    rows = slc_ref.shape[0]
    blocks_per_tile = SEL_TILE // SEL_LEN
    lane = lax.broadcasted_iota(jnp.int32, (rows, LANES), 1)
    tok = (pl.program_id(0) * rows + lax.broadcasted_iota(jnp.int32, (rows, LANES), 0)) % t
    onehot = jnp.where(lane == SEL_LANE + (tok // SEL_LEN) % blocks_per_tile, 1.0, 0.0)
    ones_col = jnp.where(lane == 0, 1.0, 0.0)
    k_slc = slc_ref[:, 0:LANES]
    is_key = lane < NSA_HD
    pieces = [jnp.where(is_key, k_slc, onehot), jnp.where(is_key, pltpu.roll(k_slc, NSA_HD, axis=1), onehot),
              slc_ref[:, LANES:2 * LANES], ones_col, win_ref[:, 0:LANES], win_ref[:, LANES:2 * LANES], ones_col]
    for i, piece in enumerate(pieces):
        out_ref[:, i * LANES:(i + 1) * LANES] = piece.astype(BF16)


def prompt_kv_slab(z, n_p, t):
    rows = _pick_tile(n_p, (SLAB_ROWS, 512, 256, 128))
    return pl.pallas_call(
        functools.partial(_kv_slab_kernel, t=t),
        grid=(n_p // rows,),
        in_specs=[pl.BlockSpec((rows, 2 * LANES), lambda i: (i, (ZKV + 2 * LANES) // (2 * LANES))),
                  pl.BlockSpec((rows, 2 * LANES), lambda i: (i, ZKW // (2 * LANES)))],
        out_specs=pl.BlockSpec((rows, KV_COLS), lambda i: (i, 0)),
        out_shape=jax.ShapeDtypeStruct((n_p, KV_COLS), BF16),
        compiler_params=_cparams(("parallel",)),
        name="nsa_kv_slab",
    )(z, z)


def _shift_matrices(t):
    blocks_per_tile = SEL_TILE // SEL_LEN
    out = np.zeros((t // SEL_TILE, LANES, LANES), np.float32)
    for b in range(min(t // SEL_LEN, LANES)):
        out[b // blocks_per_tile, b, SEL_LANE + b % blocks_per_tile] = 1.0
    return jnp.asarray(out).astype(BF16)


NSA_SEQ_PER_STEP = 4


def _two_part_attention(args):
    s_ps = [_dot(a[0], a[1]) + a[3] for a in args]
    s_ns = [_dot_nt(a[0], a[4]) + a[6] for a in args]
    mxs = [jnp.maximum(jnp.max(sp, axis=-1, keepdims=True), jnp.max(sn, axis=-1, keepdims=True))
           for sp, sn in zip(s_ps, s_ns)]
    p_ps = [jnp.exp(sp - mx) for sp, mx in zip(s_ps, mxs)]
    p_ns = [jnp.exp(sn - mx) for sn, mx in zip(s_ns, mxs)]
    dens = [jnp.sum(pp, axis=-1, keepdims=True) + jnp.sum(pn, axis=-1, keepdims=True) for pp, pn in zip(p_ps, p_ns)]
    return [(_dot_nt(pp.astype(BF16), a[2]) + _dot(pn.astype(BF16), a[5])) / den
            for pp, pn, a, den in zip(p_ps, p_ns, args, dens)]


def _nsa_sample_kernel(pt_ref, zq_ref, zkv_ref, cwin_ref, cache_ref, pea_ref, peb_ref, w1a_ref, w1b_ref,
                       w2_ref, e_ref, ov_ref, out_ref, buf_ref, tokm_ref, sem_ref, *, layer, past, n_pages, dec):
    r = SEQ_PAD
    nseq = NSA_SEQ_PER_STEP
    seqs = range(nseq)
    groups = range(NSA_KV)
    b = pl.program_id(0)
    nb = pl.num_programs(0)
    slot = lax.rem(b, 2)

    def page_copy(step, sl, q, p):
        return pltpu.make_async_copy(cache_ref.at[layer, pt_ref[(step * nseq + q) * n_pages + p]],
                                     buf_ref.at[sl, q, :, pl.ds(p * PAGE_SIZE, PAGE_SIZE)],
                                     sem_ref.at[sl])

    def start_all(step, sl):
        for q in seqs:
            for p in range(n_pages):
                page_copy(step, sl, q, p).start()

    @pl.when(b == 0)
    def _():
        start_all(b, slot)

    @pl.when(b + 1 < nb)
    def _():
        start_all(b + 1, 1 - slot)

    for q in seqs:
        for p in range(n_pages):
            page_copy(b, slot, q, p).wait()

    nsub = past // CMP_STRIDE
    nc = nsub - 1
    ns = -(-(past + dec) // SEL_LEN)
    for c0 in range(2):
        for p in range(past // LANES):
            for q in seqs:
                tokm_ref[q, c0, p * LANES:(p + 1) * LANES, :] = jnp.transpose(
                    buf_ref[slot, q, c0 * LANES:(c0 + 1) * LANES, p * LANES:(p + 1) * LANES])
    loads = [lambda j, q=q: jnp.concatenate([tokm_ref[q, 0, pl.ds(j, nsub, stride=CMP_STRIDE), :],
                                             tokm_ref[q, 1, pl.ds(j, nsub, stride=CMP_STRIDE), :]], axis=1)
             for q in seqs]
    kcvcs = _compress_math(loads, pea_ref, peb_ref, w1a_ref, w1b_ref, w2_ref, nsub)

    row = lax.broadcasted_iota(jnp.int32, (r, 1), 0)
    pos1 = past + row
    ncol = lax.broadcasted_iota(jnp.int32, (1, nsub), 1)
    m_c = ((ncol * CMP_STRIDE + (CMP_LEN - 1)) <= pos1) & (ncol < nc)
    ov = ov_ref[...]
    kpos_p = lax.broadcasted_iota(jnp.int32, (1, past), 1)
    tnew = lax.broadcasted_iota(jnp.int32, (1, r), 1)
    kpos_n = past + tnew
    wk = cwin_ref.shape[3]
    kpos_w = past - wk + lax.broadcasted_iota(jnp.int32, (1, wk), 1)
    in_win = lambda kp: (kp <= pos1) & (kp > pos1 - WINDOW)
    bias_wp = _tile_rows(jnp.where(in_win(kpos_w), 0.0, NEG), NSA_GRP)
    bias_wn = _tile_rows(jnp.where(in_win(kpos_n) & (tnew < dec), 0.0, NEG), NSA_GRP)
    new_blk = past // SEL_LEN

    probs = [(q, g) for q in seqs for g in groups]
    rows = lambda q: slice(q * r, (q + 1) * r)
    zkv = lambda q, c0, c1: zkv_ref[rows(q), c0:c1]
    gates = [_sigmoid(zkv(q, ZNG - ZKV, ZNG - ZKV + LANES)) for q in seqs]
    qgs = [_nsa_queries(zq_ref, g, rows(q)) for q, g in probs]
    o_cs, sels = _nsa_compressed_and_select(
        qgs, [kcvcs[q][:, 0:LANES].astype(BF16) for q, g in probs],
        [kcvcs[q][:, LANES:2 * LANES].astype(BF16) for q, g in probs], m_c, ov, pos1, ns, r)
    selfs = [jnp.where(sel, 1.0, 0.0) for sel in sels]
    allow_ps = [(_dot(sf.astype(BF16), e_ref[...]) > 0.5) & (kpos_p <= pos1) for sf in selfs]
    allow_ns = [(sf[:, new_blk:new_blk + 1] > 0.5) & (kpos_n <= pos1) & (tnew < dec) for sf in selfs]
    o_ss = _two_part_attention([
        (qgs[i], buf_ref[slot, q, 2 * LANES:3 * LANES, :].astype(BF16),
         buf_ref[slot, q, 3 * LANES:4 * LANES, :].astype(BF16),
         _tile_rows(jnp.where(allow_ps[i], 0.0, NEG), NSA_GRP),
         zkv(q, 2 * LANES, 3 * LANES).astype(BF16), zkv(q, 3 * LANES, 4 * LANES).astype(BF16),
         _tile_rows(jnp.where(allow_ns[i], 0.0, NEG), NSA_GRP)) for i, (q, g) in enumerate(probs)])
    o_ws = _two_part_attention([
        (qgs[i], cwin_ref[0, q, 0:LANES, :].astype(BF16), cwin_ref[0, q, LANES:2 * LANES, :].astype(BF16), bias_wp,
         zkv(q, ZKW - ZKV, ZKW - ZKV + LANES).astype(BF16),
         zkv(q, ZKW - ZKV + LANES, ZKW - ZKV + 2 * LANES).astype(BF16), bias_wn)
        for i, (q, g) in enumerate(probs)])
    outs = [_gate_cols(gates[q], g, 0, r) * o_cs[i] + _gate_cols(gates[q], g, 1, r) * o_ss[i]
            + _gate_cols(gates[q], g, 2, r) * o_ws[i] for i, (q, g) in enumerate(probs)]
    for q in seqs:
        out_ref[rows(q), :] = _nsa_assemble(outs[NSA_KV * q], outs[NSA_KV * q + 1], r)


def nsa_sample(z, page_table, cache_t, cache_win_t, cw, e_mat, ov, layer, row0, dbsz, past, dec):
    n_pages = past // PAGE_SIZE
    nseq = NSA_SEQ_PER_STEP
    rows = nseq * SEQ_PAD
    rb0 = row0 // rows
    wk = cache_win_t.shape[3]
    c2 = lambda b, pt: (0, 0)
    c3 = lambda b, pt: (0, 0, 0)
    grid_spec = pltpu.PrefetchScalarGridSpec(
        num_scalar_prefetch=1,
        grid=(dbsz // nseq,),
        in_specs=[pl.BlockSpec((rows, NSA_HEADS * LANES), lambda b, pt: (rb0 + b, 0)),
                  pl.BlockSpec((rows, 1024), lambda b, pt: (rb0 + b, ZKV // 1024)),
                  pl.BlockSpec((1, nseq, 2 * LANES, wk), lambda b, pt: (layer, b, 0, 0)),
                  pl.BlockSpec(memory_space=pl.ANY),
                  pl.BlockSpec((CMP_STRIDE, 256), c2),
                  pl.BlockSpec((CMP_STRIDE, 256), c2),
                  pl.BlockSpec((CMP_STRIDE, 256, 256), c3),
                  pl.BlockSpec((CMP_STRIDE, 256, 256), c3),
                  pl.BlockSpec((256, 256), c2),
                  pl.BlockSpec((LANES, past), c2),
                  pl.BlockSpec((past // CMP_STRIDE, LANES), c2)],
        out_specs=pl.BlockSpec((rows, MIX_WIDTH), lambda b, pt: (b, 0)),
        scratch_shapes=[pltpu.VMEM((2, nseq, 4 * LANES, past), F32), pltpu.VMEM((nseq, 2, past, LANES), F32),
                        pltpu.SemaphoreType.DMA((2,))],
    )
    return pl.pallas_call(
        functools.partial(_nsa_sample_kernel, layer=layer, past=past, n_pages=n_pages, dec=dec),
        grid_spec=grid_spec,
        out_shape=jax.ShapeDtypeStruct((dbsz * SEQ_PAD, MIX_WIDTH), F32),
        compiler_params=_cparams(("arbitrary",)),
        name="nsa_sample",
    )(page_table.reshape(-1), z, z, cache_win_t, cache_t, cw["pea"], cw["peb"], cw["w1a"], cw["w1b"], cw["w2"],
      e_mat, ov)


def _seg_masks(r, c):
    ri = lax.broadcasted_iota(jnp.int32, (r, r), 0)
    ci = lax.broadcasted_iota(jnp.int32, (r, r), 1)
    same = (ri // c) == (ci // c)
    return same, same & (ri >= ci), same & (ri > ci)


def _gdn_prep(y, ab, par_ref, valid, c, u_ref, w_ref, qe_ref, qk_ref, kdt_ref, eg_ref):
    r = y.shape[0]
    same, incl, strict = _seg_masks(r, c)
    y = _silu(y)
    beta_all = _sigmoid(ab)
    g_all = -jnp.exp(par_ref[0:1, :]) * _softplus(ab + par_ref[1:2, :])
    if valid is not None:
        y = jnp.where(valid, y, 0.0)
        beta_all = jnp.where(valid, beta_all, 0.0)
        g_all = jnp.where(valid, g_all, 0.0)
    gcum_all = _dot_mask_left(jnp.where(incl, 1.0, 0.0), g_all)
    glast_all = _dot_mask_left(jnp.where(same, 1.0, 0.0), g_all)
    gcum_t = jnp.transpose(gcum_all)
    eye = jnp.where(lax.broadcasted_iota(jnp.int32, (r, r), 0) == lax.broadcasted_iota(jnp.int32, (r, r), 1),
                    1.0, 0.0)
    n_double = int(math.log2(c))
    tms, pws, rhss = [], [], []
    for h in range(GDN_HEADS):
        q = y[:, h * GDN_HD:(h + 1) * GDN_HD]
        k = y[:, MIX_WIDTH + h * GDN_HD:MIX_WIDTH + (h + 1) * GDN_HD]
        v = y[:, 2 * MIX_WIDTH + h * GDN_HD:2 * MIX_WIDTH + (h + 1) * GDN_HD]
        q = q * lax.rsqrt(jnp.sum(q * q, axis=-1, keepdims=True) + EPS) * (GDN_HD ** -0.5)
        k = k * lax.rsqrt(jnp.sum(k * k, axis=-1, keepdims=True) + EPS)
        g1 = jnp.broadcast_to(gcum_all[:, h:h + 1], (r, GDN_HD))
        g2 = jnp.broadcast_to(gcum_t[h:h + 1, :], (r, r))
        gl = jnp.broadcast_to(glast_all[:, h:h + 1], (r, GDN_HD))
        beta = jnp.broadcast_to(beta_all[:, GDN_HEADS + h:GDN_HEADS + h + 1], (r, GDN_HD))
        g1r = g1 if r == GDN_HD else jnp.broadcast_to(gcum_all[:, h:h + 1], (r, r))
        decay = jnp.where(incl, jnp.exp(jnp.where(incl, g1r - g2, 0.0)), 0.0)
        betar = beta if r == GDN_HD else jnp.broadcast_to(beta_all[:, GDN_HEADS + h:GDN_HEADS + h + 1], (r, r))
        a = jnp.where(strict, _bdot_nt(k, k) * decay * betar, 0.0)
        tms.append(eye - a)
        pws.append(a)
        eg1 = jnp.exp(g1)
        rhss.append(jnp.concatenate([v * beta, k * (beta * eg1)], axis=1))
        qe_ref[h] = q * eg1
        qk_ref[h] = jnp.where(incl, _bdot_nt(q, k) * decay, 0.0)
        kdt_ref[h] = jnp.transpose(k * jnp.exp(gl - g1))
        eg_ref[h] = jnp.exp(gl)
    for _ in range(n_double - 1):
        pws = [_dot3(pw, pw) for pw in pws]
        tms = [tm + _dot3(tm, pw) for tm, pw in zip(tms, pws)]
    for h in range(GDN_HEADS):
        sol = _dot3(tms[h], rhss[h])
        u_ref[h] = sol[:, 0:GDN_HD]
        w_ref[h] = sol[:, GDN_HD:2 * GDN_HD]


def _gdn_segment(h, row0, seg, c, s, u_ref, w_ref, qe_ref, qk_ref, kdt_ref, eg_ref, vn_ref, col_seg):
    rows = pl.ds(row0, c)
    vn = u_ref[h, rows, :] - _bdot(w_ref[h, rows, :], s)
    vn_ref[h, rows, :] = vn
    vn_all = vn_ref[h].astype(BF16)
    o = _bdot(qe_ref[h, rows, :], s) + _dot(qk_ref[h, rows, :].astype(BF16), vn_all)
    kdt = jnp.where(col_seg == seg, kdt_ref[h], 0.0)
    s_new = s * eg_ref[h, pl.ds(row0, 1), :] + _dot(kdt.astype(BF16), vn_all)
    return o, s_new


def _gdn_finish(o, zg, ng_ref):
    return _rms(o, ng_ref[...]) * _silu(zg)


GDN_TILE = 128


def _gdn_prompt_kernel(x_ref, ab_ref, zg_ref, cw_ref, par_ref, ng_ref, out_ref, st_ref,
                       halo_ref, u_ref, w_ref, qe_ref, qk_ref, kdt_ref, eg_ref, vn_ref):
    i = pl.program_id(1)
    r = GDN_TILE
    c = GDN_CHUNK

    @pl.when(i == 0)
    def _():
        halo_ref[...] = jnp.zeros_like(halo_ref)
        st_ref[...] = jnp.zeros_like(st_ref)

    x = x_ref[...]
    xx = jnp.concatenate([halo_ref[...], x], axis=0)
    halo_ref[...] = x[r - SUBLANES:r, :]
    y = None
    for j in range(CONV_W):
        sh = CONV_W - 1 - j
        xs = xx if sh == 0 else pltpu.roll(xx, sh, axis=0)
        t = xs[SUBLANES:, :] * cw_ref[j:j + 1, :]
        y = t if y is None else y + t
    _gdn_prep(y, ab_ref[...], par_ref, None, c, u_ref, w_ref, qe_ref, qk_ref, kdt_ref, eg_ref)
    vn_ref[...] = jnp.zeros_like(vn_ref)
    col_seg = lax.broadcasted_iota(jnp.int32, (GDN_HD, r), 1) // c
    states = [st_ref[0, h] for h in range(GDN_HEADS)]
    os_ = [[] for _ in range(GDN_HEADS)]
    for ck in range(r // c):
        for h in range(GDN_HEADS):
            o, states[h] = _gdn_segment(h, ck * c, ck, c, states[h], u_ref, w_ref, qe_ref, qk_ref, kdt_ref,
                                        eg_ref, vn_ref, col_seg)
            os_[h].append(o)
    for h in range(GDN_HEADS):
        st_ref[0, h] = states[h]
        o = jnp.concatenate(os_[h], axis=0)
        out_ref[:, h * GDN_HD:(h + 1) * GDN_HD] = _gdn_finish(o, zg_ref[:, h * GDN_HD:(h + 1) * GDN_HD], ng_ref)


def _gdn_scratch(r):
    hs = (GDN_HEADS, r, GDN_HD)
    return [pltpu.VMEM(hs, F32), pltpu.VMEM(hs, F32), pltpu.VMEM(hs, F32), pltpu.VMEM((GDN_HEADS, r, r), F32),
            pltpu.VMEM((GDN_HEADS, GDN_HD, r), F32), pltpu.VMEM(hs, F32), pltpu.VMEM(hs, F32)]


def gdn_prompt(z, conv_w, par, norm_g, bsz, t):
    nt = t // GDN_TILE
    r = GDN_TILE
    c2 = lambda b, i: (0, 0)
    return pl.pallas_call(
        _gdn_prompt_kernel,
        grid=(bsz, nt),
        in_specs=[pl.BlockSpec((r, 3 * MIX_WIDTH), lambda b, i: (b * nt + i, ZGQ // (3 * MIX_WIDTH))),
                  pl.BlockSpec((r, LANES), lambda b, i: (b * nt + i, ZGAB // LANES)),
                  pl.BlockSpec((r, MIX_WIDTH), lambda b, i: (b * nt + i, ZGZ // MIX_WIDTH)),
                  pl.BlockSpec((CONV_W, 3 * MIX_WIDTH), c2),
                  pl.BlockSpec((SUBLANES, LANES), c2),
                  pl.BlockSpec((1, GDN_HD), c2)],
        out_specs=[pl.BlockSpec((r, MIX_WIDTH), lambda b, i: (b * nt + i, 0)),
                   pl.BlockSpec((1, GDN_HEADS, GDN_HD, GDN_HD), lambda b, i: (b, 0, 0, 0))],
        out_shape=[jax.ShapeDtypeStruct((bsz * t, MIX_WIDTH), F32),
                   jax.ShapeDtypeStruct((bsz, GDN_HEADS, GDN_HD, GDN_HD), F32)],
        scratch_shapes=[pltpu.VMEM((SUBLANES, 3 * MIX_WIDTH), F32)] + _gdn_scratch(r),
        compiler_params=_cparams(("parallel", "arbitrary")),
        name="gdn_prompt",
    )(z, z, z, conv_w, par, norm_g)


SEQ_PER_STEP = 16


def _gdn_sample_kernel(x_ref, ab_ref, zg_ref, prev_ref, s0_ref, cw_ref, par_ref, ng_ref, out_ref, st_ref,
                       u_ref, w_ref, qe_ref, qk_ref, kdt_ref, eg_ref, vn_ref, o_ref, *, dec):
    r = SEQ_PER_STEP * SEQ_PAD
    c = SEQ_PAD
    row = lax.broadcasted_iota(jnp.int32, (r, 1), 0)
    tin = row % c
    valid = tin < dec
    xx = jnp.where(tin < CONV_W - 1, prev_ref[...], pltpu.roll(x_ref[...], CONV_W - 1, axis=0))
    y = None
    for j in range(CONV_W):
        xs = xx if j == 0 else pltpu.roll(xx, r - j, axis=0)
        t = xs * cw_ref[j:j + 1, :]
        y = t if y is None else y + t
    _gdn_prep(y, ab_ref[...], par_ref, valid, c, u_ref, w_ref, qe_ref, qk_ref, kdt_ref, eg_ref)
    vn_ref[...] = jnp.zeros_like(vn_ref)
    col_seg = lax.broadcasted_iota(jnp.int32, (GDN_HD, r), 1) // c

    def body(sq, carry):
        row0 = pl.multiple_of(sq * c, c)
        for h in range(GDN_HEADS):
            o, s = _gdn_segment(h, row0, sq, c, s0_ref[0, sq, h], u_ref, w_ref, qe_ref, qk_ref, kdt_ref, eg_ref,
                                vn_ref, col_seg)
            st_ref[sq, h] = s
            o_ref[h, pl.ds(row0, c), :] = o
        return carry

    lax.fori_loop(0, SEQ_PER_STEP, body, 0, unroll=4)
    for h in range(GDN_HEADS):
        out_ref[:, h * GDN_HD:(h + 1) * GDN_HD] = _gdn_finish(o_ref[h], zg_ref[:, h * GDN_HD:(h + 1) * GDN_HD],
                                                              ng_ref)


def gdn_sample(z, prev_pad, s0_all, layer, conv_w, par, norm_g, row0, dbsz, dec):
    r = SEQ_PER_STEP * SEQ_PAD
    rb0 = row0 // r
    c2 = lambda i: (0, 0)
    return pl.pallas_call(
        functools.partial(_gdn_sample_kernel, dec=dec),
        grid=(dbsz // SEQ_PER_STEP,),
        in_specs=[pl.BlockSpec((r, 3 * MIX_WIDTH), lambda i: (rb0 + i, ZGQ // (3 * MIX_WIDTH))),
                  pl.BlockSpec((r, LANES), lambda i: (rb0 + i, ZGAB // LANES)),
                  pl.BlockSpec((r, MIX_WIDTH), lambda i: (rb0 + i, ZGZ // MIX_WIDTH)),
                  pl.BlockSpec((r, 3 * MIX_WIDTH), lambda i: (i, 0)),
                  pl.BlockSpec((1, SEQ_PER_STEP, GDN_HEADS, GDN_HD, GDN_HD), lambda i: (layer, i, 0, 0, 0)),
                  pl.BlockSpec((CONV_W, 3 * MIX_WIDTH), c2),
                  pl.BlockSpec((SUBLANES, LANES), c2),
                  pl.BlockSpec((1, GDN_HD), c2)],
        out_specs=[pl.BlockSpec((r, MIX_WIDTH), lambda i: (i, 0)),
                   pl.BlockSpec((SEQ_PER_STEP, GDN_HEADS, GDN_HD, GDN_HD), lambda i: (i, 0, 0, 0))],
        out_shape=[jax.ShapeDtypeStruct((dbsz * SEQ_PAD, MIX_WIDTH), F32),
                   jax.ShapeDtypeStruct((dbsz, GDN_HEADS, GDN_HD, GDN_HD), F32)],
        scratch_shapes=_gdn_scratch(r) + [pltpu.VMEM((GDN_HEADS, r, GDN_HD), F32)],
        compiler_params=_cparams(("parallel",)),
        name="gdn_sample",
    )(z, z, z, prev_pad, s0_all, conv_w, par, norm_g)


def _ret_log_gamma(h):
    return math.log1p(-(2.0 ** (-5.0 - h)))


def _ret_prep(x, cs, sn, valid, h):
    qh = x[:, h * RET_HD:(h + 1) * RET_HD]
    kh = x[:, MIX_WIDTH + h * RET_HD:MIX_WIDTH + (h + 1) * RET_HD]
    vh = x[:, 2 * MIX_WIDTH + h * RET_HD:2 * MIX_WIDTH + (h + 1) * RET_HD]
    qr = qh * cs + pltpu.roll(qh, RET_HD // 2, axis=1) * sn
    kr = (kh * cs + pltpu.roll(kh, RET_HD // 2, axis=1) * sn) * (RET_HD ** -0.5)
    if valid is not None:
        kr = jnp.where(valid, kr, 0.0)
        vh = jnp.where(valid, vh, 0.0)
    return qr, kr, vh


def _ret_finish(o, gate):
    o = o * lax.rsqrt(jnp.mean(o * o, axis=-1, keepdims=True) + EPS)
    return o * _silu(gate)


def _ret_prompt_kernel(x_ref, gt_ref, cs_ref, sn_ref, out_ref, st_ref):
    i = pl.program_id(1)
    c = RET_CHUNK

    @pl.when(i == 0)
    def _():
        st_ref[...] = jnp.zeros_like(st_ref)

    x = x_ref[...]
    cs = cs_ref[...]
    sn = sn_ref[...]
    ri = lax.broadcasted_iota(jnp.int32, (c, c), 0)
    ci = lax.broadcasted_iota(jnp.int32, (c, c), 1)
    diff = (ri - ci).astype(F32)
    n = lax.broadcasted_iota(jnp.int32, (c, 1), 0).astype(F32)
    heads = range(RET_HEADS)
    lgs = [_ret_log_gamma(h) for h in heads]
    qkv = [_ret_prep(x, cs, sn, None, h) for h in heads]
    states = [st_ref[0, h] for h in heads]
    scores = [_bdot_nt(qkv[h][0], qkv[h][1]) * jnp.where(diff >= 0.0, jnp.exp(jnp.maximum(diff, 0.0) * lgs[h]), 0.0)
              for h in heads]
    cross = [_bdot(qkv[h][0], states[h]) * jnp.exp((n + 1.0) * lgs[h]) for h in heads]
    outs = [_bdot(scores[h], qkv[h][2]) + cross[h] for h in heads]
    kdts = [jnp.transpose(qkv[h][1] * jnp.exp((c - 1.0 - n) * lgs[h])).astype(BF16) for h in heads]
    for h in heads:
        st_ref[0, h] = states[h] * math.exp(c * lgs[h]) + _dot(kdts[h], qkv[h][2].astype(BF16))
        out_ref[:, h * RET_HD:(h + 1) * RET_HD] = _ret_finish(outs[h], gt_ref[:, h * RET_HD:(h + 1) * RET_HD])


def ret_prompt(z, cs, sn, bsz, t):
    c = RET_CHUNK
    nt = t // c
    return pl.pallas_call(
        _ret_prompt_kernel,
        grid=(bsz, nt),
        in_specs=[pl.BlockSpec((c, 3 * MIX_WIDTH), lambda b, i: (b * nt + i, ZRQ // (3 * MIX_WIDTH))),
                  pl.BlockSpec((c, MIX_WIDTH), lambda b, i: (b * nt + i, ZRG // MIX_WIDTH)),
                  pl.BlockSpec((c, RET_HD), lambda b, i: (i, 0)),
                  pl.BlockSpec((c, RET_HD), lambda b, i: (i, 0))],
        out_specs=[pl.BlockSpec((c, MIX_WIDTH), lambda b, i: (b * nt + i, 0)),
                   pl.BlockSpec((1, RET_HEADS, RET_HD, RET_HD), lambda b, i: (b, 0, 0, 0))],
        out_shape=[jax.ShapeDtypeStruct((bsz * t, MIX_WIDTH), F32),
                   jax.ShapeDtypeStruct((bsz, RET_HEADS, RET_HD, RET_HD), F32)],
        compiler_params=_cparams(("parallel", "arbitrary")),
        name="ret_prompt",
    )(z, z, cs, sn)


def _ret_sample_kernel(x_ref, gt_ref, cs_ref, sn_ref, s0_ref, out_ref, st_ref, q_ref, kdt_ref, v_ref, o_ref,
                       *, dec):
    r = SEQ_PER_STEP * SEQ_PAD
    c = SEQ_PAD
    row = lax.broadcasted_iota(jnp.int32, (r, 1), 0)
    tin = row % c
    valid = tin < dec
    n = tin.astype(F32)
    x = x_ref[...]
    cs = cs_ref[...]
    sn = sn_ref[...]
    same, incl, _ = _seg_masks(r, c)
    ri = lax.broadcasted_iota(jnp.int32, (r, r), 0)
    ci = lax.broadcasted_iota(jnp.int32, (r, r), 1)
    diff = (ri - ci).astype(F32)
    for h in range(RET_HEADS):
        lg = _ret_log_gamma(h)
        qr, kr, vh = _ret_prep(x, cs, sn, valid, h)
        dmat = jnp.where(incl, jnp.exp(jnp.maximum(diff, 0.0) * lg), 0.0)
        o_ref[h] = _bdot(_bdot_nt(qr, kr) * dmat, vh)
        q_ref[h] = qr
        kdt_ref[h] = jnp.transpose(kr * jnp.exp((dec - 1.0 - n) * lg))
        v_ref[h] = vh
    col_seg = lax.broadcasted_iota(jnp.int32, (RET_HD, r), 1) // c
    qdec = [jnp.exp((lax.broadcasted_iota(jnp.int32, (c, 1), 0).astype(F32) + 1.0) * _ret_log_gamma(h))
            for h in range(RET_HEADS)]

    def body(sq, carry):
        row0 = pl.multiple_of(sq * c, c)
        rows = pl.ds(row0, c)
        for h in range(RET_HEADS):
            s = s0_ref[0, sq, h]
            o_ref[h, rows, :] = o_ref[h, rows, :] + _bdot(q_ref[h, rows, :], s) * qdec[h]
            kdt = jnp.where(col_seg == sq, kdt_ref[h], 0.0)
            st_ref[sq, h] = s * math.exp(dec * _ret_log_gamma(h)) + _dot(kdt.astype(BF16), v_ref[h].astype(BF16))
        return carry

    lax.fori_loop(0, SEQ_PER_STEP, body, 0, unroll=4)
    for h in range(RET_HEADS):
        out_ref[:, h * RET_HD:(h + 1) * RET_HD] = _ret_finish(o_ref[h], gt_ref[:, h * RET_HD:(h + 1) * RET_HD])


def ret_sample(z, cs, sn, s0_all, layer, row0, dbsz, dec):
    r = SEQ_PER_STEP * SEQ_PAD
    rb0 = row0 // r
    hs = (RET_HEADS, r, RET_HD)
    return pl.pallas_call(
        functools.partial(_ret_sample_kernel, dec=dec),
        grid=(dbsz // SEQ_PER_STEP,),
        in_specs=[pl.BlockSpec((r, 3 * MIX_WIDTH), lambda i: (rb0 + i, ZRQ // (3 * MIX_WIDTH))),
                  pl.BlockSpec((r, MIX_WIDTH), lambda i: (rb0 + i, ZRG // MIX_WIDTH)),
                  pl.BlockSpec((r, RET_HD), lambda i: (0, 0)),
                  pl.BlockSpec((r, RET_HD), lambda i: (0, 0)),
                  pl.BlockSpec((1, SEQ_PER_STEP, RET_HEADS, RET_HD, RET_HD), lambda i: (layer, i, 0, 0, 0))],
        out_specs=[pl.BlockSpec((r, MIX_WIDTH), lambda i: (i, 0)),
                   pl.BlockSpec((SEQ_PER_STEP, RET_HEADS, RET_HD, RET_HD), lambda i: (i, 0, 0, 0))],
        out_shape=[jax.ShapeDtypeStruct((dbsz * SEQ_PAD, MIX_WIDTH), F32),
                   jax.ShapeDtypeStruct((dbsz, RET_HEADS, RET_HD, RET_HD), F32)],
        scratch_shapes=[pltpu.VMEM(hs, F32), pltpu.VMEM((RET_HEADS, RET_HD, r), F32), pltpu.VMEM(hs, F32),
                        pltpu.VMEM(hs, F32)],
        compiler_params=_cparams(("parallel",)),
        name="ret_sample",
    )(z, z, cs, sn, s0_all)


def _prep_w_in(w):
    sizes = (NSA_HEADS * NSA_HD, 6 * NSA_KV * NSA_HD, 3 * NSA_HEADS, 3 * MIX_WIDTH, GDN_HEADS, GDN_HEADS,
             MIX_WIDTH, 3 * MIX_WIDTH, MIX_WIDTH, N_BRANCH * D_MODEL)
    src = np.concatenate([[0], np.cumsum(sizes)])
    s_q, s_kv, s_ng, s_gq, s_ga, s_gb, s_gz, s_rq, s_rg, s_mg = src[:-1].tolist()
    wb = w.astype(BF16)
    out = jnp.zeros((w.shape[0], ZW), BF16)

    def put(o, dst, s0, n):
        return lax.dynamic_update_slice(o, lax.slice_in_dim(wb, s0, s0 + n, axis=1), (0, dst))

    for h in range(NSA_HEADS):
        out = put(out, ZQ + h * LANES + NSA_HD * (h // NSA_GRP), s_q + h * NSA_HD, NSA_HD)
    out = put(out, ZKV, s_kv, sizes[1] + sizes[2])
    out = put(out, ZGAB, s_ga, 2 * GDN_HEADS)
    out = put(out, ZGZ, s_gz, MIX_WIDTH)
    out = put(out, ZRG, s_rg, MIX_WIDTH)
    out = put(out, ZGQ, s_gq, 3 * MIX_WIDTH)
    out = put(out, ZRQ, s_rq, 3 * MIX_WIDTH)
    out = put(out, ZMG, s_mg, N_BRANCH * D_MODEL)
    return out


def _prep_w_branch(wb):
    w0 = wb[0].reshape(NSA_HEADS, NSA_HD, D_MODEL)
    order = [h for j in range(NSA_GRP) for h in (j, j + NSA_GRP)]
    w0 = w0[np.array(order)].reshape(MIX_WIDTH, D_MODEL)
    return jnp.stack([w0, wb[1], wb[2]]).astype(BF16)


def _prep_compress(pe, w1, w2):
    sel_k = jnp.asarray(np.diag([1.0, 1.0, 0.0, 0.0]).astype(np.float32))
    sel_v = jnp.asarray(np.diag([0.0, 0.0, 1.0, 1.0]).astype(np.float32))

    def blockdiag(mk, mv):
        full = (sel_k[:, None, :, None] * mk[..., None, :, None, :]
                + sel_v[:, None, :, None] * mv[..., None, :, None, :])
        return full.reshape(mk.shape[:-2] + (4 * mk.shape[-2], 4 * mk.shape[-1]))

    w1r = w1.reshape(2, CMP_LEN, NSA_HD, CMP_HIDDEN)
    w1a = blockdiag(w1r[0, :CMP_STRIDE], w1r[1, :CMP_STRIDE])
    w1b = blockdiag(w1r[0, CMP_STRIDE:], w1r[1, CMP_STRIDE:])
    pea = jnp.concatenate([pe[0, :CMP_STRIDE], pe[0, :CMP_STRIDE], pe[1, :CMP_STRIDE], pe[1, :CMP_STRIDE]], axis=1)
    peb = jnp.concatenate([pe[0, CMP_STRIDE:], pe[0, CMP_STRIDE:], pe[1, CMP_STRIDE:], pe[1, CMP_STRIDE:]], axis=1)
    w2b = blockdiag(w2[0], w2[1])
    return dict(pea=pea, peb=peb, w1a=w1a.astype(BF16), w1b=w1b.astype(BF16), w2=w2b.astype(BF16))


def _overlap_matrix(nc_rows, nc, ns):
    n = np.arange(nc_rows)[:, None]
    s = np.arange(LANES)[None, :]
    c_start = n * CMP_STRIDE
    ov = (c_start < (s + 1) * SEL_LEN) & (s * SEL_LEN < c_start + CMP_LEN) & (n < nc) & (s < ns)
    return jnp.asarray(ov.astype(np.float32))


def _expand_matrix(nkeys):
    s = np.arange(LANES)[:, None]
    k = np.arange(nkeys)[None, :]
    return (k // SEL_LEN == s).astype(np.float32)


def _rope_tables(pos):
    half = RET_HD // 2
    inv = ROPE_BASE ** (-jnp.linspace(0.0, 1.0, half, dtype=F32))
    ang = pos.astype(F32)[:, None] * inv[None, :]
    cos, sin = jnp.cos(ang), jnp.sin(ang)
    return jnp.concatenate([cos, cos], axis=1), jnp.concatenate([-sin, sin], axis=1)


def _pad_rows(a, n):
    return jnp.pad(a, ((0, 0), (0, n - a.shape[1])) + ((0, 0),) * (a.ndim - 2))


def kernel(x_prompt, x_sample, cache_nsa_kv, cache_nsa_win, state_gdn_conv, state_gdn, state_ret, page_table,
           p_prompt, p_sample, g_mix, w_in, nsa_cmp_pe, nsa_cmp_w1, nsa_cmp_w2, gdn_conv_w, gdn_a_log,
           gdn_dt_bias, gdn_norm_g, w_branch, w_out, g_mlp, w_up, w_down, w_ple, w_ple_gate, g_final):
    bsz, t, d = x_prompt.shape
    dbsz, dec, _ = x_sample.shape
    depth = w_in.shape[0]
    past = page_table.shape[1] * PAGE_SIZE
    n_p = bsz * t
    n_s = dbsz * SEQ_PAD
    assert dec <= SEQ_PAD and dec >= CONV_W - 1 and (past % SEL_LEN) + dec <= SEL_LEN
    assert t % SEL_TILE == 0 and t >= WINDOW + Q_TILE and past % PAGE_SIZE == 0 and dbsz % SEQ_PER_STEP == 0

    x = jnp.concatenate([x_prompt.reshape(n_p, d), _pad_rows(x_sample, SEQ_PAD).reshape(n_s, d)], axis=0)
    p_all = jnp.concatenate([p_prompt.reshape(depth, n_p, PLE_DIM),
                             jnp.pad(p_sample, ((0, 0), (0, 0), (0, SEQ_PAD - dec), (0, 0))).reshape(depth, n_s, PLE_DIM)],
                            axis=1)

    nc_p = t // CMP_STRIDE - 1
    ov_p = _overlap_matrix(t // CMP_STRIDE, nc_p, t // SEL_LEN)
    shift_p = _shift_matrices(t)
    nc_s = past // CMP_STRIDE - 1
    ov_s = _overlap_matrix(past // CMP_STRIDE, nc_s, -(-(past + dec) // SEL_LEN))
    e_s = jnp.asarray(_expand_matrix(past)).astype(BF16)
    n_phys = cache_nsa_kv.shape[1]
    wk = cache_nsa_win.shape[2]
    cache_t = jnp.transpose(cache_nsa_kv, (0, 1, 3, 4, 5, 2)).reshape(depth, n_phys, 4 * LANES, PAGE_SIZE)
    cwin_t = jnp.transpose(cache_nsa_win, (0, 1, 3, 4, 5, 2)).reshape(depth, dbsz, 2 * LANES, wk)
    cs_p, sn_p = _rope_tables(jnp.arange(t))
    cs_s, sn_s = _rope_tables(past + jnp.arange(SEQ_PAD))
    cs_s = jnp.tile(cs_s, (SEQ_PER_STEP, 1))
    sn_s = jnp.tile(sn_s, (SEQ_PER_STEP, 1))

    outs = [[] for _ in range(10)]
    for i in range(depth):
        w_in_p = _prep_w_in(w_in[i])
        wb = _prep_w_branch(w_branch[i])
        cw = _prep_compress(nsa_cmp_pe[i], nsa_cmp_w1[i], nsa_cmp_w2[i])
        par = jnp.zeros((SUBLANES, LANES), F32).at[0, :GDN_HEADS].set(gdn_a_log[i]).at[1, :GDN_HEADS].set(gdn_dt_bias[i])
        ng = gdn_norm_g[i].reshape(1, GDN_HD)

        z = in_proj(x, g_mix[i].reshape(1, d), w_in_p)
        kv_s = lax.slice(z, (n_p, ZKV), (n_p + n_s, ZNG)).reshape(dbsz, SEQ_PAD, ZNG - ZKV)[:, :dec]

        kcmp_sub = lax.slice(z, (0, ZKV), (n_p, ZKV + 2 * LANES)).reshape(n_p // CMP_STRIDE,
                                                                           CMP_STRIDE * 2 * LANES)
        kcvc = compress_prompt(kcmp_sub, cw, bsz, t)
        o_nsa_p = nsa_prompt(z, kcvc, prompt_kv_slab(z, n_p, t), shift_p, ov_p, bsz, t)
        o_nsa_s = nsa_sample(z, page_table, cache_t, cwin_t, cw, e_s, ov_s, i, n_p, dbsz, past, dec)

        o_gdn_p, gs_p = gdn_prompt(z, gdn_conv_w[i], par, ng, bsz, t)
        prev_pad = _pad_rows(state_gdn_conv[i], SEQ_PAD).reshape(n_s, 3 * MIX_WIDTH)
        o_gdn_s, gs_s = gdn_sample(z, prev_pad, state_gdn, i, gdn_conv_w[i], par, ng, n_p, dbsz, dec)

        o_ret_p, rs_p = ret_prompt(z, cs_p, sn_p, bsz, t)
        o_ret_s, rs_s = ret_sample(z, cs_s, sn_s, state_ret, i, n_p, dbsz, dec)

        x = merge(x, (o_nsa_p, o_gdn_p, o_ret_p), (o_nsa_s, o_gdn_s, o_ret_s), z, wb, w_out[i].astype(BF16))
        x = mlp_ple(x, g_mlp[i].reshape(1, d), w_up[i].astype(BF16), w_down[i].astype(BF16), p_all[i],
                    w_ple[i].astype(BF16), w_ple_gate[i].astype(BF16), g_final.reshape(1, d),
                    final_norm=(i == depth - 1))

        nkv = ZKW - ZKV
        outs[0].append(lax.slice(z, (0, ZKV), (n_p, ZKW)).reshape(bsz, t, 4, NSA_KV, NSA_HD))
        outs[1].append(kv_s[:, :, :nkv].reshape(dbsz, dec, 4, NSA_KV, NSA_HD))
        wlen = min(WINDOW, t)
        outs[2].append(jnp.stack([lax.slice(z, (b * t + t - wlen, ZKW), (b * t + t, ZNG)) for b in range(bsz)])
                       .reshape(bsz, wlen, 2, NSA_KV, NSA_HD))
        win = jnp.concatenate([cache_nsa_win[i], kv_s[:, :, nkv:].reshape(dbsz, dec, 2, NSA_KV, NSA_HD)], axis=1)
        wlen_s = min(WINDOW, past + dec)
        outs[3].append(win[:, win.shape[1] - wlen_s:])
        nconv = CONV_W - 1
        outs[4].append(jnp.stack([lax.slice(z, (b * t + t - nconv, ZGQ), (b * t + t, ZGQ + 3 * MIX_WIDTH))
                                  for b in range(bsz)]))
        gq_s = lax.slice(z, (n_p, ZGQ), (n_p + n_s, ZGQ + 3 * MIX_WIDTH)).reshape(dbsz, SEQ_PAD, 3 * MIX_WIDTH)
        outs[5].append(gq_s[:, dec - nconv:dec])
        outs[6].append(gs_p)
        outs[7].append(gs_s)
        outs[8].append(rs_p)
        outs[9].append(rs_s)

    y_prompt = x[:n_p].reshape(bsz, t, d)
    y_sample = x[n_p:].reshape(dbsz, SEQ_PAD, d)[:, :dec]
    return (y_prompt, y_sample) + tuple(jnp.stack(o) for o in outs)
```

```python
import functools
import math

import numpy as np
import jax
import jax.numpy as jnp
from jax import lax
from jax.experimental import pallas as pl
from jax.experimental.pallas import tpu as pltpu

F32 = jnp.float32
BF16 = jnp.bfloat16

D_MODEL = 1024
MIX_WIDTH = D_MODEL // 2
N_BRANCH = 3
NSA_HEADS = 8
NSA_HD = 64
NSA_KV = 2
NSA_GRP = NSA_HEADS // NSA_KV
CMP_LEN = 32
CMP_STRIDE = 16
CMP_HIDDEN = 64
SEL_LEN = 64
SEL_TOPK = 16
WINDOW = 512
FORCE_BONUS = 1000.0
GDN_HEADS = 4
GDN_HD = 128
CONV_W = 4
GDN_CHUNK = 64
RET_HEADS = 4
RET_HD = 128
RET_CHUNK = 128
ROPE_BASE = 10000.0
D_FF = 4 * D_MODEL
PLE_DIM = 256
EPS = 1e-6
NEG = -1e30
PAGE_SIZE = 128

LANES = 128
SUBLANES = 8
SEQ_PAD = SUBLANES
VMEM_LIMIT = 56 * 1024 * 1024

ZQ = 0
ZKV = 1024
ZKW = 1536
ZNG = 1792
ZGAB = 1920
ZGZ = 2048
ZRG = 2560
ZGQ = 3072
ZRQ = 4608
ZMG = 6144
ZW = 9216


def _sigmoid(x):
    return 1.0 / (1.0 + jnp.exp(-x))


def _silu(x):
    return x * _sigmoid(x)


def _softplus(x):
    return jnp.maximum(x, 0.0) + jnp.log1p(jnp.exp(-jnp.abs(x)))


def _gelu_tanh(x):
    return 0.5 * x * (1.0 + jnp.tanh(math.sqrt(2.0 / math.pi) * (x + 0.044715 * (x * x * x))))


def _rms(x, g):
    return x * lax.rsqrt(jnp.mean(x * x, axis=-1, keepdims=True) + EPS) * g


def _dot(a, b):
    return jnp.dot(a, b, preferred_element_type=F32)


def _dot_nt(a, b):
    return lax.dot_general(a, b, (((1,), (1,)), ((), ())), preferred_element_type=F32)


def _bdot(a, b):
    return _dot(a.astype(BF16), b.astype(BF16))


def _bdot_nt(a, b):
    return _dot_nt(a.astype(BF16), b.astype(BF16))


def _split2(a):
    hi = a.astype(BF16)
    lo = (a - hi.astype(F32)).astype(BF16)
    return hi, lo


def _dot3(a, b):
    ah, al = _split2(a)
    bh, bl = _split2(b)
    return _dot(ah, bh) + (_dot(ah, bl) + _dot(al, bh))


def _dot_mask_left(m01, x):
    m = m01.astype(BF16)
    x1 = x.astype(BF16)
    r1 = x - x1.astype(F32)
    x2 = r1.astype(BF16)
    x3 = (r1 - x2.astype(F32)).astype(BF16)
    return _dot(m, x1) + (_dot(m, x2) + _dot(m, x3))


def _pick_tile(n, prefs):
    for p in prefs:
        if n % p == 0:
            return p
    raise ValueError(f"no tile in {prefs} divides {n}")


def _cparams(sem):
    return pltpu.CompilerParams(dimension_semantics=sem, vmem_limit_bytes=VMEM_LIMIT)


def _in_kernel(x_ref, g_ref, w_ref, z_ref, h_ref):
    @pl.when(pl.program_id(1) == 0)
    def _():
        h_ref[...] = _rms(x_ref[...], g_ref[...]).astype(BF16)

    z_ref[...] = _dot(h_ref[...], w_ref[...])


def in_proj(x, g, w):
    n = x.shape[0]
    tm = _pick_tile(n, (1024, 512, 256, 128))
    tn = 1024
    return pl.pallas_call(
        _in_kernel,
        grid=(n // tm, ZW // tn),
        in_specs=[pl.BlockSpec((tm, D_MODEL), lambda i, j: (i, 0)),
                  pl.BlockSpec((1, D_MODEL), lambda i, j: (0, 0)),
                  pl.BlockSpec((D_MODEL, tn), lambda i, j: (0, j))],
        out_specs=pl.BlockSpec((tm, tn), lambda i, j: (i, j)),
        out_shape=jax.ShapeDtypeStruct((n, ZW), F32),
        scratch_shapes=[pltpu.VMEM((tm, D_MODEL), BF16)],
        compiler_params=_cparams(("parallel", "arbitrary")),
        name="in_proj",
    )(x, g, w)


def _merge_kernel(x_ref, onp_ref, ogp_ref, orp_ref, ons_ref, ogs_ref, ors_ref, mg_ref, wb_ref, wo_ref, out_ref,
                  *, n_prompt_tiles):
    is_prompt = pl.program_id(0) < n_prompt_tiles
    m = None
    for b, (op_ref, os_ref) in enumerate(((onp_ref, ons_ref), (ogp_ref, ogs_ref), (orp_ref, ors_ref))):
        o = jnp.where(is_prompt, op_ref[...], os_ref[...])
        br = _dot(o.astype(BF16), wb_ref[b])
        t = _sigmoid(mg_ref[:, b * D_MODEL:(b + 1) * D_MODEL]) * br
        m = t if m is None else m + t
    out_ref[...] = x_ref[...] + _dot(m.astype(BF16), wo_ref[...])


def merge(x, o_prompt, o_sample, z, wb, wo):
    n = x.shape[0]
    n_p = o_prompt[0].shape[0]
    n_s = o_sample[0].shape[0]
    tm = _pick_tile(math.gcd(n_p, n_s), (512, 256, 128))
    npt = n_p // tm
    row = lambda i: (i, 0)
    prow = lambda i: (jnp.minimum(i, npt - 1), 0)
    srow = lambda i: (jnp.maximum(i - npt, 0), 0)
    return pl.pallas_call(
        functools.partial(_merge_kernel, n_prompt_tiles=npt),
        grid=(n // tm,),
        in_specs=[pl.BlockSpec((tm, D_MODEL), row)]
                 + [pl.BlockSpec((tm, MIX_WIDTH), prow)] * N_BRANCH
                 + [pl.BlockSpec((tm, MIX_WIDTH), srow)] * N_BRANCH
                 + [
                  pl.BlockSpec((tm, N_BRANCH * D_MODEL), lambda i: (i, ZMG // (N_BRANCH * D_MODEL))),
                  pl.BlockSpec((N_BRANCH, MIX_WIDTH, D_MODEL), lambda i: (0, 0, 0)),
                  pl.BlockSpec((D_MODEL, D_MODEL), lambda i: (0, 0))],
        out_specs=pl.BlockSpec((tm, D_MODEL), row),
        out_shape=jax.ShapeDtypeStruct((n, D_MODEL), F32),
        compiler_params=_cparams(("parallel",)),
        name="merge",
    )(x, *o_prompt, *o_sample, z, wb, wo)


def _mlp_kernel(x_ref, g_ref, wu_ref, wd_ref, p_ref, wp_ref, wpg_ref, gf_ref, out_ref, h_ref, acc_ref,
                *, final_norm):
    f = pl.program_id(1)

    @pl.when(f == 0)
    def _():
        h_ref[...] = _rms(x_ref[...], g_ref[...]).astype(BF16)
        acc_ref[...] = jnp.zeros_like(acc_ref)

    up = jnp.maximum(_dot(h_ref[...], wu_ref[...]), 0.0)
    acc_ref[...] += _dot((up * up).astype(BF16), wd_ref[...])

    @pl.when(f == pl.num_programs(1) - 1)
    def _():
        x2 = x_ref[...] + acc_ref[...]
        ple = _dot(p_ref[...].astype(BF16), wp_ref[...])
        x3 = x2 + ple * _sigmoid(_dot(x2.astype(BF16), wpg_ref[...]))
        if final_norm:
            x3 = _rms(x3, gf_ref[...])
        out_ref[...] = x3


def mlp_ple(x, g, wu, wd, p, wp, wpg, gf, final_norm):
    n = x.shape[0]
    tm = _pick_tile(n, (1024, 512, 256, 128))
    tf = 1024
    return pl.pallas_call(
        functools.partial(_mlp_kernel, final_norm=final_norm),
        grid=(n // tm, D_FF // tf),
        in_specs=[pl.BlockSpec((tm, D_MODEL), lambda i, f: (i, 0)),
                  pl.BlockSpec((1, D_MODEL), lambda i, f: (0, 0)),
                  pl.BlockSpec((D_MODEL, tf), lambda i, f: (0, f)),
                  pl.BlockSpec((tf, D_MODEL), lambda i, f: (f, 0)),
                  pl.BlockSpec((tm, PLE_DIM), lambda i, f: (i, 0)),
                  pl.BlockSpec((PLE_DIM, D_MODEL), lambda i, f: (0, 0)),
                  pl.BlockSpec((D_MODEL, D_MODEL), lambda i, f: (0, 0)),
                  pl.BlockSpec((1, D_MODEL), lambda i, f: (0, 0))],
        out_specs=pl.BlockSpec((tm, D_MODEL), lambda i, f: (i, 0)),
        out_shape=jax.ShapeDtypeStruct((n, D_MODEL), F32),
        scratch_shapes=[pltpu.VMEM((tm, D_MODEL), BF16), pltpu.VMEM((tm, D_MODEL), F32)],
        compiler_params=_cparams(("parallel", "arbitrary")),
        name="mlp_ple",
    )(x, g, wu, wd, p, wp, wpg, gf)


def _compress_math(load_sub, pea_ref, peb_ref, w1a_ref, w1b_ref, w2_ref, nsub):
    n = len(load_sub)
    acc_a = [jnp.zeros((nsub, 4 * CMP_HIDDEN), F32) for _ in range(n)]
    acc_b = [jnp.zeros((nsub, 4 * CMP_HIDDEN), F32) for _ in range(n)]
    for j in range(CMP_STRIDE):
        for i in range(n):
            xj = load_sub[i](j)
            acc_a[i] = acc_a[i] + _dot((xj + pea_ref[j:j + 1, :]).astype(BF16), w1a_ref[j])
            acc_b[i] = acc_b[i] + _dot((xj + peb_ref[j:j + 1, :]).astype(BF16), w1b_ref[j])
    hids = [_gelu_tanh(a + pltpu.roll(b, nsub - 1, axis=0)) for a, b in zip(acc_a, acc_b)]
    return [_dot(hid.astype(BF16), w2_ref[...]) for hid in hids]


def _cmp_kernel(x_ref, pea_ref, peb_ref, w1a_ref, w1b_ref, w2_ref, out_ref, *, nsub):
    load = lambda j: x_ref[:, j * 256:(j + 1) * 256]
    out_ref[0] = _compress_math([load], pea_ref, peb_ref, w1a_ref, w1b_ref, w2_ref, nsub)[0]


def compress_prompt(kcmp_sub, cw, bsz, t):
    nsub = t // CMP_STRIDE
    const2 = lambda b: (0, 0)
    const3 = lambda b: (0, 0, 0)
    return pl.pallas_call(
        functools.partial(_cmp_kernel, nsub=nsub),
        grid=(bsz,),
        in_specs=[pl.BlockSpec((nsub, CMP_STRIDE * 256), lambda b: (b, 0)),
                  pl.BlockSpec((CMP_STRIDE, 256), const2),
                  pl.BlockSpec((CMP_STRIDE, 256), const2),
                  pl.BlockSpec((CMP_STRIDE, 256, 256), const3),
                  pl.BlockSpec((CMP_STRIDE, 256, 256), const3),
                  pl.BlockSpec((256, 256), const2)],
        out_specs=pl.BlockSpec((1, nsub, 256), lambda b: (b, 0, 0)),
        out_shape=jax.ShapeDtypeStruct((bsz, nsub, 256), F32),
        compiler_params=_cparams(("parallel",)),
        name="nsa_compress",
    )(kcmp_sub, cw["pea"], cw["peb"], cw["w1a"], cw["w1b"], cw["w2"])


def _topk_mask(scores, nblk, k):
    shape = scores[0].shape
    if shape[0] == LANES:
        return _topk_mask_square(scores, k)
    lane = lax.broadcasted_iota(jnp.int32, shape, 1)
    cnts = [jnp.zeros(shape, F32) for _ in scores]
    for j in range(nblk):
        for i, score in enumerate(scores):
            cj = score[:, j:j + 1]
            ge = jnp.where(cj >= score, 1.0, 0.0)
            gt = jnp.where(cj > score, 1.0, 0.0)
            cnts[i] = cnts[i] + jnp.where(lane > j, ge, gt)
    return [cnt < float(k) for cnt in cnts]


def _topk_cols(scs, k):
    shape = scs[0].shape
    idx = lax.broadcasted_iota(jnp.int32, shape, 0).astype(F32)
    picked = [jnp.zeros(shape, F32) for _ in scs]
    for _ in range(k):
        mxs = [jnp.max(sc, axis=0, keepdims=True) for sc in scs]
        firsts = [jnp.min(jnp.where(sc == mx, idx, float(shape[0])), axis=0, keepdims=True)
                  for sc, mx in zip(scs, mxs)]
        hits = [idx == first for first in firsts]
        picked = [jnp.where(hit, 1.0, pk) for hit, pk in zip(hits, picked)]
        scs = [jnp.where(hit, 3.0 * NEG, sc) for hit, sc in zip(hits, scs)]
    return picked


def _topk_mask_square(scores, k):
    picked = _topk_cols([jnp.transpose(s) for s in scores], k)
    return [jnp.transpose(p) > 0.5 for p in picked]


def _tile_rows(a, reps):
    return jnp.concatenate([a] * reps, axis=0)


def _nsa_queries(zq_ref, g, rows=slice(None)):
    scale = NSA_HD ** -0.5
    q = jnp.concatenate([zq_ref[rows, (NSA_GRP * g + h) * LANES:(NSA_GRP * g + h + 1) * LANES]
                         for h in range(NSA_GRP)], axis=0)
    return (q * scale).astype(BF16)


def _nsa_compressed_and_select(qgs, kcs, vcs, m_c, ov, pos1, ns, r):
    bias = _tile_rows(jnp.where(m_c, 0.0, NEG), NSA_GRP)
    ss = [_dot_nt(qg, kc) + bias for qg, kc in zip(qgs, kcs)]
    mxs = [jnp.max(s, axis=-1, keepdims=True) for s in ss]
    ps = [jnp.exp(s - mx) for s, mx in zip(ss, mxs)]
    ps = [p / jnp.sum(p, axis=-1, keepdims=True) * jnp.where(mx > 0.5 * NEG, 1.0, 0.0) for p, mx in zip(ps, mxs)]
    o_cs = [_dot(p.astype(BF16), vc) for p, vc in zip(ps, vcs)]
    psums = [(p[0:r] + p[r:2 * r]) + (p[2 * r:3 * r] + p[3 * r:4 * r]) for p in ps]
    imps = [_dot_mask_left_t(psum, ov) for psum in psums]
    blk = lax.broadcasted_iota(jnp.int32, (r, LANES), 1)
    valid = blk * SEL_LEN <= pos1
    cur = pos1 // SEL_LEN
    bonus = jnp.where((blk == 0) | (blk == cur) | (blk == cur - 1), FORCE_BONUS, 0.0)
    scores = [jnp.where(valid, imp + bonus, NEG) for imp in imps]
    sels = [sel & valid for sel in _topk_mask(scores, min(ns, LANES), min(SEL_TOPK, ns))]
    return o_cs, sels


def _dot_mask_left_t(x, m01):
    m = m01.astype(BF16)
    x1 = x.astype(BF16)
    r1 = x - x1.astype(F32)
    x2 = r1.astype(BF16)
    x3 = (r1 - x2.astype(F32)).astype(BF16)
    return _dot(x1, m) + (_dot(x2, m) + _dot(x3, m))


def _gate_cols(gates, g, c, r):
    return jnp.concatenate([gates[:, 3 * (NSA_GRP * g + h) + c:3 * (NSA_GRP * g + h) + c + 1]
                            for h in range(NSA_GRP)], axis=0)


def _nsa_assemble(o0, o1, r):
    lane = lax.broadcasted_iota(jnp.int32, (r, LANES), 1)
    return jnp.concatenate([jnp.where(lane < NSA_HD, o0[j * r:(j + 1) * r], o1[j * r:(j + 1) * r])
                            for j in range(NSA_GRP)], axis=1)


SEL_TILE = 1024
Q_TILE = 128


KV_KA = 0
KV_VA = 2 * LANES
KV_KW = 4 * LANES
KV_VWA = 5 * LANES
KV_COLS = 7 * LANES
SEL_LANE = NSA_HD


def _nsa_prompt_kernel(zq_ref, zg_ref, kcvc_ref, kv_ref, shift_ref, ov_ref, out_ref, *, t):
    r = Q_TILE
    groups = range(NSA_KV)
    qb = pl.program_id(1)
    s0 = qb * r
    pos1 = s0 + lax.broadcasted_iota(jnp.int32, (r, 1), 0)
    nc = t // CMP_STRIDE
    ns = t // SEL_LEN
    kc = kcvc_ref[0][:, 0:LANES].astype(BF16)
    vc = kcvc_ref[0][:, LANES:2 * LANES].astype(BF16)
    ncol = lax.broadcasted_iota(jnp.int32, (1, nc), 1)
    m_c = ((ncol * CMP_STRIDE + (CMP_LEN - 1)) <= pos1) & (ncol < nc - 1)
    gates = _sigmoid(zg_ref[...])
    ov = ov_ref[...]

    qs = [_nsa_queries(zq_ref, g) for g in groups]
    ocs, sels = _nsa_compressed_and_select(qs, [kc] * NSA_KV, [vc] * NSA_KV, m_c, ov, pos1, ns, r)
    sels = [jnp.where(sel, 1.0, 0.0).astype(BF16) for sel in sels]

    scale = NSA_HD ** -0.5
    qsel = []
    for g in groups:
        heads = [zq_ref[:, (NSA_GRP * g + h) * LANES:(NSA_GRP * g + h + 1) * LANES] for h in range(NSA_GRP)]
        if g == 1:
            heads = [pltpu.roll(q, NSA_HD, axis=1) for q in heads]
        qsel.append(jnp.concatenate(heads, axis=0) * scale)
    lane = lax.broadcasted_iota(jnp.int32, (1, LANES), 1)
    ind = jnp.where((lane >= SEL_LANE) & (lane < SEL_LANE + SEL_TILE // SEL_LEN), 1.0, 0.0)

    def tile_step(kt, carry, diagonal):
        k0 = pl.multiple_of(kt * SEL_TILE, SEL_TILE)
        shift = shift_ref[kt]
        va = kv_ref[pl.ds(k0, SEL_TILE), KV_VA:KV_VA + 2 * LANES]
        qas = [(qsel[g] + _tile_rows(NEG * (ind - _dot(sels[g], shift)), NSA_GRP)).astype(BF16) for g in groups]
        ss = [_dot_nt(qas[g], kv_ref[pl.ds(k0, SEL_TILE), KV_KA + g * LANES:KV_KA + (g + 1) * LANES])
              for g in groups]
        if diagonal:
            kpos = k0 + lax.broadcasted_iota(jnp.int32, (1, SEL_TILE), 1)
            causal = _tile_rows(jnp.where(kpos <= pos1, 0.0, NEG), NSA_GRP)
            ss = [s + causal for s in ss]
        m_news = [jnp.maximum(carry[g][0], jnp.max(ss[g], axis=-1, keepdims=True)) for g in groups]
        alphas = [jnp.exp(carry[g][0] - m_news[g]) for g in groups]
        ps = [jnp.exp((ss[g] - m_news[g]).astype(BF16)) for g in groups]
        accs = [alphas[g] * carry[g][1] + _dot(ps[g], va) for g in groups]
        return tuple((m_news[g], accs[g]) for g in groups)

    init = tuple((jnp.full((NSA_GRP * r, 1), NEG, F32), jnp.zeros((NSA_GRP * r, 2 * LANES), F32)) for _ in groups)
    n_full = s0 // SEL_TILE
    mid = lax.fori_loop(0, n_full, lambda kt, c: tile_step(kt, c, False), init)
    fin = tile_step(n_full, mid, True)

    span = WINDOW + r
    start = pl.multiple_of(jnp.maximum(s0 - WINDOW, 0), r)
    kw = kv_ref[pl.ds(start, span), KV_KW:KV_KW + LANES]
    vwa = kv_ref[pl.ds(start, span), KV_VWA:KV_VWA + 2 * LANES]
    kposw = start + lax.broadcasted_iota(jnp.int32, (1, span), 1)
    bias_w = _tile_rows(jnp.where((kposw <= pos1) & (kposw > pos1 - WINDOW), 0.0, NEG), NSA_GRP)

    o_ss = [fin[g][1][:, 0:LANES] / fin[g][1][:, LANES:LANES + 1] for g in groups]
    ss = [_dot_nt(qs[g], kw) + bias_w for g in groups]
    ps = [jnp.exp((s - jnp.max(s, axis=-1, keepdims=True)).astype(BF16)) for s in ss]
    ows = [_dot(p, vwa) for p in ps]
    o_ws = [ow[:, 0:LANES] / ow[:, LANES:LANES + 1] for ow in ows]
    outs = [_gate_cols(gates, g, 0, r) * ocs[g] + _gate_cols(gates, g, 1, r) * o_ss[g]
            + _gate_cols(gates, g, 2, r) * o_ws[g] for g in groups]
    out_ref[...] = _nsa_assemble(outs[0], outs[1], r)


def nsa_prompt(z, kcvc, kvb, shift, ov, bsz, t):
    nqb = t // Q_TILE
    return pl.pallas_call(
        functools.partial(_nsa_prompt_kernel, t=t),
        grid=(bsz, nqb),
        in_specs=[pl.BlockSpec((Q_TILE, NSA_HEADS * LANES), lambda b, q: (b * nqb + q, 0)),
                  pl.BlockSpec((Q_TILE, LANES), lambda b, q: (b * nqb + q, ZNG // LANES)),
                  pl.BlockSpec((1, t // CMP_STRIDE, 256), lambda b, q: (b, 0, 0)),
                  pl.BlockSpec((t, KV_COLS), lambda b, q: (b, 0)),
                  pl.BlockSpec((t // SEL_TILE, LANES, LANES), lambda b, q: (0, 0, 0)),
                  pl.BlockSpec((t // CMP_STRIDE, LANES), lambda b, q: (0, 0))],
        out_specs=pl.BlockSpec((Q_TILE, MIX_WIDTH), lambda b, q: (b * nqb + q, 0)),
        out_shape=jax.ShapeDtypeStruct((bsz * t, MIX_WIDTH), F32),
        compiler_params=_cparams(("parallel", "arbitrary")),
        name="nsa_prompt",
    )(z, z, kcvc, kvb, shift, ov)


SLAB_ROWS = 1024


def _kv_slab_kernel(slc_ref, win_ref, out_ref, *, t):
    rows = slc_ref.shape[0]
    blocks_per_tile = SEL_TILE // SEL_LEN
    lane = lax.broadcasted_iota(jnp.int32, (rows, LANES), 1)
    tok = (pl.program_id(0) * rows + lax.broadcasted_iota(jnp.int32, (rows, LANES), 0)) % t
    onehot = jnp.where(lane == SEL_LANE + (tok // SEL_LEN) % blocks_per_tile, 1.0, 0.0)
    ones_col = jnp.where(lane == 0, 1.0, 0.0)
    k_slc = slc_ref[:, 0:LANES]
    is_key = lane < NSA_HD
    pieces = [jnp.where(is_key, k_slc, onehot), jnp.where(is_key, pltpu.roll(k_slc, NSA_HD, axis=1), onehot),
              slc_ref[:, LANES:2 * LANES], ones_col, win_ref[:, 0:LANES], win_ref[:, LANES:2 * LANES], ones_col]
    for i, piece in enumerate(pieces):
        out_ref[:, i * LANES:(i + 1) * LANES] = piece.astype(BF16)


def prompt_kv_slab(z, n_p, t):
    rows = _pick_tile(n_p, (SLAB_ROWS, 512, 256, 128))
    return pl.pallas_call(
        functools.partial(_kv_slab_kernel, t=t),
        grid=(n_p // rows,),
        in_specs=[pl.BlockSpec((rows, 2 * LANES), lambda i: (i, (ZKV + 2 * LANES) // (2 * LANES))),
                  pl.BlockSpec((rows, 2 * LANES), lambda i: (i, ZKW // (2 * LANES)))],
        out_specs=pl.BlockSpec((rows, KV_COLS), lambda i: (i, 0)),
        out_shape=jax.ShapeDtypeStruct((n_p, KV_COLS), BF16),
        compiler_params=_cparams(("parallel",)),
        name="nsa_kv_slab",
    )(z, z)


def _shift_matrices(t):
    blocks_per_tile = SEL_TILE // SEL_LEN
    out = np.zeros((t // SEL_TILE, LANES, LANES), np.float32)
    for b in range(min(t // SEL_LEN, LANES)):
        out[b // blocks_per_tile, b, SEL_LANE + b % blocks_per_tile] = 1.0
    return jnp.asarray(out).astype(BF16)


NSA_SEQ_PER_STEP = 4


def _two_part_attention(args):
    s_ps = [_dot(a[0], a[1]) + a[3] for a in args]
    s_ns = [_dot_nt(a[0], a[4]) + a[6] for a in args]
    mxs = [jnp.maximum(jnp.max(sp, axis=-1, keepdims=True), jnp.max(sn, axis=-1, keepdims=True))
           for sp, sn in zip(s_ps, s_ns)]
    p_ps = [jnp.exp(sp - mx) for sp, mx in zip(s_ps, mxs)]
    p_ns = [jnp.exp(sn - mx) for sn, mx in zip(s_ns, mxs)]
    dens = [jnp.sum(pp, axis=-1, keepdims=True) + jnp.sum(pn, axis=-1, keepdims=True) for pp, pn in zip(p_ps, p_ns)]
    return [(_dot_nt(pp.astype(BF16), a[2]) + _dot(pn.astype(BF16), a[5])) / den
            for pp, pn, a, den in zip(p_ps, p_ns, args, dens)]


def _nsa_sample_kernel(pt_ref, zq_ref, zkv_ref, cwin_ref, cache_ref, pea_ref, peb_ref, w1a_ref, w1b_ref,
                       w2_ref, e_ref, ov_ref, out_ref, buf_ref, tokm_ref, sem_ref, *, layer, past, n_pages, dec):
    r = SEQ_PAD
    nseq = NSA_SEQ_PER_STEP
    seqs = range(nseq)
    groups = range(NSA_KV)
    b = pl.program_id(0)
    nb = pl.num_programs(0)
    slot = lax.rem(b, 2)

    def page_copy(step, sl, q, p):
        return pltpu.make_async_copy(cache_ref.at[layer, pt_ref[(step * nseq + q) * n_pages + p]],
                                     buf_ref.at[sl, q, :, pl.ds(p * PAGE_SIZE, PAGE_SIZE)],
                                     sem_ref.at[sl])

    def start_all(step, sl):
        for q in seqs:
            for p in range(n_pages):
                page_copy(step, sl, q, p).start()

    @pl.when(b == 0)
    def _():
        start_all(b, slot)

    @pl.when(b + 1 < nb)
    def _():
        start_all(b + 1, 1 - slot)

    for q in seqs:
        for p in range(n_pages):
            page_copy(b, slot, q, p).wait()

    nsub = past // CMP_STRIDE
    nc = nsub - 1
    ns = -(-(past + dec) // SEL_LEN)
    for c0 in range(2):
        for p in range(past // LANES):
            for q in seqs:
                tokm_ref[q, c0, p * LANES:(p + 1) * LANES, :] = jnp.transpose(
                    buf_ref[slot, q, c0 * LANES:(c0 + 1) * LANES, p * LANES:(p + 1) * LANES])
    loads = [lambda j, q=q: jnp.concatenate([tokm_ref[q, 0, pl.ds(j, nsub, stride=CMP_STRIDE), :],
                                             tokm_ref[q, 1, pl.ds(j, nsub, stride=CMP_STRIDE), :]], axis=1)
             for q in seqs]
    kcvcs = _compress_math(loads, pea_ref, peb_ref, w1a_ref, w1b_ref, w2_ref, nsub)

    row = lax.broadcasted_iota(jnp.int32, (r, 1), 0)
    pos1 = past + row
    ncol = lax.broadcasted_iota(jnp.int32, (1, nsub), 1)
    m_c = ((ncol * CMP_STRIDE + (CMP_LEN - 1)) <= pos1) & (ncol < nc)
    ov = ov_ref[...]
    kpos_p = lax.broadcasted_iota(jnp.int32, (1, past), 1)
    tnew = lax.broadcasted_iota(jnp.int32, (1, r), 1)
    kpos_n = past + tnew
    wk = cwin_ref.shape[3]
    kpos_w = past - wk + lax.broadcasted_iota(jnp.int32, (1, wk), 1)
    in_win = lambda kp: (kp <= pos1) & (kp > pos1 - WINDOW)
    bias_wp = _tile_rows(jnp.where(in_win(kpos_w), 0.0, NEG), NSA_GRP)
    bias_wn = _tile_rows(jnp.where(in_win(kpos_n) & (tnew < dec), 0.0, NEG), NSA_GRP)
    new_blk = past // SEL_LEN

    probs = [(q, g) for q in seqs for g in groups]
    rows = lambda q: slice(q * r, (q + 1) * r)
    zkv = lambda q, c0, c1: zkv_ref[rows(q), c0:c1]
    gates = [_sigmoid(zkv(q, ZNG - ZKV, ZNG - ZKV + LANES)) for q in seqs]
    qgs = [_nsa_queries(zq_ref, g, rows(q)) for q, g in probs]
    o_cs, sels = _nsa_compressed_and_select(
        qgs, [kcvcs[q][:, 0:LANES].astype(BF16) for q, g in probs],
        [kcvcs[q][:, LANES:2 * LANES].astype(BF16) for q, g in probs], m_c, ov, pos1, ns, r)
    selfs = [jnp.where(sel, 1.0, 0.0) for sel in sels]
    allow_ps = [(_dot(sf.astype(BF16), e_ref[...]) > 0.5) & (kpos_p <= pos1) for sf in selfs]
    allow_ns = [(sf[:, new_blk:new_blk + 1] > 0.5) & (kpos_n <= pos1) & (tnew < dec) for sf in selfs]
    o_ss = _two_part_attention([
        (qgs[i], buf_ref[slot, q, 2 * LANES:3 * LANES, :].astype(BF16),
         buf_ref[slot, q, 3 * LANES:4 * LANES, :].astype(BF16),
         _tile_rows(jnp.where(allow_ps[i], 0.0, NEG), NSA_GRP),
         zkv(q, 2 * LANES, 3 * LANES).astype(BF16), zkv(q, 3 * LANES, 4 * LANES).astype(BF16),
         _tile_rows(jnp.where(allow_ns[i], 0.0, NEG), NSA_GRP)) for i, (q, g) in enumerate(probs)])
    o_ws = _two_part_attention([
        (qgs[i], cwin_ref[0, q, 0:LANES, :].astype(BF16), cwin_ref[0, q, LANES:2 * LANES, :].astype(BF16), bias_wp,
         zkv(q, ZKW - ZKV, ZKW - ZKV + LANES).astype(BF16),
         zkv(q, ZKW - ZKV + LANES, ZKW - ZKV + 2 * LANES).astype(BF16), bias_wn)
        for i, (q, g) in enumerate(probs)])
    outs = [_gate_cols(gates[q], g, 0, r) * o_cs[i] + _gate_cols(gates[q], g, 1, r) * o_ss[i]
            + _gate_cols(gates[q], g, 2, r) * o_ws[i] for i, (q, g) in enumerate(probs)]
    for q in seqs:
        out_ref[rows(q), :] = _nsa_assemble(outs[NSA_KV * q], outs[NSA_KV * q + 1], r)


def nsa_sample(z, page_table, cache_t, cache_win_t, cw, e_mat, ov, layer, row0, dbsz, past, dec):
    n_pages = past // PAGE_SIZE
    nseq = NSA_SEQ_PER_STEP
    rows = nseq * SEQ_PAD
    rb0 = row0 // rows
    wk = cache_win_t.shape[3]
    c2 = lambda b, pt: (0, 0)
    c3 = lambda b, pt: (0, 0, 0)
    grid_spec = pltpu.PrefetchScalarGridSpec(
        num_scalar_prefetch=1,
        grid=(dbsz // nseq,),
        in_specs=[pl.BlockSpec((rows, NSA_HEADS * LANES), lambda b, pt: (rb0 + b, 0)),
                  pl.BlockSpec((rows, 1024), lambda b, pt: (rb0 + b, ZKV // 1024)),
                  pl.BlockSpec((1, nseq, 2 * LANES, wk), lambda b, pt: (layer, b, 0, 0)),
                  pl.BlockSpec(memory_space=pl.ANY),
                  pl.BlockSpec((CMP_STRIDE, 256), c2),
                  pl.BlockSpec((CMP_STRIDE, 256), c2),
                  pl.BlockSpec((CMP_STRIDE, 256, 256), c3),
                  pl.BlockSpec((CMP_STRIDE, 256, 256), c3),
                  pl.BlockSpec((256, 256), c2),
                  pl.BlockSpec((LANES, past), c2),
                  pl.BlockSpec((past // CMP_STRIDE, LANES), c2)],
        out_specs=pl.BlockSpec((rows, MIX_WIDTH), lambda b, pt: (b, 0)),
        scratch_shapes=[pltpu.VMEM((2, nseq, 4 * LANES, past), F32), pltpu.VMEM((nseq, 2, past, LANES), F32),
                        pltpu.SemaphoreType.DMA((2,))],
    )
    return pl.pallas_call(
        functools.partial(_nsa_sample_kernel, layer=layer, past=past, n_pages=n_pages, dec=dec),
        grid_spec=grid_spec,
        out_shape=jax.ShapeDtypeStruct((dbsz * SEQ_PAD, MIX_WIDTH), F32),
        compiler_params=_cparams(("arbitrary",)),
        name="nsa_sample",
    )(page_table.reshape(-1), z, z, cache_win_t, cache_t, cw["pea"], cw["peb"], cw["w1a"], cw["w1b"], cw["w2"],
      e_mat, ov)


def _seg_masks(r, c):
    ri = lax.broadcasted_iota(jnp.int32, (r, r), 0)
    ci = lax.broadcasted_iota(jnp.int32, (r, r), 1)
    same = (ri // c) == (ci // c)
    return same, same & (ri >= ci), same & (ri > ci)


def _gdn_prep(y, ab, par_ref, valid, c, u_ref, w_ref, qe_ref, qk_ref, kdt_ref, eg_ref):
    r = y.shape[0]
    same, incl, strict = _seg_masks(r, c)
    y = _silu(y)
    beta_all = _sigmoid(ab)
    g_all = -jnp.exp(par_ref[0:1, :]) * _softplus(ab + par_ref[1:2, :])
    if valid is not None:
        y = jnp.where(valid, y, 0.0)
        beta_all = jnp.where(valid, beta_all, 0.0)
        g_all = jnp.where(valid, g_all, 0.0)
    gcum_all = _dot_mask_left(jnp.where(incl, 1.0, 0.0), g_all)
    glast_all = _dot_mask_left(jnp.where(same, 1.0, 0.0), g_all)
    gcum_t = jnp.transpose(gcum_all)
    eye = jnp.where(lax.broadcasted_iota(jnp.int32, (r, r), 0) == lax.broadcasted_iota(jnp.int32, (r, r), 1),
                    1.0, 0.0)
    n_double = int(math.log2(c))
    tms, pws, rhss = [], [], []
    for h in range(GDN_HEADS):
        q = y[:, h * GDN_HD:(h + 1) * GDN_HD]
        k = y[:, MIX_WIDTH + h * GDN_HD:MIX_WIDTH + (h + 1) * GDN_HD]
        v = y[:, 2 * MIX_WIDTH + h * GDN_HD:2 * MIX_WIDTH + (h + 1) * GDN_HD]
        q = q * lax.rsqrt(jnp.sum(q * q, axis=-1, keepdims=True) + EPS) * (GDN_HD ** -0.5)
        k = k * lax.rsqrt(jnp.sum(k * k, axis=-1, keepdims=True) + EPS)
        g1 = jnp.broadcast_to(gcum_all[:, h:h + 1], (r, GDN_HD))
        g2 = jnp.broadcast_to(gcum_t[h:h + 1, :], (r, r))
        gl = jnp.broadcast_to(glast_all[:, h:h + 1], (r, GDN_HD))
        beta = jnp.broadcast_to(beta_all[:, GDN_HEADS + h:GDN_HEADS + h + 1], (r, GDN_HD))
        g1r = g1 if r == GDN_HD else jnp.broadcast_to(gcum_all[:, h:h + 1], (r, r))
        decay = jnp.where(incl, jnp.exp(jnp.where(incl, g1r - g2, 0.0)), 0.0)
        betar = beta if r == GDN_HD else jnp.broadcast_to(beta_all[:, GDN_HEADS + h:GDN_HEADS + h + 1], (r, r))
        a = jnp.where(strict, _bdot_nt(k, k) * decay * betar, 0.0)
        tms.append(eye - a)
        pws.append(a)
        eg1 = jnp.exp(g1)
        rhss.append(jnp.concatenate([v * beta, k * (beta * eg1)], axis=1))
        qe_ref[h] = q * eg1
        qk_ref[h] = jnp.where(incl, _bdot_nt(q, k) * decay, 0.0)
        kdt_ref[h] = jnp.transpose(k * jnp.exp(gl - g1))
        eg_ref[h] = jnp.exp(gl)
    for _ in range(n_double - 1):
        pws = [_dot3(pw, pw) for pw in pws]
        tms = [tm + _dot3(tm, pw) for tm, pw in zip(tms, pws)]
    for h in range(GDN_HEADS):
        sol = _dot3(tms[h], rhss[h])
        u_ref[h] = sol[:, 0:GDN_HD]
        w_ref[h] = sol[:, GDN_HD:2 * GDN_HD]


def _gdn_segment(h, row0, seg, c, s, u_ref, w_ref, qe_ref, qk_ref, kdt_ref, eg_ref, vn_ref, col_seg):
    rows = pl.ds(row0, c)
    vn = u_ref[h, rows, :] - _bdot(w_ref[h, rows, :], s)
    vn_ref[h, rows, :] = vn
    vn_all = vn_ref[h].astype(BF16)
    o = _bdot(qe_ref[h, rows, :], s) + _dot(qk_ref[h, rows, :].astype(BF16), vn_all)
    kdt = jnp.where(col_seg == seg, kdt_ref[h], 0.0)
    s_new = s * eg_ref[h, pl.ds(row0, 1), :] + _dot(kdt.astype(BF16), vn_all)
    return o, s_new


def _gdn_finish(o, zg, ng_ref):
    return _rms(o, ng_ref[...]) * _silu(zg)


GDN_TILE = 128


def _gdn_prompt_kernel(x_ref, ab_ref, zg_ref, cw_ref, par_ref, ng_ref, out_ref, st_ref,
                       halo_ref, u_ref, w_ref, qe_ref, qk_ref, kdt_ref, eg_ref, vn_ref):
    i = pl.program_id(1)
    r = GDN_TILE
    c = GDN_CHUNK

    @pl.when(i == 0)
    def _():
        halo_ref[...] = jnp.zeros_like(halo_ref)
        st_ref[...] = jnp.zeros_like(st_ref)

    x = x_ref[...]
    xx = jnp.concatenate([halo_ref[...], x], axis=0)
    halo_ref[...] = x[r - SUBLANES:r, :]
    y = None
    for j in range(CONV_W):
        sh = CONV_W - 1 - j
        xs = xx if sh == 0 else pltpu.roll(xx, sh, axis=0)
        t = xs[SUBLANES:, :] * cw_ref[j:j + 1, :]
        y = t if y is None else y + t
    _gdn_prep(y, ab_ref[...], par_ref, None, c, u_ref, w_ref, qe_ref, qk_ref, kdt_ref, eg_ref)
    vn_ref[...] = jnp.zeros_like(vn_ref)
    col_seg = lax.broadcasted_iota(jnp.int32, (GDN_HD, r), 1) // c
    states = [st_ref[0, h] for h in range(GDN_HEADS)]
    os_ = [[] for _ in range(GDN_HEADS)]
    for ck in range(r // c):
        for h in range(GDN_HEADS):
            o, states[h] = _gdn_segment(h, ck * c, ck, c, states[h], u_ref, w_ref, qe_ref, qk_ref, kdt_ref,
                                        eg_ref, vn_ref, col_seg)
            os_[h].append(o)
    for h in range(GDN_HEADS):
        st_ref[0, h] = states[h]
        o = jnp.concatenate(os_[h], axis=0)
        out_ref[:, h * GDN_HD:(h + 1) * GDN_HD] = _gdn_finish(o, zg_ref[:, h * GDN_HD:(h + 1) * GDN_HD], ng_ref)


def _gdn_scratch(r):
    hs = (GDN_HEADS, r, GDN_HD)
    return [pltpu.VMEM(hs, F32), pltpu.VMEM(hs, F32), pltpu.VMEM(hs, F32), pltpu.VMEM((GDN_HEADS, r, r), F32),
            pltpu.VMEM((GDN_HEADS, GDN_HD, r), F32), pltpu.VMEM(hs, F32), pltpu.VMEM(hs, F32)]


def gdn_prompt(z, conv_w, par, norm_g, bsz, t):
    nt = t // GDN_TILE
    r = GDN_TILE
    c2 = lambda b, i: (0, 0)
    return pl.pallas_call(
        _gdn_prompt_kernel,
        grid=(bsz, nt),
        in_specs=[pl.BlockSpec((r, 3 * MIX_WIDTH), lambda b, i: (b * nt + i, ZGQ // (3 * MIX_WIDTH))),
                  pl.BlockSpec((r, LANES), lambda b, i: (b * nt + i, ZGAB // LANES)),
                  pl.BlockSpec((r, MIX_WIDTH), lambda b, i: (b * nt + i, ZGZ // MIX_WIDTH)),
                  pl.BlockSpec((CONV_W, 3 * MIX_WIDTH), c2),
                  pl.BlockSpec((SUBLANES, LANES), c2),
                  pl.BlockSpec((1, GDN_HD), c2)],
        out_specs=[pl.BlockSpec((r, MIX_WIDTH), lambda b, i: (b * nt + i, 0)),
                   pl.BlockSpec((1, GDN_HEADS, GDN_HD, GDN_HD), lambda b, i: (b, 0, 0, 0))],
        out_shape=[jax.ShapeDtypeStruct((bsz * t, MIX_WIDTH), F32),
                   jax.ShapeDtypeStruct((bsz, GDN_HEADS, GDN_HD, GDN_HD), F32)],
        scratch_shapes=[pltpu.VMEM((SUBLANES, 3 * MIX_WIDTH), F32)] + _gdn_scratch(r),
        compiler_params=_cparams(("parallel", "arbitrary")),
        name="gdn_prompt",
    )(z, z, z, conv_w, par, norm_g)


SEQ_PER_STEP = 16


def _gdn_sample_kernel(x_ref, ab_ref, zg_ref, prev_ref, s0_ref, cw_ref, par_ref, ng_ref, out_ref, st_ref,
                       u_ref, w_ref, qe_ref, qk_ref, kdt_ref, eg_ref, vn_ref, o_ref, *, dec):
    r = SEQ_PER_STEP * SEQ_PAD
    c = SEQ_PAD
    row = lax.broadcasted_iota(jnp.int32, (r, 1), 0)
    tin = row % c
    valid = tin < dec
    xx = jnp.where(tin < CONV_W - 1, prev_ref[...], pltpu.roll(x_ref[...], CONV_W - 1, axis=0))
    y = None
    for j in range(CONV_W):
        xs = xx if j == 0 else pltpu.roll(xx, r - j, axis=0)
        t = xs * cw_ref[j:j + 1, :]
        y = t if y is None else y + t
    _gdn_prep(y, ab_ref[...], par_ref, valid, c, u_ref, w_ref, qe_ref, qk_ref, kdt_ref, eg_ref)
    vn_ref[...] = jnp.zeros_like(vn_ref)
    col_seg = lax.broadcasted_iota(jnp.int32, (GDN_HD, r), 1) // c

    def body(sq, carry):
        row0 = pl.multiple_of(sq * c, c)
        for h in range(GDN_HEADS):
            o, s = _gdn_segment(h, row0, sq, c, s0_ref[0, sq, h], u_ref, w_ref, qe_ref, qk_ref, kdt_ref, eg_ref,
                                vn_ref, col_seg)
            st_ref[sq, h] = s
            o_ref[h, pl.ds(row0, c), :] = o
        return carry

    lax.fori_loop(0, SEQ_PER_STEP, body, 0, unroll=4)
    for h in range(GDN_HEADS):
        out_ref[:, h * GDN_HD:(h + 1) * GDN_HD] = _gdn_finish(o_ref[h], zg_ref[:, h * GDN_HD:(h + 1) * GDN_HD],
                                                              ng_ref)


def gdn_sample(z, prev_pad, s0_all, layer, conv_w, par, norm_g, row0, dbsz, dec):
    r = SEQ_PER_STEP * SEQ_PAD
    rb0 = row0 // r
    c2 = lambda i: (0, 0)
    return pl.pallas_call(
        functools.partial(_gdn_sample_kernel, dec=dec),
        grid=(dbsz // SEQ_PER_STEP,),
        in_specs=[pl.BlockSpec((r, 3 * MIX_WIDTH), lambda i: (rb0 + i, ZGQ // (3 * MIX_WIDTH))),
                  pl.BlockSpec((r, LANES), lambda i: (rb0 + i, ZGAB // LANES)),
                  pl.BlockSpec((r, MIX_WIDTH), lambda i: (rb0 + i, ZGZ // MIX_WIDTH)),
                  pl.BlockSpec((r, 3 * MIX_WIDTH), lambda i: (i, 0)),
                  pl.BlockSpec((1, SEQ_PER_STEP, GDN_HEADS, GDN_HD, GDN_HD), lambda i: (layer, i, 0, 0, 0)),
                  pl.BlockSpec((CONV_W, 3 * MIX_WIDTH), c2),
                  pl.BlockSpec((SUBLANES, LANES), c2),
                  pl.BlockSpec((1, GDN_HD), c2)],
        out_specs=[pl.BlockSpec((r, MIX_WIDTH), lambda i: (i, 0)),
                   pl.BlockSpec((SEQ_PER_STEP, GDN_HEADS, GDN_HD, GDN_HD), lambda i: (i, 0, 0, 0))],
        out_shape=[jax.ShapeDtypeStruct((dbsz * SEQ_PAD, MIX_WIDTH), F32),
                   jax.ShapeDtypeStruct((dbsz, GDN_HEADS, GDN_HD, GDN_HD), F32)],
        scratch_shapes=_gdn_scratch(r) + [pltpu.VMEM((GDN_HEADS, r, GDN_HD), F32)],
        compiler_params=_cparams(("parallel",)),
        name="gdn_sample",
    )(z, z, z, prev_pad, s0_all, conv_w, par, norm_g)


def _ret_log_gamma(h):
    return math.log1p(-(2.0 ** (-5.0 - h)))


def _ret_prep(x, cs, sn, valid, h):
    qh = x[:, h * RET_HD:(h + 1) * RET_HD]
    kh = x[:, MIX_WIDTH + h * RET_HD:MIX_WIDTH + (h + 1) * RET_HD]
    vh = x[:, 2 * MIX_WIDTH + h * RET_HD:2 * MIX_WIDTH + (h + 1) * RET_HD]
    qr = qh * cs + pltpu.roll(qh, RET_HD // 2, axis=1) * sn
    kr = (kh * cs + pltpu.roll(kh, RET_HD // 2, axis=1) * sn) * (RET_HD ** -0.5)
    if valid is not None:
        kr = jnp.where(valid, kr, 0.0)
        vh = jnp.where(valid, vh, 0.0)
    return qr, kr, vh


def _ret_finish(o, gate):
    o = o * lax.rsqrt(jnp.mean(o * o, axis=-1, keepdims=True) + EPS)
    return o * _silu(gate)


def _ret_prompt_kernel(x_ref, gt_ref, cs_ref, sn_ref, out_ref, st_ref):
    i = pl.program_id(1)
    c = RET_CHUNK

    @pl.when(i == 0)
    def _():
        st_ref[...] = jnp.zeros_like(st_ref)

    x = x_ref[...]
    cs = cs_ref[...]
    sn = sn_ref[...]
    ri = lax.broadcasted_iota(jnp.int32, (c, c), 0)
    ci = lax.broadcasted_iota(jnp.int32, (c, c), 1)
    diff = (ri - ci).astype(F32)
    n = lax.broadcasted_iota(jnp.int32, (c, 1), 0).astype(F32)
    heads = range(RET_HEADS)
    lgs = [_ret_log_gamma(h) for h in heads]
    qkv = [_ret_prep(x, cs, sn, None, h) for h in heads]
    states = [st_ref[0, h] for h in heads]
    scores = [_bdot_nt(qkv[h][0], qkv[h][1]) * jnp.where(diff >= 0.0, jnp.exp(jnp.maximum(diff, 0.0) * lgs[h]), 0.0)
              for h in heads]
    cross = [_bdot(qkv[h][0], states[h]) * jnp.exp((n + 1.0) * lgs[h]) for h in heads]
    outs = [_bdot(scores[h], qkv[h][2]) + cross[h] for h in heads]
    kdts = [jnp.transpose(qkv[h][1] * jnp.exp((c - 1.0 - n) * lgs[h])).astype(BF16) for h in heads]
    for h in heads:
        st_ref[0, h] = states[h] * math.exp(c * lgs[h]) + _dot(kdts[h], qkv[h][2].astype(BF16))
        out_ref[:, h * RET_HD:(h + 1) * RET_HD] = _ret_finish(outs[h], gt_ref[:, h * RET_HD:(h + 1) * RET_HD])


def ret_prompt(z, cs, sn, bsz, t):
    c = RET_CHUNK
    nt = t // c
    return pl.pallas_call(
        _ret_prompt_kernel,
        grid=(bsz, nt),
        in_specs=[pl.BlockSpec((c, 3 * MIX_WIDTH), lambda b, i: (b * nt + i, ZRQ // (3 * MIX_WIDTH))),
                  pl.BlockSpec((c, MIX_WIDTH), lambda b, i: (b * nt + i, ZRG // MIX_WIDTH)),
                  pl.BlockSpec((c, RET_HD), lambda b, i: (i, 0)),
                  pl.BlockSpec((c, RET_HD), lambda b, i: (i, 0))],
        out_specs=[pl.BlockSpec((c, MIX_WIDTH), lambda b, i: (b * nt + i, 0)),
                   pl.BlockSpec((1, RET_HEADS, RET_HD, RET_HD), lambda b, i: (b, 0, 0, 0))],
        out_shape=[jax.ShapeDtypeStruct((bsz * t, MIX_WIDTH), F32),
                   jax.ShapeDtypeStruct((bsz, RET_HEADS, RET_HD, RET_HD), F32)],
        compiler_params=_cparams(("parallel", "arbitrary")),
        name="ret_prompt",
    )(z, z, cs, sn)


def _ret_sample_kernel(x_ref, gt_ref, cs_ref, sn_ref, s0_ref, out_ref, st_ref, q_ref, kdt_ref, v_ref, o_ref,
                       *, dec):
    r = SEQ_PER_STEP * SEQ_PAD
    c = SEQ_PAD
    row = lax.broadcasted_iota(jnp.int32, (r, 1), 0)
    tin = row % c
    valid = tin < dec
    n = tin.astype(F32)
    x = x_ref[...]
    cs = cs_ref[...]
    sn = sn_ref[...]
    same, incl, _ = _seg_masks(r, c)
    ri = lax.broadcasted_iota(jnp.int32, (r, r), 0)
    ci = lax.broadcasted_iota(jnp.int32, (r, r), 1)
    diff = (ri - ci).astype(F32)
    for h in range(RET_HEADS):
        lg = _ret_log_gamma(h)
        qr, kr, vh = _ret_prep(x, cs, sn, valid, h)
        dmat = jnp.where(incl, jnp.exp(jnp.maximum(diff, 0.0) * lg), 0.0)
        o_ref[h] = _bdot(_bdot_nt(qr, kr) * dmat, vh)
        q_ref[h] = qr
        kdt_ref[h] = jnp.transpose(kr * jnp.exp((dec - 1.0 - n) * lg))
        v_ref[h] = vh
    col_seg = lax.broadcasted_iota(jnp.int32, (RET_HD, r), 1) // c
    qdec = [jnp.exp((lax.broadcasted_iota(jnp.int32, (c, 1), 0).astype(F32) + 1.0) * _ret_log_gamma(h))
            for h in range(RET_HEADS)]

    def body(sq, carry):
        row0 = pl.multiple_of(sq * c, c)
        rows = pl.ds(row0, c)
        for h in range(RET_HEADS):
            s = s0_ref[0, sq, h]
            o_ref[h, rows, :] = o_ref[h, rows, :] + _bdot(q_ref[h, rows, :], s) * qdec[h]
            kdt = jnp.where(col_seg == sq, kdt_ref[h], 0.0)
            st_ref[sq, h] = s * math.exp(dec * _ret_log_gamma(h)) + _dot(kdt.astype(BF16), v_ref[h].astype(BF16))
        return carry

    lax.fori_loop(0, SEQ_PER_STEP, body, 0, unroll=4)
    for h in range(RET_HEADS):
        out_ref[:, h * RET_HD:(h + 1) * RET_HD] = _ret_finish(o_ref[h], gt_ref[:, h * RET_HD:(h + 1) * RET_HD])


def ret_sample(z, cs, sn, s0_all, layer, row0, dbsz, dec):
    r = SEQ_PER_STEP * SEQ_PAD
    rb0 = row0 // r
    hs = (RET_HEADS, r, RET_HD)
    return pl.pallas_call(
        functools.partial(_ret_sample_kernel, dec=dec),
        grid=(dbsz // SEQ_PER_STEP,),
        in_specs=[pl.BlockSpec((r, 3 * MIX_WIDTH), lambda i: (rb0 + i, ZRQ // (3 * MIX_WIDTH))),
                  pl.BlockSpec((r, MIX_WIDTH), lambda i: (rb0 + i, ZRG // MIX_WIDTH)),
                  pl.BlockSpec((r, RET_HD), lambda i: (0, 0)),
                  pl.BlockSpec((r, RET_HD), lambda i: (0, 0)),
                  pl.BlockSpec((1, SEQ_PER_STEP, RET_HEADS, RET_HD, RET_HD), lambda i: (layer, i, 0, 0, 0))],
        out_specs=[pl.BlockSpec((r, MIX_WIDTH), lambda i: (i, 0)),
                   pl.BlockSpec((SEQ_PER_STEP, RET_HEADS, RET_HD, RET_HD), lambda i: (i, 0, 0, 0))],
        out_shape=[jax.ShapeDtypeStruct((dbsz * SEQ_PAD, MIX_WIDTH), F32),
                   jax.ShapeDtypeStruct((dbsz, RET_HEADS, RET_HD, RET_HD), F32)],
        scratch_shapes=[pltpu.VMEM(hs, F32), pltpu.VMEM((RET_HEADS, RET_HD, r), F32), pltpu.VMEM(hs, F32),
                        pltpu.VMEM(hs, F32)],
        compiler_params=_cparams(("parallel",)),
        name="ret_sample",
    )(z, z, cs, sn, s0_all)


def _prep_w_in(w):
    sizes = (NSA_HEADS * NSA_HD, 6 * NSA_KV * NSA_HD, 3 * NSA_HEADS, 3 * MIX_WIDTH, GDN_HEADS, GDN_HEADS,
             MIX_WIDTH, 3 * MIX_WIDTH, MIX_WIDTH, N_BRANCH * D_MODEL)
    src = np.concatenate([[0], np.cumsum(sizes)])
    s_q, s_kv, s_ng, s_gq, s_ga, s_gb, s_gz, s_rq, s_rg, s_mg = src[:-1].tolist()
    wb = w.astype(BF16)
    out = jnp.zeros((w.shape[0], ZW), BF16)

    def put(o, dst, s0, n):
        return lax.dynamic_update_slice(o, lax.slice_in_dim(wb, s0, s0 + n, axis=1), (0, dst))

    for h in range(NSA_HEADS):
        out = put(out, ZQ + h * LANES + NSA_HD * (h // NSA_GRP), s_q + h * NSA_HD, NSA_HD)
    out = put(out, ZKV, s_kv, sizes[1] + sizes[2])
    out = put(out, ZGAB, s_ga, 2 * GDN_HEADS)
    out = put(out, ZGZ, s_gz, MIX_WIDTH)
    out = put(out, ZRG, s_rg, MIX_WIDTH)
    out = put(out, ZGQ, s_gq, 3 * MIX_WIDTH)
    out = put(out, ZRQ, s_rq, 3 * MIX_WIDTH)
    out = put(out, ZMG, s_mg, N_BRANCH * D_MODEL)
    return out


def _prep_w_branch(wb):
    w0 = wb[0].reshape(NSA_HEADS, NSA_HD, D_MODEL)
    order = [h for j in range(NSA_GRP) for h in (j, j + NSA_GRP)]
    w0 = w0[np.array(order)].reshape(MIX_WIDTH, D_MODEL)
    return jnp.stack([w0, wb[1], wb[2]]).astype(BF16)


def _prep_compress(pe, w1, w2):
    sel_k = jnp.asarray(np.diag([1.0, 1.0, 0.0, 0.0]).astype(np.float32))
    sel_v = jnp.asarray(np.diag([0.0, 0.0, 1.0, 1.0]).astype(np.float32))

    def blockdiag(mk, mv):
        full = (sel_k[:, None, :, None] * mk[..., None, :, None, :]
                + sel_v[:, None, :, None] * mv[..., None, :, None, :])
        return full.reshape(mk.shape[:-2] + (4 * mk.shape[-2], 4 * mk.shape[-1]))

    w1r = w1.reshape(2, CMP_LEN, NSA_HD, CMP_HIDDEN)
    w1a = blockdiag(w1r[0, :CMP_STRIDE], w1r[1, :CMP_STRIDE])
    w1b = blockdiag(w1r[0, CMP_STRIDE:], w1r[1, CMP_STRIDE:])
    pea = jnp.concatenate([pe[0, :CMP_STRIDE], pe[0, :CMP_STRIDE], pe[1, :CMP_STRIDE], pe[1, :CMP_STRIDE]], axis=1)
    peb = jnp.concatenate([pe[0, CMP_STRIDE:], pe[0, CMP_STRIDE:], pe[1, CMP_STRIDE:], pe[1, CMP_STRIDE:]], axis=1)
    w2b = blockdiag(w2[0], w2[1])
    return dict(pea=pea, peb=peb, w1a=w1a.astype(BF16), w1b=w1b.astype(BF16), w2=w2b.astype(BF16))


def _overlap_matrix(nc_rows, nc, ns):
    n = np.arange(nc_rows)[:, None]
    s = np.arange(LANES)[None, :]
    c_start = n * CMP_STRIDE
    ov = (c_start < (s + 1) * SEL_LEN) & (s * SEL_LEN < c_start + CMP_LEN) & (n < nc) & (s < ns)
    return jnp.asarray(ov.astype(np.float32))


def _expand_matrix(nkeys):
    s = np.arange(LANES)[:, None]
    k = np.arange(nkeys)[None, :]
    return (k // SEL_LEN == s).astype(np.float32)


def _rope_tables(pos):
    half = RET_HD // 2
    inv = ROPE_BASE ** (-jnp.linspace(0.0, 1.0, half, dtype=F32))
    ang = pos.astype(F32)[:, None] * inv[None, :]
    cos, sin = jnp.cos(ang), jnp.sin(ang)
    return jnp.concatenate([cos, cos], axis=1), jnp.concatenate([-sin, sin], axis=1)


def _pad_rows(a, n):
    return jnp.pad(a, ((0, 0), (0, n - a.shape[1])) + ((0, 0),) * (a.ndim - 2))


def kernel(x_prompt, x_sample, cache_nsa_kv, cache_nsa_win, state_gdn_conv, state_gdn, state_ret, page_table,
           p_prompt, p_sample, g_mix, w_in, nsa_cmp_pe, nsa_cmp_w1, nsa_cmp_w2, gdn_conv_w, gdn_a_log,
           gdn_dt_bias, gdn_norm_g, w_branch, w_out, g_mlp, w_up, w_down, w_ple, w_ple_gate, g_final):
    bsz, t, d = x_prompt.shape
    dbsz, dec, _ = x_sample.shape
    depth = w_in.shape[0]
    past = page_table.shape[1] * PAGE_SIZE
    n_p = bsz * t
    n_s = dbsz * SEQ_PAD
    assert dec <= SEQ_PAD and dec >= CONV_W - 1 and (past % SEL_LEN) + dec <= SEL_LEN
    assert t % SEL_TILE == 0 and t >= WINDOW + Q_TILE and past % PAGE_SIZE == 0 and dbsz % SEQ_PER_STEP == 0

    x = jnp.concatenate([x_prompt.reshape(n_p, d), _pad_rows(x_sample, SEQ_PAD).reshape(n_s, d)], axis=0)
    p_all = jnp.concatenate([p_prompt.reshape(depth, n_p, PLE_DIM),
                             jnp.pad(p_sample, ((0, 0), (0, 0), (0, SEQ_PAD - dec), (0, 0))).reshape(depth, n_s, PLE_DIM)],
                            axis=1)

    nc_p = t // CMP_STRIDE - 1
    ov_p = _overlap_matrix(t // CMP_STRIDE, nc_p, t // SEL_LEN)
    shift_p = _shift_matrices(t)
    nc_s = past // CMP_STRIDE - 1
    ov_s = _overlap_matrix(past // CMP_STRIDE, nc_s, -(-(past + dec) // SEL_LEN))
    e_s = jnp.asarray(_expand_matrix(past)).astype(BF16)
    n_phys = cache_nsa_kv.shape[1]
    wk = cache_nsa_win.shape[2]
    cache_t = jnp.transpose(cache_nsa_kv, (0, 1, 3, 4, 5, 2)).reshape(depth, n_phys, 4 * LANES, PAGE_SIZE)
    cwin_t = jnp.transpose(cache_nsa_win, (0, 1, 3, 4, 5, 2)).reshape(depth, dbsz, 2 * LANES, wk)
    cs_p, sn_p = _rope_tables(jnp.arange(t))
    cs_s, sn_s = _rope_tables(past + jnp.arange(SEQ_PAD))
    cs_s = jnp.tile(cs_s, (SEQ_PER_STEP, 1))
    sn_s = jnp.tile(sn_s, (SEQ_PER_STEP, 1))

    outs = [[] for _ in range(10)]
    for i in range(depth):
        w_in_p = _prep_w_in(w_in[i])
        wb = _prep_w_branch(w_branch[i])
        cw = _prep_compress(nsa_cmp_pe[i], nsa_cmp_w1[i], nsa_cmp_w2[i])
        par = jnp.zeros((SUBLANES, LANES), F32).at[0, :GDN_HEADS].set(gdn_a_log[i]).at[1, :GDN_HEADS].set(gdn_dt_bias[i])
        ng = gdn_norm_g[i].reshape(1, GDN_HD)

        z = in_proj(x, g_mix[i].reshape(1, d), w_in_p)
        kv_s = lax.slice(z, (n_p, ZKV), (n_p + n_s, ZNG)).reshape(dbsz, SEQ_PAD, ZNG - ZKV)[:, :dec]

        kcmp_sub = lax.slice(z, (0, ZKV), (n_p, ZKV + 2 * LANES)).reshape(n_p // CMP_STRIDE,
                                                                           CMP_STRIDE * 2 * LANES)
        kcvc = compress_prompt(kcmp_sub, cw, bsz, t)
        o_nsa_p = nsa_prompt(z, kcvc, prompt_kv_slab(z, n_p, t), shift_p, ov_p, bsz, t)
        o_nsa_s = nsa_sample(z, page_table, cache_t, cwin_t, cw, e_s, ov_s, i, n_p, dbsz, past, dec)

        o_gdn_p, gs_p = gdn_prompt(z, gdn_conv_w[i], par, ng, bsz, t)
        prev_pad = _pad_rows(state_gdn_conv[i], SEQ_PAD).reshape(n_s, 3 * MIX_WIDTH)
        o_gdn_s, gs_s = gdn_sample(z, prev_pad, state_gdn, i, gdn_conv_w[i], par, ng, n_p, dbsz, dec)

        o_ret_p, rs_p = ret_prompt(z, cs_p, sn_p, bsz, t)
        o_ret_s, rs_s = ret_sample(z, cs_s, sn_s, state_ret, i, n_p, dbsz, dec)

        x = merge(x, (o_nsa_p, o_gdn_p, o_ret_p), (o_nsa_s, o_gdn_s, o_ret_s), z, wb, w_out[i].astype(BF16))
        x = mlp_ple(x, g_mlp[i].reshape(1, d), w_up[i].astype(BF16), w_down[i].astype(BF16), p_all[i],
                    w_ple[i].astype(BF16), w_ple_gate[i].astype(BF16), g_final.reshape(1, d),
                    final_norm=(i == depth - 1))

        nkv = ZKW - ZKV
        outs[0].append(lax.slice(z, (0, ZKV), (n_p, ZKW)).reshape(bsz, t, 4, NSA_KV, NSA_HD))
        outs[1].append(kv_s[:, :, :nkv].reshape(dbsz, dec, 4, NSA_KV, NSA_HD))
        wlen = min(WINDOW, t)
        outs[2].append(jnp.stack([lax.slice(z, (b * t + t - wlen, ZKW), (b * t + t, ZNG)) for b in range(bsz)])
                       .reshape(bsz, wlen, 2, NSA_KV, NSA_HD))
        win = jnp.concatenate([cache_nsa_win[i], kv_s[:, :, nkv:].reshape(dbsz, dec, 2, NSA_KV, NSA_HD)], axis=1)
        wlen_s = min(WINDOW, past + dec)
        outs[3].append(win[:, win.shape[1] - wlen_s:])
        nconv = CONV_W - 1
        outs[4].append(jnp.stack([lax.slice(z, (b * t + t - nconv, ZGQ), (b * t + t, ZGQ + 3 * MIX_WIDTH))
                                  for b in range(bsz)]))
        gq_s = lax.slice(z, (n_p, ZGQ), (n_p + n_s, ZGQ + 3 * MIX_WIDTH)).reshape(dbsz, SEQ_PAD, 3 * MIX_WIDTH)
        outs[5].append(gq_s[:, dec - nconv:dec])
        outs[6].append(gs_p)
        outs[7].append(gs_s)
        outs[8].append(rs_p)
        outs[9].append(rs_s)

    y_prompt = x[:n_p].reshape(bsz, t, d)
    y_sample = x[n_p:].reshape(dbsz, SEQ_PAD, d)[:, :dec]
    return (y_prompt, y_sample) + tuple(jnp.stack(o) for o in outs)
```
